```python
import math
import jax
import jax.numpy as jnp
from jax import lax
import numpy as np

D_MODEL = 1024
BATCH = 16
SEQ = 2048
DEPTH = 2

RWKV_HEAD = 64
RWKV_HEADS = D_MODEL // RWKV_HEAD
RWKV_DIM = RWKV_HEADS * RWKV_HEAD
W_LORA = max(32, int(round(1.8 * D_MODEL ** 0.5 / 32)) * 32)
A_LORA = max(32, int(round(1.8 * D_MODEL ** 0.5 / 32)) * 32)
V_LORA = max(32, int(round(1.3 * D_MODEL ** 0.5 / 32)) * 32)
G_LORA = max(32, int(round(0.6 * D_MODEL ** 0.8 / 32)) * 32)
RWKV_GN_EPS = 64e-5
RWKV_COLS = 3 * RWKV_DIM + 2 * W_LORA + 2 * A_LORA + G_LORA

M_EXPAND = 2
M_INNER = M_EXPAND * D_MODEL
M_HEADDIM = 64
M_HEADS = M_INNER // M_HEADDIM
M_GROUPS = 4
M_STATE = 128
M_CONV = 5
M_CHUNK = 128
M_XBC = M_INNER + 2 * M_GROUPS * M_STATE
MAMBA_COLS = M_INNER + M_XBC + 2 * M_HEADS

DIFF_HEADS = D_MODEL // 128
DIFF_HEAD = 64
DIFF_V = 2 * DIFF_HEAD
DIFF_COLS = 3 * DIFF_HEADS * 2 * DIFF_HEAD
Q_BLOCK = 128

N_BRANCH = 3
GATE_COLS = N_BRANCH * D_MODEL
N_IN = RWKV_COLS + MAMBA_COLS + DIFF_COLS + GATE_COLS
D_FF = 256 * ((8 * D_MODEL // 3 + 255) // 256)
N_SUB = 3
EPS = 1e-6

kernel_name = "hybrid_rwkv7_mamba2_diffattn_macaron_encoder"


def _split_points(sizes):
    return [int(s) for s in np.cumsum(sizes)[:-1]]


def rmsnorm(x, w, eps=EPS):
    xf = x.astype(jnp.float32)
    y = xf * lax.rsqrt(jnp.mean(xf * xf, axis=-1, keepdims=True) + eps)
    return (y * w.astype(jnp.float32)).astype(x.dtype)


def modulated_norm(x, w, m):
    return rmsnorm(x, w) * (1.0 + m[:, 1, None]) + m[:, 0, None]


def swiglu(h, w13, w2):
    g, u = jnp.split(h @ w13, 2, axis=-1)
    return (jax.nn.silu(g) * u) @ w2


def centred_shift(s):
    pad = jnp.pad(s, ((0, 0), (1, 1), (0, 0)))
    return 0.5 * (pad[:, :-2] + pad[:, 2:])


def depthwise_conv_centred(x, w):
    k, ch = w.shape
    return lax.conv_general_dilated(
        x, w[:, None, :].astype(x.dtype), window_strides=(1,),
        padding=[((k - 1) // 2, (k - 1) // 2)],
        dimension_numbers=("NWC", "WIO", "NWC"), feature_group_count=ch)


def alibi_slopes(n_heads):
    return jnp.asarray(2.0 ** (-8.0 * np.arange(1, n_heads + 1) / n_heads), jnp.float32)


def rwkv7_scan(r, w, k, v, a, b, reverse):
    bsz, _, h, n = r.shape

    def step(state, inp):
        r_t, w_t, k_t, v_t, a_t, b_t = inp
        sa = jnp.einsum("bhij,bhj->bhi", state, a_t)
        state = (state * w_t[:, :, None, :] + sa[..., None] * b_t[:, :, None, :]
                 + v_t[..., None] * k_t[:, :, None, :])
        return state, jnp.einsum("bhij,bhj->bhi", state, r_t)

    xs = tuple(jnp.moveaxis(t, 1, 0) for t in (r, w, k, v, a, b))
    s0 = jnp.zeros((bsz, h, n, n), jnp.float32)
    _, y = lax.scan(step, s0, xs, reverse=reverse)
    return jnp.moveaxis(y, 0, 1)


def rwkv7_branch(p, w0, w2, a0, a2, g2, k_k, k_a, r_k, ln_w, ln_b, v_first, vres):
    bsz, s, _ = p.shape
    f32 = jnp.float32
    r, k, v, cw, ca, cg = jnp.split(
        p, _split_points([RWKV_DIM, RWKV_DIM, RWKV_DIM, 2 * W_LORA, 2 * A_LORA, G_LORA]), axis=-1)
    if vres is None:
        v_first = v
    else:
        v0, v1, v2 = vres
        v = v + (v_first - v) * jax.nn.sigmoid(v0 + (v @ v1) @ v2)
    w_log = -jax.nn.softplus(-(w0 + jnp.einsum("bsdr,drc->bsdc", jnp.tanh(cw.reshape(bsz, s, 2, W_LORA)), w2))) - 0.5
    decay = jnp.exp(-jnp.exp(w_log.astype(f32)))
    iclr = jax.nn.sigmoid(a0 + jnp.einsum("bsdr,drc->bsdc", ca.reshape(bsz, s, 2, A_LORA), a2)).astype(f32)
    gate = jax.nn.sigmoid(cg) @ g2

    def heads(t):
        return t.reshape(t.shape[:2] + (RWKV_HEADS, RWKV_HEAD)).astype(f32)

    kk = heads(k * k_k)
    kk = kk * lax.rsqrt(jnp.sum(kk * kk, axis=-1, keepdims=True) + 1e-12)
    k_dir = k.astype(f32)[:, :, None] * (1.0 + (iclr - 1.0) * k_a.astype(f32))
    rh, vh = heads(r), heads(v)
    y_fwd = rwkv7_scan(rh, heads(decay[:, :, 0]), heads(k_dir[:, :, 0]), vh, -kk,
                       kk * heads(iclr[:, :, 0]), reverse=False)
    y_bwd = rwkv7_scan(rh, heads(decay[:, :, 1]), heads(k_dir[:, :, 1]), vh, -kk,
                       kk * heads(iclr[:, :, 1]), reverse=True)
    y = y_fwd + y_bwd
    mu = jnp.mean(y, axis=-1, keepdims=True)
    var = jnp.mean(jnp.square(y - mu), axis=-1, keepdims=True)
    yn = ((y - mu) * lax.rsqrt(var + RWKV_GN_EPS)).reshape(bsz, s, RWKV_DIM) * ln_w + ln_b
    bonus = jnp.sum(rh * heads(k_dir[:, :, 0] + k_dir[:, :, 1]) * r_k, axis=-1, keepdims=True) * vh
    out = (yn + bonus.reshape(bsz, s, RWKV_DIM)) * gate
    return out.astype(p.dtype), v_first


def ssd_chunked(xdt, da, bg, cg):
    bsz, s, h, p = xdt.shape
    g, n = bg.shape[2], bg.shape[3]
    r = h // g
    nc = s // M_CHUNK
    xc = xdt.reshape(bsz, nc, M_CHUNK, g, r, p)
    ac = da.reshape(bsz, nc, M_CHUNK, g, r)
    bc = bg.reshape(bsz, nc, M_CHUNK, g, n)
    cc = cg.reshape(bsz, nc, M_CHUNK, g, n)
    a_cs = jnp.cumsum(ac, axis=2)
    lower = jnp.tril(jnp.ones((M_CHUNK, M_CHUNK), bool))[None, None, :, :, None, None]
    seg = a_cs[:, :, :, None] - a_cs[:, :, None, :]
    lmat = jnp.exp(jnp.where(lower, seg, -jnp.inf))
    cb = jnp.einsum("bclgn,bcsgn->bclsg", cc, bc)
    y_diag = jnp.einsum("bclsgr,bcsgrp->bclgrp", cb[..., None] * lmat, xc)
    decay_to_end = jnp.exp(a_cs[:, :, -1:] - a_cs)
    states = jnp.einsum("bclgn,bclgrp->bcgrpn", bc, xc * decay_to_end[..., None])
    chunk_decay = jnp.exp(a_cs[:, :, -1])

    def carry_step(hs, inp):
        dec, st = inp
        return hs * dec[..., None, None] + st, hs

    h0 = jnp.zeros((bsz, g, r, p, n), xdt.dtype)
    _, h_prev = lax.scan(carry_step, h0, (jnp.moveaxis(chunk_decay, 1, 0), jnp.moveaxis(states, 1, 0)))
    h_prev = jnp.moveaxis(h_prev, 0, 1)
    y_off = jnp.einsum("bclgn,bcgrpn->bclgrp", cc, h_prev) * jnp.exp(a_cs)[..., None]
    return (y_diag + y_off).reshape(bsz, s, h, p)


def mamba2_branch(p, conv_w, conv_b, dt_bias, a_log, d_skip, norm_w):
    bsz, s, _ = p.shape
    f32 = jnp.float32
    z, xbc, dt_raw = jnp.split(p, _split_points([M_INNER, M_XBC, 2 * M_HEADS]), axis=-1)
    xbc = jax.nn.silu(depthwise_conv_centred(xbc, conv_w) + conv_b)
    xm, bm, cm = jnp.split(xbc, _split_points([M_INNER, M_GROUPS * M_STATE, M_GROUPS * M_STATE]), axis=-1)
    xh = xm.reshape(bsz, s, M_HEADS, M_HEADDIM).astype(f32)
    bg = bm.reshape(bsz, s, M_GROUPS, M_STATE).astype(f32)
    cg = cm.reshape(bsz, s, M_GROUPS, M_STATE).astype(f32)
    dt = jax.nn.softplus(dt_raw.reshape(bsz, s, 2, M_HEADS).astype(f32) + dt_bias)
    a = -jnp.exp(a_log.astype(f32))
    y_fwd = ssd_chunked(xh * dt[:, :, 0, :, None], dt[:, :, 0] * a[0], bg, cg)
    y_bwd = jnp.flip(ssd_chunked(jnp.flip(xh * dt[:, :, 1, :, None], 1), jnp.flip(dt[:, :, 1] * a[1], 1),
                                 jnp.flip(bg, 1), jnp.flip(cg, 1)), 1)
    y = y_fwd + y_bwd + d_skip.astype(f32)[:, None] * xh
    y = y.reshape(bsz, s, M_INNER) * jax.nn.silu(z.astype(f32))
    y = rmsnorm(y.reshape(bsz, s, M_GROUPS, M_INNER // M_GROUPS),
                norm_w.reshape(M_GROUPS, M_INNER // M_GROUPS), 1e-5)
    return y.reshape(bsz, s, M_INNER).astype(p.dtype)


def diff_attention_branch(p, lam_vecs, subln_w, lambda_init):
    bsz, s, _ = p.shape
    q, k, v = jnp.split(p, 3, axis=-1)
    q = q.reshape(bsz, s, DIFF_HEADS, 2, DIFF_HEAD)
    k = k.reshape(bsz, s, DIFF_HEADS, 2, DIFF_HEAD)
    v = v.reshape(bsz, s, DIFF_HEADS, DIFF_V)
    lv = lam_vecs.astype(jnp.float32)
    lam = jnp.exp(jnp.sum(lv[0] * lv[1])) - jnp.exp(jnp.sum(lv[2] * lv[3])) + lambda_init
    slopes = alibi_slopes(DIFF_HEADS)
    pos = jnp.arange(s, dtype=jnp.int32)
    nb = s // Q_BLOCK
    qb = jnp.moveaxis(q.reshape(bsz, nb, Q_BLOCK, DIFF_HEADS, 2, DIFF_HEAD), 1, 0)
    pb = pos.reshape(nb, Q_BLOCK)
    scale = DIFF_HEAD ** -0.5

    def block(args):
        qi, pi = args
        sc = jnp.einsum("bqhmd,bkhmd->bhmqk", qi, k).astype(jnp.float32) * scale
        dist = jnp.abs(pi[:, None] - pos[None, :]).astype(jnp.float32)
        sc = sc - slopes[None, :, None, None, None] * dist[None, None, None]
        pm = jax.nn.softmax(sc, axis=-1)
        amap = pm[:, :, 0] - lam * pm[:, :, 1]
        return jnp.einsum("bhqk,bkhe->bqhe", amap.astype(v.dtype), v)

    o = lax.map(block, (qb, pb))
    o = jnp.moveaxis(o, 0, 1).reshape(bsz, s, DIFF_HEADS, DIFF_V)
    o = rmsnorm(o, subln_w, 1e-5) * (1.0 - lambda_init)
    return o.reshape(bsz, s, DIFF_HEADS * DIFF_V)


def token_mixer(h, w_in, rwkv_mu, rwkv_params, vres, v_first, mamba_params,
                lam_vecs, subln_w, lambda_init, w_br_r, w_br_m, w_br_d, w_out):
    bsz, s, _ = h.shape
    proj = h @ w_in
    p_r, p_m, p_d, p_g = jnp.split(proj, _split_points([RWKV_COLS, MAMBA_COLS, DIFF_COLS, GATE_COLS]), axis=-1)
    p_r = p_r + rwkv_mu * (centred_shift(p_r) - p_r)
    y_r, v_first = rwkv7_branch(p_r, *rwkv_params, v_first, vres)
    y_m = mamba2_branch(p_m, *mamba_params)
    y_d = diff_attention_branch(p_d, lam_vecs, subln_w, lambda_init)
    g = jax.nn.sigmoid(p_g).reshape(bsz, s, N_BRANCH, D_MODEL)
    merged = g[:, :, 0] * (y_r @ w_br_r) + g[:, :, 1] * (y_m @ w_br_m) + g[:, :, 2] * (y_d @ w_br_d)
    return merged @ w_out, v_first


def setup_inputs(seed: int = 0) -> dict:
    key = jax.random.key(seed)
    ks = iter(jax.random.split(key, 64))
    f32 = jnp.float32

    def nrm(shape, scale):
        return scale * jax.random.normal(next(ks), shape, f32)

    def unif(shape, lo, hi):
        return jax.random.uniform(next(ks), shape, f32, minval=lo, maxval=hi)

    x = nrm((BATCH, SEQ, D_MODEL), 1.0)
    c = nrm((BATCH, D_MODEL), 1.0)
    ada_w = nrm((DEPTH, D_MODEL, N_SUB * 3 * D_MODEL), D_MODEL ** -0.5)
    ada_b = nrm((DEPTH, N_SUB * 3 * D_MODEL), 0.02)
    norm_w = 1.0 + nrm((DEPTH, 2 * N_SUB, D_MODEL), 0.05)
    ffn_w13 = nrm((DEPTH, 2, D_MODEL, 2 * D_FF), D_MODEL ** -0.5)
    ffn_w2 = nrm((DEPTH, 2, D_FF, D_MODEL), D_FF ** -0.5)
    w_in = nrm((DEPTH, D_MODEL, N_IN), D_MODEL ** -0.5)
    rwkv_mu = unif((DEPTH, RWKV_COLS), 0.0, 1.0)
    rwkv_w0 = unif((DEPTH, 2, RWKV_DIM), -3.0, 0.0)
    rwkv_w2 = nrm((DEPTH, 2, W_LORA, RWKV_DIM), W_LORA ** -0.5)
    rwkv_a0 = nrm((DEPTH, 2, RWKV_DIM), 0.1)
    rwkv_a2 = nrm((DEPTH, 2, A_LORA, RWKV_DIM), A_LORA ** -0.5)
    rwkv_g2 = nrm((DEPTH, G_LORA, RWKV_DIM), G_LORA ** -0.5)
    rwkv_k_k = 0.85 + nrm((DEPTH, RWKV_DIM), 0.05)
    rwkv_k_a = 1.0 + nrm((DEPTH, RWKV_DIM), 0.05)
    rwkv_r_k = nrm((DEPTH, RWKV_HEADS, RWKV_HEAD), 0.1)
    rwkv_ln_w = 1.0 + nrm((DEPTH, RWKV_DIM), 0.05)
    rwkv_ln_b = nrm((DEPTH, RWKV_DIM), 0.02)
    rwkv_v0 = nrm((DEPTH - 1, RWKV_DIM), 0.1)
    rwkv_v1 = nrm((DEPTH - 1, RWKV_DIM, V_LORA), RWKV_DIM ** -0.5)
    rwkv_v2 = nrm((DEPTH - 1, V_LORA, RWKV_DIM), V_LORA ** -0.5)
    mamba_conv_w = nrm((DEPTH, M_CONV, M_XBC), M_CONV ** -0.5)
    mamba_conv_b = nrm((DEPTH, M_XBC), 0.02)
    dt0 = jnp.exp(unif((DEPTH, 2, M_HEADS), math.log(1e-3), math.log(1e-1)))
    mamba_dt_bias = dt0 + jnp.log(-jnp.expm1(-dt0))
    mamba_a_log = jnp.log(unif((DEPTH, 2, M_HEADS), 1.0, 16.0))
    mamba_d = 1.0 + nrm((DEPTH, M_HEADS), 0.1)
    mamba_norm_w = 1.0 + nrm((DEPTH, M_INNER), 0.05)
    diff_lambda = nrm((DEPTH, 4, DIFF_HEAD), 0.1)
    diff_subln_w = 1.0 + nrm((DEPTH, DIFF_V), 0.05)
    w_branch_rwkv = nrm((DEPTH, RWKV_DIM, D_MODEL), RWKV_DIM ** -0.5)
    w_branch_mamba = nrm((DEPTH, M_INNER, D_MODEL), M_INNER ** -0.5)
    w_branch_diff = nrm((DEPTH, DIFF_HEADS * DIFF_V, D_MODEL), (DIFF_HEADS * DIFF_V) ** -0.5)
    w_out = nrm((DEPTH, D_MODEL, D_MODEL), D_MODEL ** -0.5)
    return {"x": x, "c": c, "ada_w": ada_w, "ada_b": ada_b, "norm_w": norm_w,
            "ffn_w13": ffn_w13, "ffn_w2": ffn_w2, "w_in": w_in, "rwkv_mu": rwkv_mu,
            "rwkv_w0": rwkv_w0, "rwkv_w2": rwkv_w2, "rwkv_a0": rwkv_a0, "rwkv_a2": rwkv_a2,
            "rwkv_g2": rwkv_g2, "rwkv_k_k": rwkv_k_k, "rwkv_k_a": rwkv_k_a, "rwkv_r_k": rwkv_r_k,
            "rwkv_ln_w": rwkv_ln_w, "rwkv_ln_b": rwkv_ln_b, "rwkv_v0": rwkv_v0, "rwkv_v1": rwkv_v1,
            "rwkv_v2": rwkv_v2, "mamba_conv_w": mamba_conv_w, "mamba_conv_b": mamba_conv_b,
            "mamba_dt_bias": mamba_dt_bias, "mamba_a_log": mamba_a_log, "mamba_d": mamba_d,
            "mamba_norm_w": mamba_norm_w, "diff_lambda": diff_lambda, "diff_subln_w": diff_subln_w,
            "w_branch_rwkv": w_branch_rwkv, "w_branch_mamba": w_branch_mamba,
            "w_branch_diff": w_branch_diff, "w_out": w_out}


def reference(x, c, ada_w, ada_b, norm_w, ffn_w13, ffn_w2, w_in, rwkv_mu,
              rwkv_w0, rwkv_w2, rwkv_a0, rwkv_a2, rwkv_g2, rwkv_k_k, rwkv_k_a, rwkv_r_k,
              rwkv_ln_w, rwkv_ln_b, rwkv_v0, rwkv_v1, rwkv_v2, mamba_conv_w, mamba_conv_b,
              mamba_dt_bias, mamba_a_log, mamba_d, mamba_norm_w, diff_lambda, diff_subln_w,
              w_branch_rwkv, w_branch_mamba, w_branch_diff, w_out):
    bsz = x.shape[0]
    c_act = jax.nn.silu(c)
    v_first = None
    for l in range(DEPTH):
        mod = (c_act @ ada_w[l] + ada_b[l]).reshape(bsz, N_SUB, 3, D_MODEL)
        h = modulated_norm(x, norm_w[l, 0], mod[:, 0])
        x = x + 0.5 * mod[:, 0, 2, None] * rmsnorm(swiglu(h, ffn_w13[l, 0], ffn_w2[l, 0]), norm_w[l, 1])
        h = modulated_norm(x, norm_w[l, 2], mod[:, 1])
        vres = None if l == 0 else (rwkv_v0[l - 1], rwkv_v1[l - 1], rwkv_v2[l - 1])
        rwkv_params = (rwkv_w0[l], rwkv_w2[l], rwkv_a0[l], rwkv_a2[l], rwkv_g2[l], rwkv_k_k[l],
                       rwkv_k_a[l], rwkv_r_k[l], rwkv_ln_w[l], rwkv_ln_b[l])
        mamba_params = (mamba_conv_w[l], mamba_conv_b[l], mamba_dt_bias[l], mamba_a_log[l],
                        mamba_d[l], mamba_norm_w[l])
        lambda_init = 0.8 - 0.6 * math.exp(-0.3 * l)
        y, v_first = token_mixer(h, w_in[l], rwkv_mu[l], rwkv_params, vres, v_first, mamba_params,
                                 diff_lambda[l], diff_subln_w[l], lambda_init,
                                 w_branch_rwkv[l], w_branch_mamba[l], w_branch_diff[l], w_out[l])
        x = x + mod[:, 1, 2, None] * rmsnorm(y, norm_w[l, 3])
        h = modulated_norm(x, norm_w[l, 4], mod[:, 2])
        x = x + 0.5 * mod[:, 2, 2, None] * rmsnorm(swiglu(h, ffn_w13[l, 1], ffn_w2[l, 1]), norm_w[l, 5])
    return x
```

```python
import functools
import math

import numpy as np
import jax
import jax.numpy as jnp
from jax import lax
from jax.experimental import pallas as pl
from jax.experimental.pallas import tpu as pltpu

F32 = jnp.float32
BF16 = jnp.bfloat16
HI = lax.Precision.HIGHEST

D_MODEL = 1024
N_SUB = 3
EPS = 1e-6
LANES = 128
SUBLANES = 8
VMEM_LIMIT = 56 * 1024 * 1024

RW_HEAD = 64
RW_DIM = 1024
W_LORA = 64
A_LORA = 64
V_LORA = 32
G_LORA = 160
RW_GN_EPS = 64e-5
RW_COLS = 3 * RW_DIM + 2 * W_LORA + 2 * A_LORA + G_LORA
RW_CODES_PAD = 512
RW_CHUNK = 64
M_INNER = 2048
M_HEADS = 32
M_HEADDIM = 64
M_GROUPS = 4
M_STATE = 128
M_CONV = 5
M_CHUNK = 128
M_XBC = M_INNER + 2 * M_GROUPS * M_STATE
M_COLS = M_INNER + M_XBC + 2 * M_HEADS
M_GROUP_W = M_INNER // M_GROUPS
DF_HEADS = 8
DF_HEAD = 64
DF_V = 128
DF_COLS = 3 * DF_HEADS * 2 * DF_HEAD
GATE_COLS = 3 * D_MODEL
D_FF = 2816
NEG_BIG = -1e30


def _cparams(*sem):
    return pltpu.CompilerParams(dimension_semantics=sem, vmem_limit_bytes=VMEM_LIMIT)


def _dot(a, b, **kw):
    return jnp.dot(a, b, preferred_element_type=F32, **kw)


def _dot_nt(a, b, **kw):
    return lax.dot_general(a, b, (((1,), (1,)), ((), ())), preferred_element_type=F32, **kw)


def _sigmoid(x):
    return 1.0 / (1.0 + jnp.exp(-x))


def _silu(x):
    return x * _sigmoid(x)


def _softplus(x):
    return jnp.maximum(x, 0.0) + jnp.log(1.0 + jnp.exp(-jnp.abs(x)))


def _rms(x, w, eps):
    return x * lax.rsqrt(jnp.mean(x * x, axis=-1, keepdims=True) + eps) * w


def _norm_mod(x, nw, shift, scale):
    return _rms(x, nw, EPS) * (1.0 + scale) + shift


def _pair_ones(scale=1.0):
    r = lax.broadcasted_iota(jnp.int32, (LANES, LANES), 0) // RW_HEAD
    c = lax.broadcasted_iota(jnp.int32, (LANES, LANES), 1) // RW_HEAD
    return jnp.where(r == c, scale, 0.0).astype(F32)


def _shift_rows(cur, prevz, nextz, d):
    tm = cur.shape[0]
    rolled = pltpu.roll(cur, (-d) % tm, axis=0)
    rid = lax.broadcasted_iota(jnp.int32, (SUBLANES, cur.shape[1]), 0)
    if d < 0:
        fix = pltpu.roll(prevz, (-d) % SUBLANES, axis=0)
        top = jnp.where(rid < -d, fix, rolled[0:SUBLANES])
        return jnp.concatenate([top, rolled[SUBLANES:]], axis=0)
    fix = pltpu.roll(nextz, (SUBLANES - d) % SUBLANES, axis=0)
    bot = jnp.where(rid >= SUBLANES - d, fix, rolled[tm - SUBLANES:])
    return jnp.concatenate([rolled[:tm - SUBLANES], bot], axis=0)


def _halo_specs(tm, width, n_rows):
    r8 = tm // SUBLANES
    last8 = n_rows // SUBLANES - 1
    return [
        pl.BlockSpec((tm, width), lambda i: (i, 0)),
        pl.BlockSpec((SUBLANES, width), lambda i: (jnp.maximum(i * r8 - 1, 0), 0)),
        pl.BlockSpec((SUBLANES, width), lambda i: (jnp.minimum((i + 1) * r8, last8), 0)),
    ]


def _halo_values(cur_ref, prev_ref, next_ref, tm, seq):
    i = pl.program_id(0)
    first = (i * tm) % seq == 0
    last = ((i + 1) * tm) % seq == 0
    prevz = jnp.where(first, 0.0, prev_ref[...])
    nextz = jnp.where(last, 0.0, next_ref[...])
    return cur_ref[...], prevz, nextz


def _mod_kernel(c_ref, w_ref, b_ref, o_ref):
    o_ref[0] = _dot(_silu(c_ref[...]), w_ref[0], precision=HI) + b_ref[0]


def _mod_call(c, ada_w, ada_b):
    depth, d, n = ada_w.shape
    bsz = c.shape[0]
    tn = 1152
    return pl.pallas_call(
        _mod_kernel,
        grid=(depth, n // tn),
        in_specs=[pl.BlockSpec((bsz, d), lambda l, j: (0, 0)),
                  pl.BlockSpec((1, d, tn), lambda l, j: (l, 0, j)),
                  pl.BlockSpec((1, 1, tn), lambda l, j: (l, 0, j))],
        out_specs=pl.BlockSpec((1, bsz, tn), lambda l, j: (l, 0, j)),
        out_shape=jax.ShapeDtypeStruct((depth, bsz, n), F32),
        compiler_params=_cparams("parallel", "parallel"),
        name="adaln_mod",
    )(c, ada_w, ada_b.reshape(depth, 1, n))


def _ffn_kernel(x_ref, mod_ref, nw_ref, wg_ref, wu_ref, w2_ref, o_ref, h_scr, acc_scr):
    j = pl.program_id(1)

    @pl.when(j == 0)
    def _():
        m = mod_ref[0]
        h_scr[...] = _norm_mod(x_ref[...], nw_ref[0:1], m[0:1], m[1:2]).astype(BF16)
        acc_scr[...] = jnp.zeros_like(acc_scr)

    h = h_scr[...]
    g = _dot(h, wg_ref[...])
    u = _dot(h, wu_ref[...])
    acc_scr[...] += _dot((_silu(g) * u).astype(BF16), w2_ref[...])

    @pl.when(j == pl.num_programs(1) - 1)
    def _():
        m = mod_ref[0]
        o_ref[...] = x_ref[...] + 0.5 * m[2:3] * _rms(acc_scr[...], nw_ref[1:2], EPS)


def _ffn_call(x2, mod_sub, nw2, w13, w2, seq):
    t, d = x2.shape
    dff = w2.shape[0]
    tm, tf = 1024, 256
    tm = min(tm, seq)
    nf = dff // tf
    return pl.pallas_call(
        _ffn_kernel,
        grid=(t // tm, nf),
        in_specs=[pl.BlockSpec((tm, d), lambda i, j: (i, 0)),
                  pl.BlockSpec((1, 3, d), lambda i, j: (i * tm // seq, 0, 0)),
                  pl.BlockSpec((2, d), lambda i, j: (0, 0)),
                  pl.BlockSpec((d, tf), lambda i, j: (0, j)),
                  pl.BlockSpec((d, tf), lambda i, j: (0, nf + j)),
                  pl.BlockSpec((tf, d), lambda i, j: (j, 0))],
        out_specs=pl.BlockSpec((tm, d), lambda i, j: (i, 0)),
        out_shape=jax.ShapeDtypeStruct((t, d), F32),
        scratch_shapes=[pltpu.VMEM((tm, d), BF16), pltpu.VMEM((tm, d), F32)],
        compiler_params=_cparams("parallel", "arbitrary"),
        name="swiglu_halfstep",
    )(x2, mod_sub, nw2, w13, w13, w2)


def _inproj_kernel(x_ref, mod_ref, nw_ref, w_ref, o_ref, h_scr):
    @pl.when(pl.program_id(1) == 0)
    def _():
        m = mod_ref[0]
        h_scr[...] = _norm_mod(x_ref[...], nw_ref[...], m[0:1], m[1:2]).astype(BF16)

    o_ref[...] = _dot(h_scr[...], w_ref[...]).astype(o_ref.dtype)


def _inproj_call(x2, mod_sub, nw, w, out_dtype, seq, tn):
    t, d = x2.shape
    n = w.shape[1]
    tm = min(1024, seq)
    return pl.pallas_call(
        _inproj_kernel,
        grid=(t // tm, n // tn),
        in_specs=[pl.BlockSpec((tm, d), lambda i, j: (i, 0)),
                  pl.BlockSpec((1, 3, d), lambda i, j: (i * tm // seq, 0, 0)),
                  pl.BlockSpec((1, d), lambda i, j: (0, 0)),
                  pl.BlockSpec((d, tn), lambda i, j: (0, j))],
        out_specs=pl.BlockSpec((tm, tn), lambda i, j: (i, j)),
        out_shape=jax.ShapeDtypeStruct((t, n), out_dtype),
        scratch_shapes=[pltpu.VMEM((tm, d), BF16)],
        compiler_params=_cparams("parallel", "arbitrary"),
        name="norm_inproj",
    )(x2, mod_sub, nw, w)


def _rwkv_prep_kernel(*refs, tm, seq, has_vres):
    (p_ref, pp_ref, pn_ref, c_ref, cp_ref, cn_ref, mup_ref, muc_ref, pvec_ref,
     w2_ref, a2_ref, g2_ref) = refs[:12]
    refs = refs[12:]
    if has_vres:
        vf_ref, v1_ref, v2_ref = refs[:3]
        refs = refs[3:]
    r_o, v_o, kk_o, kf_o, kb_o, bf_o, bb_o, lwf_o, lwb_o, g_o = refs

    def shift_mix(cur_ref, prev_ref, next_ref, mu):
        cur, prevz, nextz = _halo_values(cur_ref, prev_ref, next_ref, tm, seq)
        nb = 0.5 * (_shift_rows(cur, prevz, nextz, -1) + _shift_rows(cur, prevz, nextz, 1))
        return cur + mu * (nb - cur)

    p = shift_mix(p_ref, pp_ref, pn_ref, mup_ref[...])
    codes = shift_mix(c_ref, cp_ref, cn_ref, muc_ref[...])
    r = p[:, 0:RW_DIM]
    k = p[:, RW_DIM:2 * RW_DIM]
    v = p[:, 2 * RW_DIM:3 * RW_DIM]
    cw = jnp.tanh(codes[:, 0:2 * W_LORA])
    ca = codes[:, 2 * W_LORA:2 * W_LORA + 2 * A_LORA]
    cg = _sigmoid(codes[:, 2 * W_LORA + 2 * A_LORA:])
    pv = pvec_ref[...]
    if has_vres:
        lo = _dot(_dot(v, v1_ref[...], precision=HI), v2_ref[...], precision=HI)
        v = v + (vf_ref[...] - v) * _sigmoid(pv[6:7] + lo)
    r_o[...] = r
    v_o[...] = v
    g_o[...] = _dot(cg, g2_ref[...], precision=HI)
    iclr = []
    for d, lw_o in ((0, lwf_o), (1, lwb_o)):
        w_log = -_softplus(-(pv[d:d + 1] + _dot(cw, w2_ref[d], precision=HI))) - 0.5
        lw_o[...] = -jnp.exp(w_log)
        iclr.append(_sigmoid(pv[2 + d:3 + d] + _dot(ca, a2_ref[d], precision=HI)))
    kf_o[...] = k * (1.0 + (iclr[0] - 1.0) * pv[5:6])
    kb_o[...] = k * (1.0 + (iclr[1] - 1.0) * pv[5:6])
    ones2 = _pair_ones()
    kkr = k * pv[4:5]
    for cb in range(RW_DIM // LANES):
        sl = slice(cb * LANES, (cb + 1) * LANES)
        blk = kkr[:, sl]
        kkn = blk * lax.rsqrt(_dot(blk * blk, ones2, precision=HI) + 1e-12)
        kk_o[:, sl] = kkn
        bf_o[:, sl] = kkn * iclr[0][:, sl]
        bb_o[:, sl] = kkn * iclr[1][:, sl]


def _rwkv_prep_call(rkv, codes, mu_p, mu_c, pvec, w2p, a2p, g2p, vres, seq):
    t = rkv.shape[0]
    tm = 128
    has_vres = vres is not None
    full = lambda shape: pl.BlockSpec(shape, lambda i: (0,) * len(shape))
    in_specs = (_halo_specs(tm, 3 * RW_DIM, t) + _halo_specs(tm, RW_CODES_PAD, t)
                + [full((1, 3 * RW_DIM)), full((1, RW_CODES_PAD)), full((8, RW_DIM)),
                   full((2, 2 * W_LORA, RW_DIM)), full((2, 2 * A_LORA, RW_DIM)), full((256, RW_DIM))])
    args = [rkv, rkv, rkv, codes, codes, codes, mu_p, mu_c, pvec, w2p, a2p, g2p]
    if has_vres:
        in_specs += [pl.BlockSpec((tm, RW_DIM), lambda i: (i, 0)), full((RW_DIM, LANES)), full((LANES, RW_DIM))]
        args += list(vres)
    out = jax.ShapeDtypeStruct((t, RW_DIM), F32)
    return pl.pallas_call(
        functools.partial(_rwkv_prep_kernel, tm=tm, seq=seq, has_vres=has_vres),
        grid=(t // tm,),
        in_specs=in_specs,
        out_specs=[pl.BlockSpec((tm, RW_DIM), lambda i: (i, 0))] * 10,
        out_shape=[out] * 10,
        compiler_params=_cparams("parallel"),
        name="rwkv_prep",
    )(*args)


def _rwkv_scan_kernel(r_ref, v_ref, kk_ref, k_ref, b_ref, lw_ref, y_ref, h_scr, *, reverse, n_chunks, n_pairs):
    c = RW_CHUNK

    @pl.when(pl.program_id(2) == 0)
    def _():
        h_scr[...] = jnp.zeros_like(h_scr)

    ri = lax.broadcasted_iota(jnp.int32, (LANES, LANES), 0)
    ci = lax.broadcasted_iota(jnp.int32, (LANES, LANES), 1)
    same = (ri // c) == (ci // c)
    tr, tc = ri % c, ci % c
    if reverse:
        strict, incl = same & (tc > tr), same & (tc >= tr)
    else:
        strict, incl = same & (tc < tr), same & (tc <= tr)
    r64 = lax.broadcasted_iota(jnp.int32, (c, c), 0)
    c64 = lax.broadcasted_iota(jnp.int32, (c, c), 1)
    tri = jnp.where((c64 >= r64) if reverse else (c64 <= r64), 1.0, 0.0).astype(F32)
    head0 = lax.broadcasted_iota(jnp.int32, (c, LANES), 1) < RW_HEAD
    end_row = 0 if reverse else c - 1

    def stack(x):
        return jnp.concatenate([jnp.where(head0, x, 0.0), jnp.where(head0, 0.0, x)], axis=0)

    def chunk(step, carry):
        cidx = (n_chunks - 1 - step) if reverse else step
        rows = pl.ds(pl.multiple_of(cidx * c, c), c)
        cum_all = _dot(tri, lw_ref[rows, :], precision=HI)
        for pr in range(n_pairs):
            ln = slice(pr * LANES, (pr + 1) * LANES)
            cum = cum_all[:, ln]
            lw = lw_ref[rows, ln]
            tot = cum[end_row:end_row + 1]
            g_inc, g_exc, g_inv = jnp.exp(cum), jnp.exp(cum - lw), jnp.exp(-cum)
            g_end, g_tot = jnp.exp(tot - cum), jnp.exp(tot)
            kk, k, b, v = kk_ref[rows, ln], k_ref[rows, ln], b_ref[rows, ln], v_ref[rows, ln]
            a_s = stack(-kk * g_exc).astype(BF16)
            r_s = stack(r_ref[rows, ln] * g_inc).astype(BF16)
            b_s = stack(b * g_inv).astype(BF16)
            k_s = stack(k * g_inv).astype(BF16)
            v_s = stack(v)
            ht = h_scr[pr]
            ht_b = ht.astype(BF16)
            a_ab = jnp.where(strict, _dot_nt(a_s, b_s), 0.0)
            a_ak = jnp.where(strict, _dot_nt(a_s, k_s), 0.0)
            a_rb = jnp.where(incl, _dot_nt(r_s, b_s), 0.0)
            a_rk = jnp.where(incl, _dot_nt(r_s, k_s), 0.0)
            v_b = v_s.astype(BF16)
            u = _dot_nt(a_s, ht_b) + _dot(a_ak.astype(BF16), v_b)
            pw = a_ab
            for it in range(6):
                u = u + _dot(pw, u, precision=HI)
                if it < 5:
                    pw = _dot(pw, pw, precision=HI)
            u_b = u.astype(BF16)
            y = _dot_nt(r_s, ht_b) + _dot(a_rb.astype(BF16), u_b) + _dot(a_rk.astype(BF16), v_b)
            y_ref[rows, ln] = y[0:c] + y[c:2 * c]
            bg = stack(b * g_end).astype(BF16)
            kg = stack(k * g_end).astype(BF16)
            h_scr[pr] = ht * g_tot + _dot(u_b.T, bg) + _dot(v_b.T, kg)
        return carry

    lax.fori_loop(0, n_chunks, chunk, 0)


def _rwkv_scan_call(r, v, kk, k, b, lw, bsz, seq, reverse):
    t = r.shape[0]
    tb = min(256, seq)
    nt = seq // tb
    n_pairs = 4
    width = n_pairs * LANES
    if reverse:
        imap = lambda bi, pi, ti: (bi * nt + nt - 1 - ti, pi)
    else:
        imap = lambda bi, pi, ti: (bi * nt + ti, pi)
    spec = pl.BlockSpec((tb, width), imap)
    return pl.pallas_call(
        functools.partial(_rwkv_scan_kernel, reverse=reverse, n_chunks=tb // RW_CHUNK, n_pairs=n_pairs),
        grid=(bsz, RW_DIM // width, nt),
        in_specs=[spec] * 6,
        out_specs=spec,
        out_shape=jax.ShapeDtypeStruct((t, RW_DIM), F32),
        scratch_shapes=[pltpu.VMEM((n_pairs, LANES, LANES), F32)],
        compiler_params=_cparams("parallel", "parallel", "arbitrary"),
        name="rwkv_scan_bwd" if reverse else "rwkv_scan_fwd",
    )(r, v, kk, k, b, lw)


def _rwkv_post_kernel(yf_ref, yb_ref, r_ref, kf_ref, kb_ref, v_ref, g_ref, pvec_ref, o_ref):
    mean2 = _pair_ones(1.0 / RW_HEAD)
    ones2 = _pair_ones()
    pv = pvec_ref[...]
    for cb in range(RW_DIM // LANES):
        sl = slice(cb * LANES, (cb + 1) * LANES)
        y = yf_ref[:, sl] + yb_ref[:, sl]
        dlt = y - _dot(y, mean2, precision=HI)
        var = _dot(dlt * dlt, mean2, precision=HI)
        yn = dlt * lax.rsqrt(var + RW_GN_EPS) * pv[0:1, sl] + pv[1:2, sl]
        rk = r_ref[:, sl] * (kf_ref[:, sl] + kb_ref[:, sl]) * pv[2:3, sl]
        bonus = _dot(rk, ones2, precision=HI) * v_ref[:, sl]
        o_ref[:, sl] = ((yn + bonus) * g_ref[:, sl]).astype(o_ref.dtype)


def _rwkv_post_call(yf, yb, r, kf, kb, v, g, pvec):
    t = r.shape[0]
    tm = 256
    spec = pl.BlockSpec((tm, RW_DIM), lambda i: (i, 0))
    return pl.pallas_call(
        _rwkv_post_kernel,
        grid=(t // tm,),
        in_specs=[spec] * 7 + [pl.BlockSpec((8, RW_DIM), lambda i: (0, 0))],
        out_specs=spec,
        out_shape=jax.ShapeDtypeStruct((t, RW_DIM), BF16),
        compiler_params=_cparams("parallel"),
        name="rwkv_post",
    )(yf, yb, r, kf, kb, v, g, pvec)


def _mamba_conv_kernel(x_ref, xp_ref, xn_ref, w_ref, b_ref, xs_o, bm_o, cm_o, *, tm, seq):
    cur, prevz, nextz = _halo_values(x_ref, xp_ref, xn_ref, tm, seq)
    w = w_ref[...]
    half = (M_CONV - 1) // 2
    acc = cur * w[half:half + 1] + b_ref[...]
    for tap in range(M_CONV):
        if tap != half:
            acc = acc + _shift_rows(cur, prevz, nextz, tap - half) * w[tap:tap + 1]
    y = _silu(acc)
    xs_o[...] = y[:, 0:M_INNER]
    bm_o[...] = y[:, M_INNER:M_INNER + M_GROUPS * M_STATE]
    cm_o[...] = y[:, M_INNER + M_GROUPS * M_STATE:]


def _mamba_conv_call(xbc, conv_w, conv_b, seq):
    t = xbc.shape[0]
    tm = 256
    gs = M_GROUPS * M_STATE
    return pl.pallas_call(
        functools.partial(_mamba_conv_kernel, tm=tm, seq=seq),
        grid=(t // tm,),
        in_specs=_halo_specs(tm, M_XBC, t) + [pl.BlockSpec((8, M_XBC), lambda i: (0, 0)),
                                             pl.BlockSpec((1, M_XBC), lambda i: (0, 0))],
        out_specs=[pl.BlockSpec((tm, M_INNER), lambda i: (i, 0)),
                   pl.BlockSpec((tm, gs), lambda i: (i, 0)),
                   pl.BlockSpec((tm, gs), lambda i: (i, 0))],
        out_shape=[jax.ShapeDtypeStruct((t, M_INNER), F32),
                   jax.ShapeDtypeStruct((t, gs), F32),
                   jax.ShapeDtypeStruct((t, gs), F32)],
        compiler_params=_cparams("parallel"),
        name="mamba_conv",
    )(xbc, xbc, xbc, conv_w, conv_b)


def _ssd_kernel(xs_ref, bm_ref, cm_ref, dt_ref, dtb_ref, e_ref, alog_ref, y_ref, h_scr, *, reverse):
    c = M_CHUNK
    n_pairs = M_GROUP_W // LANES

    @pl.when(pl.program_id(2) == 0)
    def _():
        h_scr[...] = jnp.zeros_like(h_scr)

    ri = lax.broadcasted_iota(jnp.int32, (c, c), 0)
    ci = lax.broadcasted_iota(jnp.int32, (c, c), 1)
    keep = (ci >= ri) if reverse else (ci <= ri)
    tri = jnp.where(keep, 1.0, 0.0).astype(F32)
    end_row = 0 if reverse else c - 1
    head0 = lax.broadcasted_iota(jnp.int32, (c, LANES), 1) < M_HEADDIM

    dt = _softplus(dt_ref[...] + dtb_ref[...])
    dte = _dot(dt, e_ref[0, 0], precision=HI)
    da = dte * (-jnp.exp(alog_ref[0, 0]))
    acs = _dot(tri, da, precision=HI)
    tot = acs[end_row:end_row + 1]
    xdt = xs_ref[...] * dte
    e_acs = jnp.exp(acs)
    x_end = (xdt * jnp.exp(tot - acs)).astype(BF16)
    xdt_b = xdt.astype(BF16)
    dec = jnp.exp(tot)
    bm = bm_ref[...].astype(BF16)
    cm = cm_ref[...].astype(BF16)
    cb = _dot_nt(cm, bm)
    bm_t = bm_ref[...].T.astype(BF16)
    for pr in range(n_pairs):
        ln = slice(pr * LANES, (pr + 1) * LANES)
        a_p = acs[:, ln]
        a_pt = a_p.T
        x_p = xdt_b[:, ln]
        ys = []
        for hh in range(2):
            col = a_p[:, hh * M_HEADDIM:hh * M_HEADDIM + 1]
            row = a_pt[hh * M_HEADDIM:hh * M_HEADDIM + 1, :]
            lmat = jnp.exp(jnp.where(keep, col - row, NEG_BIG))
            ys.append(_dot((cb * lmat).astype(BF16), x_p))
        h_prev = h_scr[pr]
        y_off = _dot(cm, h_prev.astype(BF16)) * e_acs[:, ln]
        y_ref[:, ln] = jnp.where(head0, ys[0], ys[1]) + y_off
        h_scr[pr] = h_prev * dec[:, ln] + _dot(bm_t, x_end[:, ln])


def _ssd_call(xs, bm, cm, dt_raw, dtb, emat, alog_e, bsz, seq, reverse):
    t = xs.shape[0]
    nc = seq // M_CHUNK
    d = 1 if reverse else 0
    if reverse:
        row = lambda bi, gi, ci: bi * nc + nc - 1 - ci
    else:
        row = lambda bi, gi, ci: bi * nc + ci
    return pl.pallas_call(
        functools.partial(_ssd_kernel, reverse=reverse),
        grid=(bsz, M_GROUPS, nc),
        in_specs=[pl.BlockSpec((M_CHUNK, M_GROUP_W), lambda bi, gi, ci: (row(bi, gi, ci), gi)),
                  pl.BlockSpec((M_CHUNK, M_STATE), lambda bi, gi, ci: (row(bi, gi, ci), gi)),
                  pl.BlockSpec((M_CHUNK, M_STATE), lambda bi, gi, ci: (row(bi, gi, ci), gi)),
                  pl.BlockSpec((M_CHUNK, LANES), lambda bi, gi, ci: (row(bi, gi, ci), 0)),
                  pl.BlockSpec((1, LANES), lambda bi, gi, ci: (0, 0)),
                  pl.BlockSpec((1, 1, LANES, M_GROUP_W), lambda bi, gi, ci: (d, gi, 0, 0)),
                  pl.BlockSpec((1, 1, 1, M_GROUP_W), lambda bi, gi, ci: (d, gi, 0, 0))],
        out_specs=pl.BlockSpec((M_CHUNK, M_GROUP_W), lambda bi, gi, ci: (row(bi, gi, ci), gi)),
        out_shape=jax.ShapeDtypeStruct((t, M_INNER), F32),
        scratch_shapes=[pltpu.VMEM((M_GROUP_W // LANES, M_STATE, LANES), F32)],
        compiler_params=_cparams("parallel", "parallel", "arbitrary"),
        name="ssd_bwd" if reverse else "ssd_fwd",
    )(xs, bm, cm, dt_raw, dtb, emat, alog_e)


def _mamba_post_kernel(yf_ref, yb_ref, xs_ref, z_ref, d_ref, nw_ref, o_ref):
    y = (yf_ref[...] + yb_ref[...] + d_ref[...] * xs_ref[...]) * _silu(z_ref[...])
    for g in range(M_GROUPS):
        sl = slice(g * M_GROUP_W, (g + 1) * M_GROUP_W)
        o_ref[:, sl] = _rms(y[:, sl], nw_ref[:, sl], 1e-5).astype(o_ref.dtype)


def _mamba_post_call(yf, yb, xs, z, d_e, nw):
    t = xs.shape[0]
    tm = 256
    spec = pl.BlockSpec((tm, M_INNER), lambda i: (i, 0))
    vec = pl.BlockSpec((1, M_INNER), lambda i: (0, 0))
    return pl.pallas_call(
        _mamba_post_kernel,
        grid=(t // tm,),
        in_specs=[spec] * 4 + [vec, vec],
        out_specs=spec,
        out_shape=jax.ShapeDtypeStruct((t, M_INNER), BF16),
        compiler_params=_cparams("parallel"),
        name="mamba_post",
    )(yf, yb, xs, z, d_e, nw)


def _diff_attn_kernel(q_ref, k_ref, v_ref, lam_ref, slope_ref, nw_ref, o_ref, *, tq, lambda_init):
    seq = k_ref.shape[0]
    q = q_ref[...]
    k = k_ref[...]
    v = v_ref[...]
    lv = lam_ref[...]
    lam = (jnp.exp(jnp.sum(lv[0:1] * lv[1:2], keepdims=True))
           - jnp.exp(jnp.sum(lv[2:3] * lv[3:4], keepdims=True)) + lambda_init)
    rows = pl.program_id(2) * tq + lax.broadcasted_iota(jnp.int32, (tq, seq), 0)
    cols = lax.broadcasted_iota(jnp.int32, (tq, seq), 1)
    bias = slope_ref[0][:, 0:1] * jnp.abs(rows - cols).astype(F32)
    map0 = lax.broadcasted_iota(jnp.int32, (tq, LANES), 1) < DF_HEAD
    scale = DF_HEAD ** -0.5

    def attend(qm):
        s = _dot_nt(qm, k) * scale - bias
        e = jnp.exp(s - jnp.max(s, axis=-1, keepdims=True))
        return _dot(e.astype(BF16), v) / jnp.sum(e, axis=-1, keepdims=True)

    o = attend(jnp.where(map0, q, jnp.zeros_like(q))) - lam * attend(jnp.where(map0, jnp.zeros_like(q), q))
    o_ref[...] = (_rms(o, nw_ref[...], 1e-5) * (1.0 - lambda_init)).astype(o_ref.dtype)


def _diff_attn_call(qkv, lam_vecs, slopes, subln_w, bsz, seq, lambda_init):
    t = qkv.shape[0]
    tq = min(256, seq)
    nq = seq // tq
    return pl.pallas_call(
        functools.partial(_diff_attn_kernel, tq=tq, lambda_init=lambda_init),
        grid=(bsz, DF_HEADS, nq),
        in_specs=[pl.BlockSpec((tq, DF_V), lambda bi, hi, qi: (bi * nq + qi, hi)),
                  pl.BlockSpec((seq, DF_V), lambda bi, hi, qi: (bi, DF_HEADS + hi)),
                  pl.BlockSpec((seq, DF_V), lambda bi, hi, qi: (bi, 2 * DF_HEADS + hi)),
                  pl.BlockSpec((4, DF_HEAD), lambda bi, hi, qi: (0, 0)),
                  pl.BlockSpec((1, 1, LANES), lambda bi, hi, qi: (hi, 0, 0)),
                  pl.BlockSpec((1, DF_V), lambda bi, hi, qi: (0, 0))],
        out_specs=pl.BlockSpec((tq, DF_V), lambda bi, hi, qi: (bi * nq + qi, hi)),
        out_shape=jax.ShapeDtypeStruct((t, DF_HEADS * DF_V), BF16),
        compiler_params=_cparams("parallel", "parallel", "arbitrary"),
        name="diff_attn",
    )(qkv, qkv, qkv, lam_vecs, slopes, subln_w)


def _merge_kernel(yr_ref, ym_ref, yd_ref, pg_ref, x_ref, mod_ref, nw_ref, wr_ref, wm_ref, wd_ref, wo_ref, o_ref):
    d = D_MODEL
    merged = (_sigmoid(pg_ref[:, 0:d]) * _dot(yr_ref[...], wr_ref[...])
              + _sigmoid(pg_ref[:, d:2 * d]) * _dot(ym_ref[...], wm_ref[...])
              + _sigmoid(pg_ref[:, 2 * d:3 * d]) * _dot(yd_ref[...], wd_ref[...]))
    y = _dot(merged.astype(BF16), wo_ref[...])
    o_ref[...] = x_ref[...] + mod_ref[0][2:3] * _rms(y, nw_ref[...], EPS)


def _merge_call(yr, ym, yd, pg, x2, mod_sub, nw, wr, wm, wd, wo, seq):
    t, d = x2.shape
    tm = 256
    row = lambda w: pl.BlockSpec((tm, w), lambda i: (i, 0))
    res = lambda a: pl.BlockSpec(a.shape, lambda i: (0, 0), pipeline_mode=pl.Buffered(1))
    return pl.pallas_call(
        _merge_kernel,
        grid=(t // tm,),
        in_specs=[row(RW_DIM), row(M_INNER), row(DF_HEADS * DF_V), row(GATE_COLS), row(d),
                  pl.BlockSpec((1, 3, d), lambda i: (i * tm // seq, 0, 0)),
                  pl.BlockSpec((1, d), lambda i: (0, 0)),
                  res(wr), res(wm), res(wd), res(wo)],
        out_specs=row(d),
        out_shape=jax.ShapeDtypeStruct((t, d), F32),
        compiler_params=_cparams("parallel"),
        name="merge_outproj",
    )(yr, ym, yd, pg, x2, mod_sub, nw, wr, wm, wd, wo)


def _pad_cols(w, n):
    return jnp.pad(w, ((0, 0), (0, n - w.shape[1])))


def _pad_rows(w, n):
    return jnp.pad(w, ((0, n - w.shape[0]), (0, 0)))


def _dir_padded(w):
    z = jnp.zeros_like(w[0])
    return jnp.stack([jnp.concatenate([w[0], z], axis=0), jnp.concatenate([z, w[1]], axis=0)])


def _head_expand_matrix():
    e = np.zeros((2, M_GROUPS, LANES, M_GROUP_W), np.float32)
    hpg = M_HEADS // M_GROUPS
    for d in range(2):
        for g in range(M_GROUPS):
            for h in range(hpg):
                e[d, g, d * M_HEADS + g * hpg + h, h * M_HEADDIM:(h + 1) * M_HEADDIM] = 1.0
    return jnp.asarray(e)


def kernel(x, c, ada_w, ada_b, norm_w, ffn_w13, ffn_w2, w_in, rwkv_mu, rwkv_w0, rwkv_w2, rwkv_a0, rwkv_a2, rwkv_g2, rwkv_k_k, rwkv_k_a, rwkv_r_k, rwkv_ln_w, rwkv_ln_b, rwkv_v0, rwkv_v1, rwkv_v2, mamba_conv_w, mamba_conv_b, mamba_dt_bias, mamba_a_log, mamba_d, mamba_norm_w, diff_lambda, diff_subln_w, w_branch_rwkv, w_branch_mamba, w_branch_diff, w_out):
    bsz, seq, d = x.shape
    depth = ada_w.shape[0]
    t = bsz * seq
    x2 = x.reshape(t, d)
    mod_all = _mod_call(c, ada_w, ada_b)
    emat = _head_expand_matrix()
    slopes = jnp.broadcast_to(
        jnp.asarray(2.0 ** (-8.0 * np.arange(1, DF_HEADS + 1) / DF_HEADS), F32)[:, None, None], (DF_HEADS, 1, LANES))
    hpg = M_HEADS // M_GROUPS
    v_first = None
    for l in range(depth):
        mod = mod_all[l].reshape(bsz, N_SUB, 3, d)
        x2 = _ffn_call(x2, mod[:, 0], norm_w[l, 0:2], ffn_w13[l, 0].astype(BF16), ffn_w2[l, 0].astype(BF16), seq)

        wl = w_in[l]
        o_m = RW_COLS
        o_d = o_m + M_COLS
        o_g = o_d + DF_COLS
        w_rkv = wl[:, 0:3 * RW_DIM].astype(BF16)
        w_codes = _pad_cols(wl[:, 3 * RW_DIM:RW_COLS], RW_CODES_PAD).astype(BF16)
        w_z = wl[:, o_m:o_m + M_INNER].astype(BF16)
        w_xbc = wl[:, o_m + M_INNER:o_m + M_INNER + M_XBC].astype(BF16)
        w_dt = _pad_cols(wl[:, o_m + M_INNER + M_XBC:o_d], LANES).astype(BF16)
        w_qkv = wl[:, o_d:o_g].astype(BF16)
        w_gate = wl[:, o_g:].astype(BF16)
        msub, nw = mod[:, 1], norm_w[l, 2:3]
        p_rkv = _inproj_call(x2, msub, nw, w_rkv, F32, seq, 512)
        p_codes = _inproj_call(x2, msub, nw, w_codes, F32, seq, RW_CODES_PAD)
        p_z = _inproj_call(x2, msub, nw, w_z, F32, seq, 512)
        p_xbc = _inproj_call(x2, msub, nw, w_xbc, F32, seq, 512)
        p_dt = _inproj_call(x2, msub, nw, w_dt, F32, seq, LANES)
        p_qkv = _inproj_call(x2, msub, nw, w_qkv, BF16, seq, 512)
        p_gate = _inproj_call(x2, msub, nw, w_gate, F32, seq, 512)

        mu = rwkv_mu[l]
        mu_p = mu[None, 0:3 * RW_DIM]
        mu_c = _pad_cols(mu[None, 3 * RW_DIM:], RW_CODES_PAD)
        v0 = rwkv_v0[l - 1] if l > 0 else jnp.zeros((RW_DIM,), F32)
        pvec = jnp.stack([rwkv_w0[l, 0], rwkv_w0[l, 1], rwkv_a0[l, 0], rwkv_a0[l, 1],
                          rwkv_k_k[l], rwkv_k_a[l], v0, jnp.zeros((RW_DIM,), F32)])
        g2p = _pad_rows(rwkv_g2[l], 256)
        vres = None
        if l > 0:
            vres = (v_first, _pad_cols(rwkv_v1[l - 1], LANES), _pad_rows(rwkv_v2[l - 1], LANES))
        r, v, kk, k_f, k_b, b_f, b_b, lw_f, lw_b, gate = _rwkv_prep_call(
            p_rkv, p_codes, mu_p, mu_c, pvec, _dir_padded(rwkv_w2[l]), _dir_padded(rwkv_a2[l]), g2p, vres, seq)
        if l == 0:
            v_first = v
        y_f = _rwkv_scan_call(r, v, kk, k_f, b_f, lw_f, bsz, seq, False)
        y_b = _rwkv_scan_call(r, v, kk, k_b, b_b, lw_b, bsz, seq, True)
        pvec2 = jnp.concatenate([jnp.stack([rwkv_ln_w[l], rwkv_ln_b[l], rwkv_r_k[l].reshape(RW_DIM)]),
                                 jnp.zeros((5, RW_DIM), F32)])
        y_r = _rwkv_post_call(y_f, y_b, r, k_f, k_b, v, gate, pvec2)

        conv_w = _pad_rows(mamba_conv_w[l], 8)
        xs, bm, cm = _mamba_conv_call(p_xbc, conv_w, mamba_conv_b[l][None], seq)
        dtb = _pad_cols(mamba_dt_bias[l].reshape(1, 2 * M_HEADS), LANES)
        alog_e = jnp.repeat(mamba_a_log[l].reshape(2, M_GROUPS, 1, hpg), M_HEADDIM, axis=-1)
        ym_f = _ssd_call(xs, bm, cm, p_dt, dtb, emat, alog_e, bsz, seq, False)
        ym_b = _ssd_call(xs, bm, cm, p_dt, dtb, emat, alog_e, bsz, seq, True)
        d_e = jnp.repeat(mamba_d[l], M_HEADDIM)[None]
        y_m = _mamba_post_call(ym_f, ym_b, xs, p_z, d_e, mamba_norm_w[l][None])

        lambda_init = 0.8 - 0.6 * math.exp(-0.3 * l)
        y_d = _diff_attn_call(p_qkv, diff_lambda[l], slopes, diff_subln_w[l][None], bsz, seq, lambda_init)

        x2 = _merge_call(y_r, y_m, y_d, p_gate, x2, mod[:, 1], norm_w[l, 3:4],
                         w_branch_rwkv[l].astype(BF16), w_branch_mamba[l].astype(BF16),
                         w_branch_diff[l].astype(BF16), w_out[l].astype(BF16), seq)

        x2 = _ffn_call(x2, mod[:, 2], norm_w[l, 4:6], ffn_w13[l, 1].astype(BF16), ffn_w2[l, 1].astype(BF16), seq)
    return x2.reshape(bsz, seq, d)
```

```python
import functools
import math

import numpy as np
import jax
import jax.numpy as jnp
from jax import lax
from jax.experimental import pallas as pl
from jax.experimental.pallas import tpu as pltpu

F32 = jnp.float32
BF16 = jnp.bfloat16
HI = lax.Precision.HIGHEST

D_MODEL = 1024
N_SUB = 3
EPS = 1e-6
LANES = 128
SUBLANES = 8
VMEM_LIMIT = 56 * 1024 * 1024

RW_HEAD = 64
RW_DIM = 1024
W_LORA = 64
A_LORA = 64
V_LORA = 32
G_LORA = 160
RW_GN_EPS = 64e-5
RW_COLS = 3 * RW_DIM + 2 * W_LORA + 2 * A_LORA + G_LORA
RW_CODES_PAD = 512
RW_CHUNK = 64
M_INNER = 2048
M_HEADS = 32
M_HEADDIM = 64
M_GROUPS = 4
M_STATE = 128
M_CONV = 5
M_CHUNK = 128
M_XBC = M_INNER + 2 * M_GROUPS * M_STATE
M_COLS = M_INNER + M_XBC + 2 * M_HEADS
M_GROUP_W = M_INNER // M_GROUPS
DF_HEADS = 8
DF_HEAD = 64
DF_V = 128
DF_COLS = 3 * DF_HEADS * 2 * DF_HEAD
GATE_COLS = 3 * D_MODEL
D_FF = 2816
NEG_BIG = -1e30


def _cparams(*sem):
    return pltpu.CompilerParams(dimension_semantics=sem, vmem_limit_bytes=VMEM_LIMIT)


def _dot(a, b, **kw):
    return jnp.dot(a, b, preferred_element_type=F32, **kw)


def _dot_nt(a, b, **kw):
    return lax.dot_general(a, b, (((1,), (1,)), ((), ())), preferred_element_type=F32, **kw)


def _split2(x):
    hi = x.astype(BF16)
    return hi, (x - hi.astype(F32)).astype(BF16)


def _dot_sel(x, m_b):
    hi, lo = _split2(x)
    return _dot(hi, m_b) + _dot(lo, m_b)


def _sel_dot(m_b, x):
    hi, lo = _split2(x)
    return _dot(m_b, hi) + _dot(m_b, lo)


def _bdot(a, b):
    return _dot(a.astype(BF16), b.astype(BF16))


def _sigmoid(x):
    return 1.0 / (1.0 + jnp.exp(-x))


def _silu(x):
    return x * _sigmoid(x)


def _softplus(x):
    return jnp.maximum(x, 0.0) + jnp.log(1.0 + jnp.exp(-jnp.abs(x)))


def _rms(x, w, eps):
    return x * lax.rsqrt(jnp.mean(x * x, axis=-1, keepdims=True) + eps) * w


def _norm_mod(x, nw, shift, scale):
    return _rms(x, nw, EPS) * (1.0 + scale) + shift


def _pair_ones(scale=1.0):
    r = lax.broadcasted_iota(jnp.int32, (LANES, LANES), 0) // RW_HEAD
    c = lax.broadcasted_iota(jnp.int32, (LANES, LANES), 1) // RW_HEAD
    return jnp.where(r == c, scale, 0.0).astype(BF16)


def _shift_rows(cur, prevz, nextz, d):
    tm = cur.shape[0]
    rolled = pltpu.roll(cur, (-d) % tm, axis=0)
    rid = lax.broadcasted_iota(jnp.int32, (SUBLANES, cur.shape[1]), 0)
    if d < 0:
        fix = pltpu.roll(prevz, (-d) % SUBLANES, axis=0)
        top = jnp.where(rid < -d, fix, rolled[0:SUBLANES])
        return jnp.concatenate([top, rolled[SUBLANES:]], axis=0)
    fix = pltpu.roll(nextz, (SUBLANES - d) % SUBLANES, axis=0)
    bot = jnp.where(rid >= SUBLANES - d, fix, rolled[tm - SUBLANES:])
    return jnp.concatenate([rolled[:tm - SUBLANES], bot], axis=0)


def _halo_specs(tm, width, n_rows):
    r8 = tm // SUBLANES
    last8 = n_rows // SUBLANES - 1
    return [
        pl.BlockSpec((tm, width), lambda i: (i, 0)),
        pl.BlockSpec((SUBLANES, width), lambda i: (jnp.maximum(i * r8 - 1, 0), 0)),
        pl.BlockSpec((SUBLANES, width), lambda i: (jnp.minimum((i + 1) * r8, last8), 0)),
    ]


def _halo_values(cur_ref, prev_ref, next_ref, tm, seq):
    i = pl.program_id(0)
    first = (i * tm) % seq == 0
    last = ((i + 1) * tm) % seq == 0
    prevz = jnp.where(first, 0.0, prev_ref[...])
    nextz = jnp.where(last, 0.0, next_ref[...])
    return cur_ref[...], prevz, nextz


def _mod_kernel(c_ref, w_ref, b_ref, o_ref):
    o_ref[0] = _dot(_silu(c_ref[...]), w_ref[0], precision=HI) + b_ref[0]


def _mod_call(c, ada_w, ada_b):
    depth, d, n = ada_w.shape
    bsz = c.shape[0]
    tn = 1152
    return pl.pallas_call(
        _mod_kernel,
        grid=(depth, n // tn),
        in_specs=[pl.BlockSpec((bsz, d), lambda l, j: (0, 0)),
                  pl.BlockSpec((1, d, tn), lambda l, j: (l, 0, j)),
                  pl.BlockSpec((1, 1, tn), lambda l, j: (l, 0, j))],
        out_specs=pl.BlockSpec((1, bsz, tn), lambda l, j: (l, 0, j)),
        out_shape=jax.ShapeDtypeStruct((depth, bsz, n), F32),
        compiler_params=_cparams("parallel", "parallel"),
        name="adaln_mod",
    )(c, ada_w, ada_b.reshape(depth, 1, n))


def _ffn_kernel(x_ref, mod_ref, nw_ref, wg_ref, wu_ref, w2_ref, o_ref, h_scr, acc_scr):
    j = pl.program_id(1)

    @pl.when(j == 0)
    def _():
        m = mod_ref[0]
        h_scr[...] = _norm_mod(x_ref[...], nw_ref[0:1], m[0:1], m[1:2]).astype(BF16)
        acc_scr[...] = jnp.zeros_like(acc_scr)

    h = h_scr[...]
    g = _dot(h, wg_ref[...])
    u = _dot(h, wu_ref[...])
    acc_scr[...] += _dot((_silu(g) * u).astype(BF16), w2_ref[...])

    @pl.when(j == pl.num_programs(1) - 1)
    def _():
        m = mod_ref[0]
        o_ref[...] = x_ref[...] + 0.5 * m[2:3] * _rms(acc_scr[...], nw_ref[1:2], EPS)


def _ffn_call(x2, mod_sub, nw2, w13, w2, seq):
    t, d = x2.shape
    dff = w2.shape[0]
    tm, tf = 1024, 256
    tm = min(tm, seq)
    nf = dff // tf
    return pl.pallas_call(
        _ffn_kernel,
        grid=(t // tm, nf),
        in_specs=[pl.BlockSpec((tm, d), lambda i, j: (i, 0)),
                  pl.BlockSpec((1, 3, d), lambda i, j: (i * tm // seq, 0, 0)),
                  pl.BlockSpec((2, d), lambda i, j: (0, 0)),
                  pl.BlockSpec((d, tf), lambda i, j: (0, j)),
                  pl.BlockSpec((d, tf), lambda i, j: (0, nf + j)),
                  pl.BlockSpec((tf, d), lambda i, j: (j, 0))],
        out_specs=pl.BlockSpec((tm, d), lambda i, j: (i, 0)),
        out_shape=jax.ShapeDtypeStruct((t, d), F32),
        scratch_shapes=[pltpu.VMEM((tm, d), BF16), pltpu.VMEM((tm, d), F32)],
        compiler_params=_cparams("parallel", "arbitrary"),
        name="swiglu_halfstep",
    )(x2, mod_sub, nw2, w13, w13, w2)


def _inproj_kernel(x_ref, mod_ref, nw_ref, w_ref, o_ref, h_scr):
    @pl.when(pl.program_id(1) == 0)
    def _():
        m = mod_ref[0]
        h_scr[...] = _norm_mod(x_ref[...], nw_ref[...], m[0:1], m[1:2]).astype(BF16)

    o_ref[...] = _dot(h_scr[...], w_ref[...]).astype(o_ref.dtype)


def _inproj_call(x2, mod_sub, nw, w, out_dtype, seq, tn):
    t, d = x2.shape
    n = w.shape[1]
    tm = min(1024, seq)
    return pl.pallas_call(
        _inproj_kernel,
        grid=(t // tm, n // tn),
        in_specs=[pl.BlockSpec((tm, d), lambda i, j: (i, 0)),
                  pl.BlockSpec((1, 3, d), lambda i, j: (i * tm // seq, 0, 0)),
                  pl.BlockSpec((1, d), lambda i, j: (0, 0)),
                  pl.BlockSpec((d, tn), lambda i, j: (0, j))],
        out_specs=pl.BlockSpec((tm, tn), lambda i, j: (i, j)),
        out_shape=jax.ShapeDtypeStruct((t, n), out_dtype),
        scratch_shapes=[pltpu.VMEM((tm, d), BF16)],
        compiler_params=_cparams("parallel", "arbitrary"),
        name="norm_inproj",
    )(x2, mod_sub, nw, w)


def _rwkv_prep_kernel(*refs, tm, seq, has_vres):
    (p_ref, pp_ref, pn_ref, c_ref, cp_ref, cn_ref, mup_ref, muc_ref, pvec_ref,
     w2_ref, a2_ref, g2_ref) = refs[:12]
    refs = refs[12:]
    if has_vres:
        vf_ref, v1_ref, v2_ref = refs[:3]
        refs = refs[3:]
    r_o, v_o, kk_o, kf_o, kb_o, bf_o, bb_o, lwf_o, lwb_o, g_o = refs

    def shift_mix(cur_ref, prev_ref, next_ref, mu):
        cur, prevz, nextz = _halo_values(cur_ref, prev_ref, next_ref, tm, seq)
        nb = 0.5 * (_shift_rows(cur, prevz, nextz, -1) + _shift_rows(cur, prevz, nextz, 1))
        return cur + mu * (nb - cur)

    p = shift_mix(p_ref, pp_ref, pn_ref, mup_ref[...])
    codes = shift_mix(c_ref, cp_ref, cn_ref, muc_ref[...])
    r = p[:, 0:RW_DIM]
    k = p[:, RW_DIM:2 * RW_DIM]
    v = p[:, 2 * RW_DIM:3 * RW_DIM]
    cw = jnp.tanh(codes[:, 0:2 * W_LORA])
    ca = codes[:, 2 * W_LORA:2 * W_LORA + 2 * A_LORA]
    cg = _sigmoid(codes[:, 2 * W_LORA + 2 * A_LORA:])
    pv = pvec_ref[...]
    if has_vres:
        lo = _bdot(_bdot(v, v1_ref[...]), v2_ref[...])
        v = v + (vf_ref[...] - v) * _sigmoid(pv[6:7] + lo)
    r_o[...] = r
    v_o[...] = v
    g_o[...] = _bdot(cg, g2_ref[...])
    iclr = []
    for d, lw_o in ((0, lwf_o), (1, lwb_o)):
        w_log = -_softplus(-(pv[d:d + 1] + _bdot(cw, w2_ref[d]))) - 0.5
        lw_o[...] = -jnp.exp(w_log)
        iclr.append(_sigmoid(pv[2 + d:3 + d] + _bdot(ca, a2_ref[d])))
    kf_o[...] = k * (1.0 + (iclr[0] - 1.0) * pv[5:6])
    kb_o[...] = k * (1.0 + (iclr[1] - 1.0) * pv[5:6])
    ones2 = _pair_ones()
    kkr = k * pv[4:5]
    for cb in range(RW_DIM // LANES):
        sl = slice(cb * LANES, (cb + 1) * LANES)
        blk = kkr[:, sl]
        kkn = blk * lax.rsqrt(_dot_sel(blk * blk, ones2) + 1e-12)
        kk_o[:, sl] = kkn
        bf_o[:, sl] = kkn * iclr[0][:, sl]
        bb_o[:, sl] = kkn * iclr[1][:, sl]


def _rwkv_prep_call(rkv, codes, mu_p, mu_c, pvec, w2p, a2p, g2p, vres, seq):
    t = rkv.shape[0]
    tm = 128
    has_vres = vres is not None
    full = lambda shape: pl.BlockSpec(shape, lambda i: (0,) * len(shape))
    in_specs = (_halo_specs(tm, 3 * RW_DIM, t) + _halo_specs(tm, RW_CODES_PAD, t)
                + [full((1, 3 * RW_DIM)), full((1, RW_CODES_PAD)), full((8, RW_DIM)),
                   full((2, 2 * W_LORA, RW_DIM)), full((2, 2 * A_LORA, RW_DIM)), full((256, RW_DIM))])
    args = [rkv, rkv, rkv, codes, codes, codes, mu_p, mu_c, pvec, w2p, a2p, g2p]
    if has_vres:
        in_specs += [pl.BlockSpec((tm, RW_DIM), lambda i: (i, 0)), full((RW_DIM, LANES)), full((LANES, RW_DIM))]
        args += list(vres)
    out = jax.ShapeDtypeStruct((t, RW_DIM), F32)
    return pl.pallas_call(
        functools.partial(_rwkv_prep_kernel, tm=tm, seq=seq, has_vres=has_vres),
        grid=(t // tm,),
        in_specs=in_specs,
        out_specs=[pl.BlockSpec((tm, RW_DIM), lambda i: (i, 0))] * 10,
        out_shape=[out] * 10,
        compiler_params=_cparams("parallel"),
        name="rwkv_prep",
    )(*args)


def _rwkv_scan_kernel(r_ref, v_ref, kk_ref, k_ref, b_ref, lw_ref, y_ref, h_scr, *, reverse, n_chunks, n_pairs):
    c = RW_CHUNK

    @pl.when(pl.program_id(2) == 0)
    def _():
        h_scr[...] = jnp.zeros_like(h_scr)

    ri = lax.broadcasted_iota(jnp.int32, (LANES, LANES), 0)
    ci = lax.broadcasted_iota(jnp.int32, (LANES, LANES), 1)
    same = (ri // c) == (ci // c)
    tr, tc = ri % c, ci % c
    if reverse:
        strict, incl = same & (tc > tr), same & (tc >= tr)
    else:
        strict, incl = same & (tc < tr), same & (tc <= tr)
    blk = [(ri // w) == (ci // w) for w in (8, 16, 32, 64)]
    eye = jnp.where(ri == ci, 1.0, 0.0).astype(F32)
    r64 = lax.broadcasted_iota(jnp.int32, (c, c), 0)
    c64 = lax.broadcasted_iota(jnp.int32, (c, c), 1)
    tri = jnp.where((c64 >= r64) if reverse else (c64 <= r64), 1.0, 0.0).astype(BF16)
    head0 = lax.broadcasted_iota(jnp.int32, (c, LANES), 1) < RW_HEAD
    end_row = 0 if reverse else c - 1

    def stack(x):
        return jnp.concatenate([jnp.where(head0, x, 0.0), jnp.where(head0, 0.0, x)], axis=0)

    def chunk(step, carry):
        cidx = (n_chunks - 1 - step) if reverse else step
        rows = pl.ds(pl.multiple_of(cidx * c, c), c)
        prs = range(n_pairs)
        lns = [slice(pr * LANES, (pr + 1) * LANES) for pr in prs]
        lw_all = lw_ref[rows, :]
        cum_all = _sel_dot(tri, lw_all)
        tot_all = cum_all[end_row:end_row + 1]
        g_inc, g_exc, g_inv = jnp.exp(cum_all), jnp.exp(cum_all - lw_all), jnp.exp(-cum_all)
        g_end, g_tot = jnp.exp(tot_all - cum_all), jnp.exp(tot_all)
        kk_all, k_all, b_all = kk_ref[rows, :], k_ref[rows, :], b_ref[rows, :]
        at_all, rt_all = -kk_all * g_exc, r_ref[rows, :] * g_inc
        bt_all, kt_all = b_all * g_inv, k_all * g_inv
        bg_all, kg_all = b_all * g_end, k_all * g_end
        v_all = v_ref[rows, :]
        a_s = [stack(at_all[:, ln]).astype(BF16) for ln in lns]
        r_s = [stack(rt_all[:, ln]).astype(BF16) for ln in lns]
        b_s = [stack(bt_all[:, ln]).astype(BF16) for ln in lns]
        k_s = [stack(kt_all[:, ln]).astype(BF16) for ln in lns]
        v_s = [stack(v_all[:, ln]) for ln in lns]
        v_b = [x.astype(BF16) for x in v_s]
        ht = [h_scr[pr] for pr in prs]
        ht_b = [x.astype(BF16) for x in ht]
        a_ab = [jnp.where(strict, _dot_nt(a_s[p], b_s[p]), 0.0) for p in prs]
        a_ak = [jnp.where(strict, _dot_nt(a_s[p], k_s[p]), 0.0).astype(BF16) for p in prs]
        a_rb = [jnp.where(incl, _dot_nt(r_s[p], b_s[p]), 0.0).astype(BF16) for p in prs]
        a_rk = [jnp.where(incl, _dot_nt(r_s[p], k_s[p]), 0.0).astype(BF16) for p in prs]
        x0 = [(_dot_nt(a_s[p], ht_b[p]) + _dot(a_ak[p], v_b[p])).astype(BF16) for p in prs]
        y0 = [_dot_nt(r_s[p], ht_b[p]) + _dot(a_rk[p], v_b[p]) for p in prs]
        p1 = [jnp.where(blk[0], a_ab[p], 0.0).astype(BF16) for p in prs]
        p2 = [_dot(p1[p], p1[p]) for p in prs]
        p2_b = [x.astype(BF16) for x in p2]
        p4_b = [_dot(p2_b[p], p2_b[p]).astype(BF16) for p in prs]
        t_inv = [eye + p1[p] + p2[p] + _dot(p1[p], p2_b[p]) for p in prs]
        t_inv = [t_inv[p] + _dot(t_inv[p].astype(BF16), p4_b[p]) for p in prs]
        for lvl in range(1, 4):
            off = [jnp.where(blk[lvl] & ~blk[lvl - 1], a_ab[p], 0.0).astype(BF16) for p in prs]
            t_b = [x.astype(BF16) for x in t_inv]
            mid = [_dot(t_b[p], off[p]).astype(BF16) for p in prs]
            t_inv = [t_inv[p] + _dot(mid[p], t_b[p]) for p in prs]
        u = [_dot(t_inv[p].astype(BF16), x0[p]) for p in prs]
        u_b = [x.astype(BF16) for x in u]
        y = [y0[p] + _dot(a_rb[p], u_b[p]) for p in prs]
        y_ref[rows, :] = jnp.concatenate([y[p][0:c] + y[p][c:2 * c] for p in prs], axis=1)
        for p in prs:
            bg = stack(bg_all[:, lns[p]]).astype(BF16)
            kg = stack(kg_all[:, lns[p]]).astype(BF16)
            h_scr[p] = ht[p] * g_tot[:, lns[p]] + _dot(u[p].T.astype(BF16), bg) + _dot(v_s[p].T.astype(BF16), kg)
        return carry

    lax.fori_loop(0, n_chunks, chunk, 0)


def _rwkv_scan_call(r, v, kk, k, b, lw, bsz, seq, reverse):
    t = r.shape[0]
    tb = min(256, seq)
    nt = seq // tb
    n_pairs = RW_DIM // LANES
    width = n_pairs * LANES
    if reverse:
        imap = lambda bi, pi, ti: (bi * nt + nt - 1 - ti, pi)
    else:
        imap = lambda bi, pi, ti: (bi * nt + ti, pi)
    spec = pl.BlockSpec((tb, width), imap)
    return pl.pallas_call(
        functools.partial(_rwkv_scan_kernel, reverse=reverse, n_chunks=tb // RW_CHUNK, n_pairs=n_pairs),
        grid=(bsz, RW_DIM // width, nt),
        in_specs=[spec] * 6,
        out_specs=spec,
        out_shape=jax.ShapeDtypeStruct((t, RW_DIM), F32),
        scratch_shapes=[pltpu.VMEM((n_pairs, LANES, LANES), F32)],
        compiler_params=_cparams("parallel", "parallel", "arbitrary"),
        name="rwkv_scan_bwd" if reverse else "rwkv_scan_fwd",
    )(r, v, kk, k, b, lw)


def _rwkv_post_kernel(yf_ref, yb_ref, r_ref, kf_ref, kb_ref, v_ref, g_ref, pvec_ref, o_ref):
    mean2 = _pair_ones(1.0 / RW_HEAD)
    ones2 = _pair_ones()
    pv = pvec_ref[...]
    for cb in range(RW_DIM // LANES):
        sl = slice(cb * LANES, (cb + 1) * LANES)
        y = yf_ref[:, sl] + yb_ref[:, sl]
        dlt = y - _dot_sel(y, mean2)
        var = _dot_sel(dlt * dlt, mean2)
        yn = dlt * lax.rsqrt(var + RW_GN_EPS) * pv[0:1, sl] + pv[1:2, sl]
        rk = r_ref[:, sl] * (kf_ref[:, sl] + kb_ref[:, sl]) * pv[2:3, sl]
        bonus = _dot_sel(rk, ones2) * v_ref[:, sl]
        o_ref[:, sl] = ((yn + bonus) * g_ref[:, sl]).astype(o_ref.dtype)


def _rwkv_post_call(yf, yb, r, kf, kb, v, g, pvec):
    t = r.shape[0]
    tm = 256
    spec = pl.BlockSpec((tm, RW_DIM), lambda i: (i, 0))
    return pl.pallas_call(
        _rwkv_post_kernel,
        grid=(t // tm,),
        in_specs=[spec] * 7 + [pl.BlockSpec((8, RW_DIM), lambda i: (0, 0))],
        out_specs=spec,
        out_shape=jax.ShapeDtypeStruct((t, RW_DIM), BF16),
        compiler_params=_cparams("parallel"),
        name="rwkv_post",
    )(yf, yb, r, kf, kb, v, g, pvec)


def _mamba_conv_kernel(x_ref, xp_ref, xn_ref, w_ref, b_ref, xs_o, bm_o, cm_o, *, tm, seq):
    cur, prevz, nextz = _halo_values(x_ref, xp_ref, xn_ref, tm, seq)
    w = w_ref[...]
    half = (M_CONV - 1) // 2
    acc = cur * w[half:half + 1] + b_ref[...]
    for tap in range(M_CONV):
        if tap != half:
            acc = acc + _shift_rows(cur, prevz, nextz, tap - half) * w[tap:tap + 1]
    y = _silu(acc)
    xs_o[...] = y[:, 0:M_INNER]
    bm_o[...] = y[:, M_INNER:M_INNER + M_GROUPS * M_STATE]
    cm_o[...] = y[:, M_INNER + M_GROUPS * M_STATE:]


def _mamba_conv_call(xbc, conv_w, conv_b, seq):
    t = xbc.shape[0]
    tm = 256
    gs = M_GROUPS * M_STATE
    return pl.pallas_call(
        functools.partial(_mamba_conv_kernel, tm=tm, seq=seq),
        grid=(t // tm,),
        in_specs=_halo_specs(tm, M_XBC, t) + [pl.BlockSpec((8, M_XBC), lambda i: (0, 0)),
                                             pl.BlockSpec((1, M_XBC), lambda i: (0, 0))],
        out_specs=[pl.BlockSpec((tm, M_INNER), lambda i: (i, 0)),
                   pl.BlockSpec((tm, gs), lambda i: (i, 0)),
                   pl.BlockSpec((tm, gs), lambda i: (i, 0))],
        out_shape=[jax.ShapeDtypeStruct((t, M_INNER), F32),
                   jax.ShapeDtypeStruct((t, gs), F32),
                   jax.ShapeDtypeStruct((t, gs), F32)],
        compiler_params=_cparams("parallel"),
        name="mamba_conv",
    )(xbc, xbc, xbc, conv_w, conv_b)


def _ssd_kernel(xs_ref, bm_ref, cm_ref, dt_ref, dtb_ref, e_ref, alog_ref, y_ref, h_scr, *, reverse):
    c = M_CHUNK
    n_pairs = M_GROUP_W // LANES

    @pl.when(pl.program_id(2) == 0)
    def _():
        h_scr[...] = jnp.zeros_like(h_scr)

    ri = lax.broadcasted_iota(jnp.int32, (c, c), 0)
    ci = lax.broadcasted_iota(jnp.int32, (c, c), 1)
    keep = (ci >= ri) if reverse else (ci <= ri)
    tri = jnp.where(keep, 1.0, 0.0).astype(BF16)
    end_row = 0 if reverse else c - 1
    head0 = lax.broadcasted_iota(jnp.int32, (c, LANES), 1) < M_HEADDIM

    dt = _softplus(dt_ref[...] + dtb_ref[...])
    dte = _dot_sel(dt, e_ref[0, 0])
    da = dte * (-jnp.exp(alog_ref[0, 0]))
    acs = _sel_dot(tri, da)
    tot = acs[end_row:end_row + 1]
    xdt = xs_ref[...] * dte
    e_acs = jnp.exp(acs)
    x_end = (xdt * jnp.exp(tot - acs)).astype(BF16)
    xdt_b = xdt.astype(BF16)
    dec = jnp.exp(tot)
    bm = bm_ref[...].astype(BF16)
    cm = cm_ref[...].astype(BF16)
    cb = _dot_nt(cm, bm)
    bm_t = bm_ref[...].T.astype(BF16)
    for pr in range(n_pairs):
        ln = slice(pr * LANES, (pr + 1) * LANES)
        a_p = acs[:, ln]
        a_pt = a_p.T
        x_p = xdt_b[:, ln]
        ys = []
        for hh in range(2):
            col = a_p[:, hh * M_HEADDIM:hh * M_HEADDIM + 1]
            row = a_pt[hh * M_HEADDIM:hh * M_HEADDIM + 1, :]
            lmat = jnp.exp(jnp.where(keep, col - row, NEG_BIG))
            ys.append(_dot((cb * lmat).astype(BF16), x_p))
        h_prev = h_scr[pr]
        y_off = _dot(cm, h_prev.astype(BF16)) * e_acs[:, ln]
        y_ref[:, ln] = jnp.where(head0, ys[0], ys[1]) + y_off
        h_scr[pr] = h_prev * dec[:, ln] + _dot(bm_t, x_end[:, ln])


def _ssd_call(xs, bm, cm, dt_raw, dtb, emat, alog_e, bsz, seq, reverse):
    t = xs.shape[0]
    nc = seq // M_CHUNK
    d = 1 if reverse else 0
    if reverse:
        row = lambda bi, gi, ci: bi * nc + nc - 1 - ci
    else:
        row = lambda bi, gi, ci: bi * nc + ci
    return pl.pallas_call(
        functools.partial(_ssd_kernel, reverse=reverse),
        grid=(bsz, M_GROUPS, nc),
        in_specs=[pl.BlockSpec((M_CHUNK, M_GROUP_W), lambda bi, gi, ci: (row(bi, gi, ci), gi)),
                  pl.BlockSpec((M_CHUNK, M_STATE), lambda bi, gi, ci: (row(bi, gi, ci), gi)),
                  pl.BlockSpec((M_CHUNK, M_STATE), lambda bi, gi, ci: (row(bi, gi, ci), gi)),
                  pl.BlockSpec((M_CHUNK, LANES), lambda bi, gi, ci: (row(bi, gi, ci), 0)),
                  pl.BlockSpec((1, LANES), lambda bi, gi, ci: (0, 0)),
                  pl.BlockSpec((1, 1, LANES, M_GROUP_W), lambda bi, gi, ci: (d, gi, 0, 0)),
                  pl.BlockSpec((1, 1, 1, M_GROUP_W), lambda bi, gi, ci: (d, gi, 0, 0))],
        out_specs=pl.BlockSpec((M_CHUNK, M_GROUP_W), lambda bi, gi, ci: (row(bi, gi, ci), gi)),
        out_shape=jax.ShapeDtypeStruct((t, M_INNER), F32),
        scratch_shapes=[pltpu.VMEM((M_GROUP_W // LANES, M_STATE, LANES), F32)],
        compiler_params=_cparams("parallel", "parallel", "arbitrary"),
        name="ssd_bwd" if reverse else "ssd_fwd",
    )(xs, bm, cm, dt_raw, dtb, emat, alog_e)


def _mamba_post_kernel(yf_ref, yb_ref, xs_ref, z_ref, d_ref, nw_ref, o_ref):
    y = (yf_ref[...] + yb_ref[...] + d_ref[...] * xs_ref[...]) * _silu(z_ref[...])
    for g in range(M_GROUPS):
        sl = slice(g * M_GROUP_W, (g + 1) * M_GROUP_W)
        o_ref[:, sl] = _rms(y[:, sl], nw_ref[:, sl], 1e-5).astype(o_ref.dtype)


def _mamba_post_call(yf, yb, xs, z, d_e, nw):
    t = xs.shape[0]
    tm = 256
    spec = pl.BlockSpec((tm, M_INNER), lambda i: (i, 0))
    vec = pl.BlockSpec((1, M_INNER), lambda i: (0, 0))
    return pl.pallas_call(
        _mamba_post_kernel,
        grid=(t // tm,),
        in_specs=[spec] * 4 + [vec, vec],
        out_specs=spec,
        out_shape=jax.ShapeDtypeStruct((t, M_INNER), BF16),
        compiler_params=_cparams("parallel"),
        name="mamba_post",
    )(yf, yb, xs, z, d_e, nw)


def _diff_attn_kernel(q_ref, k_ref, v_ref, lam_ref, slope_ref, nw_ref, o_ref, *, tq, lambda_init):
    seq = k_ref.shape[0]
    q = q_ref[...]
    k = k_ref[...]
    v = v_ref[...]
    lv = lam_ref[...]
    lam = (jnp.exp(jnp.sum(lv[0:1] * lv[1:2], keepdims=True))
           - jnp.exp(jnp.sum(lv[2:3] * lv[3:4], keepdims=True)) + lambda_init)
    rows = pl.program_id(2) * tq + lax.broadcasted_iota(jnp.int32, (tq, seq), 0)
    cols = lax.broadcasted_iota(jnp.int32, (tq, seq), 1)
    bias = slope_ref[0][:, 0:1] * jnp.abs(rows - cols).astype(F32)
    map0 = lax.broadcasted_iota(jnp.int32, (tq, LANES), 1) < DF_HEAD
    scale = DF_HEAD ** -0.5

    def attend(qm):
        s = _dot_nt(qm, k) * scale - bias
        e = jnp.exp(s - jnp.max(s, axis=-1, keepdims=True))
        return _dot(e.astype(BF16), v) / jnp.sum(e, axis=-1, keepdims=True)

    o = attend(jnp.where(map0, q, jnp.zeros_like(q))) - lam * attend(jnp.where(map0, jnp.zeros_like(q), q))
    o_ref[...] = (_rms(o, nw_ref[...], 1e-5) * (1.0 - lambda_init)).astype(o_ref.dtype)


def _diff_attn_call(qkv, lam_vecs, slopes, subln_w, bsz, seq, lambda_init):
    t = qkv.shape[0]
    tq = min(256, seq)
    nq = seq // tq
    return pl.pallas_call(
        functools.partial(_diff_attn_kernel, tq=tq, lambda_init=lambda_init),
        grid=(bsz, DF_HEADS, nq),
        in_specs=[pl.BlockSpec((tq, DF_V), lambda bi, hi, qi: (bi * nq + qi, hi)),
                  pl.BlockSpec((seq, DF_V), lambda bi, hi, qi: (bi, DF_HEADS + hi)),
                  pl.BlockSpec((seq, DF_V), lambda bi, hi, qi: (bi, 2 * DF_HEADS + hi)),
                  pl.BlockSpec((4, DF_HEAD), lambda bi, hi, qi: (0, 0)),
                  pl.BlockSpec((1, 1, LANES), lambda bi, hi, qi: (hi, 0, 0)),
                  pl.BlockSpec((1, DF_V), lambda bi, hi, qi: (0, 0))],
        out_specs=pl.BlockSpec((tq, DF_V), lambda bi, hi, qi: (bi * nq + qi, hi)),
        out_shape=jax.ShapeDtypeStruct((t, DF_HEADS * DF_V), BF16),
        compiler_params=_cparams("parallel", "parallel", "arbitrary"),
        name="diff_attn",
    )(qkv, qkv, qkv, lam_vecs, slopes, subln_w)


def _merge_kernel(yr_ref, ym_ref, yd_ref, pg_ref, x_ref, mod_ref, nw_ref, wr_ref, wm_ref, wd_ref, wo_ref, o_ref):
    d = D_MODEL
    merged = (_sigmoid(pg_ref[:, 0:d]) * _dot(yr_ref[...], wr_ref[...])
              + _sigmoid(pg_ref[:, d:2 * d]) * _dot(ym_ref[...], wm_ref[...])
              + _sigmoid(pg_ref[:, 2 * d:3 * d]) * _dot(yd_ref[...], wd_ref[...]))
    y = _dot(merged.astype(BF16), wo_ref[...])
    o_ref[...] = x_ref[...] + mod_ref[0][2:3] * _rms(y, nw_ref[...], EPS)


def _merge_call(yr, ym, yd, pg, x2, mod_sub, nw, wr, wm, wd, wo, seq):
    t, d = x2.shape
    tm = 256
    row = lambda w: pl.BlockSpec((tm, w), lambda i: (i, 0))
    res = lambda a: pl.BlockSpec(a.shape, lambda i: (0, 0), pipeline_mode=pl.Buffered(1))
    return pl.pallas_call(
        _merge_kernel,
        grid=(t // tm,),
        in_specs=[row(RW_DIM), row(M_INNER), row(DF_HEADS * DF_V), row(GATE_COLS), row(d),
                  pl.BlockSpec((1, 3, d), lambda i: (i * tm // seq, 0, 0)),
                  pl.BlockSpec((1, d), lambda i: (0, 0)),
                  res(wr), res(wm), res(wd), res(wo)],
        out_specs=row(d),
        out_shape=jax.ShapeDtypeStruct((t, d), F32),
        compiler_params=_cparams("parallel"),
        name="merge_outproj",
    )(yr, ym, yd, pg, x2, mod_sub, nw, wr, wm, wd, wo)


def _pad_cols(w, n):
    return jnp.pad(w, ((0, 0), (0, n - w.shape[1])))


def _pad_rows(w, n):
    return jnp.pad(w, ((0, n - w.shape[0]), (0, 0)))


def _dir_padded(w):
    z = jnp.zeros_like(w[0])
    return jnp.stack([jnp.concatenate([w[0], z], axis=0), jnp.concatenate([z, w[1]], axis=0)])


def _head_expand_matrix():
    e = np.zeros((2, M_GROUPS, LANES, M_GROUP_W), np.float32)
    hpg = M_HEADS // M_GROUPS
    for d in range(2):
        for g in range(M_GROUPS):
            for h in range(hpg):
                e[d, g, d * M_HEADS + g * hpg + h, h * M_HEADDIM:(h + 1) * M_HEADDIM] = 1.0
    return jnp.asarray(e, dtype=BF16)


def kernel(x, c, ada_w, ada_b, norm_w, ffn_w13, ffn_w2, w_in, rwkv_mu, rwkv_w0, rwkv_w2, rwkv_a0, rwkv_a2, rwkv_g2, rwkv_k_k, rwkv_k_a, rwkv_r_k, rwkv_ln_w, rwkv_ln_b, rwkv_v0, rwkv_v1, rwkv_v2, mamba_conv_w, mamba_conv_b, mamba_dt_bias, mamba_a_log, mamba_d, mamba_norm_w, diff_lambda, diff_subln_w, w_branch_rwkv, w_branch_mamba, w_branch_diff, w_out):
    bsz, seq, d = x.shape
    depth = ada_w.shape[0]
    t = bsz * seq
    x2 = x.reshape(t, d)
    mod_all = _mod_call(c, ada_w, ada_b)
    emat = _head_expand_matrix()
    slopes = jnp.broadcast_to(
        jnp.asarray(2.0 ** (-8.0 * np.arange(1, DF_HEADS + 1) / DF_HEADS), F32)[:, None, None], (DF_HEADS, 1, LANES))
    hpg = M_HEADS // M_GROUPS
    v_first = None
    for l in range(depth):
        mod = mod_all[l].reshape(bsz, N_SUB, 3, d)
        x2 = _ffn_call(x2, mod[:, 0], norm_w[l, 0:2], ffn_w13[l, 0].astype(BF16), ffn_w2[l, 0].astype(BF16), seq)

        wl = w_in[l]
        o_m = RW_COLS
        o_d = o_m + M_COLS
        o_g = o_d + DF_COLS
        w_rkv = wl[:, 0:3 * RW_DIM].astype(BF16)
        w_codes = _pad_cols(wl[:, 3 * RW_DIM:RW_COLS], RW_CODES_PAD).astype(BF16)
        w_z = wl[:, o_m:o_m + M_INNER].astype(BF16)
        w_xbc = wl[:, o_m + M_INNER:o_m + M_INNER + M_XBC].astype(BF16)
        w_dt = _pad_cols(wl[:, o_m + M_INNER + M_XBC:o_d], LANES).astype(BF16)
        w_qkv = wl[:, o_d:o_g].astype(BF16)
        w_gate = wl[:, o_g:].astype(BF16)
        msub, nw = mod[:, 1], norm_w[l, 2:3]
        p_rkv = _inproj_call(x2, msub, nw, w_rkv, F32, seq, 512)
        p_codes = _inproj_call(x2, msub, nw, w_codes, F32, seq, RW_CODES_PAD)
        p_z = _inproj_call(x2, msub, nw, w_z, F32, seq, 512)
        p_xbc = _inproj_call(x2, msub, nw, w_xbc, F32, seq, 512)
        p_dt = _inproj_call(x2, msub, nw, w_dt, F32, seq, LANES)
        p_qkv = _inproj_call(x2, msub, nw, w_qkv, BF16, seq, 512)
        p_gate = _inproj_call(x2, msub, nw, w_gate, F32, seq, 512)

        mu = rwkv_mu[l]
        mu_p = mu[None, 0:3 * RW_DIM]
        mu_c = _pad_cols(mu[None, 3 * RW_DIM:], RW_CODES_PAD)
        v0 = rwkv_v0[l - 1] if l > 0 else jnp.zeros((RW_DIM,), F32)
        pvec = jnp.stack([rwkv_w0[l, 0], rwkv_w0[l, 1], rwkv_a0[l, 0], rwkv_a0[l, 1],
                          rwkv_k_k[l], rwkv_k_a[l], v0, jnp.zeros((RW_DIM,), F32)])
        g2p = _pad_rows(rwkv_g2[l], 256)
        vres = None
        if l > 0:
            vres = (v_first, _pad_cols(rwkv_v1[l - 1], LANES), _pad_rows(rwkv_v2[l - 1], LANES))
        r, v, kk, k_f, k_b, b_f, b_b, lw_f, lw_b, gate = _rwkv_prep_call(
            p_rkv, p_codes, mu_p, mu_c, pvec, _dir_padded(rwkv_w2[l]), _dir_padded(rwkv_a2[l]), g2p, vres, seq)
        if l == 0:
            v_first = v
        y_f = _rwkv_scan_call(r, v, kk, k_f, b_f, lw_f, bsz, seq, False)
        y_b = _rwkv_scan_call(r, v, kk, k_b, b_b, lw_b, bsz, seq, True)
        pvec2 = jnp.concatenate([jnp.stack([rwkv_ln_w[l], rwkv_ln_b[l], rwkv_r_k[l].reshape(RW_DIM)]),
                                 jnp.zeros((5, RW_DIM), F32)])
        y_r = _rwkv_post_call(y_f, y_b, r, k_f, k_b, v, gate, pvec2)

        conv_w = _pad_rows(mamba_conv_w[l], 8)
        xs, bm, cm = _mamba_conv_call(p_xbc, conv_w, mamba_conv_b[l][None], seq)
        dtb = _pad_cols(mamba_dt_bias[l].reshape(1, 2 * M_HEADS), LANES)
        alog_e = jnp.repeat(mamba_a_log[l].reshape(2, M_GROUPS, 1, hpg), M_HEADDIM, axis=-1)
        ym_f = _ssd_call(xs, bm, cm, p_dt, dtb, emat, alog_e, bsz, seq, False)
        ym_b = _ssd_call(xs, bm, cm, p_dt, dtb, emat, alog_e, bsz, seq, True)
        d_e = jnp.repeat(mamba_d[l], M_HEADDIM)[None]
        y_m = _mamba_post_call(ym_f, ym_b, xs, p_z, d_e, mamba_norm_w[l][None])

        lambda_init = 0.8 - 0.6 * math.exp(-0.3 * l)
        y_d = _diff_attn_call(p_qkv, diff_lambda[l], slopes, diff_subln_w[l][None], bsz, seq, lambda_init)

        x2 = _merge_call(y_r, y_m, y_d, p_gate, x2, mod[:, 1], norm_w[l, 3:4],
                         w_branch_rwkv[l].astype(BF16), w_branch_mamba[l].astype(BF16),
                         w_branch_diff[l].astype(BF16), w_out[l].astype(BF16), seq)

        x2 = _ffn_call(x2, mod[:, 2], norm_w[l, 4:6], ffn_w13[l, 1].astype(BF16), ffn_w2[l, 1].astype(BF16), seq)
    return x2.reshape(bsz, seq, d)
```

```python
import functools
import math

import numpy as np
import jax
import jax.numpy as jnp
from jax import lax
from jax.experimental import pallas as pl
from jax.experimental.pallas import tpu as pltpu

F32 = jnp.float32
BF16 = jnp.bfloat16
HI = lax.Precision.HIGHEST

D_MODEL = 1024
N_SUB = 3
EPS = 1e-6
LANES = 128
SUBLANES = 8
VMEM_LIMIT = 56 * 1024 * 1024

RW_HEAD = 64
RW_DIM = 1024
W_LORA = 64
A_LORA = 64
V_LORA = 32
G_LORA = 160
RW_GN_EPS = 64e-5
RW_COLS = 3 * RW_DIM + 2 * W_LORA + 2 * A_LORA + G_LORA
RW_CODES_PAD = 512
RW_CHUNK = 64
M_INNER = 2048
M_HEADS = 32
M_HEADDIM = 64
M_GROUPS = 4
M_STATE = 128
M_CONV = 5
M_CHUNK = 128
M_XBC = M_INNER + 2 * M_GROUPS * M_STATE
M_COLS = M_INNER + M_XBC + 2 * M_HEADS
M_GROUP_W = M_INNER // M_GROUPS
DF_HEADS = 8
DF_HEAD = 64
DF_V = 128
DF_COLS = 3 * DF_HEADS * 2 * DF_HEAD
GATE_COLS = 3 * D_MODEL
D_FF = 2816
NEG_BIG = -1e30


def _cparams(*sem):
    return pltpu.CompilerParams(dimension_semantics=sem, vmem_limit_bytes=VMEM_LIMIT)


def _dot(a, b, **kw):
    return jnp.dot(a, b, preferred_element_type=F32, **kw)


def _dot_nt(a, b, **kw):
    return lax.dot_general(a, b, (((1,), (1,)), ((), ())), preferred_element_type=F32, **kw)


def _split2(x):
    hi = x.astype(BF16)
    return hi, (x - hi.astype(F32)).astype(BF16)


def _dot_sel(x, m_b):
    hi, lo = _split2(x)
    return _dot(hi, m_b) + _dot(lo, m_b)


def _sel_dot(m_b, x):
    hi, lo = _split2(x)
    return _dot(m_b, hi) + _dot(m_b, lo)


def _bdot(a, b):
    return _dot(a.astype(BF16), b.astype(BF16))


def _sigmoid(x):
    return 1.0 / (1.0 + jnp.exp(-x))


def _silu(x):
    return x * _sigmoid(x)


def _softplus(x):
    return jnp.maximum(x, 0.0) + jnp.log(1.0 + jnp.exp(-jnp.abs(x)))


def _rms(x, w, eps):
    return x * lax.rsqrt(jnp.mean(x * x, axis=-1, keepdims=True) + eps) * w


def _norm_mod(x, nw, shift, scale):
    return _rms(x, nw, EPS) * (1.0 + scale) + shift


def _pair_ones(scale=1.0):
    r = lax.broadcasted_iota(jnp.int32, (LANES, LANES), 0) // RW_HEAD
    c = lax.broadcasted_iota(jnp.int32, (LANES, LANES), 1) // RW_HEAD
    return jnp.where(r == c, scale, 0.0).astype(BF16)


def _shift_rows(cur, prevz, nextz, d):
    tm = cur.shape[0]
    rolled = pltpu.roll(cur, (-d) % tm, axis=0)
    rid = lax.broadcasted_iota(jnp.int32, (SUBLANES, cur.shape[1]), 0)
    if d < 0:
        fix = pltpu.roll(prevz, (-d) % SUBLANES, axis=0)
        top = jnp.where(rid < -d, fix, rolled[0:SUBLANES])
        return jnp.concatenate([top, rolled[SUBLANES:]], axis=0)
    fix = pltpu.roll(nextz, (SUBLANES - d) % SUBLANES, axis=0)
    bot = jnp.where(rid >= SUBLANES - d, fix, rolled[tm - SUBLANES:])
    return jnp.concatenate([rolled[:tm - SUBLANES], bot], axis=0)


def _halo_specs(tm, width, n_rows):
    r8 = tm // SUBLANES
    last8 = n_rows // SUBLANES - 1
    return [
        pl.BlockSpec((tm, width), lambda i: (i, 0)),
        pl.BlockSpec((SUBLANES, width), lambda i: (jnp.maximum(i * r8 - 1, 0), 0)),
        pl.BlockSpec((SUBLANES, width), lambda i: (jnp.minimum((i + 1) * r8, last8), 0)),
    ]


def _halo_values(cur_ref, prev_ref, next_ref, tm, seq):
    i = pl.program_id(0)
    first = (i * tm) % seq == 0
    last = ((i + 1) * tm) % seq == 0
    prevz = jnp.where(first, 0.0, prev_ref[...])
    nextz = jnp.where(last, 0.0, next_ref[...])
    return cur_ref[...], prevz, nextz


def _mod_kernel(c_ref, w_ref, b_ref, o_ref):
    o_ref[0] = _dot(_silu(c_ref[...]), w_ref[0], precision=HI) + b_ref[0]


def _mod_call(c, ada_w, ada_b):
    depth, d, n = ada_w.shape
    bsz = c.shape[0]
    tn = 1152
    return pl.pallas_call(
        _mod_kernel,
        grid=(depth, n // tn),
        in_specs=[pl.BlockSpec((bsz, d), lambda l, j: (0, 0)),
                  pl.BlockSpec((1, d, tn), lambda l, j: (l, 0, j)),
                  pl.BlockSpec((1, 1, tn), lambda l, j: (l, 0, j))],
        out_specs=pl.BlockSpec((1, bsz, tn), lambda l, j: (l, 0, j)),
        out_shape=jax.ShapeDtypeStruct((depth, bsz, n), F32),
        compiler_params=_cparams("parallel", "parallel"),
        name="adaln_mod",
    )(c, ada_w, ada_b.reshape(depth, 1, n))


def _ffn_kernel(x_ref, mod_ref, nw_ref, wg_ref, wu_ref, w2_ref, o_ref, h_scr, acc_scr):
    j = pl.program_id(1)

    @pl.when(j == 0)
    def _():
        m = mod_ref[0]
        h_scr[...] = _norm_mod(x_ref[...], nw_ref[0:1], m[0:1], m[1:2]).astype(BF16)
        acc_scr[...] = jnp.zeros_like(acc_scr)

    h = h_scr[...]
    g = _dot(h, wg_ref[...])
    u = _dot(h, wu_ref[...])
    acc_scr[...] += _dot((_silu(g) * u).astype(BF16), w2_ref[...])

    @pl.when(j == pl.num_programs(1) - 1)
    def _():
        m = mod_ref[0]
        o_ref[...] = x_ref[...] + 0.5 * m[2:3] * _rms(acc_scr[...], nw_ref[1:2], EPS)


def _ffn_call(x2, mod_sub, nw2, w13, w2, seq):
    t, d = x2.shape
    dff = w2.shape[0]
    tm, tf = 1024, 256
    tm = min(tm, seq)
    nf = dff // tf
    return pl.pallas_call(
        _ffn_kernel,
        grid=(t // tm, nf),
        in_specs=[pl.BlockSpec((tm, d), lambda i, j: (i, 0)),
                  pl.BlockSpec((1, 3, d), lambda i, j: (i * tm // seq, 0, 0)),
                  pl.BlockSpec((2, d), lambda i, j: (0, 0)),
                  pl.BlockSpec((d, tf), lambda i, j: (0, j)),
                  pl.BlockSpec((d, tf), lambda i, j: (0, nf + j)),
                  pl.BlockSpec((tf, d), lambda i, j: (j, 0))],
        out_specs=pl.BlockSpec((tm, d), lambda i, j: (i, 0)),
        out_shape=jax.ShapeDtypeStruct((t, d), F32),
        scratch_shapes=[pltpu.VMEM((tm, d), BF16), pltpu.VMEM((tm, d), F32)],
        compiler_params=_cparams("parallel", "arbitrary"),
        name="swiglu_halfstep",
    )(x2, mod_sub, nw2, w13, w13, w2)


INPROJ_TN = 512


def _inproj_kernel(x_ref, mod_ref, nw_ref, w_ref, *rest, starts):
    o_refs, h_scr = rest[:-1], rest[-1]
    j = pl.program_id(1)

    @pl.when(j == 0)
    def _():
        m = mod_ref[0]
        h_scr[...] = _norm_mod(x_ref[...], nw_ref[...], m[0:1], m[1:2]).astype(BF16)

    for k, o_ref in enumerate(o_refs):
        @pl.when((j >= starts[k]) & (j < starts[k + 1]))
        def _(o_ref=o_ref):
            o_ref[...] = _dot(h_scr[...], w_ref[...]).astype(o_ref.dtype)


def _inproj_call(x2, mod_sub, nw, w, widths, dtypes, seq):
    t, d = x2.shape
    tn = INPROJ_TN
    tm = min(1024, seq)
    starts = [0]
    for wd in widths:
        starts.append(starts[-1] + wd // tn)

    def out_spec(k):
        return pl.BlockSpec((tm, tn), lambda i, j: (i, jnp.clip(j - starts[k], 0, widths[k] // tn - 1)))

    return pl.pallas_call(
        functools.partial(_inproj_kernel, starts=tuple(starts)),
        grid=(t // tm, starts[-1]),
        in_specs=[pl.BlockSpec((tm, d), lambda i, j: (i, 0)),
                  pl.BlockSpec((1, 3, d), lambda i, j: (i * tm // seq, 0, 0)),
                  pl.BlockSpec((1, d), lambda i, j: (0, 0)),
                  pl.BlockSpec((d, tn), lambda i, j: (0, j))],
        out_specs=[out_spec(k) for k in range(len(widths))],
        out_shape=[jax.ShapeDtypeStruct((t, wd), dt) for wd, dt in zip(widths, dtypes)],
        scratch_shapes=[pltpu.VMEM((tm, d), BF16)],
        compiler_params=_cparams("parallel", "arbitrary"),
        name="norm_inproj",
    )(x2, mod_sub, nw, w)


def _rwkv_prep_kernel(*refs, tm, seq, has_vres):
    (p_ref, pp_ref, pn_ref, c_ref, cp_ref, cn_ref, mup_ref, muc_ref, pvec_ref,
     w2_ref, a2_ref, g2_ref) = refs[:12]
    refs = refs[12:]
    if has_vres:
        vf_ref, v1_ref, v2_ref = refs[:3]
        refs = refs[3:]
    r_o, v_o, kk_o, kf_o, kb_o, bf_o, bb_o, lwf_o, lwb_o, g_o = refs

    def shift_mix(cur_ref, prev_ref, next_ref, mu):
        cur, prevz, nextz = _halo_values(cur_ref, prev_ref, next_ref, tm, seq)
        nb = 0.5 * (_shift_rows(cur, prevz, nextz, -1) + _shift_rows(cur, prevz, nextz, 1))
        return cur + mu * (nb - cur)

    p = shift_mix(p_ref, pp_ref, pn_ref, mup_ref[...])
    codes = shift_mix(c_ref, cp_ref, cn_ref, muc_ref[...])
    r = p[:, 0:RW_DIM]
    k = p[:, RW_DIM:2 * RW_DIM]
    v = p[:, 2 * RW_DIM:3 * RW_DIM]
    cw = jnp.tanh(codes[:, 0:2 * W_LORA])
    ca = codes[:, 2 * W_LORA:2 * W_LORA + 2 * A_LORA]
    cg = _sigmoid(codes[:, 2 * W_LORA + 2 * A_LORA:])
    pv = pvec_ref[...]
    if has_vres:
        lo = _bdot(_bdot(v, v1_ref[...]), v2_ref[...])
        v = v + (vf_ref[...] - v) * _sigmoid(pv[6:7] + lo)
    r_o[...] = r
    v_o[...] = v
    g_o[...] = _bdot(cg, g2_ref[...])
    iclr = []
    for d, lw_o in ((0, lwf_o), (1, lwb_o)):
        w_log = -_softplus(-(pv[d:d + 1] + _bdot(cw, w2_ref[d]))) - 0.5
        lw_o[...] = -jnp.exp(w_log)
        iclr.append(_sigmoid(pv[2 + d:3 + d] + _bdot(ca, a2_ref[d])))
    kf_o[...] = k * (1.0 + (iclr[0] - 1.0) * pv[5:6])
    kb_o[...] = k * (1.0 + (iclr[1] - 1.0) * pv[5:6])
    ones2 = _pair_ones()
    kkr = k * pv[4:5]
    for cb in range(RW_DIM // LANES):
        sl = slice(cb * LANES, (cb + 1) * LANES)
        blk = kkr[:, sl]
        kkn = blk * lax.rsqrt(_dot_sel(blk * blk, ones2) + 1e-12)
        kk_o[:, sl] = kkn
        bf_o[:, sl] = kkn * iclr[0][:, sl]
        bb_o[:, sl] = kkn * iclr[1][:, sl]


def _rwkv_prep_call(rkv, codes, mu_p, mu_c, pvec, w2p, a2p, g2p, vres, seq):
    t = rkv.shape[0]
    tm = 128
    has_vres = vres is not None
    full = lambda shape: pl.BlockSpec(shape, lambda i: (0,) * len(shape))
    in_specs = (_halo_specs(tm, 3 * RW_DIM, t) + _halo_specs(tm, RW_CODES_PAD, t)
                + [full((1, 3 * RW_DIM)), full((1, RW_CODES_PAD)), full((8, RW_DIM)),
                   full((2, 2 * W_LORA, RW_DIM)), full((2, 2 * A_LORA, RW_DIM)), full((256, RW_DIM))])
    args = [rkv, rkv, rkv, codes, codes, codes, mu_p, mu_c, pvec, w2p, a2p, g2p]
    if has_vres:
        in_specs += [pl.BlockSpec((tm, RW_DIM), lambda i: (i, 0)), full((RW_DIM, LANES)), full((LANES, RW_DIM))]
        args += list(vres)
    out = jax.ShapeDtypeStruct((t, RW_DIM), F32)
    return pl.pallas_call(
        functools.partial(_rwkv_prep_kernel, tm=tm, seq=seq, has_vres=has_vres),
        grid=(t // tm,),
        in_specs=in_specs,
        out_specs=[pl.BlockSpec((tm, RW_DIM), lambda i: (i, 0))] * 10,
        out_shape=[out] * 10,
        compiler_params=_cparams("parallel"),
        name="rwkv_prep",
    )(*args)


def _rwkv_scan_kernel(r_ref, v_ref, kk_ref, k_ref, b_ref, lw_ref, y_ref, h_scr, *, reverse, n_chunks, n_pairs):
    c = RW_CHUNK

    @pl.when(pl.program_id(2) == 0)
    def _():
        h_scr[...] = jnp.zeros_like(h_scr)

    ri = lax.broadcasted_iota(jnp.int32, (LANES, LANES), 0)
    ci = lax.broadcasted_iota(jnp.int32, (LANES, LANES), 1)
    same = (ri // c) == (ci // c)
    tr, tc = ri % c, ci % c
    if reverse:
        strict, incl = same & (tc > tr), same & (tc >= tr)
    else:
        strict, incl = same & (tc < tr), same & (tc <= tr)
    blk = [(ri // w) == (ci // w) for w in (8, 16, 32, 64)]
    eye = jnp.where(ri == ci, 1.0, 0.0).astype(F32)
    r64 = lax.broadcasted_iota(jnp.int32, (c, c), 0)
    c64 = lax.broadcasted_iota(jnp.int32, (c, c), 1)
    tri = jnp.where((c64 >= r64) if reverse else (c64 <= r64), 1.0, 0.0).astype(BF16)
    head0 = lax.broadcasted_iota(jnp.int32, (c, LANES), 1) < RW_HEAD
    end_row = 0 if reverse else c - 1

    def stack(x):
        return jnp.concatenate([jnp.where(head0, x, 0.0), jnp.where(head0, 0.0, x)], axis=0)

    def chunk(step, carry):
        cidx = (n_chunks - 1 - step) if reverse else step
        rows = pl.ds(pl.multiple_of(cidx * c, c), c)
        prs = range(n_pairs)
        lns = [slice(pr * LANES, (pr + 1) * LANES) for pr in prs]
        lw_all = lw_ref[rows, :]
        cum_all = _sel_dot(tri, lw_all)
        tot_all = cum_all[end_row:end_row + 1]
        g_inc, g_exc, g_inv = jnp.exp(cum_all), jnp.exp(cum_all - lw_all), jnp.exp(-cum_all)
        g_end, g_tot = jnp.exp(tot_all - cum_all), jnp.exp(tot_all)
        kk_all, k_all, b_all = kk_ref[rows, :], k_ref[rows, :], b_ref[rows, :]
        at_all, rt_all = -kk_all * g_exc, r_ref[rows, :] * g_inc
        bt_all, kt_all = b_all * g_inv, k_all * g_inv
        bg_all, kg_all = b_all * g_end, k_all * g_end
        v_all = v_ref[rows, :]
        a_s = [stack(at_all[:, ln]).astype(BF16) for ln in lns]
        r_s = [stack(rt_all[:, ln]).astype(BF16) for ln in lns]
        b_s = [stack(bt_all[:, ln]).astype(BF16) for ln in lns]
        k_s = [stack(kt_all[:, ln]).astype(BF16) for ln in lns]
        v_s = [stack(v_all[:, ln]) for ln in lns]
        v_b = [x.astype(BF16) for x in v_s]
        ht = [h_scr[pr] for pr in prs]
        ht_b = [x.astype(BF16) for x in ht]
        a_ab = [jnp.where(strict, _dot_nt(a_s[p], b_s[p]), 0.0) for p in prs]
        a_ak = [jnp.where(strict, _dot_nt(a_s[p], k_s[p]), 0.0).astype(BF16) for p in prs]
        a_rb = [jnp.where(incl, _dot_nt(r_s[p], b_s[p]), 0.0).astype(BF16) for p in prs]
        a_rk = [jnp.where(incl, _dot_nt(r_s[p], k_s[p]), 0.0).astype(BF16) for p in prs]
        x0 = [(_dot_nt(a_s[p], ht_b[p]) + _dot(a_ak[p], v_b[p])).astype(BF16) for p in prs]
        y0 = [_dot_nt(r_s[p], ht_b[p]) + _dot(a_rk[p], v_b[p]) for p in prs]
        p1 = [jnp.where(blk[0], a_ab[p], 0.0).astype(BF16) for p in prs]
        p2 = [_dot(p1[p], p1[p]) for p in prs]
        p2_b = [x.astype(BF16) for x in p2]
        p4_b = [_dot(p2_b[p], p2_b[p]).astype(BF16) for p in prs]
        t_inv = [eye + p1[p] + p2[p] + _dot(p1[p], p2_b[p]) for p in prs]
        t_inv = [t_inv[p] + _dot(t_inv[p].astype(BF16), p4_b[p]) for p in prs]
        for lvl in range(1, 4):
            off = [jnp.where(blk[lvl] & ~blk[lvl - 1], a_ab[p], 0.0).astype(BF16) for p in prs]
            t_b = [x.astype(BF16) for x in t_inv]
            mid = [_dot(t_b[p], off[p]).astype(BF16) for p in prs]
            t_inv = [t_inv[p] + _dot(mid[p], t_b[p]) for p in prs]
        u = [_dot(t_inv[p].astype(BF16), x0[p]) for p in prs]
        u_b = [x.astype(BF16) for x in u]
        y = [y0[p] + _dot(a_rb[p], u_b[p]) for p in prs]
        y_ref[rows, :] = jnp.concatenate([y[p][0:c] + y[p][c:2 * c] for p in prs], axis=1)
        for p in prs:
            bg = stack(bg_all[:, lns[p]]).astype(BF16)
            kg = stack(kg_all[:, lns[p]]).astype(BF16)
            h_scr[p] = ht[p] * g_tot[:, lns[p]] + _dot(u[p].T.astype(BF16), bg) + _dot(v_s[p].T.astype(BF16), kg)
        return carry

    lax.fori_loop(0, n_chunks, chunk, 0)


def _rwkv_scan_call(r, v, kk, k, b, lw, bsz, seq, reverse):
    t = r.shape[0]
    tb = min(256, seq)
    nt = seq // tb
    n_pairs = RW_DIM // LANES
    width = n_pairs * LANES
    if reverse:
        imap = lambda bi, pi, ti: (bi * nt + nt - 1 - ti, pi)
    else:
        imap = lambda bi, pi, ti: (bi * nt + ti, pi)
    spec = pl.BlockSpec((tb, width), imap)
    return pl.pallas_call(
        functools.partial(_rwkv_scan_kernel, reverse=reverse, n_chunks=tb // RW_CHUNK, n_pairs=n_pairs),
        grid=(bsz, RW_DIM // width, nt),
        in_specs=[spec] * 6,
        out_specs=spec,
        out_shape=jax.ShapeDtypeStruct((t, RW_DIM), F32),
        scratch_shapes=[pltpu.VMEM((n_pairs, LANES, LANES), F32)],
        compiler_params=_cparams("parallel", "parallel", "arbitrary"),
        name="rwkv_scan_bwd" if reverse else "rwkv_scan_fwd",
    )(r, v, kk, k, b, lw)


def _rwkv_post_kernel(yf_ref, yb_ref, r_ref, kf_ref, kb_ref, v_ref, g_ref, pvec_ref, o_ref):
    mean2 = _pair_ones(1.0 / RW_HEAD)
    ones2 = _pair_ones()
    pv = pvec_ref[...]
    for cb in range(RW_DIM // LANES):
        sl = slice(cb * LANES, (cb + 1) * LANES)
        y = yf_ref[:, sl] + yb_ref[:, sl]
        dlt = y - _dot_sel(y, mean2)
        var = _dot_sel(dlt * dlt, mean2)
        yn = dlt * lax.rsqrt(var + RW_GN_EPS) * pv[0:1, sl] + pv[1:2, sl]
        rk = r_ref[:, sl] * (kf_ref[:, sl] + kb_ref[:, sl]) * pv[2:3, sl]
        bonus = _dot_sel(rk, ones2) * v_ref[:, sl]
        o_ref[:, sl] = ((yn + bonus) * g_ref[:, sl]).astype(o_ref.dtype)


def _rwkv_post_call(yf, yb, r, kf, kb, v, g, pvec):
    t = r.shape[0]
    tm = 256
    spec = pl.BlockSpec((tm, RW_DIM), lambda i: (i, 0))
    return pl.pallas_call(
        _rwkv_post_kernel,
        grid=(t // tm,),
        in_specs=[spec] * 7 + [pl.BlockSpec((8, RW_DIM), lambda i: (0, 0))],
        out_specs=spec,
        out_shape=jax.ShapeDtypeStruct((t, RW_DIM), BF16),
        compiler_params=_cparams("parallel"),
        name="rwkv_post",
    )(yf, yb, r, kf, kb, v, g, pvec)


def _mamba_conv_kernel(x_ref, xp_ref, xn_ref, w_ref, b_ref, xs_o, bm_o, cm_o, *, tm, seq):
    cur, prevz, nextz = _halo_values(x_ref, xp_ref, xn_ref, tm, seq)
    w = w_ref[...]
    half = (M_CONV - 1) // 2
    acc = cur * w[half:half + 1] + b_ref[...]
    for tap in range(M_CONV):
        if tap != half:
            acc = acc + _shift_rows(cur, prevz, nextz, tap - half) * w[tap:tap + 1]
    y = _silu(acc)
    xs_o[...] = y[:, 0:M_INNER]
    bm_o[...] = y[:, M_INNER:M_INNER + M_GROUPS * M_STATE]
    cm_o[...] = y[:, M_INNER + M_GROUPS * M_STATE:]


def _mamba_conv_call(xbc, conv_w, conv_b, seq):
    t = xbc.shape[0]
    tm = 256
    gs = M_GROUPS * M_STATE
    return pl.pallas_call(
        functools.partial(_mamba_conv_kernel, tm=tm, seq=seq),
        grid=(t // tm,),
        in_specs=_halo_specs(tm, M_XBC, t) + [pl.BlockSpec((8, M_XBC), lambda i: (0, 0)),
                                             pl.BlockSpec((1, M_XBC), lambda i: (0, 0))],
        out_specs=[pl.BlockSpec((tm, M_INNER), lambda i: (i, 0)),
                   pl.BlockSpec((tm, gs), lambda i: (i, 0)),
                   pl.BlockSpec((tm, gs), lambda i: (i, 0))],
        out_shape=[jax.ShapeDtypeStruct((t, M_INNER), F32),
                   jax.ShapeDtypeStruct((t, gs), F32),
                   jax.ShapeDtypeStruct((t, gs), F32)],
        compiler_params=_cparams("parallel"),
        name="mamba_conv",
    )(xbc, xbc, xbc, conv_w, conv_b)


def _ssd_kernel(xs_ref, bm_ref, cm_ref, dt_ref, dtb_ref, e_ref, aloge_ref, alogc_ref, y_ref, h_scr, *, reverse):
    c = M_CHUNK
    d = 1 if reverse else 0
    n_pairs = M_INNER // LANES
    ppg = n_pairs // M_GROUPS

    @pl.when(pl.program_id(1) == 0)
    def _():
        h_scr[...] = jnp.zeros_like(h_scr)

    ri = lax.broadcasted_iota(jnp.int32, (c, c), 0)
    ci = lax.broadcasted_iota(jnp.int32, (c, c), 1)
    keep = (ci >= ri) if reverse else (ci <= ri)
    tri = jnp.where(keep, 1.0, 0.0).astype(BF16)
    end_row = 0 if reverse else c - 1
    head0 = lax.broadcasted_iota(jnp.int32, (c, LANES), 1) < M_HEADDIM

    dt = _softplus(dt_ref[...] + dtb_ref[...])
    acs_c = _sel_dot(tri, dt * (-jnp.exp(alogc_ref[...])))
    acs_ct = acs_c.T
    dte = _dot_sel(dt, e_ref[0])
    acs = _sel_dot(tri, dte * (-jnp.exp(aloge_ref[0])))
    tot = acs[end_row:end_row + 1]
    xdt = xs_ref[...] * dte
    e_acs = jnp.exp(acs)
    x_end = (xdt * jnp.exp(tot - acs)).astype(BF16)
    xdt_b = xdt.astype(BF16)
    dec = jnp.exp(tot)
    gsl = [slice(g * M_STATE, (g + 1) * M_STATE) for g in range(M_GROUPS)]
    bm_b = [bm_ref[:, sl].astype(BF16) for sl in gsl]
    cm_b = [cm_ref[:, sl].astype(BF16) for sl in gsl]
    bm_t = [bm_ref[:, sl].T.astype(BF16) for sl in gsl]
    cb = [_dot_nt(cm_b[g], bm_b[g]) for g in range(M_GROUPS)]
    prs = range(n_pairs)
    lns = [slice(p * LANES, (p + 1) * LANES) for p in prs]
    ys = []
    for p in prs:
        pair = []
        for hh in range(2):
            idx = d * M_HEADS + 2 * p + hh
            seg = acs_c[:, idx:idx + 1] - acs_ct[idx:idx + 1, :]
            lmat = jnp.exp(jnp.where(keep, seg, NEG_BIG))
            pair.append(_dot((cb[p // ppg] * lmat).astype(BF16), xdt_b[:, lns[p]]))
        ys.append(jnp.where(head0, pair[0], pair[1]))
    h_prev = [h_scr[p] for p in prs]
    y_off = [_dot(cm_b[p // ppg], h_prev[p].astype(BF16)) for p in prs]
    y_ref[...] = jnp.concatenate(ys, axis=1) + jnp.concatenate(y_off, axis=1) * e_acs
    for p in prs:
        h_scr[p] = h_prev[p] * dec[:, lns[p]] + _dot(bm_t[p // ppg], x_end[:, lns[p]])


def _ssd_call(xs, bm, cm, dt_raw, dtb, emat, alog_e, alog_c, bsz, seq, reverse):
    t = xs.shape[0]
    nc = seq // M_CHUNK
    d = 1 if reverse else 0
    gs = M_GROUPS * M_STATE
    if reverse:
        row = lambda bi, ci: (bi * nc + nc - 1 - ci, 0)
    else:
        row = lambda bi, ci: (bi * nc + ci, 0)
    return pl.pallas_call(
        functools.partial(_ssd_kernel, reverse=reverse),
        grid=(bsz, nc),
        in_specs=[pl.BlockSpec((M_CHUNK, M_INNER), row),
                  pl.BlockSpec((M_CHUNK, gs), row),
                  pl.BlockSpec((M_CHUNK, gs), row),
                  pl.BlockSpec((M_CHUNK, LANES), row),
                  pl.BlockSpec((1, LANES), lambda bi, ci: (0, 0)),
                  pl.BlockSpec((1, LANES, M_INNER), lambda bi, ci: (d, 0, 0)),
                  pl.BlockSpec((1, 1, M_INNER), lambda bi, ci: (d, 0, 0)),
                  pl.BlockSpec((1, LANES), lambda bi, ci: (0, 0))],
        out_specs=pl.BlockSpec((M_CHUNK, M_INNER), row),
        out_shape=jax.ShapeDtypeStruct((t, M_INNER), F32),
        scratch_shapes=[pltpu.VMEM((M_INNER // LANES, M_STATE, LANES), F32)],
        compiler_params=_cparams("parallel", "arbitrary"),
        name="ssd_bwd" if reverse else "ssd_fwd",
    )(xs, bm, cm, dt_raw, dtb, emat, alog_e, alog_c)


def _mamba_post_kernel(yf_ref, yb_ref, xs_ref, z_ref, d_ref, nw_ref, o_ref):
    y = (yf_ref[...] + yb_ref[...] + d_ref[...] * xs_ref[...]) * _silu(z_ref[...])
    for g in range(M_GROUPS):
        sl = slice(g * M_GROUP_W, (g + 1) * M_GROUP_W)
        o_ref[:, sl] = _rms(y[:, sl], nw_ref[:, sl], 1e-5).astype(o_ref.dtype)


def _mamba_post_call(yf, yb, xs, z, d_e, nw):
    t = xs.shape[0]
    tm = 256
    spec = pl.BlockSpec((tm, M_INNER), lambda i: (i, 0))
    vec = pl.BlockSpec((1, M_INNER), lambda i: (0, 0))
    return pl.pallas_call(
        _mamba_post_kernel,
        grid=(t // tm,),
        in_specs=[spec] * 4 + [vec, vec],
        out_specs=spec,
        out_shape=jax.ShapeDtypeStruct((t, M_INNER), BF16),
        compiler_params=_cparams("parallel"),
        name="mamba_post",
    )(yf, yb, xs, z, d_e, nw)


def _diff_attn_kernel(q_ref, k_ref, v_ref, lam_ref, slope_ref, nw_ref, o_ref, bias_scr, *, tq, lambda_init):
    seq = k_ref.shape[0]
    log2e = math.log2(math.e)

    @pl.when(pl.program_id(2) == 0)
    def _():
        rows = pl.program_id(1) * tq + lax.broadcasted_iota(jnp.int32, (tq, seq), 0)
        cols = lax.broadcasted_iota(jnp.int32, (tq, seq), 1)
        bias_scr[...] = (slope_ref[0][:, 0:1] * log2e) * jnp.abs(rows - cols).astype(F32)

    q = (q_ref[...].astype(F32) * (DF_HEAD ** -0.5 * log2e)).astype(BF16)
    k = k_ref[...]
    v = v_ref[...]
    lv = lam_ref[...]
    lam = (jnp.exp(jnp.sum(lv[0:1] * lv[1:2], keepdims=True))
           - jnp.exp(jnp.sum(lv[2:3] * lv[3:4], keepdims=True)) + lambda_init)
    map0 = lax.broadcasted_iota(jnp.int32, (tq, LANES), 1) < DF_HEAD

    def attend(qm):
        s = _dot_nt(qm, k) - bias_scr[...]
        e = jnp.exp2(s - jnp.max(s, axis=-1, keepdims=True))
        return _dot(e.astype(BF16), v) / jnp.sum(e, axis=-1, keepdims=True)

    o = attend(jnp.where(map0, q, jnp.zeros_like(q))) - lam * attend(jnp.where(map0, jnp.zeros_like(q), q))
    o_ref[...] = (_rms(o, nw_ref[...], 1e-5) * (1.0 - lambda_init)).astype(o_ref.dtype)


def _diff_attn_call(qkv, lam_vecs, slopes, subln_w, bsz, seq, lambda_init):
    t = qkv.shape[0]
    tq = min(256, seq)
    nq = seq // tq
    return pl.pallas_call(
        functools.partial(_diff_attn_kernel, tq=tq, lambda_init=lambda_init),
        grid=(DF_HEADS, nq, bsz),
        in_specs=[pl.BlockSpec((tq, DF_V), lambda hi, qi, bi: (bi * nq + qi, hi)),
                  pl.BlockSpec((seq, DF_V), lambda hi, qi, bi: (bi, DF_HEADS + hi)),
                  pl.BlockSpec((seq, DF_V), lambda hi, qi, bi: (bi, 2 * DF_HEADS + hi)),
                  pl.BlockSpec((4, DF_HEAD), lambda hi, qi, bi: (0, 0)),
                  pl.BlockSpec((1, 1, LANES), lambda hi, qi, bi: (hi, 0, 0)),
                  pl.BlockSpec((1, DF_V), lambda hi, qi, bi: (0, 0))],
        out_specs=pl.BlockSpec((tq, DF_V), lambda hi, qi, bi: (bi * nq + qi, hi)),
        out_shape=jax.ShapeDtypeStruct((t, DF_HEADS * DF_V), BF16),
        scratch_shapes=[pltpu.VMEM((tq, seq), F32)],
        compiler_params=_cparams("parallel", "parallel", "arbitrary"),
        name="diff_attn",
    )(qkv, qkv, qkv, lam_vecs, slopes, subln_w)


def _merge_kernel(yr_ref, ym_ref, yd_ref, pg_ref, x_ref, mod_ref, nw_ref, wr_ref, wm_ref, wd_ref, wo_ref, o_ref):
    d = D_MODEL
    merged = (_sigmoid(pg_ref[:, 0:d]) * _dot(yr_ref[...], wr_ref[...])
              + _sigmoid(pg_ref[:, d:2 * d]) * _dot(ym_ref[...], wm_ref[...])
              + _sigmoid(pg_ref[:, 2 * d:3 * d]) * _dot(yd_ref[...], wd_ref[...]))
    y = _dot(merged.astype(BF16), wo_ref[...])
    o_ref[...] = x_ref[...] + mod_ref[0][2:3] * _rms(y, nw_ref[...], EPS)


def _merge_call(yr, ym, yd, pg, x2, mod_sub, nw, wr, wm, wd, wo, seq):
    t, d = x2.shape
    tm = 256
    row = lambda w: pl.BlockSpec((tm, w), lambda i: (i, 0))
    res = lambda a: pl.BlockSpec(a.shape, lambda i: (0, 0), pipeline_mode=pl.Buffered(1))
    return pl.pallas_call(
        _merge_kernel,
        grid=(t // tm,),
        in_specs=[row(RW_DIM), row(M_INNER), row(DF_HEADS * DF_V), row(GATE_COLS), row(d),
                  pl.BlockSpec((1, 3, d), lambda i: (i * tm // seq, 0, 0)),
                  pl.BlockSpec((1, d), lambda i: (0, 0)),
                  res(wr), res(wm), res(wd), res(wo)],
        out_specs=row(d),
        out_shape=jax.ShapeDtypeStruct((t, d), F32),
        compiler_params=_cparams("parallel"),
        name="merge_outproj",
    )(yr, ym, yd, pg, x2, mod_sub, nw, wr, wm, wd, wo)


def _pad_cols(w, n):
    return jnp.pad(w, ((0, 0), (0, n - w.shape[1])))


def _pad_rows(w, n):
    return jnp.pad(w, ((0, n - w.shape[0]), (0, 0)))


def _dir_padded(w):
    z = jnp.zeros_like(w[0])
    return jnp.stack([jnp.concatenate([w[0], z], axis=0), jnp.concatenate([z, w[1]], axis=0)])


def _head_expand_matrix():
    e = np.zeros((2, LANES, M_INNER), np.float32)
    for d in range(2):
        for h in range(M_HEADS):
            e[d, d * M_HEADS + h, h * M_HEADDIM:(h + 1) * M_HEADDIM] = 1.0
    return jnp.asarray(e, dtype=BF16)


def kernel(x, c, ada_w, ada_b, norm_w, ffn_w13, ffn_w2, w_in, rwkv_mu, rwkv_w0, rwkv_w2, rwkv_a0, rwkv_a2, rwkv_g2, rwkv_k_k, rwkv_k_a, rwkv_r_k, rwkv_ln_w, rwkv_ln_b, rwkv_v0, rwkv_v1, rwkv_v2, mamba_conv_w, mamba_conv_b, mamba_dt_bias, mamba_a_log, mamba_d, mamba_norm_w, diff_lambda, diff_subln_w, w_branch_rwkv, w_branch_mamba, w_branch_diff, w_out):
    bsz, seq, d = x.shape
    depth = ada_w.shape[0]
    t = bsz * seq
    x2 = x.reshape(t, d)
    mod_all = _mod_call(c, ada_w, ada_b)
    emat = _head_expand_matrix()
    slopes = jnp.broadcast_to(
        jnp.asarray(2.0 ** (-8.0 * np.arange(1, DF_HEADS + 1) / DF_HEADS), F32)[:, None, None], (DF_HEADS, 1, LANES))
    v_first = None
    for l in range(depth):
        mod = mod_all[l].reshape(bsz, N_SUB, 3, d)
        x2 = _ffn_call(x2, mod[:, 0], norm_w[l, 0:2], ffn_w13[l, 0].astype(BF16), ffn_w2[l, 0].astype(BF16), seq)

        wl = w_in[l]
        o_m = RW_COLS
        o_d = o_m + M_COLS
        o_g = o_d + DF_COLS
        w_cat = jnp.concatenate([
            wl[:, 0:3 * RW_DIM],
            _pad_cols(wl[:, 3 * RW_DIM:RW_COLS], RW_CODES_PAD),
            wl[:, o_m:o_m + M_INNER],
            wl[:, o_m + M_INNER:o_m + M_INNER + M_XBC],
            _pad_cols(wl[:, o_m + M_INNER + M_XBC:o_d], INPROJ_TN),
            wl[:, o_d:o_g],
            wl[:, o_g:],
        ], axis=1).astype(BF16)
        p_rkv, p_codes, p_z, p_xbc, p_dt, p_qkv, p_gate = _inproj_call(
            x2, mod[:, 1], norm_w[l, 2:3], w_cat,
            (3 * RW_DIM, RW_CODES_PAD, M_INNER, M_XBC, INPROJ_TN, DF_COLS, GATE_COLS),
            (F32, F32, F32, F32, F32, BF16, F32), seq)

        mu = rwkv_mu[l]
        mu_p = mu[None, 0:3 * RW_DIM]
        mu_c = _pad_cols(mu[None, 3 * RW_DIM:], RW_CODES_PAD)
        v0 = rwkv_v0[l - 1] if l > 0 else jnp.zeros((RW_DIM,), F32)
        pvec = jnp.stack([rwkv_w0[l, 0], rwkv_w0[l, 1], rwkv_a0[l, 0], rwkv_a0[l, 1],
                          rwkv_k_k[l], rwkv_k_a[l], v0, jnp.zeros((RW_DIM,), F32)])
        g2p = _pad_rows(rwkv_g2[l], 256)
        vres = None
        if l > 0:
            vres = (v_first, _pad_cols(rwkv_v1[l - 1], LANES), _pad_rows(rwkv_v2[l - 1], LANES))
        r, v, kk, k_f, k_b, b_f, b_b, lw_f, lw_b, gate = _rwkv_prep_call(
            p_rkv, p_codes, mu_p, mu_c, pvec, _dir_padded(rwkv_w2[l]), _dir_padded(rwkv_a2[l]), g2p, vres, seq)
        if l == 0:
            v_first = v
        y_f = _rwkv_scan_call(r, v, kk, k_f, b_f, lw_f, bsz, seq, False)
        y_b = _rwkv_scan_call(r, v, kk, k_b, b_b, lw_b, bsz, seq, True)
        pvec2 = jnp.concatenate([jnp.stack([rwkv_ln_w[l], rwkv_ln_b[l], rwkv_r_k[l].reshape(RW_DIM)]),
                                 jnp.zeros((5, RW_DIM), F32)])
        y_r = _rwkv_post_call(y_f, y_b, r, k_f, k_b, v, gate, pvec2)

        conv_w = _pad_rows(mamba_conv_w[l], 8)
        xs, bm, cm = _mamba_conv_call(p_xbc, conv_w, mamba_conv_b[l][None], seq)
        dtb = _pad_cols(mamba_dt_bias[l].reshape(1, 2 * M_HEADS), LANES)
        alog_e = jnp.repeat(mamba_a_log[l], M_HEADDIM, axis=-1)[:, None]
        alog_c = _pad_cols(mamba_a_log[l].reshape(1, 2 * M_HEADS), LANES)
        ym_f = _ssd_call(xs, bm, cm, p_dt, dtb, emat, alog_e, alog_c, bsz, seq, False)
        ym_b = _ssd_call(xs, bm, cm, p_dt, dtb, emat, alog_e, alog_c, bsz, seq, True)
        d_e = jnp.repeat(mamba_d[l], M_HEADDIM)[None]
        y_m = _mamba_post_call(ym_f, ym_b, xs, p_z, d_e, mamba_norm_w[l][None])

        lambda_init = 0.8 - 0.6 * math.exp(-0.3 * l)
        y_d = _diff_attn_call(p_qkv, diff_lambda[l], slopes, diff_subln_w[l][None], bsz, seq, lambda_init)

        x2 = _merge_call(y_r, y_m, y_d, p_gate, x2, mod[:, 1], norm_w[l, 3:4],
                         w_branch_rwkv[l].astype(BF16), w_branch_mamba[l].astype(BF16),
                         w_branch_diff[l].astype(BF16), w_out[l].astype(BF16), seq)

        x2 = _ffn_call(x2, mod[:, 2], norm_w[l, 4:6], ffn_w13[l, 1].astype(BF16), ffn_w2[l, 1].astype(BF16), seq)
    return x2.reshape(bsz, seq, d)
```

```python
import functools
import math

import numpy as np
import jax
import jax.numpy as jnp
from jax import lax
from jax.experimental import pallas as pl
from jax.experimental.pallas import tpu as pltpu

F32 = jnp.float32
BF16 = jnp.bfloat16
HI = lax.Precision.HIGHEST

D_MODEL = 1024
N_SUB = 3
EPS = 1e-6
LANES = 128
SUBLANES = 8
VMEM_LIMIT = 56 * 1024 * 1024

RW_HEAD = 64
RW_DIM = 1024
W_LORA = 64
A_LORA = 64
V_LORA = 32
G_LORA = 160
RW_GN_EPS = 64e-5
RW_COLS = 3 * RW_DIM + 2 * W_LORA + 2 * A_LORA + G_LORA
RW_CODES_PAD = 512
RW_CHUNK = 64
M_INNER = 2048
M_HEADS = 32
M_HEADDIM = 64
M_GROUPS = 4
M_STATE = 128
M_CONV = 5
M_CHUNK = 128
M_XBC = M_INNER + 2 * M_GROUPS * M_STATE
M_COLS = M_INNER + M_XBC + 2 * M_HEADS
M_GROUP_W = M_INNER // M_GROUPS
DF_HEADS = 8
DF_HEAD = 64
DF_V = 128
DF_COLS = 3 * DF_HEADS * 2 * DF_HEAD
GATE_COLS = 3 * D_MODEL
D_FF = 2816
NEG_BIG = -1e30


def _cparams(*sem):
    return pltpu.CompilerParams(dimension_semantics=sem, vmem_limit_bytes=VMEM_LIMIT)


def _dot(a, b, **kw):
    return jnp.dot(a, b, preferred_element_type=F32, **kw)


def _dot_nt(a, b, **kw):
    return lax.dot_general(a, b, (((1,), (1,)), ((), ())), preferred_element_type=F32, **kw)


def _split2(x):
    hi = x.astype(BF16)
    return hi, (x - hi.astype(F32)).astype(BF16)


def _dot_sel(x, m_b):
    hi, lo = _split2(x)
    return _dot(hi, m_b) + _dot(lo, m_b)


def _sel_dot(m_b, x):
    hi, lo = _split2(x)
    return _dot(m_b, hi) + _dot(m_b, lo)


def _bdot(a, b):
    return _dot(a.astype(BF16), b.astype(BF16))


def _sigmoid(x):
    return 1.0 / (1.0 + jnp.exp(-x))


def _silu(x):
    return x * _sigmoid(x)


def _softplus(x):
    return jnp.maximum(x, 0.0) + jnp.log(1.0 + jnp.exp(-jnp.abs(x)))


def _rms(x, w, eps):
    return x * lax.rsqrt(jnp.mean(x * x, axis=-1, keepdims=True) + eps) * w


def _norm_mod(x, nw, shift, scale):
    return _rms(x, nw, EPS) * (1.0 + scale) + shift


def _pair_ones(scale=1.0):
    r = lax.broadcasted_iota(jnp.int32, (LANES, LANES), 0) // RW_HEAD
    c = lax.broadcasted_iota(jnp.int32, (LANES, LANES), 1) // RW_HEAD
    return jnp.where(r == c, scale, 0.0).astype(BF16)


def _shift_rows(cur, prevz, nextz, d):
    tm = cur.shape[0]
    rolled = pltpu.roll(cur, (-d) % tm, axis=0)
    rid = lax.broadcasted_iota(jnp.int32, (SUBLANES, cur.shape[1]), 0)
    if d < 0:
        fix = pltpu.roll(prevz, (-d) % SUBLANES, axis=0)
        top = jnp.where(rid < -d, fix, rolled[0:SUBLANES])
        return jnp.concatenate([top, rolled[SUBLANES:]], axis=0)
    fix = pltpu.roll(nextz, (SUBLANES - d) % SUBLANES, axis=0)
    bot = jnp.where(rid >= SUBLANES - d, fix, rolled[tm - SUBLANES:])
    return jnp.concatenate([rolled[:tm - SUBLANES], bot], axis=0)


def _halo_specs(tm, width, n_rows):
    r8 = tm // SUBLANES
    last8 = n_rows // SUBLANES - 1
    return [
        pl.BlockSpec((tm, width), lambda i: (i, 0)),
        pl.BlockSpec((SUBLANES, width), lambda i: (jnp.maximum(i * r8 - 1, 0), 0)),
        pl.BlockSpec((SUBLANES, width), lambda i: (jnp.minimum((i + 1) * r8, last8), 0)),
    ]


def _halo_values(cur_ref, prev_ref, next_ref, tm, seq):
    i = pl.program_id(0)
    first = (i * tm) % seq == 0
    last = ((i + 1) * tm) % seq == 0
    prevz = jnp.where(first, 0.0, prev_ref[...])
    nextz = jnp.where(last, 0.0, next_ref[...])
    return cur_ref[...], prevz, nextz


def _mod_kernel(c_ref, w_ref, b_ref, o_ref):
    o_ref[0] = _dot(_silu(c_ref[...]), w_ref[0], precision=HI) + b_ref[0]


def _mod_call(c, ada_w, ada_b):
    depth, d, n = ada_w.shape
    bsz = c.shape[0]
    tn = 1152
    return pl.pallas_call(
        _mod_kernel,
        grid=(depth, n // tn),
        in_specs=[pl.BlockSpec((bsz, d), lambda l, j: (0, 0)),
                  pl.BlockSpec((1, d, tn), lambda l, j: (l, 0, j)),
                  pl.BlockSpec((1, 1, tn), lambda l, j: (l, 0, j))],
        out_specs=pl.BlockSpec((1, bsz, tn), lambda l, j: (l, 0, j)),
        out_shape=jax.ShapeDtypeStruct((depth, bsz, n), F32),
        compiler_params=_cparams("parallel", "parallel"),
        name="adaln_mod",
    )(c, ada_w, ada_b.reshape(depth, 1, n))


def _ffn_kernel(x_ref, mod_ref, nw_ref, wg_ref, wu_ref, w2_ref, o_ref, h_scr, acc_scr):
    j = pl.program_id(1)

    @pl.when(j == 0)
    def _():
        m = mod_ref[0]
        h_scr[...] = _norm_mod(x_ref[...], nw_ref[0:1], m[0:1], m[1:2]).astype(BF16)
        acc_scr[...] = jnp.zeros_like(acc_scr)

    h = h_scr[...]
    g = _dot(h, wg_ref[...])
    u = _dot(h, wu_ref[...])
    acc_scr[...] += _dot((_silu(g) * u).astype(BF16), w2_ref[...])

    @pl.when(j == pl.num_programs(1) - 1)
    def _():
        m = mod_ref[0]
        o_ref[...] = x_ref[...] + 0.5 * m[2:3] * _rms(acc_scr[...], nw_ref[1:2], EPS)


def _ffn_call(x2, mod_sub, nw2, w13, w2, seq):
    t, d = x2.shape
    dff = w2.shape[0]
    tm, tf = 1024, 256
    tm = min(tm, seq)
    nf = dff // tf
    return pl.pallas_call(
        _ffn_kernel,
        grid=(t // tm, nf),
        in_specs=[pl.BlockSpec((tm, d), lambda i, j: (i, 0)),
                  pl.BlockSpec((1, 3, d), lambda i, j: (i * tm // seq, 0, 0)),
                  pl.BlockSpec((2, d), lambda i, j: (0, 0)),
                  pl.BlockSpec((d, tf), lambda i, j: (0, j)),
                  pl.BlockSpec((d, tf), lambda i, j: (0, nf + j)),
                  pl.BlockSpec((tf, d), lambda i, j: (j, 0))],
        out_specs=pl.BlockSpec((tm, d), lambda i, j: (i, 0)),
        out_shape=jax.ShapeDtypeStruct((t, d), F32),
        scratch_shapes=[pltpu.VMEM((tm, d), BF16), pltpu.VMEM((tm, d), F32)],
        compiler_params=_cparams("parallel", "arbitrary"),
        name="swiglu_halfstep",
    )(x2, mod_sub, nw2, w13, w13, w2)


INPROJ_TN = 512


def _inproj_kernel(x_ref, mod_ref, nw_ref, w_ref, *rest, starts):
    o_refs, h_scr = rest[:-1], rest[-1]
    j = pl.program_id(1)

    @pl.when(j == 0)
    def _():
        m = mod_ref[0]
        h_scr[...] = _norm_mod(x_ref[...], nw_ref[...], m[0:1], m[1:2]).astype(BF16)

    for k, o_ref in enumerate(o_refs):
        @pl.when((j >= starts[k]) & (j < starts[k + 1]))
        def _(o_ref=o_ref):
            o_ref[...] = _dot(h_scr[...], w_ref[...]).astype(o_ref.dtype)


def _inproj_call(x2, mod_sub, nw, w, widths, dtypes, seq):
    t, d = x2.shape
    tn = INPROJ_TN
    tm = min(1024, seq)
    starts = [0]
    for wd in widths:
        starts.append(starts[-1] + wd // tn)

    def out_spec(k):
        return pl.BlockSpec((tm, tn), lambda i, j: (i, jnp.clip(j - starts[k], 0, widths[k] // tn - 1)))

    return pl.pallas_call(
        functools.partial(_inproj_kernel, starts=tuple(starts)),
        grid=(t // tm, starts[-1]),
        in_specs=[pl.BlockSpec((tm, d), lambda i, j: (i, 0)),
                  pl.BlockSpec((1, 3, d), lambda i, j: (i * tm // seq, 0, 0)),
                  pl.BlockSpec((1, d), lambda i, j: (0, 0)),
                  pl.BlockSpec((d, tn), lambda i, j: (0, j))],
        out_specs=[out_spec(k) for k in range(len(widths))],
        out_shape=[jax.ShapeDtypeStruct((t, wd), dt) for wd, dt in zip(widths, dtypes)],
        scratch_shapes=[pltpu.VMEM((tm, d), BF16)],
        compiler_params=_cparams("parallel", "arbitrary"),
        name="norm_inproj",
    )(x2, mod_sub, nw, w)


def _rwkv_prep_kernel(*refs, tm, seq, has_vres):
    (p_ref, pp_ref, pn_ref, c_ref, cp_ref, cn_ref, mup_ref, muc_ref, pvec_ref,
     w2_ref, a2_ref, g2_ref) = refs[:12]
    refs = refs[12:]
    if has_vres:
        vf_ref, v1_ref, v2_ref = refs[:3]
        refs = refs[3:]
    r_o, v_o, kk_o, kf_o, kb_o, bf_o, bb_o, lwf_o, lwb_o, g_o = refs

    def shift_mix(cur_ref, prev_ref, next_ref, mu):
        cur, prevz, nextz = _halo_values(cur_ref, prev_ref, next_ref, tm, seq)
        nb = 0.5 * (_shift_rows(cur, prevz, nextz, -1) + _shift_rows(cur, prevz, nextz, 1))
        return cur + mu * (nb - cur)

    p = shift_mix(p_ref, pp_ref, pn_ref, mup_ref[...])
    codes = shift_mix(c_ref, cp_ref, cn_ref, muc_ref[...])
    r = p[:, 0:RW_DIM]
    k = p[:, RW_DIM:2 * RW_DIM]
    v = p[:, 2 * RW_DIM:3 * RW_DIM]
    cw = jnp.tanh(codes[:, 0:2 * W_LORA])
    ca = codes[:, 2 * W_LORA:2 * W_LORA + 2 * A_LORA]
    cg = _sigmoid(codes[:, 2 * W_LORA + 2 * A_LORA:])
    pv = pvec_ref[...]
    if has_vres:
        lo = _bdot(_bdot(v, v1_ref[...]), v2_ref[...])
        v = v + (vf_ref[...] - v) * _sigmoid(pv[6:7] + lo)
    r_o[...] = r
    v_o[...] = v
    g_o[...] = _bdot(cg, g2_ref[...])
    iclr = []
    for d, lw_o in ((0, lwf_o), (1, lwb_o)):
        w_log = -_softplus(-(pv[d:d + 1] + _bdot(cw, w2_ref[d]))) - 0.5
        lw_o[...] = -jnp.exp(w_log)
        iclr.append(_sigmoid(pv[2 + d:3 + d] + _bdot(ca, a2_ref[d])))
    kf_o[...] = k * (1.0 + (iclr[0] - 1.0) * pv[5:6])
    kb_o[...] = k * (1.0 + (iclr[1] - 1.0) * pv[5:6])
    ones2 = _pair_ones()
    kkr = k * pv[4:5]
    for cb in range(RW_DIM // LANES):
        sl = slice(cb * LANES, (cb + 1) * LANES)
        blk = kkr[:, sl]
        kkn = blk * lax.rsqrt(_dot_sel(blk * blk, ones2) + 1e-12)
        kk_o[:, sl] = kkn
        bf_o[:, sl] = kkn * iclr[0][:, sl]
        bb_o[:, sl] = kkn * iclr[1][:, sl]


def _rwkv_prep_call(rkv, codes, mu_p, mu_c, pvec, w2p, a2p, g2p, vres, seq):
    t = rkv.shape[0]
    tm = 128
    has_vres = vres is not None
    full = lambda shape: pl.BlockSpec(shape, lambda i: (0,) * len(shape))
    in_specs = (_halo_specs(tm, 3 * RW_DIM, t) + _halo_specs(tm, RW_CODES_PAD, t)
                + [full((1, 3 * RW_DIM)), full((1, RW_CODES_PAD)), full((8, RW_DIM)),
                   full((2, 2 * W_LORA, RW_DIM)), full((2, 2 * A_LORA, RW_DIM)), full((256, RW_DIM))])
    args = [rkv, rkv, rkv, codes, codes, codes, mu_p, mu_c, pvec, w2p, a2p, g2p]
    if has_vres:
        in_specs += [pl.BlockSpec((tm, RW_DIM), lambda i: (i, 0)), full((RW_DIM, LANES)), full((LANES, RW_DIM))]
        args += list(vres)
    out = jax.ShapeDtypeStruct((t, RW_DIM), F32)
    return pl.pallas_call(
        functools.partial(_rwkv_prep_kernel, tm=tm, seq=seq, has_vres=has_vres),
        grid=(t // tm,),
        in_specs=in_specs,
        out_specs=[pl.BlockSpec((tm, RW_DIM), lambda i: (i, 0))] * 10,
        out_shape=[out] * 10,
        compiler_params=_cparams("parallel"),
        name="rwkv_prep",
    )(*args)


def _rwkv_scan_kernel(rf_ref, vf_ref, kkf_ref, kf_ref, bf_ref, lwf_ref, rb_ref, vb_ref, kkb_ref, kb_ref, bb_ref,
                      lwb_ref, yf_ref, yb_ref, h_scr, *, n_chunks):
    c = RW_CHUNK
    n_pairs = RW_DIM // LANES

    @pl.when(pl.program_id(1) == 0)
    def _():
        h_scr[...] = jnp.zeros_like(h_scr)

    ri = lax.broadcasted_iota(jnp.int32, (LANES, LANES), 0)
    ci = lax.broadcasted_iota(jnp.int32, (LANES, LANES), 1)
    same = (ri // c) == (ci // c)
    tr, tc = ri % c, ci % c
    strict = [same & (tc < tr), same & (tc > tr)]
    incl = [same & (tc <= tr), same & (tc >= tr)]
    blk = [(ri // w) == (ci // w) for w in (8, 16, 32, 64)]
    eye = jnp.where(ri == ci, 1.0, 0.0).astype(F32)
    r64 = lax.broadcasted_iota(jnp.int32, (c, c), 0)
    c64 = lax.broadcasted_iota(jnp.int32, (c, c), 1)
    tri = [jnp.where(c64 <= r64, 1.0, 0.0).astype(BF16), jnp.where(c64 >= r64, 1.0, 0.0).astype(BF16)]
    head0 = lax.broadcasted_iota(jnp.int32, (c, LANES), 1) < RW_HEAD
    end_row = [c - 1, 0]
    refs = [(rf_ref, vf_ref, kkf_ref, kf_ref, bf_ref, lwf_ref, yf_ref),
            (rb_ref, vb_ref, kkb_ref, kb_ref, bb_ref, lwb_ref, yb_ref)]
    lns = [slice(pr * LANES, (pr + 1) * LANES) for pr in range(n_pairs)]
    chains = [(d, p) for p in range(n_pairs) for d in range(2)]

    def stack(x):
        return jnp.concatenate([jnp.where(head0, x, 0.0), jnp.where(head0, 0.0, x)], axis=0)

    def chunk(step, carry):
        rows = [pl.ds(pl.multiple_of(step * c, c), c), pl.ds(pl.multiple_of((n_chunks - 1 - step) * c, c), c)]
        wide = []
        for d in range(2):
            r_ref, v_ref, kk_ref, k_ref, b_ref, lw_ref, _ = refs[d]
            lw = lw_ref[rows[d], :]
            cum = _sel_dot(tri[d], lw)
            tot = cum[end_row[d]:end_row[d] + 1]
            g_inv, g_end = jnp.exp(-cum), jnp.exp(tot - cum)
            k_all, b_all = k_ref[rows[d], :], b_ref[rows[d], :]
            wide.append(dict(at=-kk_ref[rows[d], :] * jnp.exp(cum - lw), rt=r_ref[rows[d], :] * jnp.exp(cum),
                             bt=b_all * g_inv, kt=k_all * g_inv, bg=b_all * g_end, kg=k_all * g_end,
                             v=v_ref[rows[d], :], g_tot=jnp.exp(tot)))
        ar_s = [jnp.concatenate([stack(wide[d]["at"][:, lns[p]]), stack(wide[d]["rt"][:, lns[p]])], axis=0)
                .astype(BF16) for d, p in chains]
        bk_s = [jnp.concatenate([stack(wide[d]["bt"][:, lns[p]]), stack(wide[d]["kt"][:, lns[p]])], axis=0)
                .astype(BF16) for d, p in chains]
        v_s = [stack(wide[d]["v"][:, lns[p]]) for d, p in chains]
        ht = [h_scr[d, p] for d, p in chains]
        idx = range(len(chains))
        gram = [_dot_nt(ar_s[i], bk_s[i]) for i in idx]
        a_ab = [jnp.where(strict[d], gram[i][0:LANES, 0:LANES], 0.0) for i, (d, p) in enumerate(chains)]
        lhs = [jnp.concatenate([
            ar_s[i],
            jnp.concatenate([jnp.where(strict[d], gram[i][0:LANES, LANES:], 0.0),
                             jnp.where(incl[d], gram[i][LANES:, LANES:], 0.0)], axis=0).astype(BF16)], axis=1)
            for i, (d, p) in enumerate(chains)]
        rhs = [jnp.concatenate([ht[i].T, v_s[i]], axis=0).astype(BF16) for i in idx]
        xy0 = [_dot(lhs[i], rhs[i]) for i in idx]
        a_rb = [jnp.where(incl[d], gram[i][LANES:, 0:LANES], 0.0).astype(BF16) for i, (d, p) in enumerate(chains)]
        p1 = [jnp.where(blk[0], a_ab[i], 0.0).astype(BF16) for i in idx]
        p2 = [_dot(p1[i], p1[i]) for i in idx]
        p2_b = [x.astype(BF16) for x in p2]
        p4_b = [_dot(p2_b[i], p2_b[i]).astype(BF16) for i in idx]
        t_inv = [eye + p1[i] + p2[i] + _dot(p1[i], p2_b[i]) for i in idx]
        t_inv = [t_inv[i] + _dot(t_inv[i].astype(BF16), p4_b[i]) for i in idx]
        for lvl in range(1, 4):
            off = [jnp.where(blk[lvl] & ~blk[lvl - 1], a_ab[i], 0.0).astype(BF16) for i in idx]
            t_b = [x.astype(BF16) for x in t_inv]
            mid = [_dot(t_b[i], off[i]).astype(BF16) for i in idx]
            t_inv = [t_inv[i] + _dot(mid[i], t_b[i]) for i in idx]
        u = [_dot(t_inv[i].astype(BF16), xy0[i][0:LANES].astype(BF16)) for i in idx]
        y = [xy0[i][LANES:] + _dot(a_rb[i], u[i].astype(BF16)) for i in idx]
        for d in range(2):
            ys = [y[i][0:c] + y[i][c:2 * c] for i, (dd, p) in enumerate(chains) if dd == d]
            refs[d][6][rows[d], :] = jnp.concatenate(ys, axis=1)
        for i, (d, p) in enumerate(chains):
            uv_t = jnp.concatenate([u[i], v_s[i]], axis=0).T.astype(BF16)
            bkg = jnp.concatenate([stack(wide[d]["bg"][:, lns[p]]), stack(wide[d]["kg"][:, lns[p]])],
                                  axis=0).astype(BF16)
            h_scr[d, p] = ht[i] * wide[d]["g_tot"][:, lns[p]] + _dot(uv_t, bkg)
        return carry

    lax.fori_loop(0, n_chunks, chunk, 0)


def _rwkv_scan_call(r, v, kk, k_f, b_f, lw_f, k_b, b_b, lw_b, bsz, seq):
    t = r.shape[0]
    tb = min(2 * RW_CHUNK, seq)
    nt = seq // tb
    fwd = pl.BlockSpec((tb, RW_DIM), lambda bi, ti: (bi * nt + ti, 0))
    bwd = pl.BlockSpec((tb, RW_DIM), lambda bi, ti: (bi * nt + nt - 1 - ti, 0))
    out = jax.ShapeDtypeStruct((t, RW_DIM), F32)
    return pl.pallas_call(
        functools.partial(_rwkv_scan_kernel, n_chunks=tb // RW_CHUNK),
        grid=(bsz, nt),
        in_specs=[fwd] * 6 + [bwd] * 6,
        out_specs=[fwd, bwd],
        out_shape=[out, out],
        scratch_shapes=[pltpu.VMEM((2, RW_DIM // LANES, LANES, LANES), F32)],
        compiler_params=_cparams("parallel", "arbitrary"),
        name="rwkv_scan",
    )(r, v, kk, k_f, b_f, lw_f, r, v, kk, k_b, b_b, lw_b)


def _rwkv_post_kernel(yf_ref, yb_ref, r_ref, kf_ref, kb_ref, v_ref, g_ref, pvec_ref, o_ref):
    mean2 = _pair_ones(1.0 / RW_HEAD)
    ones2 = _pair_ones()
    pv = pvec_ref[...]
    for cb in range(RW_DIM // LANES):
        sl = slice(cb * LANES, (cb + 1) * LANES)
        y = yf_ref[:, sl] + yb_ref[:, sl]
        dlt = y - _dot_sel(y, mean2)
        var = _dot_sel(dlt * dlt, mean2)
        yn = dlt * lax.rsqrt(var + RW_GN_EPS) * pv[0:1, sl] + pv[1:2, sl]
        rk = r_ref[:, sl] * (kf_ref[:, sl] + kb_ref[:, sl]) * pv[2:3, sl]
        bonus = _dot_sel(rk, ones2) * v_ref[:, sl]
        o_ref[:, sl] = ((yn + bonus) * g_ref[:, sl]).astype(o_ref.dtype)


def _rwkv_post_call(yf, yb, r, kf, kb, v, g, pvec):
    t = r.shape[0]
    tm = 256
    spec = pl.BlockSpec((tm, RW_DIM), lambda i: (i, 0))
    return pl.pallas_call(
        _rwkv_post_kernel,
        grid=(t // tm,),
        in_specs=[spec] * 7 + [pl.BlockSpec((8, RW_DIM), lambda i: (0, 0))],
        out_specs=spec,
        out_shape=jax.ShapeDtypeStruct((t, RW_DIM), BF16),
        compiler_params=_cparams("parallel"),
        name="rwkv_post",
    )(yf, yb, r, kf, kb, v, g, pvec)


def _mamba_conv_kernel(x_ref, xp_ref, xn_ref, w_ref, b_ref, xs_o, bm_o, cm_o, *, tm, seq):
    cur, prevz, nextz = _halo_values(x_ref, xp_ref, xn_ref, tm, seq)
    w = w_ref[...]
    half = (M_CONV - 1) // 2
    acc = cur * w[half:half + 1] + b_ref[...]
    for tap in range(M_CONV):
        if tap != half:
            acc = acc + _shift_rows(cur, prevz, nextz, tap - half) * w[tap:tap + 1]
    y = _silu(acc)
    xs_o[...] = y[:, 0:M_INNER]
    bm_o[...] = y[:, M_INNER:M_INNER + M_GROUPS * M_STATE]
    cm_o[...] = y[:, M_INNER + M_GROUPS * M_STATE:]


def _mamba_conv_call(xbc, conv_w, conv_b, seq):
    t = xbc.shape[0]
    tm = 256
    gs = M_GROUPS * M_STATE
    return pl.pallas_call(
        functools.partial(_mamba_conv_kernel, tm=tm, seq=seq),
        grid=(t // tm,),
        in_specs=_halo_specs(tm, M_XBC, t) + [pl.BlockSpec((8, M_XBC), lambda i: (0, 0)),
                                             pl.BlockSpec((1, M_XBC), lambda i: (0, 0))],
        out_specs=[pl.BlockSpec((tm, M_INNER), lambda i: (i, 0)),
                   pl.BlockSpec((tm, gs), lambda i: (i, 0)),
                   pl.BlockSpec((tm, gs), lambda i: (i, 0))],
        out_shape=[jax.ShapeDtypeStruct((t, M_INNER), F32),
                   jax.ShapeDtypeStruct((t, gs), F32),
                   jax.ShapeDtypeStruct((t, gs), F32)],
        compiler_params=_cparams("parallel"),
        name="mamba_conv",
    )(xbc, xbc, xbc, conv_w, conv_b)


def _ssd_kernel(xs_ref, bm_ref, cm_ref, dt_ref, dtb_ref, e_ref, aloge_ref, alogc_ref, y_ref, h_scr, *, reverse):
    c = M_CHUNK
    d = 1 if reverse else 0
    n_pairs = M_INNER // LANES
    ppg = n_pairs // M_GROUPS

    @pl.when(pl.program_id(1) == 0)
    def _():
        h_scr[...] = jnp.zeros_like(h_scr)

    ri = lax.broadcasted_iota(jnp.int32, (c, c), 0)
    ci = lax.broadcasted_iota(jnp.int32, (c, c), 1)
    keep = (ci >= ri) if reverse else (ci <= ri)
    tri = jnp.where(keep, 1.0, 0.0).astype(BF16)
    end_row = 0 if reverse else c - 1
    head0 = lax.broadcasted_iota(jnp.int32, (c, LANES), 1) < M_HEADDIM

    dt = _softplus(dt_ref[...] + dtb_ref[...])
    acs_c = _sel_dot(tri, dt * (-jnp.exp(alogc_ref[...])))
    acs_ct = acs_c.T
    dte = _dot_sel(dt, e_ref[0])
    acs = _sel_dot(tri, dte * (-jnp.exp(aloge_ref[0])))
    tot = acs[end_row:end_row + 1]
    xdt = xs_ref[...] * dte
    e_acs = jnp.exp(acs)
    x_end = (xdt * jnp.exp(tot - acs)).astype(BF16)
    xdt_b = xdt.astype(BF16)
    dec = jnp.exp(tot)
    gsl = [slice(g * M_STATE, (g + 1) * M_STATE) for g in range(M_GROUPS)]
    bm_b = [bm_ref[:, sl].astype(BF16) for sl in gsl]
    cm_b = [cm_ref[:, sl].astype(BF16) for sl in gsl]
    bm_t = [bm_ref[:, sl].T.astype(BF16) for sl in gsl]
    cb = [_dot_nt(cm_b[g], bm_b[g]) for g in range(M_GROUPS)]
    prs = range(n_pairs)
    lns = [slice(p * LANES, (p + 1) * LANES) for p in prs]
    ys = []
    for p in prs:
        pair = []
        for hh in range(2):
            idx = d * M_HEADS + 2 * p + hh
            seg = acs_c[:, idx:idx + 1] - acs_ct[idx:idx + 1, :]
            lmat = jnp.exp(jnp.where(keep, seg, NEG_BIG))
            pair.append(_dot((cb[p // ppg] * lmat).astype(BF16), xdt_b[:, lns[p]]))
        ys.append(jnp.where(head0, pair[0], pair[1]))
    h_prev = [h_scr[p] for p in prs]
    y_off = [_dot(cm_b[p // ppg], h_prev[p].astype(BF16)) for p in prs]
    y_ref[...] = jnp.concatenate(ys, axis=1) + jnp.concatenate(y_off, axis=1) * e_acs
    for p in prs:
        h_scr[p] = h_prev[p] * dec[:, lns[p]] + _dot(bm_t[p // ppg], x_end[:, lns[p]])


def _ssd_call(xs, bm, cm, dt_raw, dtb, emat, alog_e, alog_c, bsz, seq, reverse):
    t = xs.shape[0]
    nc = seq // M_CHUNK
    d = 1 if reverse else 0
    gs = M_GROUPS * M_STATE
    if reverse:
        row = lambda bi, ci: (bi * nc + nc - 1 - ci, 0)
    else:
        row = lambda bi, ci: (bi * nc + ci, 0)
    return pl.pallas_call(
        functools.partial(_ssd_kernel, reverse=reverse),
        grid=(bsz, nc),
        in_specs=[pl.BlockSpec((M_CHUNK, M_INNER), row),
                  pl.BlockSpec((M_CHUNK, gs), row),
                  pl.BlockSpec((M_CHUNK, gs), row),
                  pl.BlockSpec((M_CHUNK, LANES), row),
                  pl.BlockSpec((1, LANES), lambda bi, ci: (0, 0)),
                  pl.BlockSpec((1, LANES, M_INNER), lambda bi, ci: (d, 0, 0)),
                  pl.BlockSpec((1, 1, M_INNER), lambda bi, ci: (d, 0, 0)),
                  pl.BlockSpec((1, LANES), lambda bi, ci: (0, 0))],
        out_specs=pl.BlockSpec((M_CHUNK, M_INNER), row),
        out_shape=jax.ShapeDtypeStruct((t, M_INNER), F32),
        scratch_shapes=[pltpu.VMEM((M_INNER // LANES, M_STATE, LANES), F32)],
        compiler_params=_cparams("parallel", "arbitrary"),
        name="ssd_bwd" if reverse else "ssd_fwd",
    )(xs, bm, cm, dt_raw, dtb, emat, alog_e, alog_c)


def _mamba_post_kernel(yf_ref, yb_ref, xs_ref, z_ref, d_ref, nw_ref, o_ref):
    y = (yf_ref[...] + yb_ref[...] + d_ref[...] * xs_ref[...]) * _silu(z_ref[...])
    for g in range(M_GROUPS):
        sl = slice(g * M_GROUP_W, (g + 1) * M_GROUP_W)
        o_ref[:, sl] = _rms(y[:, sl], nw_ref[:, sl], 1e-5).astype(o_ref.dtype)


def _mamba_post_call(yf, yb, xs, z, d_e, nw):
    t = xs.shape[0]
    tm = 256
    spec = pl.BlockSpec((tm, M_INNER), lambda i: (i, 0))
    vec = pl.BlockSpec((1, M_INNER), lambda i: (0, 0))
    return pl.pallas_call(
        _mamba_post_kernel,
        grid=(t // tm,),
        in_specs=[spec] * 4 + [vec, vec],
        out_specs=spec,
        out_shape=jax.ShapeDtypeStruct((t, M_INNER), BF16),
        compiler_params=_cparams("parallel"),
        name="mamba_post",
    )(yf, yb, xs, z, d_e, nw)


def _diff_attn_kernel(zero_ref, q_ref, k_ref, v_ref, lam_ref, slope_ref, nw_ref, o_ref, bias_scr, s_scr, *, tq, tk,
                      lambda_init):
    seq = k_ref.shape[0]
    log2e = math.log2(math.e)

    @pl.when(pl.program_id(2) == 0)
    def _():
        rows = pl.program_id(1) * tq + lax.broadcasted_iota(jnp.int32, (tq, seq), 0)
        cols = lax.broadcasted_iota(jnp.int32, (tq, seq), 1)
        bias_scr[...] = (slope_ref[0][:, 0:1] * log2e) * jnp.abs(rows - cols).astype(F32)

    q = (q_ref[...].astype(F32) * (DF_HEAD ** -0.5 * log2e)).astype(BF16)
    lv = lam_ref[...]
    lam = (jnp.exp(jnp.sum(lv[0:1] * lv[1:2], keepdims=True))
           - jnp.exp(jnp.sum(lv[2:3] * lv[3:4], keepdims=True)) + lambda_init)
    map0 = lax.broadcasted_iota(jnp.int32, (tq, LANES), 1) < DF_HEAD
    qm = [jnp.where(map0, q, jnp.zeros_like(q)), jnp.where(map0, jnp.zeros_like(q), q)]
    tiles = [slice(t * tk, (t + 1) * tk) for t in range(seq // tk)]
    rt = min(LANES, tq)
    streams = [(mp, slice(rh * rt, (rh + 1) * rt)) for rh in range(tq // rt) for mp in range(2)]
    v1 = jnp.concatenate([v_ref[...], jnp.ones((seq, LANES), BF16)], axis=1)
    zero = zero_ref[0]

    def scores_tile(st, ts, mx):
        mp, rows = st
        s = _dot_nt(qm[mp][rows], k_ref[ts, :]) - bias_scr[rows, ts]
        s_scr[mp, rows, ts] = s
        for cs in range(tk // LANES):
            blk = s[:, cs * LANES:(cs + 1) * LANES]
            mx = blk if mx is None else jnp.maximum(mx, blk)
        return mx

    def weights(st, m):
        mp, rows = st
        return _dot(jnp.exp2(s_scr[mp + zero, rows, :] - m).astype(BF16), v1)

    m, acc = {}, {}
    for i in range(len(streams) + 1):
        if i < len(streams):
            mx = None
            for ts in tiles:
                mx = scores_tile(streams[i], ts, mx)
            m[i] = jnp.max(mx, axis=-1, keepdims=True)
        if i >= 1:
            acc[i - 1] = weights(streams[i - 1], m[i - 1])
    for rh in range(tq // rt):
        om = [acc[2 * rh + mp][:, 0:LANES] / acc[2 * rh + mp][:, LANES:LANES + 1] for mp in range(2)]
        o = om[0] - lam * om[1]
        o_ref[rh * rt:(rh + 1) * rt, :] = (_rms(o, nw_ref[...], 1e-5) * (1.0 - lambda_init)).astype(o_ref.dtype)


def _diff_attn_call(qkv, lam_vecs, slopes, subln_w, bsz, seq, lambda_init):
    t = qkv.shape[0]
    tq = min(512, seq)
    nq = seq // tq
    return pl.pallas_call(
        functools.partial(_diff_attn_kernel, tq=tq, tk=min(512, seq), lambda_init=lambda_init),
        grid=(DF_HEADS, nq, bsz),
        in_specs=[pl.BlockSpec(memory_space=pltpu.SMEM),
                  pl.BlockSpec((tq, DF_V), lambda hi, qi, bi: (bi * nq + qi, hi)),
                  pl.BlockSpec((seq, DF_V), lambda hi, qi, bi: (bi, DF_HEADS + hi)),
                  pl.BlockSpec((seq, DF_V), lambda hi, qi, bi: (bi, 2 * DF_HEADS + hi)),
                  pl.BlockSpec((4, DF_HEAD), lambda hi, qi, bi: (0, 0)),
                  pl.BlockSpec((1, 1, LANES), lambda hi, qi, bi: (hi, 0, 0)),
                  pl.BlockSpec((1, DF_V), lambda hi, qi, bi: (0, 0))],
        out_specs=pl.BlockSpec((tq, DF_V), lambda hi, qi, bi: (bi * nq + qi, hi)),
        out_shape=jax.ShapeDtypeStruct((t, DF_HEADS * DF_V), BF16),
        scratch_shapes=[pltpu.VMEM((tq, seq), F32), pltpu.VMEM((2, tq, seq), F32)],
        compiler_params=_cparams("parallel", "parallel", "arbitrary"),
        name="diff_attn",
    )(jnp.zeros((1,), jnp.int32), qkv, qkv, qkv, lam_vecs, slopes, subln_w)


def _merge_kernel(yr_ref, ym_ref, yd_ref, pg_ref, x_ref, mod_ref, nw_ref, wr_ref, wm_ref, wd_ref, wo_ref, o_ref):
    d = D_MODEL
    merged = (_sigmoid(pg_ref[:, 0:d]) * _dot(yr_ref[...], wr_ref[...])
              + _sigmoid(pg_ref[:, d:2 * d]) * _dot(ym_ref[...], wm_ref[...])
              + _sigmoid(pg_ref[:, 2 * d:3 * d]) * _dot(yd_ref[...], wd_ref[...]))
    y = _dot(merged.astype(BF16), wo_ref[...])
    o_ref[...] = x_ref[...] + mod_ref[0][2:3] * _rms(y, nw_ref[...], EPS)


def _merge_call(yr, ym, yd, pg, x2, mod_sub, nw, wr, wm, wd, wo, seq):
    t, d = x2.shape
    tm = 256
    row = lambda w: pl.BlockSpec((tm, w), lambda i: (i, 0))
    res = lambda a: pl.BlockSpec(a.shape, lambda i: (0, 0), pipeline_mode=pl.Buffered(1))
    return pl.pallas_call(
        _merge_kernel,
        grid=(t // tm,),
        in_specs=[row(RW_DIM), row(M_INNER), row(DF_HEADS * DF_V), row(GATE_COLS), row(d),
                  pl.BlockSpec((1, 3, d), lambda i: (i * tm // seq, 0, 0)),
                  pl.BlockSpec((1, d), lambda i: (0, 0)),
                  res(wr), res(wm), res(wd), res(wo)],
        out_specs=row(d),
        out_shape=jax.ShapeDtypeStruct((t, d), F32),
        compiler_params=_cparams("parallel"),
        name="merge_outproj",
    )(yr, ym, yd, pg, x2, mod_sub, nw, wr, wm, wd, wo)


def _pad_cols(w, n):
    return jnp.pad(w, ((0, 0), (0, n - w.shape[1])))


def _pad_rows(w, n):
    return jnp.pad(w, ((0, n - w.shape[0]), (0, 0)))


def _dir_padded(w):
    z = jnp.zeros_like(w[0])
    return jnp.stack([jnp.concatenate([w[0], z], axis=0), jnp.concatenate([z, w[1]], axis=0)])


def _head_expand_matrix():
    e = np.zeros((2, LANES, M_INNER), np.float32)
    for d in range(2):
        for h in range(M_HEADS):
            e[d, d * M_HEADS + h, h * M_HEADDIM:(h + 1) * M_HEADDIM] = 1.0
    return jnp.asarray(e, dtype=BF16)


def kernel(x, c, ada_w, ada_b, norm_w, ffn_w13, ffn_w2, w_in, rwkv_mu, rwkv_w0, rwkv_w2, rwkv_a0, rwkv_a2, rwkv_g2, rwkv_k_k, rwkv_k_a, rwkv_r_k, rwkv_ln_w, rwkv_ln_b, rwkv_v0, rwkv_v1, rwkv_v2, mamba_conv_w, mamba_conv_b, mamba_dt_bias, mamba_a_log, mamba_d, mamba_norm_w, diff_lambda, diff_subln_w, w_branch_rwkv, w_branch_mamba, w_branch_diff, w_out):
    bsz, seq, d = x.shape
    depth = ada_w.shape[0]
    t = bsz * seq
    x2 = x.reshape(t, d)
    mod_all = _mod_call(c, ada_w, ada_b)
    emat = _head_expand_matrix()
    slopes = jnp.broadcast_to(
        jnp.asarray(2.0 ** (-8.0 * np.arange(1, DF_HEADS + 1) / DF_HEADS), F32)[:, None, None], (DF_HEADS, 1, LANES))
    v_first = None
    for l in range(depth):
        mod = mod_all[l].reshape(bsz, N_SUB, 3, d)
        x2 = _ffn_call(x2, mod[:, 0], norm_w[l, 0:2], ffn_w13[l, 0].astype(BF16), ffn_w2[l, 0].astype(BF16), seq)

        wl = w_in[l]
        o_m = RW_COLS
        o_d = o_m + M_COLS
        o_g = o_d + DF_COLS
        w_cat = jnp.concatenate([
            wl[:, 0:3 * RW_DIM],
            _pad_cols(wl[:, 3 * RW_DIM:RW_COLS], RW_CODES_PAD),
            wl[:, o_m:o_m + M_INNER],
            wl[:, o_m + M_INNER:o_m + M_INNER + M_XBC],
            _pad_cols(wl[:, o_m + M_INNER + M_XBC:o_d], INPROJ_TN),
            wl[:, o_d:o_g],
            wl[:, o_g:],
        ], axis=1).astype(BF16)
        p_rkv, p_codes, p_z, p_xbc, p_dt, p_qkv, p_gate = _inproj_call(
            x2, mod[:, 1], norm_w[l, 2:3], w_cat,
            (3 * RW_DIM, RW_CODES_PAD, M_INNER, M_XBC, INPROJ_TN, DF_COLS, GATE_COLS),
            (F32, F32, F32, F32, F32, BF16, F32), seq)

        mu = rwkv_mu[l]
        mu_p = mu[None, 0:3 * RW_DIM]
        mu_c = _pad_cols(mu[None, 3 * RW_DIM:], RW_CODES_PAD)
        v0 = rwkv_v0[l - 1] if l > 0 else jnp.zeros((RW_DIM,), F32)
        pvec = jnp.stack([rwkv_w0[l, 0], rwkv_w0[l, 1], rwkv_a0[l, 0], rwkv_a0[l, 1],
                          rwkv_k_k[l], rwkv_k_a[l], v0, jnp.zeros((RW_DIM,), F32)])
        g2p = _pad_rows(rwkv_g2[l], 256)
        vres = None
        if l > 0:
            vres = (v_first, _pad_cols(rwkv_v1[l - 1], LANES), _pad_rows(rwkv_v2[l - 1], LANES))
        r, v, kk, k_f, k_b, b_f, b_b, lw_f, lw_b, gate = _rwkv_prep_call(
            p_rkv, p_codes, mu_p, mu_c, pvec, _dir_padded(rwkv_w2[l]), _dir_padded(rwkv_a2[l]), g2p, vres, seq)
        if l == 0:
            v_first = v
        y_f, y_b = _rwkv_scan_call(r, v, kk, k_f, b_f, lw_f, k_b, b_b, lw_b, bsz, seq)
        pvec2 = jnp.concatenate([jnp.stack([rwkv_ln_w[l], rwkv_ln_b[l], rwkv_r_k[l].reshape(RW_DIM)]),
                                 jnp.zeros((5, RW_DIM), F32)])
        y_r = _rwkv_post_call(y_f, y_b, r, k_f, k_b, v, gate, pvec2)

        conv_w = _pad_rows(mamba_conv_w[l], 8)
        xs, bm, cm = _mamba_conv_call(p_xbc, conv_w, mamba_conv_b[l][None], seq)
        dtb = _pad_cols(mamba_dt_bias[l].reshape(1, 2 * M_HEADS), LANES)
        alog_e = jnp.repeat(mamba_a_log[l], M_HEADDIM, axis=-1)[:, None]
        alog_c = _pad_cols(mamba_a_log[l].reshape(1, 2 * M_HEADS), LANES)
        ym_f = _ssd_call(xs, bm, cm, p_dt, dtb, emat, alog_e, alog_c, bsz, seq, False)
        ym_b = _ssd_call(xs, bm, cm, p_dt, dtb, emat, alog_e, alog_c, bsz, seq, True)
        d_e = jnp.repeat(mamba_d[l], M_HEADDIM)[None]
        y_m = _mamba_post_call(ym_f, ym_b, xs, p_z, d_e, mamba_norm_w[l][None])

        lambda_init = 0.8 - 0.6 * math.exp(-0.3 * l)
        y_d = _diff_attn_call(p_qkv, diff_lambda[l], slopes, diff_subln_w[l][None], bsz, seq, lambda_init)

        x2 = _merge_call(y_r, y_m, y_d, p_gate, x2, mod[:, 1], norm_w[l, 3:4],
                         w_branch_rwkv[l].astype(BF16), w_branch_mamba[l].astype(BF16),
                         w_branch_diff[l].astype(BF16), w_out[l].astype(BF16), seq)

        x2 = _ffn_call(x2, mod[:, 2], norm_w[l, 4:6], ffn_w13[l, 1].astype(BF16), ffn_w2[l, 1].astype(BF16), seq)
    return x2.reshape(bsz, seq, d)
```

```python
import functools
import math

import numpy as np
import jax
import jax.numpy as jnp
from jax import lax
from jax.experimental import pallas as pl
from jax.experimental.pallas import tpu as pltpu

F32 = jnp.float32
BF16 = jnp.bfloat16
HI = lax.Precision.HIGHEST

D_MODEL = 1024
N_SUB = 3
EPS = 1e-6
LANES = 128
SUBLANES = 8
HALO = 2 * SUBLANES
VMEM_LIMIT = 56 * 1024 * 1024

RW_HEAD = 64
RW_DIM = 1024
W_LORA = 64
A_LORA = 64
V_LORA = 32
G_LORA = 160
RW_GN_EPS = 64e-5
RW_COLS = 3 * RW_DIM + 2 * W_LORA + 2 * A_LORA + G_LORA
RW_CODES_PAD = 512
RW_CHUNK = 64
M_INNER = 2048
M_HEADS = 32
M_HEADDIM = 64
M_GROUPS = 4
M_STATE = 128
M_CONV = 5
M_CHUNK = 128
M_XBC = M_INNER + 2 * M_GROUPS * M_STATE
M_COLS = M_INNER + M_XBC + 2 * M_HEADS
M_GROUP_W = M_INNER // M_GROUPS
DF_HEADS = 8
DF_HEAD = 64
DF_V = 128
DF_COLS = 3 * DF_HEADS * 2 * DF_HEAD
GATE_COLS = 3 * D_MODEL
D_FF = 2816
NEG_BIG = -1e30


def _cparams(*sem):
    return pltpu.CompilerParams(dimension_semantics=sem, vmem_limit_bytes=VMEM_LIMIT)


def _dot(a, b, **kw):
    return jnp.dot(a, b, preferred_element_type=F32, **kw)


def _dot_nt(a, b, **kw):
    return lax.dot_general(a, b, (((1,), (1,)), ((), ())), preferred_element_type=F32, **kw)


def _split2(x):
    hi = x.astype(BF16)
    return hi, (x - hi.astype(F32)).astype(BF16)


def _dot_sel(x, m_b):
    hi, lo = _split2(x)
    return _dot(hi, m_b) + _dot(lo, m_b)


def _sel_dot(m_b, x):
    hi, lo = _split2(x)
    return _dot(m_b, hi) + _dot(m_b, lo)


def _bdot(a, b):
    return _dot(a.astype(BF16), b.astype(BF16))


def _sigmoid(x):
    return 1.0 / (1.0 + jnp.exp(-x))


def _silu(x):
    return x * _sigmoid(x)


def _softplus(x):
    return jnp.maximum(x, 0.0) + jnp.log(1.0 + jnp.exp(-jnp.abs(x)))


def _rms(x, w, eps):
    return x * lax.rsqrt(jnp.mean(x * x, axis=-1, keepdims=True) + eps) * w


def _norm_mod(x, nw, shift, scale):
    return _rms(x, nw, EPS) * (1.0 + scale) + shift


def _pair_ones(scale=1.0):
    r = lax.broadcasted_iota(jnp.int32, (LANES, LANES), 0) // RW_HEAD
    c = lax.broadcasted_iota(jnp.int32, (LANES, LANES), 1) // RW_HEAD
    return jnp.where(r == c, scale, 0.0).astype(BF16)


def _shift_rows(cur, prevz, nextz, d):
    tm = cur.shape[0]
    rolled = pltpu.roll(cur, (-d) % tm, axis=0)
    rid = lax.broadcasted_iota(jnp.int32, (SUBLANES, cur.shape[1]), 0)
    if d < 0:
        fix = pltpu.roll(prevz, (-d) % SUBLANES, axis=0)
        top = jnp.where(rid < -d, fix, rolled[0:SUBLANES])
        return jnp.concatenate([top, rolled[SUBLANES:]], axis=0)
    fix = pltpu.roll(nextz, (SUBLANES - d) % SUBLANES, axis=0)
    bot = jnp.where(rid >= SUBLANES - d, fix, rolled[tm - SUBLANES:])
    return jnp.concatenate([rolled[:tm - SUBLANES], bot], axis=0)


def _halo_specs(tm, width, n_rows):
    rb = tm // HALO
    last = n_rows // HALO - 1
    return [
        pl.BlockSpec((tm, width), lambda i: (i, 0)),
        pl.BlockSpec((HALO, width), lambda i: (jnp.maximum(i * rb - 1, 0), 0)),
        pl.BlockSpec((HALO, width), lambda i: (jnp.minimum((i + 1) * rb, last), 0)),
    ]


def _halo_values(cur_ref, prev_ref, next_ref, tm, seq):
    i = pl.program_id(0)
    first = (i * tm) % seq == 0
    last = ((i + 1) * tm) % seq == 0
    prevz = jnp.where(first, 0.0, prev_ref[...].astype(F32)[HALO - SUBLANES:])
    nextz = jnp.where(last, 0.0, next_ref[...].astype(F32)[:SUBLANES])
    return cur_ref[...].astype(F32), prevz, nextz


def _mod_kernel(c_ref, w_ref, b_ref, o_ref):
    o_ref[0] = _dot(_silu(c_ref[...]), w_ref[0], precision=HI) + b_ref[0]


def _mod_call(c, ada_w, ada_b):
    depth, d, n = ada_w.shape
    bsz = c.shape[0]
    tn = 1152
    return pl.pallas_call(
        _mod_kernel,
        grid=(depth, n // tn),
        in_specs=[pl.BlockSpec((bsz, d), lambda l, j: (0, 0)),
                  pl.BlockSpec((1, d, tn), lambda l, j: (l, 0, j)),
                  pl.BlockSpec((1, 1, tn), lambda l, j: (l, 0, j))],
        out_specs=pl.BlockSpec((1, bsz, tn), lambda l, j: (l, 0, j)),
        out_shape=jax.ShapeDtypeStruct((depth, bsz, n), F32),
        compiler_params=_cparams("parallel", "parallel"),
        name="adaln_mod",
    )(c, ada_w, ada_b.reshape(depth, 1, n))


def _ffn_kernel(x_ref, mod_ref, nw_ref, wg_ref, wu_ref, w2_ref, o_ref, h_scr, acc_scr):
    j = pl.program_id(1)

    @pl.when(j == 0)
    def _():
        m = mod_ref[0]
        h_scr[...] = _norm_mod(x_ref[...], nw_ref[0:1], m[0:1], m[1:2]).astype(BF16)
        acc_scr[...] = jnp.zeros_like(acc_scr)

    h = h_scr[...]
    g = _dot(h, wg_ref[...])
    u = _dot(h, wu_ref[...])
    acc_scr[...] += _dot((_silu(g) * u).astype(BF16), w2_ref[...])

    @pl.when(j == pl.num_programs(1) - 1)
    def _():
        m = mod_ref[0]
        o_ref[...] = x_ref[...] + 0.5 * m[2:3] * _rms(acc_scr[...], nw_ref[1:2], EPS)


def _ffn_call(x2, mod_sub, nw2, w13, w2, seq):
    t, d = x2.shape
    dff = w2.shape[0]
    tm, tf = 1024, 256
    tm = min(tm, seq)
    nf = dff // tf
    return pl.pallas_call(
        _ffn_kernel,
        grid=(t // tm, nf),
        in_specs=[pl.BlockSpec((tm, d), lambda i, j: (i, 0)),
                  pl.BlockSpec((1, 3, d), lambda i, j: (i * tm // seq, 0, 0)),
                  pl.BlockSpec((2, d), lambda i, j: (0, 0)),
                  pl.BlockSpec((d, tf), lambda i, j: (0, j)),
                  pl.BlockSpec((d, tf), lambda i, j: (0, nf + j)),
                  pl.BlockSpec((tf, d), lambda i, j: (j, 0))],
        out_specs=pl.BlockSpec((tm, d), lambda i, j: (i, 0)),
        out_shape=jax.ShapeDtypeStruct((t, d), F32),
        scratch_shapes=[pltpu.VMEM((tm, d), BF16), pltpu.VMEM((tm, d), F32)],
        compiler_params=_cparams("parallel", "arbitrary"),
        name="swiglu_halfstep",
    )(x2, mod_sub, nw2, w13, w13, w2)


INPROJ_TN = 512


def _inproj_kernel(x_ref, mod_ref, nw_ref, w_ref, *rest, starts):
    o_refs, h_scr = rest[:-1], rest[-1]
    j = pl.program_id(1)

    @pl.when(j == 0)
    def _():
        m = mod_ref[0]
        h_scr[...] = _norm_mod(x_ref[...], nw_ref[...], m[0:1], m[1:2]).astype(BF16)

    for k, o_ref in enumerate(o_refs):
        @pl.when((j >= starts[k]) & (j < starts[k + 1]))
        def _(o_ref=o_ref):
            o_ref[...] = _dot(h_scr[...], w_ref[...]).astype(o_ref.dtype)


def _inproj_call(x2, mod_sub, nw, w, widths, dtypes, seq):
    t, d = x2.shape
    tn = INPROJ_TN
    tm = min(1024, seq)
    starts = [0]
    for wd in widths:
        starts.append(starts[-1] + wd // tn)

    def out_spec(k):
        return pl.BlockSpec((tm, tn), lambda i, j: (i, jnp.clip(j - starts[k], 0, widths[k] // tn - 1)))

    return pl.pallas_call(
        functools.partial(_inproj_kernel, starts=tuple(starts)),
        grid=(t // tm, starts[-1]),
        in_specs=[pl.BlockSpec((tm, d), lambda i, j: (i, 0)),
                  pl.BlockSpec((1, 3, d), lambda i, j: (i * tm // seq, 0, 0)),
                  pl.BlockSpec((1, d), lambda i, j: (0, 0)),
                  pl.BlockSpec((d, tn), lambda i, j: (0, j))],
        out_specs=[out_spec(k) for k in range(len(widths))],
        out_shape=[jax.ShapeDtypeStruct((t, wd), dt) for wd, dt in zip(widths, dtypes)],
        scratch_shapes=[pltpu.VMEM((tm, d), BF16)],
        compiler_params=_cparams("parallel", "arbitrary"),
        name="norm_inproj",
    )(x2, mod_sub, nw, w)


def _rwkv_prep_kernel(*refs, tm, seq, has_vres):
    (p_ref, pp_ref, pn_ref, c_ref, cp_ref, cn_ref, mup_ref, muc_ref, pvec_ref,
     w2_ref, a2_ref, g2_ref) = refs[:12]
    refs = refs[12:]
    if has_vres:
        vf_ref, v1_ref, v2_ref = refs[:3]
        refs = refs[3:]
    r_o, v_o, kk_o, kf_o, kb_o, bf_o, bb_o, lwf_o, lwb_o, g_o = refs

    def shift_mix(cur_ref, prev_ref, next_ref, mu):
        cur, prevz, nextz = _halo_values(cur_ref, prev_ref, next_ref, tm, seq)
        nb = 0.5 * (_shift_rows(cur, prevz, nextz, -1) + _shift_rows(cur, prevz, nextz, 1))
        return cur + mu * (nb - cur)

    p = shift_mix(p_ref, pp_ref, pn_ref, mup_ref[...])
    codes = shift_mix(c_ref, cp_ref, cn_ref, muc_ref[...])
    r = p[:, 0:RW_DIM]
    k = p[:, RW_DIM:2 * RW_DIM]
    v = p[:, 2 * RW_DIM:3 * RW_DIM]
    cw = jnp.tanh(codes[:, 0:2 * W_LORA])
    ca = codes[:, 2 * W_LORA:2 * W_LORA + 2 * A_LORA]
    cg = _sigmoid(codes[:, 2 * W_LORA + 2 * A_LORA:])
    pv = pvec_ref[...]
    if has_vres:
        lo = _bdot(_bdot(v, v1_ref[...]), v2_ref[...])
        v = v + (vf_ref[...].astype(F32) - v) * _sigmoid(pv[6:7] + lo)
    r_o[...] = r.astype(r_o.dtype)
    v_o[...] = v.astype(v_o.dtype)
    g_o[...] = _bdot(cg, g2_ref[...]).astype(g_o.dtype)
    iclr = []
    for d, lw_o in ((0, lwf_o), (1, lwb_o)):
        w_log = -_softplus(-(pv[d:d + 1] + _bdot(cw, w2_ref[d]))) - 0.5
        lw_o[...] = -jnp.exp(w_log)
        iclr.append(_sigmoid(pv[2 + d:3 + d] + _bdot(ca, a2_ref[d])))
    kf_o[...] = (k * (1.0 + (iclr[0] - 1.0) * pv[5:6])).astype(kf_o.dtype)
    kb_o[...] = (k * (1.0 + (iclr[1] - 1.0) * pv[5:6])).astype(kb_o.dtype)
    ones2 = _pair_ones()
    kkr = k * pv[4:5]
    for cb in range(RW_DIM // LANES):
        sl = slice(cb * LANES, (cb + 1) * LANES)
        blk = kkr[:, sl]
        kkn = blk * lax.rsqrt(_dot_sel(blk * blk, ones2) + 1e-12)
        kk_o[:, sl] = kkn.astype(kk_o.dtype)
        bf_o[:, sl] = (kkn * iclr[0][:, sl]).astype(bf_o.dtype)
        bb_o[:, sl] = (kkn * iclr[1][:, sl]).astype(bb_o.dtype)


def _rwkv_prep_call(rkv, codes, mu_p, mu_c, pvec, w2p, a2p, g2p, vres, seq):
    t = rkv.shape[0]
    tm = 128
    has_vres = vres is not None
    full = lambda shape: pl.BlockSpec(shape, lambda i: (0,) * len(shape))
    in_specs = (_halo_specs(tm, 3 * RW_DIM, t) + _halo_specs(tm, RW_CODES_PAD, t)
                + [full((1, 3 * RW_DIM)), full((1, RW_CODES_PAD)), full((8, RW_DIM)),
                   full((2, 2 * W_LORA, RW_DIM)), full((2, 2 * A_LORA, RW_DIM)), full((256, RW_DIM))])
    args = [rkv, rkv, rkv, codes, codes, codes, mu_p, mu_c, pvec, w2p, a2p, g2p]
    if has_vres:
        in_specs += [pl.BlockSpec((tm, RW_DIM), lambda i: (i, 0)), full((RW_DIM, LANES)), full((LANES, RW_DIM))]
        args += list(vres)
    dtypes = [BF16] * 7 + [F32, F32, BF16]
    return pl.pallas_call(
        functools.partial(_rwkv_prep_kernel, tm=tm, seq=seq, has_vres=has_vres),
        grid=(t // tm,),
        in_specs=in_specs,
        out_specs=[pl.BlockSpec((tm, RW_DIM), lambda i: (i, 0))] * 10,
        out_shape=[jax.ShapeDtypeStruct((t, RW_DIM), dt) for dt in dtypes],
        compiler_params=_cparams("parallel"),
        name="rwkv_prep",
    )(*args)


def _rwkv_scan_kernel(rf_ref, vf_ref, kkf_ref, kf_ref, bf_ref, lwf_ref, rb_ref, vb_ref, kkb_ref, kb_ref, bb_ref,
                      lwb_ref, yf_ref, yb_ref, h_scr, *, n_chunks):
    c = RW_CHUNK
    n_pairs = RW_DIM // LANES

    @pl.when(pl.program_id(1) == 0)
    def _():
        h_scr[...] = jnp.zeros_like(h_scr)

    ri = lax.broadcasted_iota(jnp.int32, (LANES, LANES), 0)
    ci = lax.broadcasted_iota(jnp.int32, (LANES, LANES), 1)
    same = (ri // c) == (ci // c)
    tr, tc = ri % c, ci % c
    strict = [same & (tc < tr), same & (tc > tr)]
    incl = [same & (tc <= tr), same & (tc >= tr)]
    blk = [(ri // w) == (ci // w) for w in (8, 16, 32, 64)]
    eye = jnp.where(ri == ci, 1.0, 0.0).astype(F32)
    r64 = lax.broadcasted_iota(jnp.int32, (c, c), 0)
    c64 = lax.broadcasted_iota(jnp.int32, (c, c), 1)
    tri = [jnp.where(c64 <= r64, 1.0, 0.0).astype(BF16), jnp.where(c64 >= r64, 1.0, 0.0).astype(BF16)]
    head0 = lax.broadcasted_iota(jnp.int32, (c, LANES), 1) < RW_HEAD
    end_row = [c - 1, 0]
    refs = [(rf_ref, vf_ref, kkf_ref, kf_ref, bf_ref, lwf_ref, yf_ref),
            (rb_ref, vb_ref, kkb_ref, kb_ref, bb_ref, lwb_ref, yb_ref)]
    lns = [slice(pr * LANES, (pr + 1) * LANES) for pr in range(n_pairs)]
    chains = [(d, p) for p in range(n_pairs) for d in range(2)]

    def stack(x):
        return jnp.concatenate([jnp.where(head0, x, 0.0), jnp.where(head0, 0.0, x)], axis=0)

    def chunk(step, carry):
        rows = [pl.ds(pl.multiple_of(step * c, c), c), pl.ds(pl.multiple_of((n_chunks - 1 - step) * c, c), c)]
        wide = []
        for d in range(2):
            r_ref, v_ref, kk_ref, k_ref, b_ref, lw_ref, _ = refs[d]
            lw = lw_ref[rows[d], :]
            ld = lambda ref: ref[rows[d], :].astype(F32)
            cum = _sel_dot(tri[d], lw)
            tot = cum[end_row[d]:end_row[d] + 1]
            g_inv, g_end = jnp.exp(-cum), jnp.exp(tot - cum)
            k_all, b_all = ld(k_ref), ld(b_ref)
            wide.append(dict(at=-ld(kk_ref) * jnp.exp(cum - lw), rt=ld(r_ref) * jnp.exp(cum),
                             bt=b_all * g_inv, kt=k_all * g_inv, bg=b_all * g_end, kg=k_all * g_end,
                             v=ld(v_ref), g_tot=jnp.exp(tot)))
        ar_s = [jnp.concatenate([stack(wide[d]["at"][:, lns[p]]), stack(wide[d]["rt"][:, lns[p]])], axis=0)
                .astype(BF16) for d, p in chains]
        bk_s = [jnp.concatenate([stack(wide[d]["bt"][:, lns[p]]), stack(wide[d]["kt"][:, lns[p]])], axis=0)
                .astype(BF16) for d, p in chains]
        v_s = [stack(wide[d]["v"][:, lns[p]]) for d, p in chains]
        ht = [h_scr[d, p] for d, p in chains]
        idx = range(len(chains))
        gram = [_dot_nt(ar_s[i], bk_s[i]) for i in idx]
        a_ab = [jnp.where(strict[d], gram[i][0:LANES, 0:LANES], 0.0) for i, (d, p) in enumerate(chains)]
        lhs = [jnp.concatenate([
            ar_s[i],
            jnp.concatenate([jnp.where(strict[d], gram[i][0:LANES, LANES:], 0.0),
                             jnp.where(incl[d], gram[i][LANES:, LANES:], 0.0)], axis=0).astype(BF16)], axis=1)
            for i, (d, p) in enumerate(chains)]
        rhs = [jnp.concatenate([ht[i].T, v_s[i]], axis=0).astype(BF16) for i in idx]
        xy0 = [_dot(lhs[i], rhs[i]) for i in idx]
        a_rb = [jnp.where(incl[d], gram[i][LANES:, 0:LANES], 0.0).astype(BF16) for i, (d, p) in enumerate(chains)]
        p1 = [jnp.where(blk[0], a_ab[i], 0.0).astype(BF16) for i in idx]
        p2 = [_dot(p1[i], p1[i]) for i in idx]
        p2_b = [x.astype(BF16) for x in p2]
        p4_b = [_dot(p2_b[i], p2_b[i]).astype(BF16) for i in idx]
        t_inv = [eye + p1[i] + p2[i] + _dot(p1[i], p2_b[i]) for i in idx]
        t_inv = [t_inv[i] + _dot(t_inv[i].astype(BF16), p4_b[i]) for i in idx]
        for lvl in range(1, 4):
            off = [jnp.where(blk[lvl] & ~blk[lvl - 1], a_ab[i], 0.0).astype(BF16) for i in idx]
            t_b = [x.astype(BF16) for x in t_inv]
            mid = [_dot(t_b[i], off[i]).astype(BF16) for i in idx]
            t_inv = [t_inv[i] + _dot(mid[i], t_b[i]) for i in idx]
        u = [_dot(t_inv[i].astype(BF16), xy0[i][0:LANES].astype(BF16)) for i in idx]
        y = [xy0[i][LANES:] + _dot(a_rb[i], u[i].astype(BF16)) for i in idx]
        for d in range(2):
            ys = [y[i][0:c] + y[i][c:2 * c] for i, (dd, p) in enumerate(chains) if dd == d]
            refs[d][6][rows[d], :] = jnp.concatenate(ys, axis=1).astype(refs[d][6].dtype)
        for i, (d, p) in enumerate(chains):
            uv_t = jnp.concatenate([u[i], v_s[i]], axis=0).T.astype(BF16)
            bkg = jnp.concatenate([stack(wide[d]["bg"][:, lns[p]]), stack(wide[d]["kg"][:, lns[p]])],
                                  axis=0).astype(BF16)
            h_scr[d, p] = ht[i] * wide[d]["g_tot"][:, lns[p]] + _dot(uv_t, bkg)
        return carry

    lax.fori_loop(0, n_chunks, chunk, 0)


def _rwkv_scan_call(r, v, kk, k_f, b_f, lw_f, k_b, b_b, lw_b, bsz, seq):
    t = r.shape[0]
    tb = min(2 * RW_CHUNK, seq)
    nt = seq // tb
    fwd = pl.BlockSpec((tb, RW_DIM), lambda bi, ti: (bi * nt + ti, 0))
    bwd = pl.BlockSpec((tb, RW_DIM), lambda bi, ti: (bi * nt + nt - 1 - ti, 0))
    out = jax.ShapeDtypeStruct((t, RW_DIM), BF16)
    return pl.pallas_call(
        functools.partial(_rwkv_scan_kernel, n_chunks=tb // RW_CHUNK),
        grid=(bsz, nt),
        in_specs=[fwd] * 6 + [bwd] * 6,
        out_specs=[fwd, bwd],
        out_shape=[out, out],
        scratch_shapes=[pltpu.VMEM((2, RW_DIM // LANES, LANES, LANES), F32)],
        compiler_params=_cparams("parallel", "arbitrary"),
        name="rwkv_scan",
    )(r, v, kk, k_f, b_f, lw_f, r, v, kk, k_b, b_b, lw_b)


def _rwkv_post_kernel(yf_ref, yb_ref, r_ref, kf_ref, kb_ref, v_ref, g_ref, pvec_ref, o_ref):
    mean2 = _pair_ones(1.0 / RW_HEAD)
    ones2 = _pair_ones()
    pv = pvec_ref[...]
    for cb in range(RW_DIM // LANES):
        sl = slice(cb * LANES, (cb + 1) * LANES)
        ld = lambda ref: ref[:, sl].astype(F32)
        y = ld(yf_ref) + ld(yb_ref)
        dlt = y - _dot_sel(y, mean2)
        var = _dot_sel(dlt * dlt, mean2)
        yn = dlt * lax.rsqrt(var + RW_GN_EPS) * pv[0:1, sl] + pv[1:2, sl]
        rk = ld(r_ref) * (ld(kf_ref) + ld(kb_ref)) * pv[2:3, sl]
        bonus = _dot_sel(rk, ones2) * ld(v_ref)
        o_ref[:, sl] = ((yn + bonus) * ld(g_ref)).astype(o_ref.dtype)


def _rwkv_post_call(yf, yb, r, kf, kb, v, g, pvec):
    t = r.shape[0]
    tm = 256
    spec = pl.BlockSpec((tm, RW_DIM), lambda i: (i, 0))
    return pl.pallas_call(
        _rwkv_post_kernel,
        grid=(t // tm,),
        in_specs=[spec] * 7 + [pl.BlockSpec((8, RW_DIM), lambda i: (0, 0))],
        out_specs=spec,
        out_shape=jax.ShapeDtypeStruct((t, RW_DIM), BF16),
        compiler_params=_cparams("parallel"),
        name="rwkv_post",
    )(yf, yb, r, kf, kb, v, g, pvec)


def _mamba_conv_kernel(x_ref, xp_ref, xn_ref, w_ref, b_ref, xs_o, bm_o, cm_o, *, tm, seq):
    cur, prevz, nextz = _halo_values(x_ref, xp_ref, xn_ref, tm, seq)
    w = w_ref[...]
    half = (M_CONV - 1) // 2
    acc = cur * w[half:half + 1] + b_ref[...]
    for tap in range(M_CONV):
        if tap != half:
            acc = acc + _shift_rows(cur, prevz, nextz, tap - half) * w[tap:tap + 1]
    y = _silu(acc)
    xs_o[...] = y[:, 0:M_INNER].astype(xs_o.dtype)
    bm_o[...] = y[:, M_INNER:M_INNER + M_GROUPS * M_STATE].astype(bm_o.dtype)
    cm_o[...] = y[:, M_INNER + M_GROUPS * M_STATE:].astype(cm_o.dtype)


def _mamba_conv_call(xbc, conv_w, conv_b, seq):
    t = xbc.shape[0]
    tm = 256
    gs = M_GROUPS * M_STATE
    return pl.pallas_call(
        functools.partial(_mamba_conv_kernel, tm=tm, seq=seq),
        grid=(t // tm,),
        in_specs=_halo_specs(tm, M_XBC, t) + [pl.BlockSpec((8, M_XBC), lambda i: (0, 0)),
                                             pl.BlockSpec((1, M_XBC), lambda i: (0, 0))],
        out_specs=[pl.BlockSpec((tm, M_INNER), lambda i: (i, 0)),
                   pl.BlockSpec((tm, gs), lambda i: (i, 0)),
                   pl.BlockSpec((tm, gs), lambda i: (i, 0))],
        out_shape=[jax.ShapeDtypeStruct((t, M_INNER), BF16),
                   jax.ShapeDtypeStruct((t, gs), BF16),
                   jax.ShapeDtypeStruct((t, gs), BF16)],
        compiler_params=_cparams("parallel"),
        name="mamba_conv",
    )(xbc, xbc, xbc, conv_w, conv_b)


def _ssd_kernel(xs_ref, bm_ref, cm_ref, dt_ref, dtb_ref, e_ref, aloge_ref, alogc_ref, y_ref, h_scr, *, reverse):
    c = M_CHUNK
    d = 1 if reverse else 0
    n_pairs = M_INNER // LANES
    ppg = n_pairs // M_GROUPS

    @pl.when(pl.program_id(1) == 0)
    def _():
        h_scr[...] = jnp.zeros_like(h_scr)

    ri = lax.broadcasted_iota(jnp.int32, (c, c), 0)
    ci = lax.broadcasted_iota(jnp.int32, (c, c), 1)
    keep = (ci >= ri) if reverse else (ci <= ri)
    tri = jnp.where(keep, 1.0, 0.0).astype(BF16)
    end_row = 0 if reverse else c - 1
    head0 = lax.broadcasted_iota(jnp.int32, (c, LANES), 1) < M_HEADDIM

    dt = _softplus(dt_ref[...] + dtb_ref[...])
    acs_c = _sel_dot(tri, dt * (-jnp.exp(alogc_ref[...])))
    acs_ct = acs_c.T
    dte = _dot_sel(dt, e_ref[0])
    acs = _sel_dot(tri, dte * (-jnp.exp(aloge_ref[0])))
    tot = acs[end_row:end_row + 1]
    xdt = xs_ref[...].astype(F32) * dte
    e_acs = jnp.exp(acs)
    x_end = (xdt * jnp.exp(tot - acs)).astype(BF16)
    xdt_b = xdt.astype(BF16)
    dec = jnp.exp(tot)
    gsl = [slice(g * M_STATE, (g + 1) * M_STATE) for g in range(M_GROUPS)]
    bm_b = [bm_ref[:, sl] for sl in gsl]
    cm_b = [cm_ref[:, sl] for sl in gsl]
    bm_t = [bm_ref[:, sl].astype(F32).T.astype(BF16) for sl in gsl]
    cb = [_dot_nt(cm_b[g], bm_b[g]) for g in range(M_GROUPS)]
    prs = range(n_pairs)
    lns = [slice(p * LANES, (p + 1) * LANES) for p in prs]
    ys = []
    for p in prs:
        pair = []
        for hh in range(2):
            idx = d * M_HEADS + 2 * p + hh
            seg = acs_c[:, idx:idx + 1] - acs_ct[idx:idx + 1, :]
            lmat = jnp.exp(jnp.where(keep, seg, NEG_BIG))
            pair.append(_dot((cb[p // ppg] * lmat).astype(BF16), xdt_b[:, lns[p]]))
        ys.append(jnp.where(head0, pair[0], pair[1]))
    h_prev = [h_scr[p] for p in prs]
    y_off = [_dot(cm_b[p // ppg], h_prev[p].astype(BF16)) for p in prs]
    y_ref[...] = (jnp.concatenate(ys, axis=1) + jnp.concatenate(y_off, axis=1) * e_acs).astype(y_ref.dtype)
    for p in prs:
        h_scr[p] = h_prev[p] * dec[:, lns[p]] + _dot(bm_t[p // ppg], x_end[:, lns[p]])


def _ssd_call(xs, bm, cm, dt_raw, dtb, emat, alog_e, alog_c, bsz, seq, reverse):
    t = xs.shape[0]
    nc = seq // M_CHUNK
    d = 1 if reverse else 0
    gs = M_GROUPS * M_STATE
    if reverse:
        row = lambda bi, ci: (bi * nc + nc - 1 - ci, 0)
    else:
        row = lambda bi, ci: (bi * nc + ci, 0)
    return pl.pallas_call(
        functools.partial(_ssd_kernel, reverse=reverse),
        grid=(bsz, nc),
        in_specs=[pl.BlockSpec((M_CHUNK, M_INNER), row),
                  pl.BlockSpec((M_CHUNK, gs), row),
                  pl.BlockSpec((M_CHUNK, gs), row),
                  pl.BlockSpec((M_CHUNK, LANES), row),
                  pl.BlockSpec((1, LANES), lambda bi, ci: (0, 0)),
                  pl.BlockSpec((1, LANES, M_INNER), lambda bi, ci: (d, 0, 0)),
                  pl.BlockSpec((1, 1, M_INNER), lambda bi, ci: (d, 0, 0)),
                  pl.BlockSpec((1, LANES), lambda bi, ci: (0, 0))],
        out_specs=pl.BlockSpec((M_CHUNK, M_INNER), row),
        out_shape=jax.ShapeDtypeStruct((t, M_INNER), BF16),
        scratch_shapes=[pltpu.VMEM((M_INNER // LANES, M_STATE, LANES), F32)],
        compiler_params=_cparams("parallel", "arbitrary"),
        name="ssd_bwd" if reverse else "ssd_fwd",
    )(xs, bm, cm, dt_raw, dtb, emat, alog_e, alog_c)


def _mamba_post_kernel(yf_ref, yb_ref, xs_ref, z_ref, d_ref, nw_ref, o_ref):
    ld = lambda ref: ref[...].astype(F32)
    y = (ld(yf_ref) + ld(yb_ref) + d_ref[...] * ld(xs_ref)) * _silu(ld(z_ref))
    for g in range(M_GROUPS):
        sl = slice(g * M_GROUP_W, (g + 1) * M_GROUP_W)
        o_ref[:, sl] = _rms(y[:, sl], nw_ref[:, sl], 1e-5).astype(o_ref.dtype)


def _mamba_post_call(yf, yb, xs, z, d_e, nw):
    t = xs.shape[0]
    tm = 256
    spec = pl.BlockSpec((tm, M_INNER), lambda i: (i, 0))
    vec = pl.BlockSpec((1, M_INNER), lambda i: (0, 0))
    return pl.pallas_call(
        _mamba_post_kernel,
        grid=(t // tm,),
        in_specs=[spec] * 4 + [vec, vec],
        out_specs=spec,
        out_shape=jax.ShapeDtypeStruct((t, M_INNER), BF16),
        compiler_params=_cparams("parallel"),
        name="mamba_post",
    )(yf, yb, xs, z, d_e, nw)


def _diff_attn_kernel(zero_ref, q_ref, k_ref, v_ref, lam_ref, slope_ref, nw_ref, o_ref, bias_scr, s_scr, *, tq, tk,
                      lambda_init):
    seq = k_ref.shape[0]
    log2e = math.log2(math.e)

    @pl.when(pl.program_id(2) == 0)
    def _():
        rows = pl.program_id(1) * tq + lax.broadcasted_iota(jnp.int32, (tq, seq), 0)
        cols = lax.broadcasted_iota(jnp.int32, (tq, seq), 1)
        bias_scr[...] = (slope_ref[0][:, 0:1] * log2e) * jnp.abs(rows - cols).astype(F32)

    q = (q_ref[...].astype(F32) * (DF_HEAD ** -0.5 * log2e)).astype(BF16)
    lv = lam_ref[...]
    lam = (jnp.exp(jnp.sum(lv[0:1] * lv[1:2], keepdims=True))
           - jnp.exp(jnp.sum(lv[2:3] * lv[3:4], keepdims=True)) + lambda_init)
    map0 = lax.broadcasted_iota(jnp.int32, (tq, LANES), 1) < DF_HEAD
    qm = [jnp.where(map0, q, jnp.zeros_like(q)), jnp.where(map0, jnp.zeros_like(q), q)]
    tiles = [slice(t * tk, (t + 1) * tk) for t in range(seq // tk)]
    rt = min(LANES, tq)
    streams = [(mp, slice(rh * rt, (rh + 1) * rt)) for rh in range(tq // rt) for mp in range(2)]
    v1 = jnp.concatenate([v_ref[...], jnp.ones((seq, LANES), BF16)], axis=1)
    zero = zero_ref[0]

    def scores_tile(st, ts, mx):
        mp, rows = st
        s = _dot_nt(qm[mp][rows], k_ref[ts, :]) - bias_scr[rows, ts]
        s_scr[mp, rows, ts] = s
        for cs in range(tk // LANES):
            blk = s[:, cs * LANES:(cs + 1) * LANES]
            mx = blk if mx is None else jnp.maximum(mx, blk)
        return mx

    def weights(st, m):
        mp, rows = st
        return _dot(jnp.exp2(s_scr[mp + zero, rows, :] - m).astype(BF16), v1)

    m, acc = {}, {}
    for i in range(len(streams) + 1):
        if i < len(streams):
            mx = None
            for ts in tiles:
                mx = scores_tile(streams[i], ts, mx)
            m[i] = jnp.max(mx, axis=-1, keepdims=True)
        if i >= 1:
            acc[i - 1] = weights(streams[i - 1], m[i - 1])
    for rh in range(tq // rt):
        om = [acc[2 * rh + mp][:, 0:LANES] / acc[2 * rh + mp][:, LANES:LANES + 1] for mp in range(2)]
        o = om[0] - lam * om[1]
        o_ref[rh * rt:(rh + 1) * rt, :] = (_rms(o, nw_ref[...], 1e-5) * (1.0 - lambda_init)).astype(o_ref.dtype)


def _diff_attn_call(qkv, lam_vecs, slopes, subln_w, bsz, seq, lambda_init):
    t = qkv.shape[0]
    tq = min(512, seq)
    nq = seq // tq
    return pl.pallas_call(
        functools.partial(_diff_attn_kernel, tq=tq, tk=min(512, seq), lambda_init=lambda_init),
        grid=(DF_HEADS, nq, bsz),
        in_specs=[pl.BlockSpec(memory_space=pltpu.SMEM),
                  pl.BlockSpec((tq, DF_V), lambda hi, qi, bi: (bi * nq + qi, hi)),
                  pl.BlockSpec((seq, DF_V), lambda hi, qi, bi: (bi, DF_HEADS + hi)),
                  pl.BlockSpec((seq, DF_V), lambda hi, qi, bi: (bi, 2 * DF_HEADS + hi)),
                  pl.BlockSpec((4, DF_HEAD), lambda hi, qi, bi: (0, 0)),
                  pl.BlockSpec((1, 1, LANES), lambda hi, qi, bi: (hi, 0, 0)),
                  pl.BlockSpec((1, DF_V), lambda hi, qi, bi: (0, 0))],
        out_specs=pl.BlockSpec((tq, DF_V), lambda hi, qi, bi: (bi * nq + qi, hi)),
        out_shape=jax.ShapeDtypeStruct((t, DF_HEADS * DF_V), BF16),
        scratch_shapes=[pltpu.VMEM((tq, seq), F32), pltpu.VMEM((2, tq, seq), F32)],
        compiler_params=_cparams("parallel", "parallel", "arbitrary"),
        name="diff_attn",
    )(jnp.zeros((1,), jnp.int32), qkv, qkv, qkv, lam_vecs, slopes, subln_w)


def _merge_kernel(yr_ref, ym_ref, yd_ref, pg_ref, x_ref, mod_ref, nw_ref, wr_ref, wm_ref, wd_ref, wo_ref, o_ref):
    d = D_MODEL
    gate = lambda g: _sigmoid(pg_ref[:, g * d:(g + 1) * d].astype(F32))
    merged = (gate(0) * _dot(yr_ref[...], wr_ref[...]) + gate(1) * _dot(ym_ref[...], wm_ref[...])
              + gate(2) * _dot(yd_ref[...], wd_ref[...]))
    y = _dot(merged.astype(BF16), wo_ref[...])
    o_ref[...] = x_ref[...] + mod_ref[0][2:3] * _rms(y, nw_ref[...], EPS)


def _merge_call(yr, ym, yd, pg, x2, mod_sub, nw, wr, wm, wd, wo, seq):
    t, d = x2.shape
    tm = 256
    row = lambda w: pl.BlockSpec((tm, w), lambda i: (i, 0))
    res = lambda a: pl.BlockSpec(a.shape, lambda i: (0, 0), pipeline_mode=pl.Buffered(1))
    return pl.pallas_call(
        _merge_kernel,
        grid=(t // tm,),
        in_specs=[row(RW_DIM), row(M_INNER), row(DF_HEADS * DF_V), row(GATE_COLS), row(d),
                  pl.BlockSpec((1, 3, d), lambda i: (i * tm // seq, 0, 0)),
                  pl.BlockSpec((1, d), lambda i: (0, 0)),
                  res(wr), res(wm), res(wd), res(wo)],
        out_specs=row(d),
        out_shape=jax.ShapeDtypeStruct((t, d), F32),
        compiler_params=_cparams("parallel"),
        name="merge_outproj",
    )(yr, ym, yd, pg, x2, mod_sub, nw, wr, wm, wd, wo)


def _pad_cols(w, n):
    return jnp.pad(w, ((0, 0), (0, n - w.shape[1])))


def _pad_rows(w, n):
    return jnp.pad(w, ((0, n - w.shape[0]), (0, 0)))


def _dir_padded(w):
    z = jnp.zeros_like(w[0])
    return jnp.stack([jnp.concatenate([w[0], z], axis=0), jnp.concatenate([z, w[1]], axis=0)])


def _head_expand_matrix():
    e = np.zeros((2, LANES, M_INNER), np.float32)
    for d in range(2):
        for h in range(M_HEADS):
            e[d, d * M_HEADS + h, h * M_HEADDIM:(h + 1) * M_HEADDIM] = 1.0
    return jnp.asarray(e, dtype=BF16)


def kernel(x, c, ada_w, ada_b, norm_w, ffn_w13, ffn_w2, w_in, rwkv_mu, rwkv_w0, rwkv_w2, rwkv_a0, rwkv_a2, rwkv_g2, rwkv_k_k, rwkv_k_a, rwkv_r_k, rwkv_ln_w, rwkv_ln_b, rwkv_v0, rwkv_v1, rwkv_v2, mamba_conv_w, mamba_conv_b, mamba_dt_bias, mamba_a_log, mamba_d, mamba_norm_w, diff_lambda, diff_subln_w, w_branch_rwkv, w_branch_mamba, w_branch_diff, w_out):
    bsz, seq, d = x.shape
    depth = ada_w.shape[0]
    t = bsz * seq
    x2 = x.reshape(t, d)
    mod_all = _mod_call(c, ada_w, ada_b)
    emat = _head_expand_matrix()
    slopes = jnp.broadcast_to(
        jnp.asarray(2.0 ** (-8.0 * np.arange(1, DF_HEADS + 1) / DF_HEADS), F32)[:, None, None], (DF_HEADS, 1, LANES))
    v_first = None
    for l in range(depth):
        mod = mod_all[l].reshape(bsz, N_SUB, 3, d)
        x2 = _ffn_call(x2, mod[:, 0], norm_w[l, 0:2], ffn_w13[l, 0].astype(BF16), ffn_w2[l, 0].astype(BF16), seq)

        wl = w_in[l]
        o_m = RW_COLS
        o_d = o_m + M_COLS
        o_g = o_d + DF_COLS
        w_cat = jnp.concatenate([
            wl[:, 0:3 * RW_DIM],
            _pad_cols(wl[:, 3 * RW_DIM:RW_COLS], RW_CODES_PAD),
            wl[:, o_m:o_m + M_INNER],
            wl[:, o_m + M_INNER:o_m + M_INNER + M_XBC],
            _pad_cols(wl[:, o_m + M_INNER + M_XBC:o_d], INPROJ_TN),
            wl[:, o_d:o_g],
            wl[:, o_g:],
        ], axis=1).astype(BF16)
        p_rkv, p_codes, p_z, p_xbc, p_dt, p_qkv, p_gate = _inproj_call(
            x2, mod[:, 1], norm_w[l, 2:3], w_cat,
            (3 * RW_DIM, RW_CODES_PAD, M_INNER, M_XBC, INPROJ_TN, DF_COLS, GATE_COLS),
            (BF16, F32, BF16, BF16, F32, BF16, BF16), seq)

        mu = rwkv_mu[l]
        mu_p = mu[None, 0:3 * RW_DIM]
        mu_c = _pad_cols(mu[None, 3 * RW_DIM:], RW_CODES_PAD)
        v0 = rwkv_v0[l - 1] if l > 0 else jnp.zeros((RW_DIM,), F32)
        pvec = jnp.stack([rwkv_w0[l, 0], rwkv_w0[l, 1], rwkv_a0[l, 0], rwkv_a0[l, 1],
                          rwkv_k_k[l], rwkv_k_a[l], v0, jnp.zeros((RW_DIM,), F32)])
        g2p = _pad_rows(rwkv_g2[l], 256).astype(BF16)
        vres = None
        if l > 0:
            vres = (v_first, _pad_cols(rwkv_v1[l - 1], LANES).astype(BF16),
                    _pad_rows(rwkv_v2[l - 1], LANES).astype(BF16))
        r, v, kk, k_f, k_b, b_f, b_b, lw_f, lw_b, gate = _rwkv_prep_call(
            p_rkv, p_codes, mu_p, mu_c, pvec, _dir_padded(rwkv_w2[l]).astype(BF16), _dir_padded(rwkv_a2[l]).astype(BF16), g2p, vres, seq)
        if l == 0:
            v_first = v
        y_f, y_b = _rwkv_scan_call(r, v, kk, k_f, b_f, lw_f, k_b, b_b, lw_b, bsz, seq)
        pvec2 = jnp.concatenate([jnp.stack([rwkv_ln_w[l], rwkv_ln_b[l], rwkv_r_k[l].reshape(RW_DIM)]),
                                 jnp.zeros((5, RW_DIM), F32)])
        y_r = _rwkv_post_call(y_f, y_b, r, k_f, k_b, v, gate, pvec2)

        conv_w = _pad_rows(mamba_conv_w[l], 8)
        xs, bm, cm = _mamba_conv_call(p_xbc, conv_w, mamba_conv_b[l][None], seq)
        dtb = _pad_cols(mamba_dt_bias[l].reshape(1, 2 * M_HEADS), LANES)
        alog_e = jnp.repeat(mamba_a_log[l], M_HEADDIM, axis=-1)[:, None]
        alog_c = _pad_cols(mamba_a_log[l].reshape(1, 2 * M_HEADS), LANES)
        ym_f = _ssd_call(xs, bm, cm, p_dt, dtb, emat, alog_e, alog_c, bsz, seq, False)
        ym_b = _ssd_call(xs, bm, cm, p_dt, dtb, emat, alog_e, alog_c, bsz, seq, True)
        d_e = jnp.repeat(mamba_d[l], M_HEADDIM)[None]
        y_m = _mamba_post_call(ym_f, ym_b, xs, p_z, d_e, mamba_norm_w[l][None])

        lambda_init = 0.8 - 0.6 * math.exp(-0.3 * l)
        y_d = _diff_attn_call(p_qkv, diff_lambda[l], slopes, diff_subln_w[l][None], bsz, seq, lambda_init)

        x2 = _merge_call(y_r, y_m, y_d, p_gate, x2, mod[:, 1], norm_w[l, 3:4],
                         w_branch_rwkv[l].astype(BF16), w_branch_mamba[l].astype(BF16),
                         w_branch_diff[l].astype(BF16), w_out[l].astype(BF16), seq)

        x2 = _ffn_call(x2, mod[:, 2], norm_w[l, 4:6], ffn_w13[l, 1].astype(BF16), ffn_w2[l, 1].astype(BF16), seq)
    return x2.reshape(bsz, seq, d)
```

```python
import functools
import math

import numpy as np
import jax
import jax.numpy as jnp
from jax import lax
from jax.experimental import pallas as pl
from jax.experimental.pallas import tpu as pltpu

F32 = jnp.float32
BF16 = jnp.bfloat16
HI = lax.Precision.HIGHEST

D_MODEL = 1024
N_SUB = 3
EPS = 1e-6
LANES = 128
SUBLANES = 8
HALO = 2 * SUBLANES
VMEM_LIMIT = 56 * 1024 * 1024

RW_HEAD = 64
RW_DIM = 1024
W_LORA = 64
A_LORA = 64
V_LORA = 32
G_LORA = 160
RW_GN_EPS = 64e-5
RW_COLS = 3 * RW_DIM + 2 * W_LORA + 2 * A_LORA + G_LORA
RW_CODES_PAD = 512
RW_CHUNK = 64
M_INNER = 2048
M_HEADS = 32
M_HEADDIM = 64
M_GROUPS = 4
M_STATE = 128
M_CONV = 5
M_CHUNK = 128
M_XBC = M_INNER + 2 * M_GROUPS * M_STATE
M_COLS = M_INNER + M_XBC + 2 * M_HEADS
M_GROUP_W = M_INNER // M_GROUPS
DF_HEADS = 8
DF_HEAD = 64
DF_V = 128
DF_COLS = 3 * DF_HEADS * 2 * DF_HEAD
GATE_COLS = 3 * D_MODEL
D_FF = 2816
NEG_BIG = -1e30


def _cparams(*sem):
    return pltpu.CompilerParams(dimension_semantics=sem, vmem_limit_bytes=VMEM_LIMIT)


def _dot(a, b, **kw):
    return jnp.dot(a, b, preferred_element_type=F32, **kw)


def _dot_nt(a, b, **kw):
    return lax.dot_general(a, b, (((1,), (1,)), ((), ())), preferred_element_type=F32, **kw)


def _split2(x):
    hi = x.astype(BF16)
    return hi, (x - hi.astype(F32)).astype(BF16)


def _dot_sel(x, m_b):
    hi, lo = _split2(x)
    return _dot(hi, m_b) + _dot(lo, m_b)


def _sel_dot(m_b, x):
    hi, lo = _split2(x)
    return _dot(m_b, hi) + _dot(m_b, lo)


def _bdot(a, b):
    return _dot(a.astype(BF16), b.astype(BF16))


def _sigmoid(x):
    return 1.0 / (1.0 + jnp.exp(-x))


def _silu(x):
    return x * _sigmoid(x)


def _softplus(x):
    return jnp.maximum(x, 0.0) + jnp.log(1.0 + jnp.exp(-jnp.abs(x)))


def _rms(x, w, eps):
    return x * lax.rsqrt(jnp.mean(x * x, axis=-1, keepdims=True) + eps) * w


def _norm_mod(x, nw, shift, scale):
    return _rms(x, nw, EPS) * (1.0 + scale) + shift


def _pair_ones(scale=1.0):
    r = lax.broadcasted_iota(jnp.int32, (LANES, LANES), 0) // RW_HEAD
    c = lax.broadcasted_iota(jnp.int32, (LANES, LANES), 1) // RW_HEAD
    return jnp.where(r == c, scale, 0.0).astype(BF16)


def _shift_rows(cur, prevz, nextz, d):
    tm = cur.shape[0]
    rolled = pltpu.roll(cur, (-d) % tm, axis=0)
    rid = lax.broadcasted_iota(jnp.int32, (SUBLANES, cur.shape[1]), 0)
    if d < 0:
        fix = pltpu.roll(prevz, (-d) % SUBLANES, axis=0)
        top = jnp.where(rid < -d, fix, rolled[0:SUBLANES])
        return jnp.concatenate([top, rolled[SUBLANES:]], axis=0)
    fix = pltpu.roll(nextz, (SUBLANES - d) % SUBLANES, axis=0)
    bot = jnp.where(rid >= SUBLANES - d, fix, rolled[tm - SUBLANES:])
    return jnp.concatenate([rolled[:tm - SUBLANES], bot], axis=0)


def _shift_rows_mxu(cur_b, prevz, nextz, d):
    assert cur_b.dtype == BF16
    tm = cur_b.shape[0]
    ri = lax.broadcasted_iota(jnp.int32, (tm, tm), 0)
    ci = lax.broadcasted_iota(jnp.int32, (tm, tm), 1)
    shifted = _dot(jnp.where(ci == ri + d, 1.0, 0.0).astype(BF16), cur_b)
    rid = lax.broadcasted_iota(jnp.int32, (SUBLANES, cur_b.shape[1]), 0)
    if d < 0:
        fix = jnp.where(rid < -d, pltpu.roll(prevz, (-d) % SUBLANES, axis=0), 0.0)
        return jnp.concatenate([shifted[0:SUBLANES] + fix, shifted[SUBLANES:]], axis=0)
    fix = jnp.where(rid >= SUBLANES - d, pltpu.roll(nextz, (SUBLANES - d) % SUBLANES, axis=0), 0.0)
    return jnp.concatenate([shifted[:tm - SUBLANES], shifted[tm - SUBLANES:] + fix], axis=0)


def _halo_specs(tm, width, n_rows):
    rb = tm // HALO
    last = n_rows // HALO - 1
    return [
        pl.BlockSpec((tm, width), lambda i: (i, 0)),
        pl.BlockSpec((HALO, width), lambda i: (jnp.maximum(i * rb - 1, 0), 0)),
        pl.BlockSpec((HALO, width), lambda i: (jnp.minimum((i + 1) * rb, last), 0)),
    ]


def _halo_values(cur_ref, prev_ref, next_ref, tm, seq):
    i = pl.program_id(0)
    first = (i * tm) % seq == 0
    last = ((i + 1) * tm) % seq == 0
    prevz = jnp.where(first, 0.0, prev_ref[...].astype(F32)[HALO - SUBLANES:])
    nextz = jnp.where(last, 0.0, next_ref[...].astype(F32)[:SUBLANES])
    return cur_ref[...].astype(F32), prevz, nextz


def _mod_kernel(c_ref, w_ref, b_ref, o_ref):
    o_ref[0] = _dot(_silu(c_ref[...]), w_ref[0], precision=HI) + b_ref[0]


def _mod_call(c, ada_w, ada_b):
    depth, d, n = ada_w.shape
    bsz = c.shape[0]
    tn = 1152
    return pl.pallas_call(
        _mod_kernel,
        grid=(depth, n // tn),
        in_specs=[pl.BlockSpec((bsz, d), lambda l, j: (0, 0)),
                  pl.BlockSpec((1, d, tn), lambda l, j: (l, 0, j)),
                  pl.BlockSpec((1, 1, tn), lambda l, j: (l, 0, j))],
        out_specs=pl.BlockSpec((1, bsz, tn), lambda l, j: (l, 0, j)),
        out_shape=jax.ShapeDtypeStruct((depth, bsz, n), F32),
        compiler_params=_cparams("parallel", "parallel"),
        name="adaln_mod",
    )(c, ada_w, ada_b.reshape(depth, 1, n))


def _ffn_kernel(x_ref, mod_ref, nw_ref, wg_ref, wu_ref, w2_ref, o_ref, h_scr, acc_scr):
    j = pl.program_id(1)

    @pl.when(j == 0)
    def _():
        m = mod_ref[0]
        h_scr[...] = _norm_mod(x_ref[...], nw_ref[0:1], m[0:1], m[1:2]).astype(BF16)
        acc_scr[...] = jnp.zeros_like(acc_scr)

    h = h_scr[...]
    g = _dot(h, wg_ref[...])
    u = _dot(h, wu_ref[...])
    acc_scr[...] += _dot((_silu(g) * u).astype(BF16), w2_ref[...])

    @pl.when(j == pl.num_programs(1) - 1)
    def _():
        m = mod_ref[0]
        o_ref[...] = x_ref[...] + 0.5 * m[2:3] * _rms(acc_scr[...], nw_ref[1:2], EPS)


def _ffn_call(x2, mod_sub, nw2, w13, w2, seq):
    t, d = x2.shape
    dff = w2.shape[0]
    tm, tf = 1024, 256
    tm = min(tm, seq)
    nf = dff // tf
    return pl.pallas_call(
        _ffn_kernel,
        grid=(t // tm, nf),
        in_specs=[pl.BlockSpec((tm, d), lambda i, j: (i, 0)),
                  pl.BlockSpec((1, 3, d), lambda i, j: (i * tm // seq, 0, 0)),
                  pl.BlockSpec((2, d), lambda i, j: (0, 0)),
                  pl.BlockSpec((d, tf), lambda i, j: (0, j)),
                  pl.BlockSpec((d, tf), lambda i, j: (0, nf + j)),
                  pl.BlockSpec((tf, d), lambda i, j: (j, 0))],
        out_specs=pl.BlockSpec((tm, d), lambda i, j: (i, 0)),
        out_shape=jax.ShapeDtypeStruct((t, d), F32),
        scratch_shapes=[pltpu.VMEM((tm, d), BF16), pltpu.VMEM((tm, d), F32)],
        compiler_params=_cparams("parallel", "arbitrary"),
        name="swiglu_halfstep",
    )(x2, mod_sub, nw2, w13, w13, w2)


INPROJ_TN = 512


def _inproj_kernel(x_ref, mod_ref, nw_ref, w_ref, *rest, starts):
    o_refs, h_scr = rest[:-1], rest[-1]
    j = pl.program_id(1)

    @pl.when(j == 0)
    def _():
        m = mod_ref[0]
        h_scr[...] = _norm_mod(x_ref[...], nw_ref[...], m[0:1], m[1:2]).astype(BF16)

    for k, o_ref in enumerate(o_refs):
        @pl.when((j >= starts[k]) & (j < starts[k + 1]))
        def _(o_ref=o_ref):
            o_ref[...] = _dot(h_scr[...], w_ref[...]).astype(o_ref.dtype)


def _inproj_call(x2, mod_sub, nw, w, widths, dtypes, seq):
    t, d = x2.shape
    tn = INPROJ_TN
    tm = min(1024, seq)
    starts = [0]
    for wd in widths:
        starts.append(starts[-1] + wd // tn)

    def out_spec(k):
        return pl.BlockSpec((tm, tn), lambda i, j: (i, jnp.clip(j - starts[k], 0, widths[k] // tn - 1)))

    return pl.pallas_call(
        functools.partial(_inproj_kernel, starts=tuple(starts)),
        grid=(t // tm, starts[-1]),
        in_specs=[pl.BlockSpec((tm, d), lambda i, j: (i, 0)),
                  pl.BlockSpec((1, 3, d), lambda i, j: (i * tm // seq, 0, 0)),
                  pl.BlockSpec((1, d), lambda i, j: (0, 0)),
                  pl.BlockSpec((d, tn), lambda i, j: (0, j))],
        out_specs=[out_spec(k) for k in range(len(widths))],
        out_shape=[jax.ShapeDtypeStruct((t, wd), dt) for wd, dt in zip(widths, dtypes)],
        scratch_shapes=[pltpu.VMEM((tm, d), BF16)],
        compiler_params=_cparams("parallel", "arbitrary"),
        name="norm_inproj",
    )(x2, mod_sub, nw, w)


def _rwkv_prep_kernel(*refs, tm, seq, has_vres):
    (p_ref, pp_ref, pn_ref, c_ref, cp_ref, cn_ref, mup_ref, muc_ref, pvec_ref,
     w2_ref, a2_ref, g2_ref) = refs[:12]
    refs = refs[12:]
    if has_vres:
        vf_ref, v1_ref, v2_ref = refs[:3]
        refs = refs[3:]
    r_o, v_o, kk_o, kf_o, kb_o, bf_o, bb_o, lwf_o, lwb_o, g_o = refs

    def shift_mix(cur_ref, prev_ref, next_ref, mu):
        cur, prevz, nextz = _halo_values(cur_ref, prev_ref, next_ref, tm, seq)
        if cur_ref.dtype == BF16:
            nb = 0.5 * (_shift_rows_mxu(cur_ref[...], prevz, nextz, -1) + _shift_rows_mxu(cur_ref[...], prevz, nextz, 1))
        else:
            nb = 0.5 * (_shift_rows(cur, prevz, nextz, -1) + _shift_rows(cur, prevz, nextz, 1))
        return cur + mu * (nb - cur)

    p = shift_mix(p_ref, pp_ref, pn_ref, mup_ref[...])
    codes = shift_mix(c_ref, cp_ref, cn_ref, muc_ref[...])
    r = p[:, 0:RW_DIM]
    k = p[:, RW_DIM:2 * RW_DIM]
    v = p[:, 2 * RW_DIM:3 * RW_DIM]
    cw = jnp.tanh(codes[:, 0:2 * W_LORA])
    ca = codes[:, 2 * W_LORA:2 * W_LORA + 2 * A_LORA]
    cg = _sigmoid(codes[:, 2 * W_LORA + 2 * A_LORA:])
    pv = pvec_ref[...]
    if has_vres:
        lo = _bdot(_bdot(v, v1_ref[...]), v2_ref[...])
        v = v + (vf_ref[...].astype(F32) - v) * _sigmoid(pv[6:7] + lo)
    r_o[...] = r.astype(r_o.dtype)
    v_o[...] = v.astype(v_o.dtype)
    g_o[...] = _bdot(cg, g2_ref[...]).astype(g_o.dtype)
    iclr = []
    for d, lw_o in ((0, lwf_o), (1, lwb_o)):
        lw_o[...] = -math.exp(-0.5) * _sigmoid(pv[d:d + 1] + _bdot(cw, w2_ref[d]))
        iclr.append(_sigmoid(pv[2 + d:3 + d] + _bdot(ca, a2_ref[d])))
    kf_o[...] = (k * (1.0 + (iclr[0] - 1.0) * pv[5:6])).astype(kf_o.dtype)
    kb_o[...] = (k * (1.0 + (iclr[1] - 1.0) * pv[5:6])).astype(kb_o.dtype)
    ones2 = _pair_ones()
    kkr = k * pv[4:5]
    for cb in range(RW_DIM // LANES):
        sl = slice(cb * LANES, (cb + 1) * LANES)
        blk = kkr[:, sl]
        kkn = blk * lax.rsqrt(_dot_sel(blk * blk, ones2) + 1e-12)
        kk_o[:, sl] = kkn.astype(kk_o.dtype)
        bf_o[:, sl] = (kkn * iclr[0][:, sl]).astype(bf_o.dtype)
        bb_o[:, sl] = (kkn * iclr[1][:, sl]).astype(bb_o.dtype)


def _rwkv_prep_call(rkv, codes, mu_p, mu_c, pvec, w2p, a2p, g2p, vres, seq):
    t = rkv.shape[0]
    tm = 128
    has_vres = vres is not None
    full = lambda shape: pl.BlockSpec(shape, lambda i: (0,) * len(shape))
    in_specs = (_halo_specs(tm, 3 * RW_DIM, t) + _halo_specs(tm, RW_CODES_PAD, t)
                + [full((1, 3 * RW_DIM)), full((1, RW_CODES_PAD)), full((8, RW_DIM)),
                   full((2, 2 * W_LORA, RW_DIM)), full((2, 2 * A_LORA, RW_DIM)), full((256, RW_DIM))])
    args = [rkv, rkv, rkv, codes, codes, codes, mu_p, mu_c, pvec, w2p, a2p, g2p]
    if has_vres:
        in_specs += [pl.BlockSpec((tm, RW_DIM), lambda i: (i, 0)), full((RW_DIM, LANES)), full((LANES, RW_DIM))]
        args += list(vres)
    dtypes = [BF16] * 7 + [F32, F32, BF16]
    return pl.pallas_call(
        functools.partial(_rwkv_prep_kernel, tm=tm, seq=seq, has_vres=has_vres),
        grid=(t // tm,),
        in_specs=in_specs,
        out_specs=[pl.BlockSpec((tm, RW_DIM), lambda i: (i, 0))] * 10,
        out_shape=[jax.ShapeDtypeStruct((t, RW_DIM), dt) for dt in dtypes],
        compiler_params=_cparams("parallel"),
        name="rwkv_prep",
    )(*args)


def _rwkv_scan_kernel(rf_ref, vf_ref, kkf_ref, kf_ref, bf_ref, lwf_ref, rb_ref, vb_ref, kkb_ref, kb_ref, bb_ref,
                      lwb_ref, yf_ref, yb_ref, h_scr, *, n_chunks):
    c = RW_CHUNK
    n_pairs = RW_DIM // LANES

    @pl.when(pl.program_id(1) == 0)
    def _():
        h_scr[...] = jnp.zeros_like(h_scr)

    ri = lax.broadcasted_iota(jnp.int32, (LANES, LANES), 0)
    ci = lax.broadcasted_iota(jnp.int32, (LANES, LANES), 1)
    same = (ri // c) == (ci // c)
    tr, tc = ri % c, ci % c
    strict = [same & (tc < tr), same & (tc > tr)]
    incl = [same & (tc <= tr), same & (tc >= tr)]
    blk = [(ri // w) == (ci // w) for w in (8, 16, 32, 64)]
    eye = jnp.where(ri == ci, 1.0, 0.0).astype(F32)
    r64 = lax.broadcasted_iota(jnp.int32, (c, c), 0)
    c64 = lax.broadcasted_iota(jnp.int32, (c, c), 1)
    tri = [jnp.where(c64 <= r64, 1.0, 0.0).astype(BF16), jnp.where(c64 >= r64, 1.0, 0.0).astype(BF16)]
    head0 = lax.broadcasted_iota(jnp.int32, (c, LANES), 1) < RW_HEAD
    end_row = [c - 1, 0]
    refs = [(rf_ref, vf_ref, kkf_ref, kf_ref, bf_ref, lwf_ref, yf_ref),
            (rb_ref, vb_ref, kkb_ref, kb_ref, bb_ref, lwb_ref, yb_ref)]
    lns = [slice(pr * LANES, (pr + 1) * LANES) for pr in range(n_pairs)]
    chains = [(d, p) for p in range(n_pairs) for d in range(2)]

    def stack(x):
        return jnp.concatenate([jnp.where(head0, x, 0.0), jnp.where(head0, 0.0, x)], axis=0)

    def chunk(step, carry):
        rows = [pl.ds(pl.multiple_of(step * c, c), c), pl.ds(pl.multiple_of((n_chunks - 1 - step) * c, c), c)]
        wide = []
        for d in range(2):
            r_ref, v_ref, kk_ref, k_ref, b_ref, lw_ref, _ = refs[d]
            lw = lw_ref[rows[d], :]
            ld = lambda ref: ref[rows[d], :].astype(F32)
            cum = _sel_dot(tri[d], lw)
            tot = cum[end_row[d]:end_row[d] + 1]
            g_inv, g_end = jnp.exp(-cum), jnp.exp(tot - cum)
            k_all, b_all = ld(k_ref), ld(b_ref)
            wide.append(dict(at=-ld(kk_ref) * jnp.exp(cum - lw), rt=ld(r_ref) * jnp.exp(cum),
                             bt=b_all * g_inv, kt=k_all * g_inv, bg=b_all * g_end, kg=k_all * g_end,
                             v=ld(v_ref), g_tot=jnp.exp(tot)))
        ar_s = [jnp.concatenate([stack(wide[d]["at"][:, lns[p]]), stack(wide[d]["rt"][:, lns[p]])], axis=0)
                .astype(BF16) for d, p in chains]
        bk_s = [jnp.concatenate([stack(wide[d]["bt"][:, lns[p]]), stack(wide[d]["kt"][:, lns[p]])], axis=0)
                .astype(BF16) for d, p in chains]
        v_s = [stack(wide[d]["v"][:, lns[p]]) for d, p in chains]
        ht = [h_scr[d, p] for d, p in chains]
        idx = range(len(chains))
        gram = [_dot_nt(ar_s[i], bk_s[i]) for i in idx]
        a_ab = [jnp.where(strict[d], gram[i][0:LANES, 0:LANES], 0.0) for i, (d, p) in enumerate(chains)]
        lhs = [jnp.concatenate([
            ar_s[i],
            jnp.concatenate([jnp.where(strict[d], gram[i][0:LANES, LANES:], 0.0),
                             jnp.where(incl[d], gram[i][LANES:, LANES:], 0.0)], axis=0).astype(BF16)], axis=1)
            for i, (d, p) in enumerate(chains)]
        rhs = [jnp.concatenate([ht[i].T, v_s[i]], axis=0).astype(BF16) for i in idx]
        xy0 = [_dot(lhs[i], rhs[i]) for i in idx]
        a_rb = [jnp.where(incl[d], gram[i][LANES:, 0:LANES], 0.0).astype(BF16) for i, (d, p) in enumerate(chains)]
        p1 = [jnp.where(blk[0], a_ab[i], 0.0).astype(BF16) for i in idx]
        p2 = [_dot(p1[i], p1[i]) for i in idx]
        p2_b = [x.astype(BF16) for x in p2]
        p4_b = [_dot(p2_b[i], p2_b[i]).astype(BF16) for i in idx]
        t_inv = [eye + p1[i] + p2[i] + _dot(p1[i], p2_b[i]) for i in idx]
        t_inv = [t_inv[i] + _dot(t_inv[i].astype(BF16), p4_b[i]) for i in idx]
        for lvl in range(1, 4):
            off = [jnp.where(blk[lvl] & ~blk[lvl - 1], a_ab[i], 0.0).astype(BF16) for i in idx]
            t_b = [x.astype(BF16) for x in t_inv]
            mid = [_dot(t_b[i], off[i]).astype(BF16) for i in idx]
            t_inv = [t_inv[i] + _dot(mid[i], t_b[i]) for i in idx]
        u = [_dot(t_inv[i].astype(BF16), xy0[i][0:LANES].astype(BF16)) for i in idx]
        y = [xy0[i][LANES:] + _dot(a_rb[i], u[i].astype(BF16)) for i in idx]
        for d in range(2):
            ys = [y[i][0:c] + y[i][c:2 * c] for i, (dd, p) in enumerate(chains) if dd == d]
            refs[d][6][rows[d], :] = jnp.concatenate(ys, axis=1).astype(refs[d][6].dtype)
        for i, (d, p) in enumerate(chains):
            uv_t = jnp.concatenate([u[i], v_s[i]], axis=0).T.astype(BF16)
            bkg = jnp.concatenate([stack(wide[d]["bg"][:, lns[p]]), stack(wide[d]["kg"][:, lns[p]])],
                                  axis=0).astype(BF16)
            h_scr[d, p] = ht[i] * wide[d]["g_tot"][:, lns[p]] + _dot(uv_t, bkg)
        return carry

    lax.fori_loop(0, n_chunks, chunk, 0)


def _rwkv_scan_call(r, v, kk, k_f, b_f, lw_f, k_b, b_b, lw_b, bsz, seq):
    t = r.shape[0]
    tb = min(2 * RW_CHUNK, seq)
    nt = seq // tb
    fwd = pl.BlockSpec((tb, RW_DIM), lambda bi, ti: (bi * nt + ti, 0))
    bwd = pl.BlockSpec((tb, RW_DIM), lambda bi, ti: (bi * nt + nt - 1 - ti, 0))
    out = jax.ShapeDtypeStruct((t, RW_DIM), BF16)
    return pl.pallas_call(
        functools.partial(_rwkv_scan_kernel, n_chunks=tb // RW_CHUNK),
        grid=(bsz, nt),
        in_specs=[fwd] * 6 + [bwd] * 6,
        out_specs=[fwd, bwd],
        out_shape=[out, out],
        scratch_shapes=[pltpu.VMEM((2, RW_DIM // LANES, LANES, LANES), F32)],
        compiler_params=_cparams("parallel", "arbitrary"),
        name="rwkv_scan",
    )(r, v, kk, k_f, b_f, lw_f, r, v, kk, k_b, b_b, lw_b)


def _rwkv_post_kernel(yf_ref, yb_ref, r_ref, kf_ref, kb_ref, v_ref, g_ref, pvec_ref, o_ref):
    mean2 = _pair_ones(1.0 / RW_HEAD)
    ones2 = _pair_ones()
    pv = pvec_ref[...]
    for cb in range(RW_DIM // LANES):
        sl = slice(cb * LANES, (cb + 1) * LANES)
        ld = lambda ref: ref[:, sl].astype(F32)
        y = ld(yf_ref) + ld(yb_ref)
        dlt = y - _dot_sel(y, mean2)
        var = _dot_sel(dlt * dlt, mean2)
        yn = dlt * lax.rsqrt(var + RW_GN_EPS) * pv[0:1, sl] + pv[1:2, sl]
        rk = ld(r_ref) * (ld(kf_ref) + ld(kb_ref)) * pv[2:3, sl]
        bonus = _dot_sel(rk, ones2) * ld(v_ref)
        o_ref[:, sl] = ((yn + bonus) * ld(g_ref)).astype(o_ref.dtype)


def _rwkv_post_call(yf, yb, r, kf, kb, v, g, pvec):
    t = r.shape[0]
    tm = 256
    spec = pl.BlockSpec((tm, RW_DIM), lambda i: (i, 0))
    return pl.pallas_call(
        _rwkv_post_kernel,
        grid=(t // tm,),
        in_specs=[spec] * 7 + [pl.BlockSpec((8, RW_DIM), lambda i: (0, 0))],
        out_specs=spec,
        out_shape=jax.ShapeDtypeStruct((t, RW_DIM), BF16),
        compiler_params=_cparams("parallel"),
        name="rwkv_post",
    )(yf, yb, r, kf, kb, v, g, pvec)


def _mamba_conv_kernel(x_ref, xp_ref, xn_ref, w_ref, b_ref, xs_o, bm_o, cm_o, *, tm, seq):
    cur, prevz, nextz = _halo_values(x_ref, xp_ref, xn_ref, tm, seq)
    w = w_ref[...]
    half = (M_CONV - 1) // 2
    acc = cur * w[half:half + 1] + b_ref[...]
    for tap in range(M_CONV):
        if tap != half:
            acc = acc + _shift_rows_mxu(x_ref[...], prevz, nextz, tap - half) * w[tap:tap + 1]
    y = _silu(acc)
    xs_o[...] = y[:, 0:M_INNER].astype(xs_o.dtype)
    bm_o[...] = y[:, M_INNER:M_INNER + M_GROUPS * M_STATE].astype(bm_o.dtype)
    cm_o[...] = y[:, M_INNER + M_GROUPS * M_STATE:].astype(cm_o.dtype)


def _mamba_conv_call(xbc, conv_w, conv_b, seq):
    t = xbc.shape[0]
    tm = 128
    gs = M_GROUPS * M_STATE
    return pl.pallas_call(
        functools.partial(_mamba_conv_kernel, tm=tm, seq=seq),
        grid=(t // tm,),
        in_specs=_halo_specs(tm, M_XBC, t) + [pl.BlockSpec((8, M_XBC), lambda i: (0, 0)),
                                             pl.BlockSpec((1, M_XBC), lambda i: (0, 0))],
        out_specs=[pl.BlockSpec((tm, M_INNER), lambda i: (i, 0)),
                   pl.BlockSpec((tm, gs), lambda i: (i, 0)),
                   pl.BlockSpec((tm, gs), lambda i: (i, 0))],
        out_shape=[jax.ShapeDtypeStruct((t, M_INNER), BF16),
                   jax.ShapeDtypeStruct((t, gs), BF16),
                   jax.ShapeDtypeStruct((t, gs), BF16)],
        compiler_params=_cparams("parallel"),
        name="mamba_conv",
    )(xbc, xbc, xbc, conv_w, conv_b)


def _ssd_kernel(xs_ref, bm_ref, cm_ref, dt_ref, dtb_ref, e_ref, aloge_ref, alogc_ref, y_ref, h_scr, *, reverse):
    c = M_CHUNK
    d = 1 if reverse else 0
    n_pairs = M_INNER // LANES
    ppg = n_pairs // M_GROUPS

    @pl.when(pl.program_id(1) == 0)
    def _():
        h_scr[...] = jnp.zeros_like(h_scr)

    ri = lax.broadcasted_iota(jnp.int32, (c, c), 0)
    ci = lax.broadcasted_iota(jnp.int32, (c, c), 1)
    keep = (ci >= ri) if reverse else (ci <= ri)
    tri = jnp.where(keep, 1.0, 0.0).astype(BF16)
    end_row = 0 if reverse else c - 1
    head0 = lax.broadcasted_iota(jnp.int32, (c, LANES), 1) < M_HEADDIM

    dt = _softplus(dt_ref[...] + dtb_ref[...])
    acs_c = _sel_dot(tri, dt * (-jnp.exp(alogc_ref[...])))
    acs_ct = acs_c.T
    dte = _dot_sel(dt, e_ref[0])
    acs = _sel_dot(tri, dte * (-jnp.exp(aloge_ref[0])))
    tot = acs[end_row:end_row + 1]
    xdt = xs_ref[...].astype(F32) * dte
    e_acs = jnp.exp(acs)
    x_end = (xdt * jnp.exp(tot - acs)).astype(BF16)
    xdt_b = xdt.astype(BF16)
    dec = jnp.exp(tot)
    gsl = [slice(g * M_STATE, (g + 1) * M_STATE) for g in range(M_GROUPS)]
    bm_b = [bm_ref[:, sl] for sl in gsl]
    cm_b = [cm_ref[:, sl] for sl in gsl]
    bm_t = [bm_ref[:, sl].astype(F32).T.astype(BF16) for sl in gsl]
    cb = [_dot_nt(cm_b[g], bm_b[g]) for g in range(M_GROUPS)]
    prs = range(n_pairs)
    lns = [slice(p * LANES, (p + 1) * LANES) for p in prs]
    ys = []
    for p in prs:
        pair = []
        for hh in range(2):
            idx = d * M_HEADS + 2 * p + hh
            seg = acs_c[:, idx:idx + 1] - acs_ct[idx:idx + 1, :]
            lmat = jnp.exp(jnp.where(keep, seg, NEG_BIG))
            pair.append(_dot((cb[p // ppg] * lmat).astype(BF16), xdt_b[:, lns[p]]))
        ys.append(jnp.where(head0, pair[0], pair[1]))
    h_prev = [h_scr[p] for p in prs]
    y_off = [_dot(cm_b[p // ppg], h_prev[p].astype(BF16)) for p in prs]
    y_ref[...] = (jnp.concatenate(ys, axis=1) + jnp.concatenate(y_off, axis=1) * e_acs).astype(y_ref.dtype)
    for p in prs:
        h_scr[p] = h_prev[p] * dec[:, lns[p]] + _dot(bm_t[p // ppg], x_end[:, lns[p]])


def _ssd_call(xs, bm, cm, dt_raw, dtb, emat, alog_e, alog_c, bsz, seq, reverse):
    t = xs.shape[0]
    nc = seq // M_CHUNK
    d = 1 if reverse else 0
    gs = M_GROUPS * M_STATE
    if reverse:
        row = lambda bi, ci: (bi * nc + nc - 1 - ci, 0)
    else:
        row = lambda bi, ci: (bi * nc + ci, 0)
    return pl.pallas_call(
        functools.partial(_ssd_kernel, reverse=reverse),
        grid=(bsz, nc),
        in_specs=[pl.BlockSpec((M_CHUNK, M_INNER), row),
                  pl.BlockSpec((M_CHUNK, gs), row),
                  pl.BlockSpec((M_CHUNK, gs), row),
                  pl.BlockSpec((M_CHUNK, LANES), row),
                  pl.BlockSpec((1, LANES), lambda bi, ci: (0, 0)),
                  pl.BlockSpec((1, LANES, M_INNER), lambda bi, ci: (d, 0, 0)),
                  pl.BlockSpec((1, 1, M_INNER), lambda bi, ci: (d, 0, 0)),
                  pl.BlockSpec((1, LANES), lambda bi, ci: (0, 0))],
        out_specs=pl.BlockSpec((M_CHUNK, M_INNER), row),
        out_shape=jax.ShapeDtypeStruct((t, M_INNER), BF16),
        scratch_shapes=[pltpu.VMEM((M_INNER // LANES, M_STATE, LANES), F32)],
        compiler_params=_cparams("parallel", "arbitrary"),
        name="ssd_bwd" if reverse else "ssd_fwd",
    )(xs, bm, cm, dt_raw, dtb, emat, alog_e, alog_c)


def _mamba_post_kernel(yf_ref, yb_ref, xs_ref, z_ref, d_ref, nw_ref, o_ref):
    ld = lambda ref: ref[...].astype(F32)
    y = (ld(yf_ref) + ld(yb_ref) + d_ref[...] * ld(xs_ref)) * _silu(ld(z_ref))
    for g in range(M_GROUPS):
        sl = slice(g * M_GROUP_W, (g + 1) * M_GROUP_W)
        o_ref[:, sl] = _rms(y[:, sl], nw_ref[:, sl], 1e-5).astype(o_ref.dtype)


def _mamba_post_call(yf, yb, xs, z, d_e, nw):
    t = xs.shape[0]
    tm = 256
    spec = pl.BlockSpec((tm, M_INNER), lambda i: (i, 0))
    vec = pl.BlockSpec((1, M_INNER), lambda i: (0, 0))
    return pl.pallas_call(
        _mamba_post_kernel,
        grid=(t // tm,),
        in_specs=[spec] * 4 + [vec, vec],
        out_specs=spec,
        out_shape=jax.ShapeDtypeStruct((t, M_INNER), BF16),
        compiler_params=_cparams("parallel"),
        name="mamba_post",
    )(yf, yb, xs, z, d_e, nw)


def _diff_attn_kernel(zero_ref, q_ref, k_ref, v_ref, lam_ref, slope_ref, nw_ref, o_ref, bias_scr, s_scr, *, tq, tk,
                      lambda_init):
    seq = k_ref.shape[0]
    log2e = math.log2(math.e)

    @pl.when(pl.program_id(2) == 0)
    def _():
        rows = pl.program_id(1) * tq + lax.broadcasted_iota(jnp.int32, (tq, seq), 0)
        cols = lax.broadcasted_iota(jnp.int32, (tq, seq), 1)
        bias_scr[...] = (slope_ref[0][:, 0:1] * log2e) * jnp.abs(rows - cols).astype(F32)

    q = (q_ref[...].astype(F32) * (DF_HEAD ** -0.5 * log2e)).astype(BF16)
    lv = lam_ref[...]
    lam = (jnp.exp(jnp.sum(lv[0:1] * lv[1:2], keepdims=True))
           - jnp.exp(jnp.sum(lv[2:3] * lv[3:4], keepdims=True)) + lambda_init)
    map0 = lax.broadcasted_iota(jnp.int32, (tq, LANES), 1) < DF_HEAD
    qm = [jnp.where(map0, q, jnp.zeros_like(q)), jnp.where(map0, jnp.zeros_like(q), q)]
    tiles = [slice(t * tk, (t + 1) * tk) for t in range(seq // tk)]
    rt = min(LANES, tq)
    streams = [(mp, slice(rh * rt, (rh + 1) * rt)) for rh in range(tq // rt) for mp in range(2)]
    v1 = jnp.concatenate([v_ref[...], jnp.ones((seq, LANES), BF16)], axis=1)
    zero = zero_ref[0]

    def scores_tile(st, ts, mx):
        mp, rows = st
        s = _dot_nt(qm[mp][rows], k_ref[ts, :]) - bias_scr[rows, ts]
        s_scr[mp, rows, ts] = s
        for cs in range(tk // LANES):
            blk = s[:, cs * LANES:(cs + 1) * LANES]
            mx = blk if mx is None else jnp.maximum(mx, blk)
        return mx

    def weights(st, m):
        mp, rows = st
        return _dot(jnp.exp2(s_scr[mp + zero, rows, :] - m).astype(BF16), v1)

    m, acc = {}, {}
    for i in range(len(streams) + 1):
        if i < len(streams):
            mx = None
            for ts in tiles:
                mx = scores_tile(streams[i], ts, mx)
            m[i] = jnp.max(mx, axis=-1, keepdims=True)
        if i >= 1:
            acc[i - 1] = weights(streams[i - 1], m[i - 1])
    for rh in range(tq // rt):
        om = [acc[2 * rh + mp][:, 0:LANES] / acc[2 * rh + mp][:, LANES:LANES + 1] for mp in range(2)]
        o = om[0] - lam * om[1]
        o_ref[rh * rt:(rh + 1) * rt, :] = (_rms(o, nw_ref[...], 1e-5) * (1.0 - lambda_init)).astype(o_ref.dtype)


def _diff_attn_call(qkv, lam_vecs, slopes, subln_w, bsz, seq, lambda_init):
    t = qkv.shape[0]
    tq = min(512, seq)
    nq = seq // tq
    return pl.pallas_call(
        functools.partial(_diff_attn_kernel, tq=tq, tk=min(512, seq), lambda_init=lambda_init),
        grid=(DF_HEADS, nq, bsz),
        in_specs=[pl.BlockSpec(memory_space=pltpu.SMEM),
                  pl.BlockSpec((tq, DF_V), lambda hi, qi, bi: (bi * nq + qi, hi)),
                  pl.BlockSpec((seq, DF_V), lambda hi, qi, bi: (bi, DF_HEADS + hi)),
                  pl.BlockSpec((seq, DF_V), lambda hi, qi, bi: (bi, 2 * DF_HEADS + hi)),
                  pl.BlockSpec((4, DF_HEAD), lambda hi, qi, bi: (0, 0)),
                  pl.BlockSpec((1, 1, LANES), lambda hi, qi, bi: (hi, 0, 0)),
                  pl.BlockSpec((1, DF_V), lambda hi, qi, bi: (0, 0))],
        out_specs=pl.BlockSpec((tq, DF_V), lambda hi, qi, bi: (bi * nq + qi, hi)),
        out_shape=jax.ShapeDtypeStruct((t, DF_HEADS * DF_V), BF16),
        scratch_shapes=[pltpu.VMEM((tq, seq), F32), pltpu.VMEM((2, tq, seq), F32)],
        compiler_params=_cparams("parallel", "parallel", "arbitrary"),
        name="diff_attn",
    )(jnp.zeros((1,), jnp.int32), qkv, qkv, qkv, lam_vecs, slopes, subln_w)


def _merge_kernel(yr_ref, ym_ref, yd_ref, pg_ref, x_ref, mod_ref, nw_ref, wr_ref, wm_ref, wd_ref, wo_ref, o_ref):
    d = D_MODEL
    gate = lambda g: _sigmoid(pg_ref[:, g * d:(g + 1) * d].astype(F32))
    merged = (gate(0) * _dot(yr_ref[...], wr_ref[...]) + gate(1) * _dot(ym_ref[...], wm_ref[...])
              + gate(2) * _dot(yd_ref[...], wd_ref[...]))
    y = _dot(merged.astype(BF16), wo_ref[...])
    o_ref[...] = x_ref[...] + mod_ref[0][2:3] * _rms(y, nw_ref[...], EPS)


def _merge_call(yr, ym, yd, pg, x2, mod_sub, nw, wr, wm, wd, wo, seq):
    t, d = x2.shape
    tm = 256
    row = lambda w: pl.BlockSpec((tm, w), lambda i: (i, 0))
    res = lambda a: pl.BlockSpec(a.shape, lambda i: (0, 0), pipeline_mode=pl.Buffered(1))
    return pl.pallas_call(
        _merge_kernel,
        grid=(t // tm,),
        in_specs=[row(RW_DIM), row(M_INNER), row(DF_HEADS * DF_V), row(GATE_COLS), row(d),
                  pl.BlockSpec((1, 3, d), lambda i: (i * tm // seq, 0, 0)),
                  pl.BlockSpec((1, d), lambda i: (0, 0)),
                  res(wr), res(wm), res(wd), res(wo)],
        out_specs=row(d),
        out_shape=jax.ShapeDtypeStruct((t, d), F32),
        compiler_params=_cparams("parallel"),
        name="merge_outproj",
    )(yr, ym, yd, pg, x2, mod_sub, nw, wr, wm, wd, wo)


def _pad_cols(w, n):
    return jnp.pad(w, ((0, 0), (0, n - w.shape[1])))


def _pad_rows(w, n):
    return jnp.pad(w, ((0, n - w.shape[0]), (0, 0)))


def _dir_padded(w):
    z = jnp.zeros_like(w[0])
    return jnp.stack([jnp.concatenate([w[0], z], axis=0), jnp.concatenate([z, w[1]], axis=0)])


def _head_expand_matrix():
    e = np.zeros((2, LANES, M_INNER), np.float32)
    for d in range(2):
        for h in range(M_HEADS):
            e[d, d * M_HEADS + h, h * M_HEADDIM:(h + 1) * M_HEADDIM] = 1.0
    return jnp.asarray(e, dtype=BF16)


def kernel(x, c, ada_w, ada_b, norm_w, ffn_w13, ffn_w2, w_in, rwkv_mu, rwkv_w0, rwkv_w2, rwkv_a0, rwkv_a2, rwkv_g2, rwkv_k_k, rwkv_k_a, rwkv_r_k, rwkv_ln_w, rwkv_ln_b, rwkv_v0, rwkv_v1, rwkv_v2, mamba_conv_w, mamba_conv_b, mamba_dt_bias, mamba_a_log, mamba_d, mamba_norm_w, diff_lambda, diff_subln_w, w_branch_rwkv, w_branch_mamba, w_branch_diff, w_out):
    bsz, seq, d = x.shape
    depth = ada_w.shape[0]
    t = bsz * seq
    x2 = x.reshape(t, d)
    mod_all = _mod_call(c, ada_w, ada_b)
    emat = _head_expand_matrix()
    slopes = jnp.broadcast_to(
        jnp.asarray(2.0 ** (-8.0 * np.arange(1, DF_HEADS + 1) / DF_HEADS), F32)[:, None, None], (DF_HEADS, 1, LANES))
    v_first = None
    for l in range(depth):
        mod = mod_all[l].reshape(bsz, N_SUB, 3, d)
        x2 = _ffn_call(x2, mod[:, 0], norm_w[l, 0:2], ffn_w13[l, 0].astype(BF16), ffn_w2[l, 0].astype(BF16), seq)

        wl = w_in[l]
        o_m = RW_COLS
        o_d = o_m + M_COLS
        o_g = o_d + DF_COLS
        w_cat = jnp.concatenate([
            wl[:, 0:3 * RW_DIM],
            _pad_cols(wl[:, 3 * RW_DIM:RW_COLS], RW_CODES_PAD),
            wl[:, o_m:o_m + M_INNER],
            wl[:, o_m + M_INNER:o_m + M_INNER + M_XBC],
            _pad_cols(wl[:, o_m + M_INNER + M_XBC:o_d], INPROJ_TN),
            wl[:, o_d:o_g],
            wl[:, o_g:],
        ], axis=1).astype(BF16)
        p_rkv, p_codes, p_z, p_xbc, p_dt, p_qkv, p_gate = _inproj_call(
            x2, mod[:, 1], norm_w[l, 2:3], w_cat,
            (3 * RW_DIM, RW_CODES_PAD, M_INNER, M_XBC, INPROJ_TN, DF_COLS, GATE_COLS),
            (BF16, F32, BF16, BF16, F32, BF16, BF16), seq)

        mu = rwkv_mu[l]
        mu_p = mu[None, 0:3 * RW_DIM]
        mu_c = _pad_cols(mu[None, 3 * RW_DIM:], RW_CODES_PAD)
        v0 = rwkv_v0[l - 1] if l > 0 else jnp.zeros((RW_DIM,), F32)
        pvec = jnp.stack([rwkv_w0[l, 0], rwkv_w0[l, 1], rwkv_a0[l, 0], rwkv_a0[l, 1],
                          rwkv_k_k[l], rwkv_k_a[l], v0, jnp.zeros((RW_DIM,), F32)])
        g2p = _pad_rows(rwkv_g2[l], 256).astype(BF16)
        vres = None
        if l > 0:
            vres = (v_first, _pad_cols(rwkv_v1[l - 1], LANES).astype(BF16),
                    _pad_rows(rwkv_v2[l - 1], LANES).astype(BF16))
        r, v, kk, k_f, k_b, b_f, b_b, lw_f, lw_b, gate = _rwkv_prep_call(
            p_rkv, p_codes, mu_p, mu_c, pvec, _dir_padded(rwkv_w2[l]).astype(BF16), _dir_padded(rwkv_a2[l]).astype(BF16), g2p, vres, seq)
        if l == 0:
            v_first = v
        y_f, y_b = _rwkv_scan_call(r, v, kk, k_f, b_f, lw_f, k_b, b_b, lw_b, bsz, seq)
        pvec2 = jnp.concatenate([jnp.stack([rwkv_ln_w[l], rwkv_ln_b[l], rwkv_r_k[l].reshape(RW_DIM)]),
                                 jnp.zeros((5, RW_DIM), F32)])
        y_r = _rwkv_post_call(y_f, y_b, r, k_f, k_b, v, gate, pvec2)

        conv_w = _pad_rows(mamba_conv_w[l], 8)
        xs, bm, cm = _mamba_conv_call(p_xbc, conv_w, mamba_conv_b[l][None], seq)
        dtb = _pad_cols(mamba_dt_bias[l].reshape(1, 2 * M_HEADS), LANES)
        alog_e = jnp.repeat(mamba_a_log[l], M_HEADDIM, axis=-1)[:, None]
        alog_c = _pad_cols(mamba_a_log[l].reshape(1, 2 * M_HEADS), LANES)
        ym_f = _ssd_call(xs, bm, cm, p_dt, dtb, emat, alog_e, alog_c, bsz, seq, False)
        ym_b = _ssd_call(xs, bm, cm, p_dt, dtb, emat, alog_e, alog_c, bsz, seq, True)
        d_e = jnp.repeat(mamba_d[l], M_HEADDIM)[None]
        y_m = _mamba_post_call(ym_f, ym_b, xs, p_z, d_e, mamba_norm_w[l][None])

        lambda_init = 0.8 - 0.6 * math.exp(-0.3 * l)
        y_d = _diff_attn_call(p_qkv, diff_lambda[l], slopes, diff_subln_w[l][None], bsz, seq, lambda_init)

        x2 = _merge_call(y_r, y_m, y_d, p_gate, x2, mod[:, 1], norm_w[l, 3:4],
                         w_branch_rwkv[l].astype(BF16), w_branch_mamba[l].astype(BF16),
                         w_branch_diff[l].astype(BF16), w_out[l].astype(BF16), seq)

        x2 = _ffn_call(x2, mod[:, 2], norm_w[l, 4:6], ffn_w13[l, 1].astype(BF16), ffn_w2[l, 1].astype(BF16), seq)
    return x2.reshape(bsz, seq, d)
```

```python
import functools
import math

import numpy as np
import jax
import jax.numpy as jnp
from jax import lax
from jax.experimental import pallas as pl
from jax.experimental.pallas import tpu as pltpu

F32 = jnp.float32
BF16 = jnp.bfloat16
HI = lax.Precision.HIGHEST

D_MODEL = 1024
N_SUB = 3
EPS = 1e-6
LANES = 128
SUBLANES = 8
HALO = 2 * SUBLANES
VMEM_LIMIT = 56 * 1024 * 1024

RW_HEAD = 64
RW_DIM = 1024
W_LORA = 64
A_LORA = 64
V_LORA = 32
G_LORA = 160
RW_GN_EPS = 64e-5
RW_COLS = 3 * RW_DIM + 2 * W_LORA + 2 * A_LORA + G_LORA
RW_CODES_PAD = 512
RW_CHUNK = 64
M_INNER = 2048
M_HEADS = 32
M_HEADDIM = 64
M_GROUPS = 4
M_STATE = 128
M_CONV = 5
M_CHUNK = 128
M_XBC = M_INNER + 2 * M_GROUPS * M_STATE
M_COLS = M_INNER + M_XBC + 2 * M_HEADS
M_GROUP_W = M_INNER // M_GROUPS
DF_HEADS = 8
DF_HEAD = 64
DF_V = 128
DF_COLS = 3 * DF_HEADS * 2 * DF_HEAD
GATE_COLS = 3 * D_MODEL
D_FF = 2816
NEG_BIG = -1e30


def _cparams(*sem):
    return pltpu.CompilerParams(dimension_semantics=sem, vmem_limit_bytes=VMEM_LIMIT)


def _dot(a, b, **kw):
    return jnp.dot(a, b, preferred_element_type=F32, **kw)


def _dot_nt(a, b, **kw):
    return lax.dot_general(a, b, (((1,), (1,)), ((), ())), preferred_element_type=F32, **kw)


def _split2(x):
    hi = x.astype(BF16)
    return hi, (x - hi.astype(F32)).astype(BF16)


def _dot_sel(x, m_b):
    hi, lo = _split2(x)
    return _dot(hi, m_b) + _dot(lo, m_b)


def _sel_dot(m_b, x):
    hi, lo = _split2(x)
    return _dot(m_b, hi) + _dot(m_b, lo)


def _bdot(a, b):
    return _dot(a.astype(BF16), b.astype(BF16))


def _sigmoid(x):
    return 1.0 / (1.0 + jnp.exp(-x))


def _silu(x):
    return x * _sigmoid(x)


def _softplus(x):
    return jnp.maximum(x, 0.0) + jnp.log(1.0 + jnp.exp(-jnp.abs(x)))


def _rms(x, w, eps):
    return x * lax.rsqrt(jnp.mean(x * x, axis=-1, keepdims=True) + eps) * w


def _norm_mod(x, nw, shift, scale):
    return _rms(x, nw, EPS) * (1.0 + scale) + shift


def _pair_ones(scale=1.0):
    r = lax.broadcasted_iota(jnp.int32, (LANES, LANES), 0) // RW_HEAD
    c = lax.broadcasted_iota(jnp.int32, (LANES, LANES), 1) // RW_HEAD
    return jnp.where(r == c, scale, 0.0).astype(BF16)


def _shift_rows(cur, prevz, nextz, d):
    tm = cur.shape[0]
    rolled = pltpu.roll(cur, (-d) % tm, axis=0)
    rid = lax.broadcasted_iota(jnp.int32, (SUBLANES, cur.shape[1]), 0)
    if d < 0:
        fix = pltpu.roll(prevz, (-d) % SUBLANES, axis=0)
        top = jnp.where(rid < -d, fix, rolled[0:SUBLANES])
        return jnp.concatenate([top, rolled[SUBLANES:]], axis=0)
    fix = pltpu.roll(nextz, (SUBLANES - d) % SUBLANES, axis=0)
    bot = jnp.where(rid >= SUBLANES - d, fix, rolled[tm - SUBLANES:])
    return jnp.concatenate([rolled[:tm - SUBLANES], bot], axis=0)


def _shift_rows_mxu(cur_b, prevz, nextz, d):
    assert cur_b.dtype == BF16
    tm = cur_b.shape[0]
    ri = lax.broadcasted_iota(jnp.int32, (tm, tm), 0)
    ci = lax.broadcasted_iota(jnp.int32, (tm, tm), 1)
    shifted = _dot(jnp.where(ci == ri + d, 1.0, 0.0).astype(BF16), cur_b)
    rid = lax.broadcasted_iota(jnp.int32, (SUBLANES, cur_b.shape[1]), 0)
    if d < 0:
        fix = jnp.where(rid < -d, pltpu.roll(prevz, (-d) % SUBLANES, axis=0), 0.0)
        return jnp.concatenate([shifted[0:SUBLANES] + fix, shifted[SUBLANES:]], axis=0)
    fix = jnp.where(rid >= SUBLANES - d, pltpu.roll(nextz, (SUBLANES - d) % SUBLANES, axis=0), 0.0)
    return jnp.concatenate([shifted[:tm - SUBLANES], shifted[tm - SUBLANES:] + fix], axis=0)


def _halo_specs(tm, width, n_rows):
    rb = tm // HALO
    last = n_rows // HALO - 1
    return [
        pl.BlockSpec((tm, width), lambda i: (i, 0)),
        pl.BlockSpec((HALO, width), lambda i: (jnp.maximum(i * rb - 1, 0), 0)),
        pl.BlockSpec((HALO, width), lambda i: (jnp.minimum((i + 1) * rb, last), 0)),
    ]


def _halo_values(cur_ref, prev_ref, next_ref, tm, seq):
    i = pl.program_id(0)
    first = (i * tm) % seq == 0
    last = ((i + 1) * tm) % seq == 0
    prevz = jnp.where(first, 0.0, prev_ref[...].astype(F32)[HALO - SUBLANES:])
    nextz = jnp.where(last, 0.0, next_ref[...].astype(F32)[:SUBLANES])
    return cur_ref[...].astype(F32), prevz, nextz


def _mod_kernel(c_ref, w_ref, b_ref, o_ref):
    o_ref[0] = _dot(_silu(c_ref[...]), w_ref[0], precision=HI) + b_ref[0]


def _mod_call(c, ada_w, ada_b):
    depth, d, n = ada_w.shape
    bsz = c.shape[0]
    tn = 1152
    return pl.pallas_call(
        _mod_kernel,
        grid=(depth, n // tn),
        in_specs=[pl.BlockSpec((bsz, d), lambda l, j: (0, 0)),
                  pl.BlockSpec((1, d, tn), lambda l, j: (l, 0, j)),
                  pl.BlockSpec((1, 1, tn), lambda l, j: (l, 0, j))],
        out_specs=pl.BlockSpec((1, bsz, tn), lambda l, j: (l, 0, j)),
        out_shape=jax.ShapeDtypeStruct((depth, bsz, n), F32),
        compiler_params=_cparams("parallel", "parallel"),
        name="adaln_mod",
    )(c, ada_w, ada_b.reshape(depth, 1, n))


def _ffn_kernel(x_ref, mod_ref, nw_ref, w13_ref, w2_ref, o_ref, *, tf):
    dff = w2_ref.shape[0]
    m = mod_ref[0]
    x = x_ref[...]
    h = _norm_mod(x, nw_ref[0:1], m[0:1], m[1:2]).astype(BF16)
    acc = None
    for j in range(dff // tf):
        g = _dot(h, w13_ref[:, j * tf:(j + 1) * tf])
        u = _dot(h, w13_ref[:, dff + j * tf:dff + (j + 1) * tf])
        part = _dot((_silu(g) * u).astype(BF16), w2_ref[j * tf:(j + 1) * tf, :])
        acc = part if acc is None else acc + part
    o_ref[...] = x + 0.5 * m[2:3] * _rms(acc, nw_ref[1:2], EPS)


def _ffn_call(x2, mod_sub, nw2, w13, w2, seq):
    t, d = x2.shape
    tm = min(1024, seq)
    resident = lambda a: pl.BlockSpec(a.shape, lambda i: (0, 0), pipeline_mode=pl.Buffered(1))
    return pl.pallas_call(
        functools.partial(_ffn_kernel, tf=256),
        grid=(t // tm,),
        in_specs=[pl.BlockSpec((tm, d), lambda i: (i, 0)),
                  pl.BlockSpec((1, 3, d), lambda i: (i * tm // seq, 0, 0)),
                  pl.BlockSpec((2, d), lambda i: (0, 0)),
                  resident(w13), resident(w2)],
        out_specs=pl.BlockSpec((tm, d), lambda i: (i, 0)),
        out_shape=jax.ShapeDtypeStruct((t, d), F32),
        compiler_params=_cparams("parallel"),
        name="swiglu_halfstep",
    )(x2, mod_sub, nw2, w13, w2)


INPROJ_TN = 1024


def _inproj_kernel(x_ref, mod_ref, nw_ref, w_ref, *rest, starts):
    o_refs, h_scr = rest[:-1], rest[-1]
    j = pl.program_id(1)

    @pl.when(j == 0)
    def _():
        m = mod_ref[0]
        h_scr[...] = _norm_mod(x_ref[...], nw_ref[...], m[0:1], m[1:2]).astype(BF16)

    for k, o_ref in enumerate(o_refs):
        @pl.when((j >= starts[k]) & (j < starts[k + 1]))
        def _(o_ref=o_ref):
            o_ref[...] = _dot(h_scr[...], w_ref[...]).astype(o_ref.dtype)


def _inproj_call(x2, mod_sub, nw, w, widths, dtypes, seq):
    t, d = x2.shape
    tn = INPROJ_TN
    tm = min(1024, seq)
    starts = [0]
    for wd in widths:
        starts.append(starts[-1] + wd // tn)

    def out_spec(k):
        return pl.BlockSpec((tm, tn), lambda i, j: (i, jnp.clip(j - starts[k], 0, widths[k] // tn - 1)))

    return pl.pallas_call(
        functools.partial(_inproj_kernel, starts=tuple(starts)),
        grid=(t // tm, starts[-1]),
        in_specs=[pl.BlockSpec((tm, d), lambda i, j: (i, 0)),
                  pl.BlockSpec((1, 3, d), lambda i, j: (i * tm // seq, 0, 0)),
                  pl.BlockSpec((1, d), lambda i, j: (0, 0)),
                  pl.BlockSpec((d, tn), lambda i, j: (0, j))],
        out_specs=[out_spec(k) for k in range(len(widths))],
        out_shape=[jax.ShapeDtypeStruct((t, wd), dt) for wd, dt in zip(widths, dtypes)],
        scratch_shapes=[pltpu.VMEM((tm, d), BF16)],
        compiler_params=_cparams("parallel", "arbitrary"),
        name="norm_inproj",
    )(x2, mod_sub, nw, w)


def _rwkv_prep_kernel(*refs, tm, seq, has_vres):
    (p_ref, pp_ref, pn_ref, c_ref, cp_ref, cn_ref, mup_ref, muc_ref, pvec_ref,
     w2_ref, a2_ref, g2_ref) = refs[:12]
    refs = refs[12:]
    if has_vres:
        vf_ref, v1_ref, v2_ref = refs[:3]
        refs = refs[3:]
    r_o, v_o, kk_o, kf_o, kb_o, bf_o, bb_o, lwf_o, lwb_o, g_o = refs

    def shift_mix(cur_ref, prev_ref, next_ref, mu):
        cur, prevz, nextz = _halo_values(cur_ref, prev_ref, next_ref, tm, seq)
        if cur_ref.dtype == BF16:
            nb = 0.5 * (_shift_rows_mxu(cur_ref[...], prevz, nextz, -1) + _shift_rows_mxu(cur_ref[...], prevz, nextz, 1))
        else:
            nb = 0.5 * (_shift_rows(cur, prevz, nextz, -1) + _shift_rows(cur, prevz, nextz, 1))
        return cur + mu * (nb - cur)

    p = shift_mix(p_ref, pp_ref, pn_ref, mup_ref[...])
    codes = shift_mix(c_ref, cp_ref, cn_ref, muc_ref[...])
    r = p[:, 0:RW_DIM]
    k = p[:, RW_DIM:2 * RW_DIM]
    v = p[:, 2 * RW_DIM:3 * RW_DIM]
    cw = jnp.tanh(codes[:, 0:2 * W_LORA])
    ca = codes[:, 2 * W_LORA:2 * W_LORA + 2 * A_LORA]
    cg = _sigmoid(codes[:, 2 * W_LORA + 2 * A_LORA:])
    pv = pvec_ref[...]
    if has_vres:
        lo = _bdot(_bdot(v, v1_ref[...]), v2_ref[...])
        v = v + (vf_ref[...].astype(F32) - v) * _sigmoid(pv[6:7] + lo)
    r_o[...] = r.astype(r_o.dtype)
    v_o[...] = v.astype(v_o.dtype)
    g_o[...] = _bdot(cg, g2_ref[...]).astype(g_o.dtype)
    iclr = []
    for d, lw_o in ((0, lwf_o), (1, lwb_o)):
        lw_o[...] = -math.exp(-0.5) * _sigmoid(pv[d:d + 1] + _bdot(cw, w2_ref[d]))
        iclr.append(_sigmoid(pv[2 + d:3 + d] + _bdot(ca, a2_ref[d])))
    kf_o[...] = (k * (1.0 + (iclr[0] - 1.0) * pv[5:6])).astype(kf_o.dtype)
    kb_o[...] = (k * (1.0 + (iclr[1] - 1.0) * pv[5:6])).astype(kb_o.dtype)
    ones2 = _pair_ones()
    kkr = k * pv[4:5]
    for cb in range(RW_DIM // LANES):
        sl = slice(cb * LANES, (cb + 1) * LANES)
        blk = kkr[:, sl]
        kkn = blk * lax.rsqrt(_dot_sel(blk * blk, ones2) + 1e-12)
        kk_o[:, sl] = kkn.astype(kk_o.dtype)
        bf_o[:, sl] = (kkn * iclr[0][:, sl]).astype(bf_o.dtype)
        bb_o[:, sl] = (kkn * iclr[1][:, sl]).astype(bb_o.dtype)


def _rwkv_prep_call(rkv, codes, mu_p, mu_c, pvec, w2p, a2p, g2p, vres, seq):
    t = rkv.shape[0]
    tm = 128
    has_vres = vres is not None
    full = lambda shape: pl.BlockSpec(shape, lambda i: (0,) * len(shape))
    in_specs = (_halo_specs(tm, 3 * RW_DIM, t) + _halo_specs(tm, RW_CODES_PAD, t)
                + [full((1, 3 * RW_DIM)), full((1, RW_CODES_PAD)), full((8, RW_DIM)),
                   full((2, 2 * W_LORA, RW_DIM)), full((2, 2 * A_LORA, RW_DIM)), full((256, RW_DIM))])
    args = [rkv, rkv, rkv, codes, codes, codes, mu_p, mu_c, pvec, w2p, a2p, g2p]
    if has_vres:
        in_specs += [pl.BlockSpec((tm, RW_DIM), lambda i: (i, 0)), full((RW_DIM, LANES)), full((LANES, RW_DIM))]
        args += list(vres)
    dtypes = [BF16] * 7 + [F32, F32, BF16]
    return pl.pallas_call(
        functools.partial(_rwkv_prep_kernel, tm=tm, seq=seq, has_vres=has_vres),
        grid=(t // tm,),
        in_specs=in_specs,
        out_specs=[pl.BlockSpec((tm, RW_DIM), lambda i: (i, 0))] * 10,
        out_shape=[jax.ShapeDtypeStruct((t, RW_DIM), dt) for dt in dtypes],
        compiler_params=_cparams("parallel"),
        name="rwkv_prep",
    )(*args)


def _rwkv_scan_kernel(rf_ref, vf_ref, kkf_ref, kf_ref, bf_ref, lwf_ref, rb_ref, vb_ref, kkb_ref, kb_ref, bb_ref,
                      lwb_ref, yf_ref, yb_ref, h_scr, *, n_chunks):
    c = RW_CHUNK
    n_pairs = RW_DIM // LANES

    @pl.when(pl.program_id(1) == 0)
    def _():
        h_scr[...] = jnp.zeros_like(h_scr)

    ri = lax.broadcasted_iota(jnp.int32, (LANES, LANES), 0)
    ci = lax.broadcasted_iota(jnp.int32, (LANES, LANES), 1)
    same = (ri // c) == (ci // c)
    tr, tc = ri % c, ci % c
    strict = [same & (tc < tr), same & (tc > tr)]
    incl = [same & (tc <= tr), same & (tc >= tr)]
    blk = [(ri // w) == (ci // w) for w in (8, 16, 32, 64)]
    eye = jnp.where(ri == ci, 1.0, 0.0).astype(F32)
    r64 = lax.broadcasted_iota(jnp.int32, (c, c), 0)
    c64 = lax.broadcasted_iota(jnp.int32, (c, c), 1)
    tri = [jnp.where(c64 <= r64, 1.0, 0.0).astype(BF16), jnp.where(c64 >= r64, 1.0, 0.0).astype(BF16)]
    head0 = lax.broadcasted_iota(jnp.int32, (c, LANES), 1) < RW_HEAD
    end_row = [c - 1, 0]
    refs = [(rf_ref, vf_ref, kkf_ref, kf_ref, bf_ref, lwf_ref, yf_ref),
            (rb_ref, vb_ref, kkb_ref, kb_ref, bb_ref, lwb_ref, yb_ref)]
    lns = [slice(pr * LANES, (pr + 1) * LANES) for pr in range(n_pairs)]
    chains = [(d, p) for p in range(n_pairs) for d in range(2)]

    def stack(x):
        return jnp.concatenate([jnp.where(head0, x, 0.0), jnp.where(head0, 0.0, x)], axis=0)

    def chunk(step, carry):
        rows = [pl.ds(pl.multiple_of(step * c, c), c), pl.ds(pl.multiple_of((n_chunks - 1 - step) * c, c), c)]
        wide = []
        for d in range(2):
            r_ref, v_ref, kk_ref, k_ref, b_ref, lw_ref, _ = refs[d]
            lw = lw_ref[rows[d], :]
            ld = lambda ref: ref[rows[d], :].astype(F32)
            cum = _sel_dot(tri[d], lw)
            tot = cum[end_row[d]:end_row[d] + 1]
            g_inv, g_end = jnp.exp(-cum), jnp.exp(tot - cum)
            k_all, b_all = ld(k_ref), ld(b_ref)
            wide.append(dict(at=-ld(kk_ref) * jnp.exp(cum - lw), rt=ld(r_ref) * jnp.exp(cum),
                             bt=b_all * g_inv, kt=k_all * g_inv, bg=b_all * g_end, kg=k_all * g_end,
                             v=ld(v_ref), g_tot=jnp.exp(tot)))
        ar_s = [jnp.concatenate([stack(wide[d]["at"][:, lns[p]]), stack(wide[d]["rt"][:, lns[p]])], axis=0)
                .astype(BF16) for d, p in chains]
        bk_s = [jnp.concatenate([stack(wide[d]["bt"][:, lns[p]]), stack(wide[d]["kt"][:, lns[p]])], axis=0)
                .astype(BF16) for d, p in chains]
        v_s = [stack(wide[d]["v"][:, lns[p]]) for d, p in chains]
        ht = [h_scr[d, p] for d, p in chains]
        idx = range(len(chains))
        gram = [_dot_nt(ar_s[i], bk_s[i]) for i in idx]
        a_ab = [jnp.where(strict[d], gram[i][0:LANES, 0:LANES], 0.0) for i, (d, p) in enumerate(chains)]
        lhs = [jnp.concatenate([
            ar_s[i],
            jnp.concatenate([jnp.where(strict[d], gram[i][0:LANES, LANES:], 0.0),
                             jnp.where(incl[d], gram[i][LANES:, LANES:], 0.0)], axis=0).astype(BF16)], axis=1)
            for i, (d, p) in enumerate(chains)]
        rhs = [jnp.concatenate([ht[i].T, v_s[i]], axis=0).astype(BF16) for i in idx]
        xy0 = [_dot(lhs[i], rhs[i]) for i in idx]
        a_rb = [jnp.where(incl[d], gram[i][LANES:, 0:LANES], 0.0).astype(BF16) for i, (d, p) in enumerate(chains)]
        p1 = [jnp.where(blk[0], a_ab[i], 0.0).astype(BF16) for i in idx]
        p2 = [_dot(p1[i], p1[i]) for i in idx]
        p2_b = [x.astype(BF16) for x in p2]
        p4_b = [_dot(p2_b[i], p2_b[i]).astype(BF16) for i in idx]
        t_inv = [eye + p1[i] + p2[i] + _dot(p1[i], p2_b[i]) for i in idx]
        t_inv = [t_inv[i] + _dot(t_inv[i].astype(BF16), p4_b[i]) for i in idx]
        for lvl in range(1, 4):
            off = [jnp.where(blk[lvl] & ~blk[lvl - 1], a_ab[i], 0.0).astype(BF16) for i in idx]
            t_b = [x.astype(BF16) for x in t_inv]
            mid = [_dot(t_b[i], off[i]).astype(BF16) for i in idx]
            t_inv = [t_inv[i] + _dot(mid[i], t_b[i]) for i in idx]
        u = [_dot(t_inv[i].astype(BF16), xy0[i][0:LANES].astype(BF16)) for i in idx]
        y = [xy0[i][LANES:] + _dot(a_rb[i], u[i].astype(BF16)) for i in idx]
        for d in range(2):
            ys = [y[i][0:c] + y[i][c:2 * c] for i, (dd, p) in enumerate(chains) if dd == d]
            refs[d][6][rows[d], :] = jnp.concatenate(ys, axis=1).astype(refs[d][6].dtype)
        for i, (d, p) in enumerate(chains):
            uv_t = jnp.concatenate([u[i], v_s[i]], axis=0).T.astype(BF16)
            bkg = jnp.concatenate([stack(wide[d]["bg"][:, lns[p]]), stack(wide[d]["kg"][:, lns[p]])],
                                  axis=0).astype(BF16)
            h_scr[d, p] = ht[i] * wide[d]["g_tot"][:, lns[p]] + _dot(uv_t, bkg)
        return carry

    lax.fori_loop(0, n_chunks, chunk, 0)


def _rwkv_scan_call(r, v, kk, k_f, b_f, lw_f, k_b, b_b, lw_b, bsz, seq):
    t = r.shape[0]
    tb = min(2 * RW_CHUNK, seq)
    nt = seq // tb
    fwd = pl.BlockSpec((tb, RW_DIM), lambda bi, ti: (bi * nt + ti, 0))
    bwd = pl.BlockSpec((tb, RW_DIM), lambda bi, ti: (bi * nt + nt - 1 - ti, 0))
    out = jax.ShapeDtypeStruct((t, RW_DIM), BF16)
    return pl.pallas_call(
        functools.partial(_rwkv_scan_kernel, n_chunks=tb // RW_CHUNK),
        grid=(bsz, nt),
        in_specs=[fwd] * 6 + [bwd] * 6,
        out_specs=[fwd, bwd],
        out_shape=[out, out],
        scratch_shapes=[pltpu.VMEM((2, RW_DIM // LANES, LANES, LANES), F32)],
        compiler_params=_cparams("parallel", "arbitrary"),
        name="rwkv_scan",
    )(r, v, kk, k_f, b_f, lw_f, r, v, kk, k_b, b_b, lw_b)


def _rwkv_post_kernel(yf_ref, yb_ref, r_ref, kf_ref, kb_ref, v_ref, g_ref, pvec_ref, o_ref):
    mean2 = _pair_ones(1.0 / RW_HEAD)
    ones2 = _pair_ones()
    pv = pvec_ref[...]
    for cb in range(RW_DIM // LANES):
        sl = slice(cb * LANES, (cb + 1) * LANES)
        ld = lambda ref: ref[:, sl].astype(F32)
        y = ld(yf_ref) + ld(yb_ref)
        dlt = y - _dot_sel(y, mean2)
        var = _dot_sel(dlt * dlt, mean2)
        yn = dlt * lax.rsqrt(var + RW_GN_EPS) * pv[0:1, sl] + pv[1:2, sl]
        rk = ld(r_ref) * (ld(kf_ref) + ld(kb_ref)) * pv[2:3, sl]
        bonus = _dot_sel(rk, ones2) * ld(v_ref)
        o_ref[:, sl] = ((yn + bonus) * ld(g_ref)).astype(o_ref.dtype)


def _rwkv_post_call(yf, yb, r, kf, kb, v, g, pvec):
    t = r.shape[0]
    tm = 256
    spec = pl.BlockSpec((tm, RW_DIM), lambda i: (i, 0))
    return pl.pallas_call(
        _rwkv_post_kernel,
        grid=(t // tm,),
        in_specs=[spec] * 7 + [pl.BlockSpec((8, RW_DIM), lambda i: (0, 0))],
        out_specs=spec,
        out_shape=jax.ShapeDtypeStruct((t, RW_DIM), BF16),
        compiler_params=_cparams("parallel"),
        name="rwkv_post",
    )(yf, yb, r, kf, kb, v, g, pvec)


def _mamba_conv_kernel(x_ref, xp_ref, xn_ref, w_ref, b_ref, xs_o, bm_o, cm_o, *, tm, seq):
    cur, prevz, nextz = _halo_values(x_ref, xp_ref, xn_ref, tm, seq)
    w = w_ref[...]
    half = (M_CONV - 1) // 2
    acc = cur * w[half:half + 1] + b_ref[...]
    for tap in range(M_CONV):
        if tap != half:
            acc = acc + _shift_rows_mxu(x_ref[...], prevz, nextz, tap - half) * w[tap:tap + 1]
    y = _silu(acc)
    xs_o[...] = y[:, 0:M_INNER].astype(xs_o.dtype)
    bm_o[...] = y[:, M_INNER:M_INNER + M_GROUPS * M_STATE].astype(bm_o.dtype)
    cm_o[...] = y[:, M_INNER + M_GROUPS * M_STATE:].astype(cm_o.dtype)


def _mamba_conv_call(xbc, conv_w, conv_b, seq):
    t = xbc.shape[0]
    tm = 128
    gs = M_GROUPS * M_STATE
    return pl.pallas_call(
        functools.partial(_mamba_conv_kernel, tm=tm, seq=seq),
        grid=(t // tm,),
        in_specs=_halo_specs(tm, M_XBC, t) + [pl.BlockSpec((8, M_XBC), lambda i: (0, 0)),
                                             pl.BlockSpec((1, M_XBC), lambda i: (0, 0))],
        out_specs=[pl.BlockSpec((tm, M_INNER), lambda i: (i, 0)),
                   pl.BlockSpec((tm, gs), lambda i: (i, 0)),
                   pl.BlockSpec((tm, gs), lambda i: (i, 0))],
        out_shape=[jax.ShapeDtypeStruct((t, M_INNER), BF16),
                   jax.ShapeDtypeStruct((t, gs), BF16),
                   jax.ShapeDtypeStruct((t, gs), BF16)],
        compiler_params=_cparams("parallel"),
        name="mamba_conv",
    )(xbc, xbc, xbc, conv_w, conv_b)


def _ssd_kernel(xs_ref, bm_ref, cm_ref, dt_ref, dtb_ref, e_ref, aloge_ref, alogc_ref, y_ref, h_scr, *, reverse):
    c = M_CHUNK
    d = 1 if reverse else 0
    n_pairs = M_INNER // LANES
    ppg = n_pairs // M_GROUPS

    @pl.when(pl.program_id(1) == 0)
    def _():
        h_scr[...] = jnp.zeros_like(h_scr)

    ri = lax.broadcasted_iota(jnp.int32, (c, c), 0)
    ci = lax.broadcasted_iota(jnp.int32, (c, c), 1)
    keep = (ci >= ri) if reverse else (ci <= ri)
    tri = jnp.where(keep, 1.0, 0.0).astype(BF16)
    end_row = 0 if reverse else c - 1
    head0 = lax.broadcasted_iota(jnp.int32, (c, LANES), 1) < M_HEADDIM

    dt = _softplus(dt_ref[...] + dtb_ref[...])
    acs_c = _sel_dot(tri, dt * (-jnp.exp(alogc_ref[...])))
    acs_ct = acs_c.T
    dte = _dot_sel(dt, e_ref[0])
    acs = _sel_dot(tri, dte * (-jnp.exp(aloge_ref[0])))
    tot = acs[end_row:end_row + 1]
    xdt = xs_ref[...].astype(F32) * dte
    e_acs = jnp.exp(acs)
    x_end = (xdt * jnp.exp(tot - acs)).astype(BF16)
    xdt_b = xdt.astype(BF16)
    dec = jnp.exp(tot)
    gsl = [slice(g * M_STATE, (g + 1) * M_STATE) for g in range(M_GROUPS)]
    bm_b = [bm_ref[:, sl] for sl in gsl]
    cm_b = [cm_ref[:, sl] for sl in gsl]
    bm_t = [bm_ref[:, sl].astype(F32).T.astype(BF16) for sl in gsl]
    cb = [_dot_nt(cm_b[g], bm_b[g]) for g in range(M_GROUPS)]
    prs = range(n_pairs)
    lns = [slice(p * LANES, (p + 1) * LANES) for p in prs]
    ys = []
    for p in prs:
        pair = []
        for hh in range(2):
            idx = d * M_HEADS + 2 * p + hh
            seg = acs_c[:, idx:idx + 1] - acs_ct[idx:idx + 1, :]
            lmat = jnp.exp(jnp.where(keep, seg, NEG_BIG))
            pair.append(_dot((cb[p // ppg] * lmat).astype(BF16), xdt_b[:, lns[p]]))
        ys.append(jnp.where(head0, pair[0], pair[1]))
    h_prev = [h_scr[p] for p in prs]
    y_off = [_dot(cm_b[p // ppg], h_prev[p].astype(BF16)) for p in prs]
    y_ref[...] = (jnp.concatenate(ys, axis=1) + jnp.concatenate(y_off, axis=1) * e_acs).astype(y_ref.dtype)
    for p in prs:
        h_scr[p] = h_prev[p] * dec[:, lns[p]] + _dot(bm_t[p // ppg], x_end[:, lns[p]])


def _ssd_call(xs, bm, cm, dt_raw, dtb, emat, alog_e, alog_c, bsz, seq, reverse):
    t = xs.shape[0]
    nc = seq // M_CHUNK
    d = 1 if reverse else 0
    gs = M_GROUPS * M_STATE
    if reverse:
        row = lambda bi, ci: (bi * nc + nc - 1 - ci, 0)
    else:
        row = lambda bi, ci: (bi * nc + ci, 0)
    return pl.pallas_call(
        functools.partial(_ssd_kernel, reverse=reverse),
        grid=(bsz, nc),
        in_specs=[pl.BlockSpec((M_CHUNK, M_INNER), row),
                  pl.BlockSpec((M_CHUNK, gs), row),
                  pl.BlockSpec((M_CHUNK, gs), row),
                  pl.BlockSpec((M_CHUNK, LANES), lambda bi, ci: (row(bi, ci)[0], RW_CODES_PAD // LANES)),
                  pl.BlockSpec((1, LANES), lambda bi, ci: (0, 0)),
                  pl.BlockSpec((1, LANES, M_INNER), lambda bi, ci: (d, 0, 0)),
                  pl.BlockSpec((1, 1, M_INNER), lambda bi, ci: (d, 0, 0)),
                  pl.BlockSpec((1, LANES), lambda bi, ci: (0, 0))],
        out_specs=pl.BlockSpec((M_CHUNK, M_INNER), row),
        out_shape=jax.ShapeDtypeStruct((t, M_INNER), BF16),
        scratch_shapes=[pltpu.VMEM((M_INNER // LANES, M_STATE, LANES), F32)],
        compiler_params=_cparams("parallel", "arbitrary"),
        name="ssd_bwd" if reverse else "ssd_fwd",
    )(xs, bm, cm, dt_raw, dtb, emat, alog_e, alog_c)


def _mamba_post_kernel(yf_ref, yb_ref, xs_ref, z_ref, d_ref, nw_ref, o_ref):
    ld = lambda ref: ref[...].astype(F32)
    y = (ld(yf_ref) + ld(yb_ref) + d_ref[...] * ld(xs_ref)) * _silu(ld(z_ref))
    for g in range(M_GROUPS):
        sl = slice(g * M_GROUP_W, (g + 1) * M_GROUP_W)
        o_ref[:, sl] = _rms(y[:, sl], nw_ref[:, sl], 1e-5).astype(o_ref.dtype)


def _mamba_post_call(yf, yb, xs, z, d_e, nw):
    t = xs.shape[0]
    tm = 256
    spec = pl.BlockSpec((tm, M_INNER), lambda i: (i, 0))
    vec = pl.BlockSpec((1, M_INNER), lambda i: (0, 0))
    return pl.pallas_call(
        _mamba_post_kernel,
        grid=(t // tm,),
        in_specs=[spec] * 4 + [vec, vec],
        out_specs=spec,
        out_shape=jax.ShapeDtypeStruct((t, M_INNER), BF16),
        compiler_params=_cparams("parallel"),
        name="mamba_post",
    )(yf, yb, xs, z, d_e, nw)


def _diff_attn_kernel(zero_ref, q_ref, k_ref, v_ref, lam_ref, slope_ref, nw_ref, o_ref, bias_scr, s_scr, *, tq, tk,
                      lambda_init):
    seq = k_ref.shape[0]
    log2e = math.log2(math.e)

    @pl.when(pl.program_id(2) == 0)
    def _():
        rows = pl.program_id(1) * tq + lax.broadcasted_iota(jnp.int32, (tq, seq), 0)
        cols = lax.broadcasted_iota(jnp.int32, (tq, seq), 1)
        bias_scr[...] = (slope_ref[0][:, 0:1] * log2e) * jnp.abs(rows - cols).astype(F32)

    q = (q_ref[...].astype(F32) * (DF_HEAD ** -0.5 * log2e)).astype(BF16)
    lv = lam_ref[...]
    lam = (jnp.exp(jnp.sum(lv[0:1] * lv[1:2], keepdims=True))
           - jnp.exp(jnp.sum(lv[2:3] * lv[3:4], keepdims=True)) + lambda_init)
    map0 = lax.broadcasted_iota(jnp.int32, (tq, LANES), 1) < DF_HEAD
    qm = [jnp.where(map0, q, jnp.zeros_like(q)), jnp.where(map0, jnp.zeros_like(q), q)]
    tiles = [slice(t * tk, (t + 1) * tk) for t in range(seq // tk)]
    rt = min(LANES, tq)
    streams = [(mp, slice(rh * rt, (rh + 1) * rt)) for rh in range(tq // rt) for mp in range(2)]
    v1 = jnp.concatenate([v_ref[...], jnp.ones((seq, LANES), BF16)], axis=1)
    zero = zero_ref[0]

    def scores_tile(st, ts, mx):
        mp, rows = st
        s = _dot_nt(qm[mp][rows], k_ref[ts, :]) - bias_scr[rows, ts]
        s_scr[mp, rows, ts] = s
        for cs in range(tk // LANES):
            blk = s[:, cs * LANES:(cs + 1) * LANES]
            mx = blk if mx is None else jnp.maximum(mx, blk)
        return mx

    def weights(st, m):
        mp, rows = st
        return _dot(jnp.exp2(s_scr[mp + zero, rows, :] - m).astype(BF16), v1)

    m, acc = {}, {}
    for i in range(len(streams) + 1):
        if i < len(streams):
            mx = None
            for ts in tiles:
                mx = scores_tile(streams[i], ts, mx)
            m[i] = jnp.max(mx, axis=-1, keepdims=True)
        if i >= 1:
            acc[i - 1] = weights(streams[i - 1], m[i - 1])
        if i >= 2 and i % 2 == 0:
            rh = i // 2 - 1
            om = [acc[2 * rh + mp][:, 0:LANES] / acc[2 * rh + mp][:, LANES:LANES + 1] for mp in range(2)]
            o = om[0] - lam * om[1]
            o_ref[rh * rt:(rh + 1) * rt, :] = (_rms(o, nw_ref[...], 1e-5) * (1.0 - lambda_init)).astype(o_ref.dtype)


def _diff_attn_call(qkv, lam_vecs, slopes, subln_w, bsz, seq, lambda_init):
    t = qkv.shape[0]
    tq = min(1024, seq)
    nq = seq // tq
    return pl.pallas_call(
        functools.partial(_diff_attn_kernel, tq=tq, tk=min(512, seq), lambda_init=lambda_init),
        grid=(DF_HEADS, nq, bsz),
        in_specs=[pl.BlockSpec(memory_space=pltpu.SMEM),
                  pl.BlockSpec((tq, DF_V), lambda hi, qi, bi: (bi * nq + qi, hi)),
                  pl.BlockSpec((seq, DF_V), lambda hi, qi, bi: (bi, DF_HEADS + hi)),
                  pl.BlockSpec((seq, DF_V), lambda hi, qi, bi: (bi, 2 * DF_HEADS + hi)),
                  pl.BlockSpec((4, DF_HEAD), lambda hi, qi, bi: (0, 0)),
                  pl.BlockSpec((1, 1, LANES), lambda hi, qi, bi: (hi, 0, 0)),
                  pl.BlockSpec((1, DF_V), lambda hi, qi, bi: (0, 0))],
        out_specs=pl.BlockSpec((tq, DF_V), lambda hi, qi, bi: (bi * nq + qi, hi)),
        out_shape=jax.ShapeDtypeStruct((t, DF_HEADS * DF_V), BF16),
        scratch_shapes=[pltpu.VMEM((tq, seq), F32), pltpu.VMEM((2, tq, seq), F32)],
        compiler_params=_cparams("parallel", "parallel", "arbitrary"),
        name="diff_attn",
    )(jnp.zeros((1,), jnp.int32), qkv, qkv, qkv, lam_vecs, slopes, subln_w)


def _merge_kernel(yr_ref, ym_ref, yd_ref, pg_ref, x_ref, mod_ref, nw_ref, wr_ref, wm_ref, wd_ref, wo_ref, o_ref):
    d = D_MODEL
    gate = lambda g: _sigmoid(pg_ref[:, g * d:(g + 1) * d].astype(F32))
    merged = (gate(0) * _dot(yr_ref[...], wr_ref[...]) + gate(1) * _dot(ym_ref[...], wm_ref[...])
              + gate(2) * _dot(yd_ref[...], wd_ref[...]))
    y = _dot(merged.astype(BF16), wo_ref[...])
    o_ref[...] = x_ref[...] + mod_ref[0][2:3] * _rms(y, nw_ref[...], EPS)


def _merge_call(yr, ym, yd, pg, x2, mod_sub, nw, wr, wm, wd, wo, seq):
    t, d = x2.shape
    tm = 256
    row = lambda w: pl.BlockSpec((tm, w), lambda i: (i, 0))
    res = lambda a: pl.BlockSpec(a.shape, lambda i: (0, 0), pipeline_mode=pl.Buffered(1))
    return pl.pallas_call(
        _merge_kernel,
        grid=(t // tm,),
        in_specs=[row(RW_DIM), row(M_INNER), row(DF_HEADS * DF_V), row(GATE_COLS), row(d),
                  pl.BlockSpec((1, 3, d), lambda i: (i * tm // seq, 0, 0)),
                  pl.BlockSpec((1, d), lambda i: (0, 0)),
                  res(wr), res(wm), res(wd), res(wo)],
        out_specs=row(d),
        out_shape=jax.ShapeDtypeStruct((t, d), F32),
        compiler_params=_cparams("parallel"),
        name="merge_outproj",
    )(yr, ym, yd, pg, x2, mod_sub, nw, wr, wm, wd, wo)


def _pad_cols(w, n):
    return jnp.pad(w, ((0, 0), (0, n - w.shape[1])))


def _pad_rows(w, n):
    return jnp.pad(w, ((0, n - w.shape[0]), (0, 0)))


def _dir_padded(w):
    z = jnp.zeros_like(w[0])
    return jnp.stack([jnp.concatenate([w[0], z], axis=0), jnp.concatenate([z, w[1]], axis=0)])


def _head_expand_matrix():
    e = np.zeros((2, LANES, M_INNER), np.float32)
    for d in range(2):
        for h in range(M_HEADS):
            e[d, d * M_HEADS + h, h * M_HEADDIM:(h + 1) * M_HEADDIM] = 1.0
    return jnp.asarray(e, dtype=BF16)


def kernel(x, c, ada_w, ada_b, norm_w, ffn_w13, ffn_w2, w_in, rwkv_mu, rwkv_w0, rwkv_w2, rwkv_a0, rwkv_a2, rwkv_g2, rwkv_k_k, rwkv_k_a, rwkv_r_k, rwkv_ln_w, rwkv_ln_b, rwkv_v0, rwkv_v1, rwkv_v2, mamba_conv_w, mamba_conv_b, mamba_dt_bias, mamba_a_log, mamba_d, mamba_norm_w, diff_lambda, diff_subln_w, w_branch_rwkv, w_branch_mamba, w_branch_diff, w_out):
    bsz, seq, d = x.shape
    depth = ada_w.shape[0]
    t = bsz * seq
    x2 = x.reshape(t, d)
    mod_all = _mod_call(c, ada_w, ada_b)
    emat = _head_expand_matrix()
    slopes = jnp.broadcast_to(
        jnp.asarray(2.0 ** (-8.0 * np.arange(1, DF_HEADS + 1) / DF_HEADS), F32)[:, None, None], (DF_HEADS, 1, LANES))
    v_first = None
    for l in range(depth):
        mod = mod_all[l].reshape(bsz, N_SUB, 3, d)
        x2 = _ffn_call(x2, mod[:, 0], norm_w[l, 0:2], ffn_w13[l, 0].astype(BF16), ffn_w2[l, 0].astype(BF16), seq)

        wl = w_in[l]
        o_m = RW_COLS
        o_d = o_m + M_COLS
        o_g = o_d + DF_COLS
        w_cat = jnp.concatenate([
            wl[:, 0:3 * RW_DIM],
            _pad_cols(wl[:, 3 * RW_DIM:RW_COLS], RW_CODES_PAD),
            _pad_cols(wl[:, o_m + M_INNER + M_XBC:o_d], INPROJ_TN - RW_CODES_PAD),
            wl[:, o_m:o_m + M_INNER],
            wl[:, o_m + M_INNER:o_m + M_INNER + M_XBC],
            wl[:, o_d:o_g],
            wl[:, o_g:],
        ], axis=1).astype(BF16)
        p_rkv, p_aux, p_z, p_xbc, p_qkv, p_gate = _inproj_call(
            x2, mod[:, 1], norm_w[l, 2:3], w_cat,
            (3 * RW_DIM, INPROJ_TN, M_INNER, M_XBC, DF_COLS, GATE_COLS),
            (BF16, F32, BF16, BF16, BF16, BF16), seq)
        p_codes = p_dt = p_aux

        mu = rwkv_mu[l]
        mu_p = mu[None, 0:3 * RW_DIM]
        mu_c = _pad_cols(mu[None, 3 * RW_DIM:], RW_CODES_PAD)
        v0 = rwkv_v0[l - 1] if l > 0 else jnp.zeros((RW_DIM,), F32)
        pvec = jnp.stack([rwkv_w0[l, 0], rwkv_w0[l, 1], rwkv_a0[l, 0], rwkv_a0[l, 1],
                          rwkv_k_k[l], rwkv_k_a[l], v0, jnp.zeros((RW_DIM,), F32)])
        g2p = _pad_rows(rwkv_g2[l], 256).astype(BF16)
        vres = None
        if l > 0:
            vres = (v_first, _pad_cols(rwkv_v1[l - 1], LANES).astype(BF16),
                    _pad_rows(rwkv_v2[l - 1], LANES).astype(BF16))
        r, v, kk, k_f, k_b, b_f, b_b, lw_f, lw_b, gate = _rwkv_prep_call(
            p_rkv, p_codes, mu_p, mu_c, pvec, _dir_padded(rwkv_w2[l]).astype(BF16), _dir_padded(rwkv_a2[l]).astype(BF16), g2p, vres, seq)
        if l == 0:
            v_first = v
        y_f, y_b = _rwkv_scan_call(r, v, kk, k_f, b_f, lw_f, k_b, b_b, lw_b, bsz, seq)
        pvec2 = jnp.concatenate([jnp.stack([rwkv_ln_w[l], rwkv_ln_b[l], rwkv_r_k[l].reshape(RW_DIM)]),
                                 jnp.zeros((5, RW_DIM), F32)])
        y_r = _rwkv_post_call(y_f, y_b, r, k_f, k_b, v, gate, pvec2)

        conv_w = _pad_rows(mamba_conv_w[l], 8)
        xs, bm, cm = _mamba_conv_call(p_xbc, conv_w, mamba_conv_b[l][None], seq)
        dtb = _pad_cols(mamba_dt_bias[l].reshape(1, 2 * M_HEADS), LANES)
        alog_e = jnp.repeat(mamba_a_log[l], M_HEADDIM, axis=-1)[:, None]
        alog_c = _pad_cols(mamba_a_log[l].reshape(1, 2 * M_HEADS), LANES)
        ym_f = _ssd_call(xs, bm, cm, p_dt, dtb, emat, alog_e, alog_c, bsz, seq, False)
        ym_b = _ssd_call(xs, bm, cm, p_dt, dtb, emat, alog_e, alog_c, bsz, seq, True)
        d_e = jnp.repeat(mamba_d[l], M_HEADDIM)[None]
        y_m = _mamba_post_call(ym_f, ym_b, xs, p_z, d_e, mamba_norm_w[l][None])

        lambda_init = 0.8 - 0.6 * math.exp(-0.3 * l)
        y_d = _diff_attn_call(p_qkv, diff_lambda[l], slopes, diff_subln_w[l][None], bsz, seq, lambda_init)

        x2 = _merge_call(y_r, y_m, y_d, p_gate, x2, mod[:, 1], norm_w[l, 3:4],
                         w_branch_rwkv[l].astype(BF16), w_branch_mamba[l].astype(BF16),
                         w_branch_diff[l].astype(BF16), w_out[l].astype(BF16), seq)

        x2 = _ffn_call(x2, mod[:, 2], norm_w[l, 4:6], ffn_w13[l, 1].astype(BF16), ffn_w2[l, 1].astype(BF16), seq)
    return x2.reshape(bsz, seq, d)
```

```python
import functools
import itertools
import math

import numpy as np
import jax
import jax.numpy as jnp
from jax import lax
from jax.experimental import pallas as pl
from jax.experimental.pallas import tpu as pltpu

F32 = jnp.float32
BF16 = jnp.bfloat16
HI = lax.Precision.HIGHEST

D_MODEL = 1024
N_SUB = 3
EPS = 1e-6
LANES = 128
SUBLANES = 8
HALO = 2 * SUBLANES
VMEM_LIMIT = 56 * 1024 * 1024

RW_HEAD = 64
RW_DIM = 1024
W_LORA = 64
A_LORA = 64
V_LORA = 32
G_LORA = 160
RW_GN_EPS = 64e-5
RW_COLS = 3 * RW_DIM + 2 * W_LORA + 2 * A_LORA + G_LORA
RW_CODES_PAD = 512
RW_CHUNK = 64
M_INNER = 2048
M_HEADS = 32
M_HEADDIM = 64
M_GROUPS = 4
M_STATE = 128
M_CONV = 5
M_CHUNK = 128
M_XBC = M_INNER + 2 * M_GROUPS * M_STATE
M_COLS = M_INNER + M_XBC + 2 * M_HEADS
M_GROUP_W = M_INNER // M_GROUPS
DF_HEADS = 8
DF_HEAD = 64
DF_V = 128
DF_COLS = 3 * DF_HEADS * 2 * DF_HEAD
GATE_COLS = 3 * D_MODEL
D_FF = 2816
NEG_BIG = -1e30


def _cparams(*sem):
    return pltpu.CompilerParams(dimension_semantics=sem, vmem_limit_bytes=VMEM_LIMIT)


def _dot(a, b, **kw):
    return jnp.dot(a, b, preferred_element_type=F32, **kw)


def _dot_nt(a, b, **kw):
    return lax.dot_general(a, b, (((1,), (1,)), ((), ())), preferred_element_type=F32, **kw)


def _split2(x):
    hi = x.astype(BF16)
    return hi, (x - hi.astype(F32)).astype(BF16)


def _dot_sel(x, m_b):
    hi, lo = _split2(x)
    return _dot(hi, m_b) + _dot(lo, m_b)


def _sel_dot(m_b, x):
    hi, lo = _split2(x)
    return _dot(m_b, hi) + _dot(m_b, lo)


def _bdot(a, b):
    return _dot(a.astype(BF16), b.astype(BF16))


def _sigmoid(x):
    return 1.0 / (1.0 + jnp.exp(-x))


def _silu(x):
    return x * _sigmoid(x)


def _softplus(x):
    return jnp.maximum(x, 0.0) + jnp.log(1.0 + jnp.exp(-jnp.abs(x)))


def _rms(x, w, eps):
    return x * lax.rsqrt(jnp.mean(x * x, axis=-1, keepdims=True) + eps) * w


def _norm_mod(x, nw, shift, scale):
    return _rms(x, nw, EPS) * (1.0 + scale) + shift


def _pair_ones(scale=1.0):
    r = lax.broadcasted_iota(jnp.int32, (LANES, LANES), 0) // RW_HEAD
    c = lax.broadcasted_iota(jnp.int32, (LANES, LANES), 1) // RW_HEAD
    return jnp.where(r == c, scale, 0.0).astype(BF16)


def _shift_rows(cur, prevz, nextz, d):
    tm = cur.shape[0]
    rolled = pltpu.roll(cur, (-d) % tm, axis=0)
    rid = lax.broadcasted_iota(jnp.int32, (SUBLANES, cur.shape[1]), 0)
    if d < 0:
        fix = pltpu.roll(prevz, (-d) % SUBLANES, axis=0)
        top = jnp.where(rid < -d, fix, rolled[0:SUBLANES])
        return jnp.concatenate([top, rolled[SUBLANES:]], axis=0)
    fix = pltpu.roll(nextz, (SUBLANES - d) % SUBLANES, axis=0)
    bot = jnp.where(rid >= SUBLANES - d, fix, rolled[tm - SUBLANES:])
    return jnp.concatenate([rolled[:tm - SUBLANES], bot], axis=0)


def _shift_rows_mxu(cur_b, prevz, nextz, d):
    assert cur_b.dtype == BF16
    tm = cur_b.shape[0]
    ri = lax.broadcasted_iota(jnp.int32, (tm, tm), 0)
    ci = lax.broadcasted_iota(jnp.int32, (tm, tm), 1)
    shifted = _dot(jnp.where(ci == ri + d, 1.0, 0.0).astype(BF16), cur_b)
    rid = lax.broadcasted_iota(jnp.int32, (SUBLANES, cur_b.shape[1]), 0)
    if d < 0:
        fix = jnp.where(rid < -d, pltpu.roll(prevz, (-d) % SUBLANES, axis=0), 0.0)
        return jnp.concatenate([shifted[0:SUBLANES] + fix, shifted[SUBLANES:]], axis=0)
    fix = jnp.where(rid >= SUBLANES - d, pltpu.roll(nextz, (SUBLANES - d) % SUBLANES, axis=0), 0.0)
    return jnp.concatenate([shifted[:tm - SUBLANES], shifted[tm - SUBLANES:] + fix], axis=0)


def _halo_specs(tm, width, n_rows):
    rb = tm // HALO
    last = n_rows // HALO - 1
    return [
        pl.BlockSpec((tm, width), lambda i: (i, 0)),
        pl.BlockSpec((HALO, width), lambda i: (jnp.maximum(i * rb - 1, 0), 0)),
        pl.BlockSpec((HALO, width), lambda i: (jnp.minimum((i + 1) * rb, last), 0)),
    ]


def _halo_values(cur_ref, prev_ref, next_ref, tm, seq):
    i = pl.program_id(0)
    first = (i * tm) % seq == 0
    last = ((i + 1) * tm) % seq == 0
    prevz = jnp.where(first, 0.0, prev_ref[...].astype(F32)[HALO - SUBLANES:])
    nextz = jnp.where(last, 0.0, next_ref[...].astype(F32)[:SUBLANES])
    return cur_ref[...].astype(F32), prevz, nextz


def _mod_kernel(c_ref, w_ref, b_ref, o_ref):
    o_ref[0] = _dot(_silu(c_ref[...]), w_ref[0], precision=HI) + b_ref[0]


def _mod_call(c, ada_w, ada_b):
    depth, d, n = ada_w.shape
    bsz = c.shape[0]
    tn = 1152
    return pl.pallas_call(
        _mod_kernel,
        grid=(depth, n // tn),
        in_specs=[pl.BlockSpec((bsz, d), lambda l, j: (0, 0)),
                  pl.BlockSpec((1, d, tn), lambda l, j: (l, 0, j)),
                  pl.BlockSpec((1, 1, tn), lambda l, j: (l, 0, j))],
        out_specs=pl.BlockSpec((1, bsz, tn), lambda l, j: (l, 0, j)),
        out_shape=jax.ShapeDtypeStruct((depth, bsz, n), F32),
        compiler_params=_cparams("parallel", "parallel"),
        name="adaln_mod",
    )(c, ada_w, ada_b.reshape(depth, 1, n))


def _ffn_kernel(x_ref, mod_ref, nw_ref, w13_ref, w2_ref, o_ref, *, tf):
    dff = w2_ref.shape[0]
    m = mod_ref[0]
    x = x_ref[...]
    h = _norm_mod(x, nw_ref[0:1], m[0:1], m[1:2]).astype(BF16)
    acc = None
    for j in range(dff // tf):
        g = _dot(h, w13_ref[:, j * tf:(j + 1) * tf])
        u = _dot(h, w13_ref[:, dff + j * tf:dff + (j + 1) * tf])
        part = _dot((_silu(g) * u).astype(BF16), w2_ref[j * tf:(j + 1) * tf, :])
        acc = part if acc is None else acc + part
    o_ref[...] = x + 0.5 * m[2:3] * _rms(acc, nw_ref[1:2], EPS)


def _ffn_call(x2, mod_sub, nw2, w13, w2, seq):
    t, d = x2.shape
    tm = min(1024, seq)
    resident = lambda a: pl.BlockSpec(a.shape, lambda i: (0, 0), pipeline_mode=pl.Buffered(1))
    return pl.pallas_call(
        functools.partial(_ffn_kernel, tf=256),
        grid=(t // tm,),
        in_specs=[pl.BlockSpec((tm, d), lambda i: (i, 0)),
                  pl.BlockSpec((1, 3, d), lambda i: (i * tm // seq, 0, 0)),
                  pl.BlockSpec((2, d), lambda i: (0, 0)),
                  resident(w13), resident(w2)],
        out_specs=pl.BlockSpec((tm, d), lambda i: (i, 0)),
        out_shape=jax.ShapeDtypeStruct((t, d), F32),
        compiler_params=_cparams("parallel"),
        name="swiglu_halfstep",
    )(x2, mod_sub, nw2, w13, w2)


INPROJ_TN = 1024


def _inproj_kernel(x_ref, mod_ref, nw_ref, w_ref, *rest, starts):
    o_refs, h_scr = rest[:-1], rest[-1]
    j = pl.program_id(1)

    @pl.when(j == 0)
    def _():
        m = mod_ref[0]
        h_scr[...] = _norm_mod(x_ref[...], nw_ref[...], m[0:1], m[1:2]).astype(BF16)

    for k, o_ref in enumerate(o_refs):
        @pl.when((j >= starts[k]) & (j < starts[k + 1]))
        def _(o_ref=o_ref):
            o_ref[...] = _dot(h_scr[...], w_ref[...]).astype(o_ref.dtype)


def _inproj_call(x2, mod_sub, nw, w, widths, dtypes, seq):
    t, d = x2.shape
    tn = INPROJ_TN
    tm = min(1024, seq)
    starts = [0]
    for wd in widths:
        starts.append(starts[-1] + wd // tn)

    def out_spec(k):
        return pl.BlockSpec((tm, tn), lambda i, j: (i, jnp.clip(j - starts[k], 0, widths[k] // tn - 1)))

    return pl.pallas_call(
        functools.partial(_inproj_kernel, starts=tuple(starts)),
        grid=(t // tm, starts[-1]),
        in_specs=[pl.BlockSpec((tm, d), lambda i, j: (i, 0)),
                  pl.BlockSpec((1, 3, d), lambda i, j: (i * tm // seq, 0, 0)),
                  pl.BlockSpec((1, d), lambda i, j: (0, 0)),
                  pl.BlockSpec((d, tn), lambda i, j: (0, j))],
        out_specs=[out_spec(k) for k in range(len(widths))],
        out_shape=[jax.ShapeDtypeStruct((t, wd), dt) for wd, dt in zip(widths, dtypes)],
        scratch_shapes=[pltpu.VMEM((tm, d), BF16)],
        compiler_params=_cparams("parallel", "arbitrary"),
        name="norm_inproj",
    )(x2, mod_sub, nw, w)


def _rwkv_prep_kernel(*refs, tm, seq, has_vres):
    (p_ref, pp_ref, pn_ref, c_ref, cp_ref, cn_ref, mup_ref, muc_ref, pvec_ref,
     w2_ref, a2_ref, g2_ref) = refs[:12]
    refs = refs[12:]
    if has_vres:
        vf_ref, v1_ref, v2_ref = refs[:3]
        refs = refs[3:]
    r_o, v_o, kk_o, kf_o, kb_o, bf_o, bb_o, lwf_o, lwb_o, g_o = refs

    def shift_mix(cur_ref, prev_ref, next_ref, mu):
        cur, prevz, nextz = _halo_values(cur_ref, prev_ref, next_ref, tm, seq)
        if cur_ref.dtype == BF16:
            nb = 0.5 * (_shift_rows_mxu(cur_ref[...], prevz, nextz, -1) + _shift_rows_mxu(cur_ref[...], prevz, nextz, 1))
        else:
            nb = 0.5 * (_shift_rows(cur, prevz, nextz, -1) + _shift_rows(cur, prevz, nextz, 1))
        return cur + mu * (nb - cur)

    p = shift_mix(p_ref, pp_ref, pn_ref, mup_ref[...])
    codes = shift_mix(c_ref, cp_ref, cn_ref, muc_ref[...])
    r = p[:, 0:RW_DIM]
    k = p[:, RW_DIM:2 * RW_DIM]
    v = p[:, 2 * RW_DIM:3 * RW_DIM]
    cw = jnp.tanh(codes[:, 0:2 * W_LORA])
    ca = codes[:, 2 * W_LORA:2 * W_LORA + 2 * A_LORA]
    cg = _sigmoid(codes[:, 2 * W_LORA + 2 * A_LORA:])
    pv = pvec_ref[...]
    if has_vres:
        lo = _bdot(_bdot(v, v1_ref[...]), v2_ref[...])
        v = v + (vf_ref[...].astype(F32) - v) * _sigmoid(pv[6:7] + lo)
    r_o[...] = r.astype(r_o.dtype)
    v_o[...] = v.astype(v_o.dtype)
    g_o[...] = _bdot(cg, g2_ref[...]).astype(g_o.dtype)
    iclr = []
    for d, lw_o in ((0, lwf_o), (1, lwb_o)):
        lw_o[...] = -math.exp(-0.5) * _sigmoid(pv[d:d + 1] + _bdot(cw, w2_ref[d]))
        iclr.append(_sigmoid(pv[2 + d:3 + d] + _bdot(ca, a2_ref[d])))
    kf_o[...] = (k * (1.0 + (iclr[0] - 1.0) * pv[5:6])).astype(kf_o.dtype)
    kb_o[...] = (k * (1.0 + (iclr[1] - 1.0) * pv[5:6])).astype(kb_o.dtype)
    ones2 = _pair_ones()
    kkr = k * pv[4:5]
    for cb in range(RW_DIM // LANES):
        sl = slice(cb * LANES, (cb + 1) * LANES)
        blk = kkr[:, sl]
        kkn = blk * lax.rsqrt(_dot_sel(blk * blk, ones2) + 1e-12)
        kk_o[:, sl] = kkn.astype(kk_o.dtype)
        bf_o[:, sl] = (kkn * iclr[0][:, sl]).astype(bf_o.dtype)
        bb_o[:, sl] = (kkn * iclr[1][:, sl]).astype(bb_o.dtype)


def _rwkv_prep_call(rkv, codes, mu_p, mu_c, pvec, w2p, a2p, g2p, vres, seq):
    t = rkv.shape[0]
    tm = 256
    has_vres = vres is not None
    full = lambda shape: pl.BlockSpec(shape, lambda i: (0,) * len(shape))
    in_specs = (_halo_specs(tm, 3 * RW_DIM, t) + _halo_specs(tm, RW_CODES_PAD, t)
                + [full((1, 3 * RW_DIM)), full((1, RW_CODES_PAD)), full((8, RW_DIM)),
                   full((2, 2 * W_LORA, RW_DIM)), full((2, 2 * A_LORA, RW_DIM)), full((256, RW_DIM))])
    args = [rkv, rkv, rkv, codes, codes, codes, mu_p, mu_c, pvec, w2p, a2p, g2p]
    if has_vres:
        in_specs += [pl.BlockSpec((tm, RW_DIM), lambda i: (i, 0)), full((RW_DIM, LANES)), full((LANES, RW_DIM))]
        args += list(vres)
    dtypes = [BF16] * 7 + [F32, F32, BF16]
    return pl.pallas_call(
        functools.partial(_rwkv_prep_kernel, tm=tm, seq=seq, has_vres=has_vres),
        grid=(t // tm,),
        in_specs=in_specs,
        out_specs=[pl.BlockSpec((tm, RW_DIM), lambda i: (i, 0))] * 10,
        out_shape=[jax.ShapeDtypeStruct((t, RW_DIM), dt) for dt in dtypes],
        compiler_params=_cparams("parallel"),
        name="rwkv_prep",
    )(*args)


def _rwkv_scan_kernel(rf_ref, vf_ref, kkf_ref, kf_ref, bf_ref, lwf_ref, rb_ref, vb_ref, kkb_ref, kb_ref, bb_ref,
                      lwb_ref, yf_ref, yb_ref, h_scr, *, n_chunks):
    c = RW_CHUNK
    n_pairs = RW_DIM // LANES

    @pl.when(pl.program_id(1) == 0)
    def _():
        h_scr[...] = jnp.zeros_like(h_scr)

    ri = lax.broadcasted_iota(jnp.int32, (LANES, LANES), 0)
    ci = lax.broadcasted_iota(jnp.int32, (LANES, LANES), 1)
    same = (ri // c) == (ci // c)
    tr, tc = ri % c, ci % c
    strict = [same & (tc < tr), same & (tc > tr)]
    incl = [same & (tc <= tr), same & (tc >= tr)]
    blk = [(ri // w) == (ci // w) for w in (8, 16, 32, 64)]
    eye = jnp.where(ri == ci, 1.0, 0.0).astype(F32)
    r64 = lax.broadcasted_iota(jnp.int32, (c, c), 0)
    c64 = lax.broadcasted_iota(jnp.int32, (c, c), 1)
    tri = [jnp.where(c64 <= r64, 1.0, 0.0).astype(BF16), jnp.where(c64 >= r64, 1.0, 0.0).astype(BF16)]
    head0 = lax.broadcasted_iota(jnp.int32, (c, LANES), 1) < RW_HEAD
    end_row = [c - 1, 0]
    refs = [(rf_ref, vf_ref, kkf_ref, kf_ref, bf_ref, lwf_ref, yf_ref),
            (rb_ref, vb_ref, kkb_ref, kb_ref, bb_ref, lwb_ref, yb_ref)]
    lns = [slice(pr * LANES, (pr + 1) * LANES) for pr in range(n_pairs)]
    chains = [(d, p) for p in range(n_pairs) for d in range(2)]

    def stack(x):
        return jnp.concatenate([jnp.where(head0, x, 0.0), jnp.where(head0, 0.0, x)], axis=0)

    def chunk(step, carry):
        rows = [pl.ds(pl.multiple_of(step * c, c), c), pl.ds(pl.multiple_of((n_chunks - 1 - step) * c, c), c)]
        wide = []
        for d in range(2):
            r_ref, v_ref, kk_ref, k_ref, b_ref, lw_ref, _ = refs[d]
            lw = lw_ref[rows[d], :]
            ld = lambda ref: ref[rows[d], :].astype(F32)
            cum = _sel_dot(tri[d], lw)
            tot = cum[end_row[d]:end_row[d] + 1]
            g_inv, g_end = jnp.exp(-cum), jnp.exp(tot - cum)
            k_all, b_all = ld(k_ref), ld(b_ref)
            wide.append(dict(at=-ld(kk_ref) * jnp.exp(cum - lw), rt=ld(r_ref) * jnp.exp(cum),
                             bt=b_all * g_inv, kt=k_all * g_inv, bg=b_all * g_end, kg=k_all * g_end,
                             v=ld(v_ref), g_tot=jnp.exp(tot)))
        ar_s = [jnp.concatenate([stack(wide[d]["at"][:, lns[p]]), stack(wide[d]["rt"][:, lns[p]])], axis=0)
                .astype(BF16) for d, p in chains]
        bk_s = [jnp.concatenate([stack(wide[d]["bt"][:, lns[p]]), stack(wide[d]["kt"][:, lns[p]])], axis=0)
                .astype(BF16) for d, p in chains]
        v_s = [stack(wide[d]["v"][:, lns[p]]) for d, p in chains]
        ht = [h_scr[d, p] for d, p in chains]
        idx = range(len(chains))
        gram = [_dot_nt(ar_s[i], bk_s[i]) for i in idx]
        a_ab = [jnp.where(strict[d], gram[i][0:LANES, 0:LANES], 0.0) for i, (d, p) in enumerate(chains)]
        lhs = [jnp.concatenate([
            ar_s[i],
            jnp.concatenate([jnp.where(strict[d], gram[i][0:LANES, LANES:], 0.0),
                             jnp.where(incl[d], gram[i][LANES:, LANES:], 0.0)], axis=0).astype(BF16)], axis=1)
            for i, (d, p) in enumerate(chains)]
        rhs = [jnp.concatenate([ht[i].T, v_s[i]], axis=0).astype(BF16) for i in idx]
        xy0 = [_dot(lhs[i], rhs[i]) for i in idx]
        a_rb = [jnp.where(incl[d], gram[i][LANES:, 0:LANES], 0.0).astype(BF16) for i, (d, p) in enumerate(chains)]
        p1 = [jnp.where(blk[0], a_ab[i], 0.0).astype(BF16) for i in idx]
        p2 = [_dot(p1[i], p1[i]) for i in idx]
        p2_b = [x.astype(BF16) for x in p2]
        p4_b = [_dot(p2_b[i], p2_b[i]).astype(BF16) for i in idx]
        t_inv = [eye + p1[i] + p2[i] + _dot(p1[i], p2_b[i]) for i in idx]
        t_inv = [t_inv[i] + _dot(t_inv[i].astype(BF16), p4_b[i]) for i in idx]
        for lvl in range(1, 4):
            off = [jnp.where(blk[lvl] & ~blk[lvl - 1], a_ab[i], 0.0).astype(BF16) for i in idx]
            t_b = [x.astype(BF16) for x in t_inv]
            mid = [_dot(t_b[i], off[i]).astype(BF16) for i in idx]
            t_inv = [t_inv[i] + _dot(mid[i], t_b[i]) for i in idx]
        u = [_dot(t_inv[i].astype(BF16), xy0[i][0:LANES].astype(BF16)) for i in idx]
        y = [xy0[i][LANES:] + _dot(a_rb[i], u[i].astype(BF16)) for i in idx]
        for d in range(2):
            ys = [y[i][0:c] + y[i][c:2 * c] for i, (dd, p) in enumerate(chains) if dd == d]
            refs[d][6][rows[d], :] = jnp.concatenate(ys, axis=1).astype(refs[d][6].dtype)
        for i, (d, p) in enumerate(chains):
            uv_t = jnp.concatenate([u[i], v_s[i]], axis=0).T.astype(BF16)
            bkg = jnp.concatenate([stack(wide[d]["bg"][:, lns[p]]), stack(wide[d]["kg"][:, lns[p]])],
                                  axis=0).astype(BF16)
            h_scr[d, p] = ht[i] * wide[d]["g_tot"][:, lns[p]] + _dot(uv_t, bkg)
        return carry

    lax.fori_loop(0, n_chunks, chunk, 0)


def _rwkv_scan_call(r, v, kk, k_f, b_f, lw_f, k_b, b_b, lw_b, bsz, seq):
    t = r.shape[0]
    tb = min(4 * RW_CHUNK, seq)
    nt = seq // tb
    fwd = pl.BlockSpec((tb, RW_DIM), lambda bi, ti: (bi * nt + ti, 0))
    bwd = pl.BlockSpec((tb, RW_DIM), lambda bi, ti: (bi * nt + nt - 1 - ti, 0))
    out = jax.ShapeDtypeStruct((t, RW_DIM), BF16)
    return pl.pallas_call(
        functools.partial(_rwkv_scan_kernel, n_chunks=tb // RW_CHUNK),
        grid=(bsz, nt),
        in_specs=[fwd] * 6 + [bwd] * 6,
        out_specs=[fwd, bwd],
        out_shape=[out, out],
        scratch_shapes=[pltpu.VMEM((2, RW_DIM // LANES, LANES, LANES), F32)],
        compiler_params=_cparams("parallel", "arbitrary"),
        name="rwkv_scan",
    )(r, v, kk, k_f, b_f, lw_f, r, v, kk, k_b, b_b, lw_b)


def _rwkv_post(yf_ref, yb_ref, r_ref, kf_ref, kb_ref, v_ref, g_ref, pv):
    mean2 = _pair_ones(1.0 / RW_HEAD)
    ones2 = _pair_ones()
    outs = []
    for cb in range(RW_DIM // LANES):
        sl = slice(cb * LANES, (cb + 1) * LANES)
        ld = lambda ref: ref[:, sl].astype(F32)
        y = ld(yf_ref) + ld(yb_ref)
        dlt = y - _dot_sel(y, mean2)
        var = _dot_sel(dlt * dlt, mean2)
        yn = dlt * lax.rsqrt(var + RW_GN_EPS) * pv[0:1, sl] + pv[1:2, sl]
        rk = ld(r_ref) * (ld(kf_ref) + ld(kb_ref)) * pv[2:3, sl]
        bonus = _dot_sel(rk, ones2) * ld(v_ref)
        outs.append(((yn + bonus) * ld(g_ref)).astype(BF16))
    return jnp.concatenate(outs, axis=1)


def _mamba_conv_kernel(x_ref, xp_ref, xn_ref, w_ref, b_ref, xs_o, bm_o, cm_o, *, tm, seq):
    cur, prevz, nextz = _halo_values(x_ref, xp_ref, xn_ref, tm, seq)
    w = w_ref[...]
    half = (M_CONV - 1) // 2
    acc = cur * w[half:half + 1] + b_ref[...]
    for tap in range(M_CONV):
        if tap != half:
            acc = acc + _shift_rows_mxu(x_ref[...], prevz, nextz, tap - half) * w[tap:tap + 1]
    y = _silu(acc)
    xs_o[...] = y[:, 0:M_INNER].astype(xs_o.dtype)
    bm_o[...] = y[:, M_INNER:M_INNER + M_GROUPS * M_STATE].astype(bm_o.dtype)
    cm_o[...] = y[:, M_INNER + M_GROUPS * M_STATE:].astype(cm_o.dtype)


def _mamba_conv_call(xbc, conv_w, conv_b, seq):
    t = xbc.shape[0]
    tm = 128
    gs = M_GROUPS * M_STATE
    return pl.pallas_call(
        functools.partial(_mamba_conv_kernel, tm=tm, seq=seq),
        grid=(t // tm,),
        in_specs=_halo_specs(tm, M_XBC, t) + [pl.BlockSpec((8, M_XBC), lambda i: (0, 0)),
                                             pl.BlockSpec((1, M_XBC), lambda i: (0, 0))],
        out_specs=[pl.BlockSpec((tm, M_INNER), lambda i: (i, 0)),
                   pl.BlockSpec((tm, gs), lambda i: (i, 0)),
                   pl.BlockSpec((tm, gs), lambda i: (i, 0))],
        out_shape=[jax.ShapeDtypeStruct((t, M_INNER), BF16),
                   jax.ShapeDtypeStruct((t, gs), BF16),
                   jax.ShapeDtypeStruct((t, gs), BF16)],
        compiler_params=_cparams("parallel"),
        name="mamba_conv",
    )(xbc, xbc, xbc, conv_w, conv_b)


def _ssd_chunk(d, xs_ref, bm_ref, cm_ref, dt_ref, dtb_ref, e_ref, aloge_ref, alogc_ref, y_ref, h_scr):
    c = M_CHUNK
    n_pairs = M_INNER // LANES
    ppg = n_pairs // M_GROUPS
    ri = lax.broadcasted_iota(jnp.int32, (c, c), 0)
    ci = lax.broadcasted_iota(jnp.int32, (c, c), 1)
    keep = (ci >= ri) if d else (ci <= ri)
    tri = jnp.where(keep, 1.0, 0.0).astype(BF16)
    end_row = 0 if d else c - 1
    head0 = lax.broadcasted_iota(jnp.int32, (c, LANES), 1) < M_HEADDIM

    dt = _softplus(dt_ref[...] + dtb_ref[...])
    acs_c = _sel_dot(tri, dt * (-jnp.exp(alogc_ref[...])))
    acs_ct = acs_c.T
    dte = _dot_sel(dt, e_ref[d])
    acs = _sel_dot(tri, dte * (-jnp.exp(aloge_ref[d])))
    tot = acs[end_row:end_row + 1]
    yield
    xdt = xs_ref[...].astype(F32) * dte
    e_acs = jnp.exp(acs)
    x_end = (xdt * jnp.exp(tot - acs)).astype(BF16)
    xdt_b = xdt.astype(BF16)
    dec = jnp.exp(tot)
    gsl = [slice(g * M_STATE, (g + 1) * M_STATE) for g in range(M_GROUPS)]
    bm_b = [bm_ref[:, sl] for sl in gsl]
    cm_b = [cm_ref[:, sl] for sl in gsl]
    bm_t = [bm_ref[:, sl].astype(F32).T.astype(BF16) for sl in gsl]
    cb = [_dot_nt(cm_b[g], bm_b[g]) for g in range(M_GROUPS)]
    yield
    prs = range(n_pairs)
    lns = [slice(p * LANES, (p + 1) * LANES) for p in prs]
    ys = []
    for p in prs:
        pair = []
        for hh in range(2):
            idx = d * M_HEADS + 2 * p + hh
            seg = acs_c[:, idx:idx + 1] - acs_ct[idx:idx + 1, :]
            lmat = jnp.exp(jnp.where(keep, seg, NEG_BIG))
            pair.append(_dot((cb[p // ppg] * lmat).astype(BF16), xdt_b[:, lns[p]]))
        ys.append(jnp.where(head0, pair[0], pair[1]))
        yield
    h_prev = [h_scr[d, p] for p in prs]
    y_off = [_dot(cm_b[p // ppg], h_prev[p].astype(BF16)) for p in prs]
    y_ref[...] = (jnp.concatenate(ys, axis=1) + jnp.concatenate(y_off, axis=1) * e_acs).astype(y_ref.dtype)
    yield
    for p in prs:
        h_scr[d, p] = h_prev[p] * dec[:, lns[p]] + _dot(bm_t[p // ppg], x_end[:, lns[p]])


def _ssd_kernel(xsf_ref, bmf_ref, cmf_ref, dtf_ref, xsb_ref, bmb_ref, cmb_ref, dtb_ref, bias_ref, e_ref, aloge_ref,
                alogc_ref, yf_ref, yb_ref, h_scr):
    @pl.when(pl.program_id(1) == 0)
    def _():
        h_scr[...] = jnp.zeros_like(h_scr)

    shared = (bias_ref, e_ref, aloge_ref, alogc_ref)
    both = (_ssd_chunk(0, xsf_ref, bmf_ref, cmf_ref, dtf_ref, *shared, yf_ref, h_scr),
            _ssd_chunk(1, xsb_ref, bmb_ref, cmb_ref, dtb_ref, *shared, yb_ref, h_scr))
    for _ in itertools.zip_longest(*both):
        pass


def _ssd_call(xs, bm, cm, dt_raw, dtb, emat, alog_e, alog_c, bsz, seq):
    t = xs.shape[0]
    nc = seq // M_CHUNK
    gs = M_GROUPS * M_STATE
    dt_col = RW_CODES_PAD // LANES
    fwd = lambda bi, ci: bi * nc + ci
    bwd = lambda bi, ci: bi * nc + nc - 1 - ci
    data = lambda row: [pl.BlockSpec((M_CHUNK, M_INNER), lambda bi, ci: (row(bi, ci), 0)),
                        pl.BlockSpec((M_CHUNK, gs), lambda bi, ci: (row(bi, ci), 0)),
                        pl.BlockSpec((M_CHUNK, gs), lambda bi, ci: (row(bi, ci), 0)),
                        pl.BlockSpec((M_CHUNK, LANES), lambda bi, ci: (row(bi, ci), dt_col))]
    full = lambda a: pl.BlockSpec(a.shape, lambda bi, ci: (0,) * a.ndim)
    out = jax.ShapeDtypeStruct((t, M_INNER), BF16)
    return pl.pallas_call(
        _ssd_kernel,
        grid=(bsz, nc),
        in_specs=data(fwd) + data(bwd) + [full(dtb), full(emat), full(alog_e), full(alog_c)],
        out_specs=[pl.BlockSpec((M_CHUNK, M_INNER), lambda bi, ci: (fwd(bi, ci), 0)),
                   pl.BlockSpec((M_CHUNK, M_INNER), lambda bi, ci: (bwd(bi, ci), 0))],
        out_shape=[out, out],
        scratch_shapes=[pltpu.VMEM((2, M_INNER // LANES, M_STATE, LANES), F32)],
        compiler_params=_cparams("parallel", "arbitrary"),
        name="ssd_scan",
    )(xs, bm, cm, dt_raw, xs, bm, cm, dt_raw, dtb, emat, alog_e, alog_c)


def _mamba_post(yf_ref, yb_ref, xs_ref, z_ref, d_skip, nw):
    ld = lambda ref: ref[...].astype(F32)
    y = (ld(yf_ref) + ld(yb_ref) + d_skip * ld(xs_ref)) * _silu(ld(z_ref))
    groups = [slice(g * M_GROUP_W, (g + 1) * M_GROUP_W) for g in range(M_GROUPS)]
    return jnp.concatenate([_rms(y[:, sl], nw[:, sl], 1e-5).astype(BF16) for sl in groups], axis=1)


def _diff_attn_kernel(zero_ref, q_ref, k_ref, v_ref, lam_ref, slope_ref, nw_ref, o_ref, bias_scr, s_scr, *, tq, tk,
                      lambda_init):
    seq = k_ref.shape[0]
    log2e = math.log2(math.e)

    @pl.when(pl.program_id(2) == 0)
    def _():
        rows = pl.program_id(1) * tq + lax.broadcasted_iota(jnp.int32, (tq, seq), 0)
        cols = lax.broadcasted_iota(jnp.int32, (tq, seq), 1)
        bias_scr[...] = (slope_ref[0][:, 0:1] * log2e) * jnp.abs(rows - cols).astype(F32)

    q = (q_ref[...].astype(F32) * (DF_HEAD ** -0.5 * log2e)).astype(BF16)
    lv = lam_ref[...]
    lam = (jnp.exp(jnp.sum(lv[0:1] * lv[1:2], keepdims=True))
           - jnp.exp(jnp.sum(lv[2:3] * lv[3:4], keepdims=True)) + lambda_init)
    map0 = lax.broadcasted_iota(jnp.int32, (tq, LANES), 1) < DF_HEAD
    qm = [jnp.where(map0, q, jnp.zeros_like(q)), jnp.where(map0, jnp.zeros_like(q), q)]
    tiles = [slice(t * tk, (t + 1) * tk) for t in range(seq // tk)]
    rt = min(LANES, tq)
    streams = [(mp, slice(rh * rt, (rh + 1) * rt)) for rh in range(tq // rt) for mp in range(2)]
    v1 = jnp.concatenate([v_ref[...], jnp.ones((seq, LANES), BF16)], axis=1)
    zero = zero_ref[0]

    def scores_tile(st, ts, mx):
        mp, rows = st
        s = _dot_nt(qm[mp][rows], k_ref[ts, :]) - bias_scr[rows, ts]
        s_scr[mp, rows, ts] = s
        for cs in range(tk // LANES):
            blk = s[:, cs * LANES:(cs + 1) * LANES]
            mx = blk if mx is None else jnp.maximum(mx, blk)
        return mx

    def weights(st, m):
        mp, rows = st
        return _dot(jnp.exp2(s_scr[mp + zero, rows, :] - m).astype(BF16), v1)

    m, acc = {}, {}
    for i in range(len(streams) + 1):
        if i < len(streams):
            mx = None
            for ts in tiles:
                mx = scores_tile(streams[i], ts, mx)
            m[i] = jnp.max(mx, axis=-1, keepdims=True)
        if i >= 1:
            acc[i - 1] = weights(streams[i - 1], m[i - 1])
        if i >= 2 and i % 2 == 0:
            rh = i // 2 - 1
            om = [acc[2 * rh + mp][:, 0:LANES] / acc[2 * rh + mp][:, LANES:LANES + 1] for mp in range(2)]
            o = om[0] - lam * om[1]
            o_ref[rh * rt:(rh + 1) * rt, :] = (_rms(o, nw_ref[...], 1e-5) * (1.0 - lambda_init)).astype(o_ref.dtype)


def _diff_attn_call(qkv, lam_vecs, slopes, subln_w, bsz, seq, lambda_init):
    t = qkv.shape[0]
    tq = min(1024, seq)
    nq = seq // tq
    return pl.pallas_call(
        functools.partial(_diff_attn_kernel, tq=tq, tk=min(512, seq), lambda_init=lambda_init),
        grid=(DF_HEADS, nq, bsz),
        in_specs=[pl.BlockSpec(memory_space=pltpu.SMEM),
                  pl.BlockSpec((tq, DF_V), lambda hi, qi, bi: (bi * nq + qi, hi)),
                  pl.BlockSpec((seq, DF_V), lambda hi, qi, bi: (bi, DF_HEADS + hi)),
                  pl.BlockSpec((seq, DF_V), lambda hi, qi, bi: (bi, 2 * DF_HEADS + hi)),
                  pl.BlockSpec((4, DF_HEAD), lambda hi, qi, bi: (0, 0)),
                  pl.BlockSpec((1, 1, LANES), lambda hi, qi, bi: (hi, 0, 0)),
                  pl.BlockSpec((1, DF_V), lambda hi, qi, bi: (0, 0))],
        out_specs=pl.BlockSpec((tq, DF_V), lambda hi, qi, bi: (bi * nq + qi, hi)),
        out_shape=jax.ShapeDtypeStruct((t, DF_HEADS * DF_V), BF16),
        scratch_shapes=[pltpu.VMEM((tq, seq), F32), pltpu.VMEM((2, tq, seq), F32)],
        compiler_params=_cparams("parallel", "parallel", "arbitrary"),
        name="diff_attn",
    )(jnp.zeros((1,), jnp.int32), qkv, qkv, qkv, lam_vecs, slopes, subln_w)


def _mixer_tail_kernel(yf_ref, yb_ref, r_ref, kf_ref, kb_ref, v_ref, g_ref, pvec_ref, mf_ref, mb_ref, xs_ref, z_ref,
                       dskip_ref, mnw_ref, yd_ref, pg_ref, x_ref, mod_ref, nw_ref, wr_ref, wm_ref, wd_ref, wo_ref,
                       o_ref):
    d = D_MODEL
    y_r = _rwkv_post(yf_ref, yb_ref, r_ref, kf_ref, kb_ref, v_ref, g_ref, pvec_ref[...])
    y_m = _mamba_post(mf_ref, mb_ref, xs_ref, z_ref, dskip_ref[...], mnw_ref[...])
    gate = lambda g: _sigmoid(pg_ref[:, g * d:(g + 1) * d].astype(F32))
    merged = (gate(0) * _dot(y_r, wr_ref[...]) + gate(1) * _dot(y_m, wm_ref[...])
              + gate(2) * _dot(yd_ref[...], wd_ref[...]))
    y = _dot(merged.astype(BF16), wo_ref[...])
    o_ref[...] = x_ref[...] + mod_ref[0][2:3] * _rms(y, nw_ref[...], EPS)


def _mixer_tail_call(rw, pvec, mm, d_e, mnw, yd, pg, x2, mod_sub, nw, wr, wm, wd, wo, seq):
    t, d = x2.shape
    tm = 256
    row = lambda w: pl.BlockSpec((tm, w), lambda i: (i, 0))
    vec = lambda a: pl.BlockSpec(a.shape, lambda i: (0, 0))
    res = lambda a: pl.BlockSpec(a.shape, lambda i: (0, 0), pipeline_mode=pl.Buffered(1))
    return pl.pallas_call(
        _mixer_tail_kernel,
        grid=(t // tm,),
        in_specs=([row(RW_DIM)] * 7 + [vec(pvec)] + [row(M_INNER)] * 4 + [vec(d_e), vec(mnw)]
                  + [row(DF_HEADS * DF_V), row(GATE_COLS), row(d),
                     pl.BlockSpec((1, 3, d), lambda i: (i * tm // seq, 0, 0)), vec(nw),
                     res(wr), res(wm), res(wd), res(wo)]),
        out_specs=row(d),
        out_shape=jax.ShapeDtypeStruct((t, d), F32),
        compiler_params=_cparams("parallel"),
        name="mixer_tail",
    )(*rw, pvec, *mm, d_e, mnw, yd, pg, x2, mod_sub, nw, wr, wm, wd, wo)


def _pad_cols(w, n):
    return jnp.pad(w, ((0, 0), (0, n - w.shape[1])))


def _pad_rows(w, n):
    return jnp.pad(w, ((0, n - w.shape[0]), (0, 0)))


def _dir_padded(w):
    z = jnp.zeros_like(w[0])
    return jnp.stack([jnp.concatenate([w[0], z], axis=0), jnp.concatenate([z, w[1]], axis=0)])


def _head_expand_matrix():
    e = np.zeros((2, LANES, M_INNER), np.float32)
    for d in range(2):
        for h in range(M_HEADS):
            e[d, d * M_HEADS + h, h * M_HEADDIM:(h + 1) * M_HEADDIM] = 1.0
    return jnp.asarray(e, dtype=BF16)


def kernel(x, c, ada_w, ada_b, norm_w, ffn_w13, ffn_w2, w_in, rwkv_mu, rwkv_w0, rwkv_w2, rwkv_a0, rwkv_a2, rwkv_g2, rwkv_k_k, rwkv_k_a, rwkv_r_k, rwkv_ln_w, rwkv_ln_b, rwkv_v0, rwkv_v1, rwkv_v2, mamba_conv_w, mamba_conv_b, mamba_dt_bias, mamba_a_log, mamba_d, mamba_norm_w, diff_lambda, diff_subln_w, w_branch_rwkv, w_branch_mamba, w_branch_diff, w_out):
    bsz, seq, d = x.shape
    depth = ada_w.shape[0]
    t = bsz * seq
    x2 = x.reshape(t, d)
    mod_all = _mod_call(c, ada_w, ada_b)
    emat = _head_expand_matrix()
    slopes = jnp.broadcast_to(
        jnp.asarray(2.0 ** (-8.0 * np.arange(1, DF_HEADS + 1) / DF_HEADS), F32)[:, None, None], (DF_HEADS, 1, LANES))
    v_first = None
    for l in range(depth):
        mod = mod_all[l].reshape(bsz, N_SUB, 3, d)
        x2 = _ffn_call(x2, mod[:, 0], norm_w[l, 0:2], ffn_w13[l, 0].astype(BF16), ffn_w2[l, 0].astype(BF16), seq)

        wl = w_in[l]
        o_m = RW_COLS
        o_d = o_m + M_COLS
        o_g = o_d + DF_COLS
        w_cat = jnp.concatenate([
            wl[:, 0:3 * RW_DIM],
            _pad_cols(wl[:, 3 * RW_DIM:RW_COLS], RW_CODES_PAD),
            _pad_cols(wl[:, o_m + M_INNER + M_XBC:o_d], INPROJ_TN - RW_CODES_PAD),
            wl[:, o_m:o_m + M_INNER],
            wl[:, o_m + M_INNER:o_m + M_INNER + M_XBC],
            wl[:, o_d:o_g],
            wl[:, o_g:],
        ], axis=1).astype(BF16)
        p_rkv, p_aux, p_z, p_xbc, p_qkv, p_gate = _inproj_call(
            x2, mod[:, 1], norm_w[l, 2:3], w_cat,
            (3 * RW_DIM, INPROJ_TN, M_INNER, M_XBC, DF_COLS, GATE_COLS),
            (BF16, F32, BF16, BF16, BF16, BF16), seq)
        p_codes = p_dt = p_aux

        mu = rwkv_mu[l]
        mu_p = mu[None, 0:3 * RW_DIM]
        mu_c = _pad_cols(mu[None, 3 * RW_DIM:], RW_CODES_PAD)
        v0 = rwkv_v0[l - 1] if l > 0 else jnp.zeros((RW_DIM,), F32)
        pvec = jnp.stack([rwkv_w0[l, 0], rwkv_w0[l, 1], rwkv_a0[l, 0], rwkv_a0[l, 1],
                          rwkv_k_k[l], rwkv_k_a[l], v0, jnp.zeros((RW_DIM,), F32)])
        g2p = _pad_rows(rwkv_g2[l], 256).astype(BF16)
        vres = None
        if l > 0:
            vres = (v_first, _pad_cols(rwkv_v1[l - 1], LANES).astype(BF16),
                    _pad_rows(rwkv_v2[l - 1], LANES).astype(BF16))
        r, v, kk, k_f, k_b, b_f, b_b, lw_f, lw_b, gate = _rwkv_prep_call(
            p_rkv, p_codes, mu_p, mu_c, pvec, _dir_padded(rwkv_w2[l]).astype(BF16), _dir_padded(rwkv_a2[l]).astype(BF16), g2p, vres, seq)
        if l == 0:
            v_first = v
        y_f, y_b = _rwkv_scan_call(r, v, kk, k_f, b_f, lw_f, k_b, b_b, lw_b, bsz, seq)
        pvec2 = jnp.concatenate([jnp.stack([rwkv_ln_w[l], rwkv_ln_b[l], rwkv_r_k[l].reshape(RW_DIM)]),
                                 jnp.zeros((5, RW_DIM), F32)])

        conv_w = _pad_rows(mamba_conv_w[l], 8)
        xs, bm, cm = _mamba_conv_call(p_xbc, conv_w, mamba_conv_b[l][None], seq)
        dtb = _pad_cols(mamba_dt_bias[l].reshape(1, 2 * M_HEADS), LANES)
        alog_e = jnp.repeat(mamba_a_log[l], M_HEADDIM, axis=-1)[:, None]
        alog_c = _pad_cols(mamba_a_log[l].reshape(1, 2 * M_HEADS), LANES)
        ym_f, ym_b = _ssd_call(xs, bm, cm, p_dt, dtb, emat, alog_e, alog_c, bsz, seq)
        d_e = jnp.repeat(mamba_d[l], M_HEADDIM)[None]

        lambda_init = 0.8 - 0.6 * math.exp(-0.3 * l)
        y_d = _diff_attn_call(p_qkv, diff_lambda[l], slopes, diff_subln_w[l][None], bsz, seq, lambda_init)

        x2 = _mixer_tail_call((y_f, y_b, r, k_f, k_b, v, gate), pvec2, (ym_f, ym_b, xs, p_z), d_e,
                              mamba_norm_w[l][None], y_d, p_gate, x2, mod[:, 1], norm_w[l, 3:4],
                              w_branch_rwkv[l].astype(BF16), w_branch_mamba[l].astype(BF16),
                              w_branch_diff[l].astype(BF16), w_out[l].astype(BF16), seq)

        x2 = _ffn_call(x2, mod[:, 2], norm_w[l, 4:6], ffn_w13[l, 1].astype(BF16), ffn_w2[l, 1].astype(BF16), seq)
    return x2.reshape(bsz, seq, d)
```

```python
import functools
import itertools
import math

import numpy as np
import jax
import jax.numpy as jnp
from jax import lax
from jax.experimental import pallas as pl
from jax.experimental.pallas import tpu as pltpu

F32 = jnp.float32
BF16 = jnp.bfloat16
HI = lax.Precision.HIGHEST

D_MODEL = 1024
N_SUB = 3
EPS = 1e-6
LANES = 128
SUBLANES = 8
HALO = 2 * SUBLANES
VMEM_LIMIT = 56 * 1024 * 1024

RW_HEAD = 64
RW_DIM = 1024
W_LORA = 64
A_LORA = 64
V_LORA = 32
G_LORA = 160
RW_GN_EPS = 64e-5
RW_COLS = 3 * RW_DIM + 2 * W_LORA + 2 * A_LORA + G_LORA
RW_CODES_PAD = 512
RW_CHUNK = 64
M_INNER = 2048
M_HEADS = 32
M_HEADDIM = 64
M_GROUPS = 4
M_STATE = 128
M_CONV = 5
M_CHUNK = 128
M_XBC = M_INNER + 2 * M_GROUPS * M_STATE
M_COLS = M_INNER + M_XBC + 2 * M_HEADS
M_GROUP_W = M_INNER // M_GROUPS
CONV_SUB = 128
DF_HEADS = 8
DF_HEAD = 64
DF_V = 128
DF_COLS = 3 * DF_HEADS * 2 * DF_HEAD
GATE_COLS = 3 * D_MODEL
D_FF = 2816
NEG_BIG = -1e30


def _cparams(*sem):
    return pltpu.CompilerParams(dimension_semantics=sem, vmem_limit_bytes=VMEM_LIMIT)


def _dot(a, b, **kw):
    return jnp.dot(a, b, preferred_element_type=F32, **kw)


def _dot_nt(a, b, **kw):
    return lax.dot_general(a, b, (((1,), (1,)), ((), ())), preferred_element_type=F32, **kw)


def _split2(x):
    hi = x.astype(BF16)
    return hi, (x - hi.astype(F32)).astype(BF16)


def _dot_sel(x, m_b):
    hi, lo = _split2(x)
    return _dot(hi, m_b) + _dot(lo, m_b)


def _sel_dot(m_b, x):
    hi, lo = _split2(x)
    return _dot(m_b, hi) + _dot(m_b, lo)


def _bdot(a, b):
    return _dot(a.astype(BF16), b.astype(BF16))


def _sigmoid(x):
    return 1.0 / (1.0 + jnp.exp(-x))


def _silu(x):
    return x * _sigmoid(x)


def _softplus(x):
    return jnp.maximum(x, 0.0) + jnp.log(1.0 + jnp.exp(-jnp.abs(x)))


def _rms(x, w, eps):
    return x * lax.rsqrt(jnp.mean(x * x, axis=-1, keepdims=True) + eps) * w


def _norm_mod(x, nw, shift, scale):
    return _rms(x, nw, EPS) * (1.0 + scale) + shift


def _pair_ones(scale=1.0):
    r = lax.broadcasted_iota(jnp.int32, (LANES, LANES), 0) // RW_HEAD
    c = lax.broadcasted_iota(jnp.int32, (LANES, LANES), 1) // RW_HEAD
    return jnp.where(r == c, scale, 0.0).astype(BF16)


def _shift_rows(cur, prevz, nextz, d):
    tm = cur.shape[0]
    rolled = pltpu.roll(cur, (-d) % tm, axis=0)
    rid = lax.broadcasted_iota(jnp.int32, (SUBLANES, cur.shape[1]), 0)
    if d < 0:
        fix = pltpu.roll(prevz, (-d) % SUBLANES, axis=0)
        top = jnp.where(rid < -d, fix, rolled[0:SUBLANES])
        return jnp.concatenate([top, rolled[SUBLANES:]], axis=0)
    fix = pltpu.roll(nextz, (SUBLANES - d) % SUBLANES, axis=0)
    bot = jnp.where(rid >= SUBLANES - d, fix, rolled[tm - SUBLANES:])
    return jnp.concatenate([rolled[:tm - SUBLANES], bot], axis=0)


def _shift_rows_mxu(cur_b, prevz, nextz, d):
    assert cur_b.dtype == BF16
    tm = cur_b.shape[0]
    ri = lax.broadcasted_iota(jnp.int32, (tm, tm), 0)
    ci = lax.broadcasted_iota(jnp.int32, (tm, tm), 1)
    shifted = _dot(jnp.where(ci == ri + d, 1.0, 0.0).astype(BF16), cur_b)
    rid = lax.broadcasted_iota(jnp.int32, (SUBLANES, cur_b.shape[1]), 0)
    if d < 0:
        fix = jnp.where(rid < -d, pltpu.roll(prevz, (-d) % SUBLANES, axis=0), 0.0)
        return jnp.concatenate([shifted[0:SUBLANES] + fix, shifted[SUBLANES:]], axis=0)
    fix = jnp.where(rid >= SUBLANES - d, pltpu.roll(nextz, (SUBLANES - d) % SUBLANES, axis=0), 0.0)
    return jnp.concatenate([shifted[:tm - SUBLANES], shifted[tm - SUBLANES:] + fix], axis=0)


def _nbr_mean_minus_self_mxu(cur_b, prevz, nextz):
    assert cur_b.dtype == BF16
    tm = cur_b.shape[0]
    ri = lax.broadcasted_iota(jnp.int32, (tm, tm), 0)
    ci = lax.broadcasted_iota(jnp.int32, (tm, tm), 1)
    band = jnp.where(ci == ri, -1.0, jnp.where(jnp.abs(ci - ri) == 1, 0.5, 0.0)).astype(BF16)
    out = _dot(band, cur_b)
    rid = lax.broadcasted_iota(jnp.int32, (SUBLANES, cur_b.shape[1]), 0)
    top = jnp.where(rid == 0, 0.5 * prevz[SUBLANES - 1:SUBLANES], 0.0)
    bot = jnp.where(rid == SUBLANES - 1, 0.5 * nextz[0:1], 0.0)
    return jnp.concatenate([out[0:SUBLANES] + top, out[SUBLANES:tm - SUBLANES], out[tm - SUBLANES:] + bot], axis=0)


def _halo_specs(tm, width, n_rows):
    rb = tm // HALO
    last = n_rows // HALO - 1
    return [
        pl.BlockSpec((tm, width), lambda i: (i, 0)),
        pl.BlockSpec((HALO, width), lambda i: (jnp.maximum(i * rb - 1, 0), 0)),
        pl.BlockSpec((HALO, width), lambda i: (jnp.minimum((i + 1) * rb, last), 0)),
    ]


def _halo_values(cur_ref, prev_ref, next_ref, tm, seq):
    i = pl.program_id(0)
    first = (i * tm) % seq == 0
    last = ((i + 1) * tm) % seq == 0
    prevz = jnp.where(first, 0.0, prev_ref[...].astype(F32)[HALO - SUBLANES:])
    nextz = jnp.where(last, 0.0, next_ref[...].astype(F32)[:SUBLANES])
    return cur_ref[...].astype(F32), prevz, nextz


def _mod_kernel(c_ref, w_ref, b_ref, o_ref):
    o_ref[0] = _dot(_silu(c_ref[...]), w_ref[0], precision=HI) + b_ref[0]


def _mod_call(c, ada_w, ada_b):
    depth, d, n = ada_w.shape
    bsz = c.shape[0]
    tn = 1152
    return pl.pallas_call(
        _mod_kernel,
        grid=(depth, n // tn),
        in_specs=[pl.BlockSpec((bsz, d), lambda l, j: (0, 0)),
                  pl.BlockSpec((1, d, tn), lambda l, j: (l, 0, j)),
                  pl.BlockSpec((1, 1, tn), lambda l, j: (l, 0, j))],
        out_specs=pl.BlockSpec((1, bsz, tn), lambda l, j: (l, 0, j)),
        out_shape=jax.ShapeDtypeStruct((depth, bsz, n), F32),
        compiler_params=_cparams("parallel", "parallel"),
        name="adaln_mod",
    )(c, ada_w, ada_b.reshape(depth, 1, n))


def _ffn_kernel(x_ref, mod_ref, nw_ref, w13_ref, w2_ref, o_ref, *, tf):
    dff = w2_ref.shape[0]
    m = mod_ref[0]
    x = x_ref[...]
    h = _norm_mod(x, nw_ref[0:1], m[0:1], m[1:2]).astype(BF16)
    acc = None
    for j in range(dff // tf):
        g = _dot(h, w13_ref[:, j * tf:(j + 1) * tf])
        u = _dot(h, w13_ref[:, dff + j * tf:dff + (j + 1) * tf])
        part = _dot((_silu(g) * u).astype(BF16), w2_ref[j * tf:(j + 1) * tf, :])
        acc = part if acc is None else acc + part
    o_ref[...] = x + 0.5 * m[2:3] * _rms(acc, nw_ref[1:2], EPS)


def _ffn_call(x2, mod_sub, nw2, w13, w2, seq):
    t, d = x2.shape
    tm = min(1024, seq)
    resident = lambda a: pl.BlockSpec(a.shape, lambda i: (0, 0), pipeline_mode=pl.Buffered(1))
    return pl.pallas_call(
        functools.partial(_ffn_kernel, tf=256),
        grid=(t // tm,),
        in_specs=[pl.BlockSpec((tm, d), lambda i: (i, 0)),
                  pl.BlockSpec((1, 3, d), lambda i: (i * tm // seq, 0, 0)),
                  pl.BlockSpec((2, d), lambda i: (0, 0)),
                  resident(w13), resident(w2)],
        out_specs=pl.BlockSpec((tm, d), lambda i: (i, 0)),
        out_shape=jax.ShapeDtypeStruct((t, d), F32),
        compiler_params=_cparams("parallel"),
        name="swiglu_halfstep",
    )(x2, mod_sub, nw2, w13, w2)


INPROJ_TN = 1024


def _inproj_kernel(x_ref, mod_ref, nw_ref, w_ref, *rest, starts):
    o_refs, h_scr = rest[:-1], rest[-1]
    j = pl.program_id(1)

    @pl.when(j == 0)
    def _():
        m = mod_ref[0]
        h_scr[...] = _norm_mod(x_ref[...], nw_ref[...], m[0:1], m[1:2]).astype(BF16)

    for k, o_ref in enumerate(o_refs):
        @pl.when((j >= starts[k]) & (j < starts[k + 1]))
        def _(o_ref=o_ref):
            o_ref[...] = _dot(h_scr[...], w_ref[...]).astype(o_ref.dtype)


def _inproj_call(x2, mod_sub, nw, w, widths, dtypes, seq):
    t, d = x2.shape
    tn = INPROJ_TN
    tm = min(1024, seq)
    starts = [0]
    for wd in widths:
        starts.append(starts[-1] + wd // tn)

    def out_spec(k):
        return pl.BlockSpec((tm, tn), lambda i, j: (i, jnp.clip(j - starts[k], 0, widths[k] // tn - 1)))

    return pl.pallas_call(
        functools.partial(_inproj_kernel, starts=tuple(starts)),
        grid=(t // tm, starts[-1]),
        in_specs=[pl.BlockSpec((tm, d), lambda i, j: (i, 0)),
                  pl.BlockSpec((1, 3, d), lambda i, j: (i * tm // seq, 0, 0)),
                  pl.BlockSpec((1, d), lambda i, j: (0, 0)),
                  pl.BlockSpec((d, tn), lambda i, j: (0, j))],
        out_specs=[out_spec(k) for k in range(len(widths))],
        out_shape=[jax.ShapeDtypeStruct((t, wd), dt) for wd, dt in zip(widths, dtypes)],
        scratch_shapes=[pltpu.VMEM((tm, d), BF16)],
        compiler_params=_cparams("parallel", "arbitrary"),
        name="norm_inproj",
    )(x2, mod_sub, nw, w)


def _rwkv_prep_kernel(*refs, tm, seq, has_vres):
    (p_ref, pp_ref, pn_ref, c_ref, cp_ref, cn_ref, mup_ref, muc_ref, pvec_ref,
     w2_ref, a2_ref, g2_ref) = refs[:12]
    refs = refs[12:]
    if has_vres:
        vf_ref, v1_ref, v2_ref = refs[:3]
        refs = refs[3:]
    r_o, v_o, kk_o, kf_o, kb_o, bf_o, bb_o, lwf_o, lwb_o, g_o = refs

    def shift_mix(cur_ref, prev_ref, next_ref, mu):
        cur, prevz, nextz = _halo_values(cur_ref, prev_ref, next_ref, tm, seq)
        if cur_ref.dtype == BF16:
            return cur + mu * _nbr_mean_minus_self_mxu(cur_ref[...], prevz, nextz)
        nb = 0.5 * (_shift_rows(cur, prevz, nextz, -1) + _shift_rows(cur, prevz, nextz, 1))
        return cur + mu * (nb - cur)

    p = shift_mix(p_ref, pp_ref, pn_ref, mup_ref[...])
    codes = shift_mix(c_ref, cp_ref, cn_ref, muc_ref[...])
    r = p[:, 0:RW_DIM]
    k = p[:, RW_DIM:2 * RW_DIM]
    v = p[:, 2 * RW_DIM:3 * RW_DIM]
    cw = jnp.tanh(codes[:, 0:2 * W_LORA])
    ca = codes[:, 2 * W_LORA:2 * W_LORA + 2 * A_LORA]
    cg = _sigmoid(codes[:, 2 * W_LORA + 2 * A_LORA:])
    pv = pvec_ref[...]
    if has_vres:
        lo = _bdot(_bdot(v, v1_ref[...]), v2_ref[...])
        v = v + (vf_ref[...].astype(F32) - v) * _sigmoid(pv[6:7] + lo)
    r_o[...] = r.astype(r_o.dtype)
    v_o[...] = v.astype(v_o.dtype)
    g_o[...] = _bdot(cg, g2_ref[...]).astype(g_o.dtype)
    iclr = []
    for d, lw_o in ((0, lwf_o), (1, lwb_o)):
        lw_o[...] = -math.exp(-0.5) * _sigmoid(pv[d:d + 1] + _bdot(cw, w2_ref[d]))
        iclr.append(_sigmoid(pv[2 + d:3 + d] + _bdot(ca, a2_ref[d])))
    kf_o[...] = (k * (1.0 + (iclr[0] - 1.0) * pv[5:6])).astype(kf_o.dtype)
    kb_o[...] = (k * (1.0 + (iclr[1] - 1.0) * pv[5:6])).astype(kb_o.dtype)
    ones2 = _pair_ones()
    kkr = k * pv[4:5]
    for cb in range(RW_DIM // LANES):
        sl = slice(cb * LANES, (cb + 1) * LANES)
        blk = kkr[:, sl]
        kkn = blk * lax.rsqrt(_dot_sel(blk * blk, ones2) + 1e-12)
        kk_o[:, sl] = kkn.astype(kk_o.dtype)
        bf_o[:, sl] = (kkn * iclr[0][:, sl]).astype(bf_o.dtype)
        bb_o[:, sl] = (kkn * iclr[1][:, sl]).astype(bb_o.dtype)


def _rwkv_prep_call(rkv, codes, mu_p, mu_c, pvec, w2p, a2p, g2p, vres, seq):
    t = rkv.shape[0]
    tm = 256
    has_vres = vres is not None
    full = lambda shape: pl.BlockSpec(shape, lambda i: (0,) * len(shape))
    in_specs = (_halo_specs(tm, 3 * RW_DIM, t) + _halo_specs(tm, RW_CODES_PAD, t)
                + [full((1, 3 * RW_DIM)), full((1, RW_CODES_PAD)), full((8, RW_DIM)),
                   full((2, 2 * W_LORA, RW_DIM)), full((2, 2 * A_LORA, RW_DIM)), full((256, RW_DIM))])
    args = [rkv, rkv, rkv, codes, codes, codes, mu_p, mu_c, pvec, w2p, a2p, g2p]
    if has_vres:
        in_specs += [pl.BlockSpec((tm, RW_DIM), lambda i: (i, 0)), full((RW_DIM, LANES)), full((LANES, RW_DIM))]
        args += list(vres)
    dtypes = [BF16] * 7 + [F32, F32, BF16]
    return pl.pallas_call(
        functools.partial(_rwkv_prep_kernel, tm=tm, seq=seq, has_vres=has_vres),
        grid=(t // tm,),
        in_specs=in_specs,
        out_specs=[pl.BlockSpec((tm, RW_DIM), lambda i: (i, 0))] * 10,
        out_shape=[jax.ShapeDtypeStruct((t, RW_DIM), dt) for dt in dtypes],
        compiler_params=_cparams("parallel"),
        name="rwkv_prep",
    )(*args)


def _rwkv_scan_kernel(rf_ref, vf_ref, kkf_ref, kf_ref, bf_ref, lwf_ref, rb_ref, vb_ref, kkb_ref, kb_ref, bb_ref,
                      lwb_ref, yf_ref, yb_ref, h_scr, *, n_chunks):
    c = RW_CHUNK
    n_pairs = RW_DIM // LANES

    @pl.when(pl.program_id(1) == 0)
    def _():
        h_scr[...] = jnp.zeros_like(h_scr)

    ri = lax.broadcasted_iota(jnp.int32, (LANES, LANES), 0)
    ci = lax.broadcasted_iota(jnp.int32, (LANES, LANES), 1)
    same = (ri // c) == (ci // c)
    tr, tc = ri % c, ci % c
    strict = [same & (tc < tr), same & (tc > tr)]
    incl = [same & (tc <= tr), same & (tc >= tr)]
    blk = [(ri // w) == (ci // w) for w in (8, 16, 32, 64)]
    eye = jnp.where(ri == ci, 1.0, 0.0).astype(F32)
    r64 = lax.broadcasted_iota(jnp.int32, (c, c), 0)
    c64 = lax.broadcasted_iota(jnp.int32, (c, c), 1)
    tri = [jnp.where(c64 <= r64, 1.0, 0.0).astype(BF16), jnp.where(c64 >= r64, 1.0, 0.0).astype(BF16)]
    head0 = lax.broadcasted_iota(jnp.int32, (c, LANES), 1) < RW_HEAD
    end_row = [c - 1, 0]
    refs = [(rf_ref, vf_ref, kkf_ref, kf_ref, bf_ref, lwf_ref, yf_ref),
            (rb_ref, vb_ref, kkb_ref, kb_ref, bb_ref, lwb_ref, yb_ref)]
    lns = [slice(pr * LANES, (pr + 1) * LANES) for pr in range(n_pairs)]
    chains = [(d, p) for p in range(n_pairs) for d in range(2)]

    def stack(x):
        return jnp.concatenate([jnp.where(head0, x, 0.0), jnp.where(head0, 0.0, x)], axis=0)

    def chunk(step, carry):
        rows = [pl.ds(pl.multiple_of(step * c, c), c), pl.ds(pl.multiple_of((n_chunks - 1 - step) * c, c), c)]
        wide = []
        for d in range(2):
            r_ref, v_ref, kk_ref, k_ref, b_ref, lw_ref, _ = refs[d]
            lw = lw_ref[rows[d], :]
            ld = lambda ref: ref[rows[d], :].astype(F32)
            cum = _sel_dot(tri[d], lw)
            tot = cum[end_row[d]:end_row[d] + 1]
            g_inv, g_end = jnp.exp(-cum), jnp.exp(tot - cum)
            k_all, b_all = ld(k_ref), ld(b_ref)
            wide.append(dict(at=-ld(kk_ref) * jnp.exp(cum - lw), rt=ld(r_ref) * jnp.exp(cum),
                             bt=b_all * g_inv, kt=k_all * g_inv, bg=b_all * g_end, kg=k_all * g_end,
                             v=ld(v_ref), g_tot=jnp.exp(tot)))
        ar_s = [jnp.concatenate([stack(wide[d]["at"][:, lns[p]]), stack(wide[d]["rt"][:, lns[p]])], axis=0)
                .astype(BF16) for d, p in chains]
        bk_s = [jnp.concatenate([stack(wide[d]["bt"][:, lns[p]]), stack(wide[d]["kt"][:, lns[p]])], axis=0)
                .astype(BF16) for d, p in chains]
        v_s = [stack(wide[d]["v"][:, lns[p]]) for d, p in chains]
        ht = [h_scr[d, p] for d, p in chains]
        idx = range(len(chains))
        gram = [_dot_nt(ar_s[i], bk_s[i]) for i in idx]
        a_ab = [jnp.where(strict[d], gram[i][0:LANES, 0:LANES], 0.0) for i, (d, p) in enumerate(chains)]
        lhs = [jnp.concatenate([
            ar_s[i],
            jnp.concatenate([jnp.where(strict[d], gram[i][0:LANES, LANES:], 0.0),
                             jnp.where(incl[d], gram[i][LANES:, LANES:], 0.0)], axis=0).astype(BF16)], axis=1)
            for i, (d, p) in enumerate(chains)]
        rhs = [jnp.concatenate([ht[i].T, v_s[i]], axis=0).astype(BF16) for i in idx]
        xy0 = [_dot(lhs[i], rhs[i]) for i in idx]
        a_rb = [jnp.where(incl[d], gram[i][LANES:, 0:LANES], 0.0).astype(BF16) for i, (d, p) in enumerate(chains)]
        p1 = [jnp.where(blk[0], a_ab[i], 0.0).astype(BF16) for i in idx]
        p2 = [_dot(p1[i], p1[i]) for i in idx]
        p2_b = [x.astype(BF16) for x in p2]
        p4_b = [_dot(p2_b[i], p2_b[i]).astype(BF16) for i in idx]
        t_inv = [eye + p1[i] + p2[i] + _dot(p1[i], p2_b[i]) for i in idx]
        t_inv = [t_inv[i] + _dot(t_inv[i].astype(BF16), p4_b[i]) for i in idx]
        for lvl in range(1, 4):
            off = [jnp.where(blk[lvl] & ~blk[lvl - 1], a_ab[i], 0.0).astype(BF16) for i in idx]
            t_b = [x.astype(BF16) for x in t_inv]
            mid = [_dot(t_b[i], off[i]).astype(BF16) for i in idx]
            t_inv = [t_inv[i] + _dot(mid[i], t_b[i]) for i in idx]
        u = [_dot(t_inv[i].astype(BF16), xy0[i][0:LANES].astype(BF16)) for i in idx]
        y = [xy0[i][LANES:] + _dot(a_rb[i], u[i].astype(BF16)) for i in idx]
        for d in range(2):
            ys = [y[i][0:c] + y[i][c:2 * c] for i, (dd, p) in enumerate(chains) if dd == d]
            refs[d][6][rows[d], :] = jnp.concatenate(ys, axis=1).astype(refs[d][6].dtype)
        for i, (d, p) in enumerate(chains):
            uv_t = jnp.concatenate([u[i], v_s[i]], axis=0).T.astype(BF16)
            bkg = jnp.concatenate([stack(wide[d]["bg"][:, lns[p]]), stack(wide[d]["kg"][:, lns[p]])],
                                  axis=0).astype(BF16)
            h_scr[d, p] = ht[i] * wide[d]["g_tot"][:, lns[p]] + _dot(uv_t, bkg)
        return carry

    lax.fori_loop(0, n_chunks, chunk, 0)


def _rwkv_scan_call(r, v, kk, k_f, b_f, lw_f, k_b, b_b, lw_b, bsz, seq):
    t = r.shape[0]
    tb = min(4 * RW_CHUNK, seq)
    nt = seq // tb
    fwd = pl.BlockSpec((tb, RW_DIM), lambda bi, ti: (bi * nt + ti, 0))
    bwd = pl.BlockSpec((tb, RW_DIM), lambda bi, ti: (bi * nt + nt - 1 - ti, 0))
    out = jax.ShapeDtypeStruct((t, RW_DIM), BF16)
    return pl.pallas_call(
        functools.partial(_rwkv_scan_kernel, n_chunks=tb // RW_CHUNK),
        grid=(bsz, nt),
        in_specs=[fwd] * 6 + [bwd] * 6,
        out_specs=[fwd, bwd],
        out_shape=[out, out],
        scratch_shapes=[pltpu.VMEM((2, RW_DIM // LANES, LANES, LANES), F32)],
        compiler_params=_cparams("parallel", "arbitrary"),
        name="rwkv_scan",
    )(r, v, kk, k_f, b_f, lw_f, r, v, kk, k_b, b_b, lw_b)


def _rwkv_post(yf_ref, yb_ref, r_ref, kf_ref, kb_ref, v_ref, g_ref, pv):
    mean2 = _pair_ones(1.0 / RW_HEAD)
    ones2 = _pair_ones()
    outs = []
    for cb in range(RW_DIM // LANES):
        sl = slice(cb * LANES, (cb + 1) * LANES)
        ld = lambda ref: ref[:, sl].astype(F32)
        y = ld(yf_ref) + ld(yb_ref)
        dlt = y - _dot_sel(y, mean2)
        var = _dot_sel(dlt * dlt, mean2)
        yn = dlt * lax.rsqrt(var + RW_GN_EPS) * pv[0:1, sl] + pv[1:2, sl]
        rk = ld(r_ref) * (ld(kf_ref) + ld(kb_ref)) * pv[2:3, sl]
        bonus = _dot_sel(rk, ones2) * ld(v_ref)
        outs.append(((yn + bonus) * ld(g_ref)).astype(BF16))
    return jnp.concatenate(outs, axis=1)


def _mamba_conv_kernel(x_ref, xp_ref, xn_ref, w_ref, b_ref, xs_o, bm_o, cm_o, *, tm, seq):
    _, prevz, nextz = _halo_values(x_ref, xp_ref, xn_ref, tm, seq)
    w = w_ref[...]
    half = (M_CONV - 1) // 2
    sb = CONV_SUB
    n_sub = tm // sb
    for s in range(n_sub):
        rows = slice(s * sb, (s + 1) * sb)
        cur_b = x_ref[rows, :]
        pz = prevz if s == 0 else x_ref[s * sb - HALO:s * sb, :].astype(F32)[HALO - SUBLANES:]
        nz = nextz if s == n_sub - 1 else x_ref[(s + 1) * sb:(s + 1) * sb + HALO, :].astype(F32)[:SUBLANES]
        acc = cur_b.astype(F32) * w[half:half + 1] + b_ref[...]
        for tap in range(M_CONV):
            if tap != half:
                acc = acc + _shift_rows_mxu(cur_b, pz, nz, tap - half) * w[tap:tap + 1]
        y = _silu(acc)
        xs_o[rows, :] = y[:, 0:M_INNER].astype(xs_o.dtype)
        bm_o[rows, :] = y[:, M_INNER:M_INNER + M_GROUPS * M_STATE].astype(bm_o.dtype)
        cm_o[rows, :] = y[:, M_INNER + M_GROUPS * M_STATE:].astype(cm_o.dtype)


def _mamba_conv_call(xbc, conv_w, conv_b, seq):
    t = xbc.shape[0]
    tm = min(512, seq)
    gs = M_GROUPS * M_STATE
    return pl.pallas_call(
        functools.partial(_mamba_conv_kernel, tm=tm, seq=seq),
        grid=(t // tm,),
        in_specs=_halo_specs(tm, M_XBC, t) + [pl.BlockSpec((8, M_XBC), lambda i: (0, 0)),
                                             pl.BlockSpec((1, M_XBC), lambda i: (0, 0))],
        out_specs=[pl.BlockSpec((tm, M_INNER), lambda i: (i, 0)),
                   pl.BlockSpec((tm, gs), lambda i: (i, 0)),
                   pl.BlockSpec((tm, gs), lambda i: (i, 0))],
        out_shape=[jax.ShapeDtypeStruct((t, M_INNER), BF16),
                   jax.ShapeDtypeStruct((t, gs), BF16),
                   jax.ShapeDtypeStruct((t, gs), BF16)],
        compiler_params=_cparams("parallel"),
        name="mamba_conv",
    )(xbc, xbc, xbc, conv_w, conv_b)


def _ssd_chunk(d, xs_ref, bm_ref, cm_ref, dt_ref, dtb_ref, e_ref, alogc_ref, y_ref, h_scr):
    c = M_CHUNK
    n_pairs = M_INNER // LANES
    ppg = n_pairs // M_GROUPS
    ri = lax.broadcasted_iota(jnp.int32, (c, c), 0)
    ci = lax.broadcasted_iota(jnp.int32, (c, c), 1)
    keep = (ci >= ri) if d else (ci <= ri)
    tri = jnp.where(keep, 1.0, 0.0).astype(BF16)
    end_row = 0 if d else c - 1
    head0 = lax.broadcasted_iota(jnp.int32, (c, LANES), 1) < M_HEADDIM

    dt = _softplus(dt_ref[...] + dtb_ref[...])
    acs_c = _sel_dot(tri, dt * (-jnp.exp(alogc_ref[...])))
    acs_ct = acs_c.T
    dte = _dot_sel(dt, e_ref[d])
    acs = _dot_sel(acs_c, e_ref[d])
    tot = acs[end_row:end_row + 1]
    yield
    xdt = xs_ref[...].astype(F32) * dte
    e_acs = jnp.exp(acs)
    x_end = (xdt * jnp.exp(tot - acs)).astype(BF16)
    xdt_b = xdt.astype(BF16)
    dec = jnp.exp(tot)
    gsl = [slice(g * M_STATE, (g + 1) * M_STATE) for g in range(M_GROUPS)]
    bm_b = [bm_ref[:, sl] for sl in gsl]
    cm_b = [cm_ref[:, sl] for sl in gsl]
    bm_t = [bm_ref[:, sl].astype(F32).T.astype(BF16) for sl in gsl]
    cb = [_dot_nt(cm_b[g], bm_b[g]) for g in range(M_GROUPS)]
    yield
    prs = range(n_pairs)
    lns = [slice(p * LANES, (p + 1) * LANES) for p in prs]
    ys = []
    for p in prs:
        pair = []
        for hh in range(2):
            idx = d * M_HEADS + 2 * p + hh
            seg = acs_c[:, idx:idx + 1] - acs_ct[idx:idx + 1, :]
            lmat = jnp.exp(jnp.where(keep, seg, NEG_BIG))
            pair.append(_dot((cb[p // ppg] * lmat).astype(BF16), xdt_b[:, lns[p]]))
        ys.append(jnp.where(head0, pair[0], pair[1]))
        yield
    h_prev = [h_scr[d, p] for p in prs]
    y_off = [_dot(cm_b[g], jnp.concatenate([h_prev[p].astype(BF16) for p in range(g * ppg, (g + 1) * ppg)], axis=1))
             for g in range(M_GROUPS)]
    y_ref[...] = (jnp.concatenate(ys, axis=1) + jnp.concatenate(y_off, axis=1) * e_acs).astype(y_ref.dtype)
    yield
    for g in range(M_GROUPS):
        upd = _dot(bm_t[g], x_end[:, g * M_GROUP_W:(g + 1) * M_GROUP_W])
        for q in range(ppg):
            p = g * ppg + q
            h_scr[d, p] = h_prev[p] * dec[:, lns[p]] + upd[:, q * LANES:(q + 1) * LANES]


def _ssd_kernel(xsf_ref, bmf_ref, cmf_ref, dtf_ref, xsb_ref, bmb_ref, cmb_ref, dtb_ref, bias_ref, e_ref, alogc_ref,
                yf_ref, yb_ref, h_scr):
    @pl.when(pl.program_id(1) == 0)
    def _():
        h_scr[...] = jnp.zeros_like(h_scr)

    shared = (bias_ref, e_ref, alogc_ref)
    both = (_ssd_chunk(0, xsf_ref, bmf_ref, cmf_ref, dtf_ref, *shared, yf_ref, h_scr),
            _ssd_chunk(1, xsb_ref, bmb_ref, cmb_ref, dtb_ref, *shared, yb_ref, h_scr))
    for _ in itertools.zip_longest(*both):
        pass


def _ssd_call(xs, bm, cm, dt_raw, dtb, emat, alog_c, bsz, seq):
    t = xs.shape[0]
    nc = seq // M_CHUNK
    gs = M_GROUPS * M_STATE
    dt_col = RW_CODES_PAD // LANES
    fwd = lambda bi, ci: bi * nc + ci
    bwd = lambda bi, ci: bi * nc + nc - 1 - ci
    data = lambda row: [pl.BlockSpec((M_CHUNK, M_INNER), lambda bi, ci: (row(bi, ci), 0)),
                        pl.BlockSpec((M_CHUNK, gs), lambda bi, ci: (row(bi, ci), 0)),
                        pl.BlockSpec((M_CHUNK, gs), lambda bi, ci: (row(bi, ci), 0)),
                        pl.BlockSpec((M_CHUNK, LANES), lambda bi, ci: (row(bi, ci), dt_col))]
    full = lambda a: pl.BlockSpec(a.shape, lambda bi, ci: (0,) * a.ndim)
    out = jax.ShapeDtypeStruct((t, M_INNER), BF16)
    return pl.pallas_call(
        _ssd_kernel,
        grid=(bsz, nc),
        in_specs=data(fwd) + data(bwd) + [full(dtb), full(emat), full(alog_c)],
        out_specs=[pl.BlockSpec((M_CHUNK, M_INNER), lambda bi, ci: (fwd(bi, ci), 0)),
                   pl.BlockSpec((M_CHUNK, M_INNER), lambda bi, ci: (bwd(bi, ci), 0))],
        out_shape=[out, out],
        scratch_shapes=[pltpu.VMEM((2, M_INNER // LANES, M_STATE, LANES), F32)],
        compiler_params=_cparams("parallel", "arbitrary"),
        name="ssd_scan",
    )(xs, bm, cm, dt_raw, xs, bm, cm, dt_raw, dtb, emat, alog_c)


def _mamba_post(yf_ref, yb_ref, xs_ref, z_ref, d_skip, nw):
    ld = lambda ref: ref[...].astype(F32)
    y = (ld(yf_ref) + ld(yb_ref) + d_skip * ld(xs_ref)) * _silu(ld(z_ref))
    groups = [slice(g * M_GROUP_W, (g + 1) * M_GROUP_W) for g in range(M_GROUPS)]
    return jnp.concatenate([_rms(y[:, sl], nw[:, sl], 1e-5).astype(BF16) for sl in groups], axis=1)


def _diff_attn_kernel(zero_ref, q_ref, k_ref, v_ref, lam_ref, slope_ref, nw_ref, o_ref, bias_scr, s_scr, *, tq, tk,
                      lambda_init):
    seq = k_ref.shape[0]
    log2e = math.log2(math.e)

    @pl.when(pl.program_id(2) == 0)
    def _():
        rows = pl.program_id(1) * tq + lax.broadcasted_iota(jnp.int32, (tq, seq), 0)
        cols = lax.broadcasted_iota(jnp.int32, (tq, seq), 1)
        bias_scr[...] = (slope_ref[0][:, 0:1] * log2e) * jnp.abs(rows - cols).astype(F32)

    q = (q_ref[...].astype(F32) * (DF_HEAD ** -0.5 * log2e)).astype(BF16)
    lv = lam_ref[...]
    lam = (jnp.exp(jnp.sum(lv[0:1] * lv[1:2], keepdims=True))
           - jnp.exp(jnp.sum(lv[2:3] * lv[3:4], keepdims=True)) + lambda_init)
    map0 = lax.broadcasted_iota(jnp.int32, (tq, LANES), 1) < DF_HEAD
    qm = [jnp.where(map0, q, jnp.zeros_like(q)), jnp.where(map0, jnp.zeros_like(q), q)]
    tiles = [slice(t * tk, (t + 1) * tk) for t in range(seq // tk)]
    rt = min(LANES, tq)
    streams = [(mp, slice(rh * rt, (rh + 1) * rt)) for rh in range(tq // rt) for mp in range(2)]
    v1 = jnp.concatenate([v_ref[...], jnp.ones((seq, LANES), BF16)], axis=1)
    zero = zero_ref[0]

    def scores_tile(st, ts, mx):
        mp, rows = st
        s = _dot_nt(qm[mp][rows], k_ref[ts, :]) - bias_scr[rows, ts]
        s_scr[mp, rows, ts] = s
        for cs in range(tk // LANES):
            blk = s[:, cs * LANES:(cs + 1) * LANES]
            mx = blk if mx is None else jnp.maximum(mx, blk)
        return mx

    def weights(st, m):
        mp, rows = st
        return _dot(jnp.exp2(s_scr[mp + zero, rows, :] - m).astype(BF16), v1)

    m, acc = {}, {}
    for i in range(len(streams) + 1):
        if i < len(streams):
            mx = None
            for ts in tiles:
                mx = scores_tile(streams[i], ts, mx)
            m[i] = jnp.max(mx, axis=-1, keepdims=True)
        if i >= 1:
            acc[i - 1] = weights(streams[i - 1], m[i - 1])
        if i >= 2 and i % 2 == 0:
            rh = i // 2 - 1
            om = [acc[2 * rh + mp][:, 0:LANES] / acc[2 * rh + mp][:, LANES:LANES + 1] for mp in range(2)]
            o = om[0] - lam * om[1]
            o_ref[rh * rt:(rh + 1) * rt, :] = (_rms(o, nw_ref[...], 1e-5) * (1.0 - lambda_init)).astype(o_ref.dtype)


def _diff_attn_call(qkv, lam_vecs, slopes, subln_w, bsz, seq, lambda_init):
    t = qkv.shape[0]
    tq = min(1024, seq)
    nq = seq // tq
    return pl.pallas_call(
        functools.partial(_diff_attn_kernel, tq=tq, tk=min(512, seq), lambda_init=lambda_init),
        grid=(DF_HEADS, nq, bsz),
        in_specs=[pl.BlockSpec(memory_space=pltpu.SMEM),
                  pl.BlockSpec((tq, DF_V), lambda hi, qi, bi: (bi * nq + qi, hi)),
                  pl.BlockSpec((seq, DF_V), lambda hi, qi, bi: (bi, DF_HEADS + hi)),
                  pl.BlockSpec((seq, DF_V), lambda hi, qi, bi: (bi, 2 * DF_HEADS + hi)),
                  pl.BlockSpec((4, DF_HEAD), lambda hi, qi, bi: (0, 0)),
                  pl.BlockSpec((1, 1, LANES), lambda hi, qi, bi: (hi, 0, 0)),
                  pl.BlockSpec((1, DF_V), lambda hi, qi, bi: (0, 0))],
        out_specs=pl.BlockSpec((tq, DF_V), lambda hi, qi, bi: (bi * nq + qi, hi)),
        out_shape=jax.ShapeDtypeStruct((t, DF_HEADS * DF_V), BF16),
        scratch_shapes=[pltpu.VMEM((tq, seq), F32), pltpu.VMEM((2, tq, seq), F32)],
        compiler_params=_cparams("parallel", "parallel", "arbitrary"),
        name="diff_attn",
    )(jnp.zeros((1,), jnp.int32), qkv, qkv, qkv, lam_vecs, slopes, subln_w)


def _mixer_tail_kernel(yf_ref, yb_ref, r_ref, kf_ref, kb_ref, v_ref, g_ref, pvec_ref, mf_ref, mb_ref, xs_ref, z_ref,
                       dskip_ref, mnw_ref, yd_ref, pg_ref, x_ref, mod_ref, nw_ref, wr_ref, wm_ref, wd_ref, wo_ref,
                       o_ref):
    d = D_MODEL
    y_r = _rwkv_post(yf_ref, yb_ref, r_ref, kf_ref, kb_ref, v_ref, g_ref, pvec_ref[...])
    y_m = _mamba_post(mf_ref, mb_ref, xs_ref, z_ref, dskip_ref[...], mnw_ref[...])
    gate = lambda g: _sigmoid(pg_ref[:, g * d:(g + 1) * d].astype(F32))
    merged = (gate(0) * _dot(y_r, wr_ref[...]) + gate(1) * _dot(y_m, wm_ref[...])
              + gate(2) * _dot(yd_ref[...], wd_ref[...]))
    y = _dot(merged.astype(BF16), wo_ref[...])
    o_ref[...] = x_ref[...] + mod_ref[0][2:3] * _rms(y, nw_ref[...], EPS)


def _mixer_tail_call(rw, pvec, mm, d_e, mnw, yd, pg, x2, mod_sub, nw, wr, wm, wd, wo, seq):
    t, d = x2.shape
    tm = 256
    row = lambda w: pl.BlockSpec((tm, w), lambda i: (i, 0))
    vec = lambda a: pl.BlockSpec(a.shape, lambda i: (0, 0))
    res = lambda a: pl.BlockSpec(a.shape, lambda i: (0, 0), pipeline_mode=pl.Buffered(1))
    return pl.pallas_call(
        _mixer_tail_kernel,
        grid=(t // tm,),
        in_specs=([row(RW_DIM)] * 7 + [vec(pvec)] + [row(M_INNER)] * 4 + [vec(d_e), vec(mnw)]
                  + [row(DF_HEADS * DF_V), row(GATE_COLS), row(d),
                     pl.BlockSpec((1, 3, d), lambda i: (i * tm // seq, 0, 0)), vec(nw),
                     res(wr), res(wm), res(wd), res(wo)]),
        out_specs=row(d),
        out_shape=jax.ShapeDtypeStruct((t, d), F32),
        compiler_params=_cparams("parallel"),
        name="mixer_tail",
    )(*rw, pvec, *mm, d_e, mnw, yd, pg, x2, mod_sub, nw, wr, wm, wd, wo)


def _pad_cols(w, n):
    return jnp.pad(w, ((0, 0), (0, n - w.shape[1])))


def _pad_rows(w, n):
    return jnp.pad(w, ((0, n - w.shape[0]), (0, 0)))


def _dir_padded(w):
    z = jnp.zeros_like(w[0])
    return jnp.stack([jnp.concatenate([w[0], z], axis=0), jnp.concatenate([z, w[1]], axis=0)])


def _head_expand_matrix():
    e = np.zeros((2, LANES, M_INNER), np.float32)
    for d in range(2):
        for h in range(M_HEADS):
            e[d, d * M_HEADS + h, h * M_HEADDIM:(h + 1) * M_HEADDIM] = 1.0
    return jnp.asarray(e, dtype=BF16)


def kernel(x, c, ada_w, ada_b, norm_w, ffn_w13, ffn_w2, w_in, rwkv_mu, rwkv_w0, rwkv_w2, rwkv_a0, rwkv_a2, rwkv_g2, rwkv_k_k, rwkv_k_a, rwkv_r_k, rwkv_ln_w, rwkv_ln_b, rwkv_v0, rwkv_v1, rwkv_v2, mamba_conv_w, mamba_conv_b, mamba_dt_bias, mamba_a_log, mamba_d, mamba_norm_w, diff_lambda, diff_subln_w, w_branch_rwkv, w_branch_mamba, w_branch_diff, w_out):
    bsz, seq, d = x.shape
    depth = ada_w.shape[0]
    t = bsz * seq
    x2 = x.reshape(t, d)
    mod_all = _mod_call(c, ada_w, ada_b)
    emat = _head_expand_matrix()
    slopes = jnp.broadcast_to(
        jnp.asarray(2.0 ** (-8.0 * np.arange(1, DF_HEADS + 1) / DF_HEADS), F32)[:, None, None], (DF_HEADS, 1, LANES))
    v_first = None
    for l in range(depth):
        mod = mod_all[l].reshape(bsz, N_SUB, 3, d)
        x2 = _ffn_call(x2, mod[:, 0], norm_w[l, 0:2], ffn_w13[l, 0].astype(BF16), ffn_w2[l, 0].astype(BF16), seq)

        wl = w_in[l]
        o_m = RW_COLS
        o_d = o_m + M_COLS
        o_g = o_d + DF_COLS
        w_cat = jnp.concatenate([
            wl[:, 0:3 * RW_DIM],
            _pad_cols(wl[:, 3 * RW_DIM:RW_COLS], RW_CODES_PAD),
            _pad_cols(wl[:, o_m + M_INNER + M_XBC:o_d], INPROJ_TN - RW_CODES_PAD),
            wl[:, o_m:o_m + M_INNER],
            wl[:, o_m + M_INNER:o_m + M_INNER + M_XBC],
            wl[:, o_d:o_g],
            wl[:, o_g:],
        ], axis=1).astype(BF16)
        p_rkv, p_aux, p_z, p_xbc, p_qkv, p_gate = _inproj_call(
            x2, mod[:, 1], norm_w[l, 2:3], w_cat,
            (3 * RW_DIM, INPROJ_TN, M_INNER, M_XBC, DF_COLS, GATE_COLS),
            (BF16, F32, BF16, BF16, BF16, BF16), seq)
        p_codes = p_dt = p_aux

        mu = rwkv_mu[l]
        mu_p = mu[None, 0:3 * RW_DIM]
        mu_c = _pad_cols(mu[None, 3 * RW_DIM:], RW_CODES_PAD)
        v0 = rwkv_v0[l - 1] if l > 0 else jnp.zeros((RW_DIM,), F32)
        pvec = jnp.stack([rwkv_w0[l, 0], rwkv_w0[l, 1], rwkv_a0[l, 0], rwkv_a0[l, 1],
                          rwkv_k_k[l], rwkv_k_a[l], v0, jnp.zeros((RW_DIM,), F32)])
        g2p = _pad_rows(rwkv_g2[l], 256).astype(BF16)
        vres = None
        if l > 0:
            vres = (v_first, _pad_cols(rwkv_v1[l - 1], LANES).astype(BF16),
                    _pad_rows(rwkv_v2[l - 1], LANES).astype(BF16))
        r, v, kk, k_f, k_b, b_f, b_b, lw_f, lw_b, gate = _rwkv_prep_call(
            p_rkv, p_codes, mu_p, mu_c, pvec, _dir_padded(rwkv_w2[l]).astype(BF16), _dir_padded(rwkv_a2[l]).astype(BF16), g2p, vres, seq)
        if l == 0:
            v_first = v
        y_f, y_b = _rwkv_scan_call(r, v, kk, k_f, b_f, lw_f, k_b, b_b, lw_b, bsz, seq)
        pvec2 = jnp.concatenate([jnp.stack([rwkv_ln_w[l], rwkv_ln_b[l], rwkv_r_k[l].reshape(RW_DIM)]),
                                 jnp.zeros((5, RW_DIM), F32)])

        conv_w = _pad_rows(mamba_conv_w[l], 8)
        xs, bm, cm = _mamba_conv_call(p_xbc, conv_w, mamba_conv_b[l][None], seq)
        dtb = _pad_cols(mamba_dt_bias[l].reshape(1, 2 * M_HEADS), LANES)
        alog_c = _pad_cols(mamba_a_log[l].reshape(1, 2 * M_HEADS), LANES)
        ym_f, ym_b = _ssd_call(xs, bm, cm, p_dt, dtb, emat, alog_c, bsz, seq)
        d_e = jnp.repeat(mamba_d[l], M_HEADDIM)[None]

        lambda_init = 0.8 - 0.6 * math.exp(-0.3 * l)
        y_d = _diff_attn_call(p_qkv, diff_lambda[l], slopes, diff_subln_w[l][None], bsz, seq, lambda_init)

        x2 = _mixer_tail_call((y_f, y_b, r, k_f, k_b, v, gate), pvec2, (ym_f, ym_b, xs, p_z), d_e,
                              mamba_norm_w[l][None], y_d, p_gate, x2, mod[:, 1], norm_w[l, 3:4],
                              w_branch_rwkv[l].astype(BF16), w_branch_mamba[l].astype(BF16),
                              w_branch_diff[l].astype(BF16), w_out[l].astype(BF16), seq)

        x2 = _ffn_call(x2, mod[:, 2], norm_w[l, 4:6], ffn_w13[l, 1].astype(BF16), ffn_w2[l, 1].astype(BF16), seq)
    return x2.reshape(bsz, seq, d)
```

```python
import functools
import itertools
import math

import numpy as np
import jax
import jax.numpy as jnp
from jax import lax
from jax.experimental import pallas as pl
from jax.experimental.pallas import tpu as pltpu

F32 = jnp.float32
BF16 = jnp.bfloat16
HI = lax.Precision.HIGHEST

D_MODEL = 1024
N_SUB = 3
EPS = 1e-6
LANES = 128
SUBLANES = 8
HALO = 2 * SUBLANES
VMEM_LIMIT = 56 * 1024 * 1024

RW_HEAD = 64
RW_DIM = 1024
W_LORA = 64
A_LORA = 64
V_LORA = 32
G_LORA = 160
RW_GN_EPS = 64e-5
RW_COLS = 3 * RW_DIM + 2 * W_LORA + 2 * A_LORA + G_LORA
RW_CODES_PAD = 512
RW_CHUNK = 64
M_INNER = 2048
M_HEADS = 32
M_HEADDIM = 64
M_GROUPS = 4
M_STATE = 128
M_CONV = 5
M_CHUNK = 128
M_XBC = M_INNER + 2 * M_GROUPS * M_STATE
M_COLS = M_INNER + M_XBC + 2 * M_HEADS
M_GROUP_W = M_INNER // M_GROUPS
CONV_SUB = 128
DF_HEADS = 8
DF_HEAD = 64
DF_V = 128
DF_COLS = 3 * DF_HEADS * 2 * DF_HEAD
GATE_COLS = 3 * D_MODEL
D_FF = 2816
NEG_BIG = -1e30


def _cparams(*sem):
    return pltpu.CompilerParams(dimension_semantics=sem, vmem_limit_bytes=VMEM_LIMIT)


def _dot(a, b, **kw):
    return jnp.dot(a, b, preferred_element_type=F32, **kw)


def _dot_nt(a, b, **kw):
    return lax.dot_general(a, b, (((1,), (1,)), ((), ())), preferred_element_type=F32, **kw)


def _split2(x):
    hi = x.astype(BF16)
    return hi, (x - hi.astype(F32)).astype(BF16)


def _dot_sel(x, m_b):
    hi, lo = _split2(x)
    return _dot(hi, m_b) + _dot(lo, m_b)


def _sel_dot(m_b, x):
    hi, lo = _split2(x)
    return _dot(m_b, hi) + _dot(m_b, lo)


def _bdot(a, b):
    return _dot(a.astype(BF16), b.astype(BF16))


def _sigmoid(x):
    return 1.0 / (1.0 + jnp.exp(-x))


def _silu(x):
    return x * _sigmoid(x)


def _softplus(x):
    return jnp.maximum(x, 0.0) + jnp.log(1.0 + jnp.exp(-jnp.abs(x)))


def _rms(x, w, eps):
    return x * lax.rsqrt(jnp.mean(x * x, axis=-1, keepdims=True) + eps) * w


def _norm_mod(x, nw, shift, scale):
    return _rms(x, nw, EPS) * (1.0 + scale) + shift


def _pair_ones(scale=1.0):
    r = lax.broadcasted_iota(jnp.int32, (LANES, LANES), 0) // RW_HEAD
    c = lax.broadcasted_iota(jnp.int32, (LANES, LANES), 1) // RW_HEAD
    return jnp.where(r == c, scale, 0.0).astype(BF16)


def _shift_rows(cur, prevz, nextz, d):
    tm = cur.shape[0]
    rolled = pltpu.roll(cur, (-d) % tm, axis=0)
    rid = lax.broadcasted_iota(jnp.int32, (SUBLANES, cur.shape[1]), 0)
    if d < 0:
        fix = pltpu.roll(prevz, (-d) % SUBLANES, axis=0)
        top = jnp.where(rid < -d, fix, rolled[0:SUBLANES])
        return jnp.concatenate([top, rolled[SUBLANES:]], axis=0)
    fix = pltpu.roll(nextz, (SUBLANES - d) % SUBLANES, axis=0)
    bot = jnp.where(rid >= SUBLANES - d, fix, rolled[tm - SUBLANES:])
    return jnp.concatenate([rolled[:tm - SUBLANES], bot], axis=0)


def _shift_rows_mxu(cur_b, prevz, nextz, d):
    assert cur_b.dtype == BF16
    tm = cur_b.shape[0]
    ri = lax.broadcasted_iota(jnp.int32, (tm, tm), 0)
    ci = lax.broadcasted_iota(jnp.int32, (tm, tm), 1)
    shifted = _dot(jnp.where(ci == ri + d, 1.0, 0.0).astype(BF16), cur_b)
    rid = lax.broadcasted_iota(jnp.int32, (SUBLANES, cur_b.shape[1]), 0)
    if d < 0:
        fix = jnp.where(rid < -d, pltpu.roll(prevz, (-d) % SUBLANES, axis=0), 0.0)
        return jnp.concatenate([shifted[0:SUBLANES] + fix, shifted[SUBLANES:]], axis=0)
    fix = jnp.where(rid >= SUBLANES - d, pltpu.roll(nextz, (SUBLANES - d) % SUBLANES, axis=0), 0.0)
    return jnp.concatenate([shifted[:tm - SUBLANES], shifted[tm - SUBLANES:] + fix], axis=0)


def _nbr_mean_minus_self_mxu(cur_b, prevz, nextz):
    assert cur_b.dtype == BF16
    tm = cur_b.shape[0]
    ri = lax.broadcasted_iota(jnp.int32, (tm, tm), 0)
    ci = lax.broadcasted_iota(jnp.int32, (tm, tm), 1)
    band = jnp.where(ci == ri, -1.0, jnp.where(jnp.abs(ci - ri) == 1, 0.5, 0.0)).astype(BF16)
    out = _dot(band, cur_b)
    rid = lax.broadcasted_iota(jnp.int32, (SUBLANES, cur_b.shape[1]), 0)
    top = jnp.where(rid == 0, 0.5 * prevz[SUBLANES - 1:SUBLANES], 0.0)
    bot = jnp.where(rid == SUBLANES - 1, 0.5 * nextz[0:1], 0.0)
    return jnp.concatenate([out[0:SUBLANES] + top, out[SUBLANES:tm - SUBLANES], out[tm - SUBLANES:] + bot], axis=0)


def _halo_specs(tm, width, n_rows):
    rb = tm // HALO
    last = n_rows // HALO - 1
    return [
        pl.BlockSpec((tm, width), lambda i: (i, 0)),
        pl.BlockSpec((HALO, width), lambda i: (jnp.maximum(i * rb - 1, 0), 0)),
        pl.BlockSpec((HALO, width), lambda i: (jnp.minimum((i + 1) * rb, last), 0)),
    ]


def _halo_values(cur_ref, prev_ref, next_ref, tm, seq):
    i = pl.program_id(0)
    first = (i * tm) % seq == 0
    last = ((i + 1) * tm) % seq == 0
    prevz = jnp.where(first, 0.0, prev_ref[...].astype(F32)[HALO - SUBLANES:])
    nextz = jnp.where(last, 0.0, next_ref[...].astype(F32)[:SUBLANES])
    return cur_ref[...].astype(F32), prevz, nextz


def _mod_kernel(c_ref, w_ref, b_ref, o_ref):
    o_ref[0] = _dot(_silu(c_ref[...]), w_ref[0], precision=HI) + b_ref[0]


def _mod_call(c, ada_w, ada_b):
    depth, d, n = ada_w.shape
    bsz = c.shape[0]
    tn = 1152
    return pl.pallas_call(
        _mod_kernel,
        grid=(depth, n // tn),
        in_specs=[pl.BlockSpec((bsz, d), lambda l, j: (0, 0)),
                  pl.BlockSpec((1, d, tn), lambda l, j: (l, 0, j)),
                  pl.BlockSpec((1, 1, tn), lambda l, j: (l, 0, j))],
        out_specs=pl.BlockSpec((1, bsz, tn), lambda l, j: (l, 0, j)),
        out_shape=jax.ShapeDtypeStruct((depth, bsz, n), F32),
        compiler_params=_cparams("parallel", "parallel"),
        name="adaln_mod",
    )(c, ada_w, ada_b.reshape(depth, 1, n))


def _ffn_kernel(x_ref, mod_ref, nw_ref, w13_ref, w2_ref, o_ref, *, tf):
    dff = w2_ref.shape[0]
    m = mod_ref[0]
    x = x_ref[...]
    h = _norm_mod(x, nw_ref[0:1], m[0:1], m[1:2]).astype(BF16)
    acc = None
    for j in range(dff // tf):
        g = _dot(h, w13_ref[:, j * tf:(j + 1) * tf])
        u = _dot(h, w13_ref[:, dff + j * tf:dff + (j + 1) * tf])
        part = _dot((_silu(g) * u).astype(BF16), w2_ref[j * tf:(j + 1) * tf, :])
        acc = part if acc is None else acc + part
    o_ref[...] = x + 0.5 * m[2:3] * _rms(acc, nw_ref[1:2], EPS)


def _ffn_call(x2, mod_sub, nw2, w13, w2, seq):
    t, d = x2.shape
    tm = min(1024, seq)
    resident = lambda a: pl.BlockSpec(a.shape, lambda i: (0, 0), pipeline_mode=pl.Buffered(1))
    return pl.pallas_call(
        functools.partial(_ffn_kernel, tf=256),
        grid=(t // tm,),
        in_specs=[pl.BlockSpec((tm, d), lambda i: (i, 0)),
                  pl.BlockSpec((1, 3, d), lambda i: (i * tm // seq, 0, 0)),
                  pl.BlockSpec((2, d), lambda i: (0, 0)),
                  resident(w13), resident(w2)],
        out_specs=pl.BlockSpec((tm, d), lambda i: (i, 0)),
        out_shape=jax.ShapeDtypeStruct((t, d), F32),
        compiler_params=_cparams("parallel"),
        name="swiglu_halfstep",
    )(x2, mod_sub, nw2, w13, w2)


INPROJ_TN = 1024


def _inproj_kernel(x_ref, mod_ref, nw_ref, w_ref, *rest, starts):
    o_refs, h_scr = rest[:-1], rest[-1]
    j = pl.program_id(1)

    @pl.when(j == 0)
    def _():
        m = mod_ref[0]
        h_scr[...] = _norm_mod(x_ref[...], nw_ref[...], m[0:1], m[1:2]).astype(BF16)

    for k, o_ref in enumerate(o_refs):
        @pl.when((j >= starts[k]) & (j < starts[k + 1]))
        def _(o_ref=o_ref):
            o_ref[...] = _dot(h_scr[...], w_ref[...]).astype(o_ref.dtype)


def _inproj_call(x2, mod_sub, nw, w, widths, dtypes, seq):
    t, d = x2.shape
    tn = INPROJ_TN
    tm = min(1024, seq)
    starts = [0]
    for wd in widths:
        starts.append(starts[-1] + wd // tn)

    def out_spec(k):
        return pl.BlockSpec((tm, tn), lambda i, j: (i, jnp.clip(j - starts[k], 0, widths[k] // tn - 1)))

    return pl.pallas_call(
        functools.partial(_inproj_kernel, starts=tuple(starts)),
        grid=(t // tm, starts[-1]),
        in_specs=[pl.BlockSpec((tm, d), lambda i, j: (i, 0)),
                  pl.BlockSpec((1, 3, d), lambda i, j: (i * tm // seq, 0, 0)),
                  pl.BlockSpec((1, d), lambda i, j: (0, 0)),
                  pl.BlockSpec((d, tn), lambda i, j: (0, j))],
        out_specs=[out_spec(k) for k in range(len(widths))],
        out_shape=[jax.ShapeDtypeStruct((t, wd), dt) for wd, dt in zip(widths, dtypes)],
        scratch_shapes=[pltpu.VMEM((tm, d), BF16)],
        compiler_params=_cparams("parallel", "arbitrary"),
        name="norm_inproj",
    )(x2, mod_sub, nw, w)


def _rwkv_prep_kernel(*refs, tm, seq, has_vres):
    (p_ref, pp_ref, pn_ref, c_ref, cp_ref, cn_ref, mup_ref, muc_ref, pvec_ref,
     w2_ref, a2_ref, g2_ref) = refs[:12]
    refs = refs[12:]
    if has_vres:
        vf_ref, v1_ref, v2_ref = refs[:3]
        refs = refs[3:]
    r_o, v_o, kk_o, kf_o, kb_o, bf_o, bb_o, lwf_o, lwb_o, g_o = refs

    def shift_mix(cur_ref, prev_ref, next_ref, mu):
        cur, prevz, nextz = _halo_values(cur_ref, prev_ref, next_ref, tm, seq)
        if cur_ref.dtype == BF16:
            return cur + mu * _nbr_mean_minus_self_mxu(cur_ref[...], prevz, nextz)
        nb = 0.5 * (_shift_rows(cur, prevz, nextz, -1) + _shift_rows(cur, prevz, nextz, 1))
        return cur + mu * (nb - cur)

    p = shift_mix(p_ref, pp_ref, pn_ref, mup_ref[...])
    codes = shift_mix(c_ref, cp_ref, cn_ref, muc_ref[...])
    r = p[:, 0:RW_DIM]
    k = p[:, RW_DIM:2 * RW_DIM]
    v = p[:, 2 * RW_DIM:3 * RW_DIM]
    cw = jnp.tanh(codes[:, 0:2 * W_LORA])
    ca = codes[:, 2 * W_LORA:2 * W_LORA + 2 * A_LORA]
    cg = _sigmoid(codes[:, 2 * W_LORA + 2 * A_LORA:])
    pv = pvec_ref[...]
    if has_vres:
        lo = _bdot(_bdot(v, v1_ref[...]), v2_ref[...])
        v = v + (vf_ref[...].astype(F32) - v) * _sigmoid(pv[6:7] + lo)
    r_o[...] = r.astype(r_o.dtype)
    v_o[...] = v.astype(v_o.dtype)
    g_o[...] = _bdot(cg, g2_ref[...]).astype(g_o.dtype)
    iclr = []
    for d, lw_o in ((0, lwf_o), (1, lwb_o)):
        lw_o[...] = -math.exp(-0.5) * _sigmoid(pv[d:d + 1] + _bdot(cw, w2_ref[d]))
        iclr.append(_sigmoid(pv[2 + d:3 + d] + _bdot(ca, a2_ref[d])))
    kf_o[...] = (k * (1.0 + (iclr[0] - 1.0) * pv[5:6])).astype(kf_o.dtype)
    kb_o[...] = (k * (1.0 + (iclr[1] - 1.0) * pv[5:6])).astype(kb_o.dtype)
    ones2 = _pair_ones()
    kkr = k * pv[4:5]
    for cb in range(RW_DIM // LANES):
        sl = slice(cb * LANES, (cb + 1) * LANES)
        blk = kkr[:, sl]
        kkn = blk * lax.rsqrt(_dot_sel(blk * blk, ones2) + 1e-12)
        kk_o[:, sl] = kkn.astype(kk_o.dtype)
        bf_o[:, sl] = (kkn * iclr[0][:, sl]).astype(bf_o.dtype)
        bb_o[:, sl] = (kkn * iclr[1][:, sl]).astype(bb_o.dtype)


def _rwkv_prep_call(rkv, codes, mu_p, mu_c, pvec, w2p, a2p, g2p, vres, seq):
    t = rkv.shape[0]
    tm = 256
    has_vres = vres is not None
    full = lambda shape: pl.BlockSpec(shape, lambda i: (0,) * len(shape))
    in_specs = (_halo_specs(tm, 3 * RW_DIM, t) + _halo_specs(tm, RW_CODES_PAD, t)
                + [full((1, 3 * RW_DIM)), full((1, RW_CODES_PAD)), full((8, RW_DIM)),
                   full((2, 2 * W_LORA, RW_DIM)), full((2, 2 * A_LORA, RW_DIM)), full((256, RW_DIM))])
    args = [rkv, rkv, rkv, codes, codes, codes, mu_p, mu_c, pvec, w2p, a2p, g2p]
    if has_vres:
        in_specs += [pl.BlockSpec((tm, RW_DIM), lambda i: (i, 0)), full((RW_DIM, LANES)), full((LANES, RW_DIM))]
        args += list(vres)
    dtypes = [BF16] * 7 + [F32, F32, BF16]
    return pl.pallas_call(
        functools.partial(_rwkv_prep_kernel, tm=tm, seq=seq, has_vres=has_vres),
        grid=(t // tm,),
        in_specs=in_specs,
        out_specs=[pl.BlockSpec((tm, RW_DIM), lambda i: (i, 0))] * 10,
        out_shape=[jax.ShapeDtypeStruct((t, RW_DIM), dt) for dt in dtypes],
        compiler_params=_cparams("parallel"),
        name="rwkv_prep",
    )(*args)


def _rwkv_scan_kernel(rf_ref, vf_ref, kkf_ref, kf_ref, bf_ref, lwf_ref, rb_ref, vb_ref, kkb_ref, kb_ref, bb_ref,
                      lwb_ref, yf_ref, yb_ref, h_scr, *, n_chunks):
    c = RW_CHUNK
    n_pairs = RW_DIM // LANES

    @pl.when(pl.program_id(1) == 0)
    def _():
        h_scr[...] = jnp.zeros_like(h_scr)

    ri = lax.broadcasted_iota(jnp.int32, (LANES, LANES), 0)
    ci = lax.broadcasted_iota(jnp.int32, (LANES, LANES), 1)
    same = (ri // c) == (ci // c)
    tr = lax.broadcasted_iota(jnp.int32, (c, LANES), 0)
    tc = lax.broadcasted_iota(jnp.int32, (c, LANES), 1) % c
    strict = [tc < tr, tc > tr]
    incl = [tc <= tr, tc >= tr]
    blk = [(tr // w) == (tc // w) for w in (8, 16, 32, 64)]
    eye = jnp.where(tr == tc, 1.0, 0.0).astype(F32)
    r64 = lax.broadcasted_iota(jnp.int32, (c, c), 0)
    c64 = lax.broadcasted_iota(jnp.int32, (c, c), 1)
    tri = [jnp.where(c64 <= r64, 1.0, 0.0).astype(BF16), jnp.where(c64 >= r64, 1.0, 0.0).astype(BF16)]
    head0 = lax.broadcasted_iota(jnp.int32, (c, LANES), 1) < RW_HEAD
    end_row = [c - 1, 0]
    refs = [(rf_ref, vf_ref, kkf_ref, kf_ref, bf_ref, lwf_ref, yf_ref),
            (rb_ref, vb_ref, kkb_ref, kb_ref, bb_ref, lwb_ref, yb_ref)]
    lns = [slice(pr * LANES, (pr + 1) * LANES) for pr in range(n_pairs)]
    chains = [(d, p) for p in range(n_pairs) for d in range(2)]

    def stack(x):
        return jnp.concatenate([jnp.where(head0, x, 0.0), jnp.where(head0, 0.0, x)], axis=0).astype(BF16)

    def per_head(x, y):
        return _dot(x.astype(BF16), stack(y))

    def chunk(step, carry):
        rows = [pl.ds(pl.multiple_of(step * c, c), c), pl.ds(pl.multiple_of((n_chunks - 1 - step) * c, c), c)]
        wide = []
        for d in range(2):
            r_ref, v_ref, kk_ref, k_ref, b_ref, lw_ref, _ = refs[d]
            lw = lw_ref[rows[d], :]
            ld = lambda ref: ref[rows[d], :].astype(F32)
            cum = _sel_dot(tri[d], lw)
            tot = cum[end_row[d]:end_row[d] + 1]
            g_inv, g_end = jnp.exp(-cum), jnp.exp(tot - cum)
            k_all, b_all = ld(k_ref), ld(b_ref)
            wide.append(dict(at=-ld(kk_ref) * jnp.exp(cum - lw), rt=ld(r_ref) * jnp.exp(cum),
                             bt=b_all * g_inv, kt=k_all * g_inv, bg=b_all * g_end, kg=k_all * g_end,
                             v=ld(v_ref), g_tot=jnp.exp(tot)))
        idx = range(len(chains))
        sel = lambda name, i: wide[chains[i][0]][name][:, lns[chains[i][1]]]
        ar = [jnp.concatenate([sel("at", i), sel("rt", i)], axis=0).astype(BF16) for i in idx]
        bk = [jnp.concatenate([stack(sel("bt", i)), stack(sel("kt", i))], axis=0) for i in idx]
        v_s = [stack(sel("v", i)) for i in idx]
        ht = [h_scr[d, p] for d, p in chains]
        gram = [_dot_nt(ar[i], bk[i]) for i in idx]
        a_ab = [jnp.where(strict[d], gram[i][0:c, 0:LANES], 0.0) for i, (d, p) in enumerate(chains)]
        a_ak = [jnp.where(strict[d], gram[i][0:c, LANES:], 0.0).astype(BF16) for i, (d, p) in enumerate(chains)]
        a_rb = [jnp.where(incl[d], gram[i][c:, 0:LANES], 0.0) for i, (d, p) in enumerate(chains)]
        a_rk = [jnp.where(incl[d], gram[i][c:, LANES:], 0.0).astype(BF16) for i, (d, p) in enumerate(chains)]
        lhs = [jnp.concatenate([ar[i], jnp.concatenate([a_ak[i], a_rk[i]], axis=0)], axis=1) for i in idx]
        rhs = [jnp.concatenate([ht[i].T.astype(BF16), v_s[i]], axis=0) for i in idx]
        xy0 = [_dot(lhs[i], rhs[i]) for i in idx]
        p1 = [jnp.where(blk[0], a_ab[i], 0.0) for i in idx]
        p2 = [per_head(p1[i], p1[i]) for i in idx]
        p4 = [per_head(p2[i], p2[i]) for i in idx]
        t_inv = [eye + p1[i] + p2[i] + per_head(p1[i], p2[i]) for i in idx]
        t_inv = [t_inv[i] + per_head(t_inv[i], p4[i]) for i in idx]
        for lvl in range(1, 4):
            off = [jnp.where(blk[lvl] & ~blk[lvl - 1], a_ab[i], 0.0) for i in idx]
            mid = [per_head(t_inv[i], off[i]) for i in idx]
            t_inv = [t_inv[i] + per_head(mid[i], t_inv[i]) for i in idx]
        u = [per_head(t_inv[i], xy0[i][0:c]) for i in idx]
        y = [xy0[i][c:] + per_head(a_rb[i], u[i]) for i in idx]
        for d in range(2):
            ys = [y[i] for i, (dd, p) in enumerate(chains) if dd == d]
            refs[d][6][rows[d], :] = jnp.concatenate(ys, axis=1).astype(refs[d][6].dtype)
        for i, (d, p) in enumerate(chains):
            uv_t = jnp.concatenate([u[i], sel("v", i)], axis=0).T.astype(BF16)
            bkg = jnp.concatenate([sel("bg", i), sel("kg", i)], axis=0).astype(BF16)
            upd = jnp.where(same, _dot(uv_t, bkg), 0.0)
            h_scr[d, p] = ht[i] * wide[d]["g_tot"][:, lns[p]] + upd
        return carry

    lax.fori_loop(0, n_chunks, chunk, 0)


def _rwkv_scan_call(r, v, kk, k_f, b_f, lw_f, k_b, b_b, lw_b, bsz, seq):
    t = r.shape[0]
    tb = min(4 * RW_CHUNK, seq)
    nt = seq // tb
    fwd = pl.BlockSpec((tb, RW_DIM), lambda bi, ti: (bi * nt + ti, 0))
    bwd = pl.BlockSpec((tb, RW_DIM), lambda bi, ti: (bi * nt + nt - 1 - ti, 0))
    out = jax.ShapeDtypeStruct((t, RW_DIM), BF16)
    return pl.pallas_call(
        functools.partial(_rwkv_scan_kernel, n_chunks=tb // RW_CHUNK),
        grid=(bsz, nt),
        in_specs=[fwd] * 6 + [bwd] * 6,
        out_specs=[fwd, bwd],
        out_shape=[out, out],
        scratch_shapes=[pltpu.VMEM((2, RW_DIM // LANES, LANES, LANES), F32)],
        compiler_params=_cparams("parallel", "arbitrary"),
        name="rwkv_scan",
    )(r, v, kk, k_f, b_f, lw_f, r, v, kk, k_b, b_b, lw_b)


def _rwkv_post(yf_ref, yb_ref, r_ref, kf_ref, kb_ref, v_ref, g_ref, pv):
    mean2 = _pair_ones(1.0 / RW_HEAD)
    ones2 = _pair_ones()
    outs = []
    for cb in range(RW_DIM // LANES):
        sl = slice(cb * LANES, (cb + 1) * LANES)
        ld = lambda ref: ref[:, sl].astype(F32)
        y = ld(yf_ref) + ld(yb_ref)
        dlt = y - _dot_sel(y, mean2)
        var = _dot_sel(dlt * dlt, mean2)
        yn = dlt * lax.rsqrt(var + RW_GN_EPS) * pv[0:1, sl] + pv[1:2, sl]
        rk = ld(r_ref) * (ld(kf_ref) + ld(kb_ref)) * pv[2:3, sl]
        bonus = _dot_sel(rk, ones2) * ld(v_ref)
        outs.append(((yn + bonus) * ld(g_ref)).astype(BF16))
    return jnp.concatenate(outs, axis=1)


def _mamba_conv_kernel(x_ref, xp_ref, xn_ref, w_ref, b_ref, xs_o, bm_o, cm_o, *, tm, seq):
    _, prevz, nextz = _halo_values(x_ref, xp_ref, xn_ref, tm, seq)
    w = w_ref[...]
    half = (M_CONV - 1) // 2
    sb = CONV_SUB
    n_sub = tm // sb
    for s in range(n_sub):
        rows = slice(s * sb, (s + 1) * sb)
        cur_b = x_ref[rows, :]
        pz = prevz if s == 0 else x_ref[s * sb - HALO:s * sb, :].astype(F32)[HALO - SUBLANES:]
        nz = nextz if s == n_sub - 1 else x_ref[(s + 1) * sb:(s + 1) * sb + HALO, :].astype(F32)[:SUBLANES]
        acc = cur_b.astype(F32) * w[half:half + 1] + b_ref[...]
        for tap in range(M_CONV):
            if tap != half:
                acc = acc + _shift_rows_mxu(cur_b, pz, nz, tap - half) * w[tap:tap + 1]
        y = _silu(acc)
        xs_o[rows, :] = y[:, 0:M_INNER].astype(xs_o.dtype)
        bm_o[rows, :] = y[:, M_INNER:M_INNER + M_GROUPS * M_STATE].astype(bm_o.dtype)
        cm_o[rows, :] = y[:, M_INNER + M_GROUPS * M_STATE:].astype(cm_o.dtype)


def _mamba_conv_call(xbc, conv_w, conv_b, seq):
    t = xbc.shape[0]
    tm = min(512, seq)
    gs = M_GROUPS * M_STATE
    return pl.pallas_call(
        functools.partial(_mamba_conv_kernel, tm=tm, seq=seq),
        grid=(t // tm,),
        in_specs=_halo_specs(tm, M_XBC, t) + [pl.BlockSpec((8, M_XBC), lambda i: (0, 0)),
                                             pl.BlockSpec((1, M_XBC), lambda i: (0, 0))],
        out_specs=[pl.BlockSpec((tm, M_INNER), lambda i: (i, 0)),
                   pl.BlockSpec((tm, gs), lambda i: (i, 0)),
                   pl.BlockSpec((tm, gs), lambda i: (i, 0))],
        out_shape=[jax.ShapeDtypeStruct((t, M_INNER), BF16),
                   jax.ShapeDtypeStruct((t, gs), BF16),
                   jax.ShapeDtypeStruct((t, gs), BF16)],
        compiler_params=_cparams("parallel"),
        name="mamba_conv",
    )(xbc, xbc, xbc, conv_w, conv_b)


def _ssd_chunk(d, xs_ref, bm_ref, cm_ref, dt_ref, dtb_ref, e_ref, alogc_ref, y_ref, h_scr):
    c = M_CHUNK
    n_pairs = M_INNER // LANES
    ppg = n_pairs // M_GROUPS
    ri = lax.broadcasted_iota(jnp.int32, (c, c), 0)
    ci = lax.broadcasted_iota(jnp.int32, (c, c), 1)
    keep = (ci >= ri) if d else (ci <= ri)
    tri = jnp.where(keep, 1.0, 0.0).astype(BF16)
    end_row = 0 if d else c - 1
    head0 = lax.broadcasted_iota(jnp.int32, (c, LANES), 1) < M_HEADDIM

    dt = _softplus(dt_ref[...] + dtb_ref[...])
    acs_c = _sel_dot(tri, dt * (-jnp.exp(alogc_ref[...])))
    acs_ct = acs_c.T
    dte = _dot_sel(dt, e_ref[d])
    acs = _dot_sel(acs_c, e_ref[d])
    tot = acs[end_row:end_row + 1]
    yield
    xdt = xs_ref[...].astype(F32) * dte
    e_acs = jnp.exp(acs)
    x_end = (xdt * jnp.exp(tot - acs)).astype(BF16)
    xdt_b = xdt.astype(BF16)
    dec = jnp.exp(tot)
    gsl = [slice(g * M_STATE, (g + 1) * M_STATE) for g in range(M_GROUPS)]
    bm_b = [bm_ref[:, sl] for sl in gsl]
    cm_b = [cm_ref[:, sl] for sl in gsl]
    bm_t = [bm_ref[:, sl].astype(F32).T.astype(BF16) for sl in gsl]
    cb = [_dot_nt(cm_b[g], bm_b[g]) for g in range(M_GROUPS)]
    yield
    prs = range(n_pairs)
    lns = [slice(p * LANES, (p + 1) * LANES) for p in prs]
    ys = []
    for p in prs:
        pair = []
        for hh in range(2):
            idx = d * M_HEADS + 2 * p + hh
            seg = acs_c[:, idx:idx + 1] - acs_ct[idx:idx + 1, :]
            lmat = jnp.exp(jnp.where(keep, seg, NEG_BIG))
            pair.append(_dot((cb[p // ppg] * lmat).astype(BF16), xdt_b[:, lns[p]]))
        ys.append(jnp.where(head0, pair[0], pair[1]))
        yield
    h_prev = [h_scr[d, p] for p in prs]
    y_off = [_dot(cm_b[g], jnp.concatenate([h_prev[p].astype(BF16) for p in range(g * ppg, (g + 1) * ppg)], axis=1))
             for g in range(M_GROUPS)]
    y_ref[...] = (jnp.concatenate(ys, axis=1) + jnp.concatenate(y_off, axis=1) * e_acs).astype(y_ref.dtype)
    yield
    for g in range(M_GROUPS):
        upd = _dot(bm_t[g], x_end[:, g * M_GROUP_W:(g + 1) * M_GROUP_W])
        for q in range(ppg):
            p = g * ppg + q
            h_scr[d, p] = h_prev[p] * dec[:, lns[p]] + upd[:, q * LANES:(q + 1) * LANES]


def _ssd_kernel(xsf_ref, bmf_ref, cmf_ref, dtf_ref, xsb_ref, bmb_ref, cmb_ref, dtb_ref, bias_ref, e_ref, alogc_ref,
                yf_ref, yb_ref, h_scr):
    @pl.when(pl.program_id(1) == 0)
    def _():
        h_scr[...] = jnp.zeros_like(h_scr)

    shared = (bias_ref, e_ref, alogc_ref)
    both = (_ssd_chunk(0, xsf_ref, bmf_ref, cmf_ref, dtf_ref, *shared, yf_ref, h_scr),
            _ssd_chunk(1, xsb_ref, bmb_ref, cmb_ref, dtb_ref, *shared, yb_ref, h_scr))
    for _ in itertools.zip_longest(*both):
        pass


def _ssd_call(xs, bm, cm, dt_raw, dtb, emat, alog_c, bsz, seq):
    t = xs.shape[0]
    nc = seq // M_CHUNK
    gs = M_GROUPS * M_STATE
    dt_col = RW_CODES_PAD // LANES
    fwd = lambda bi, ci: bi * nc + ci
    bwd = lambda bi, ci: bi * nc + nc - 1 - ci
    data = lambda row: [pl.BlockSpec((M_CHUNK, M_INNER), lambda bi, ci: (row(bi, ci), 0)),
                        pl.BlockSpec((M_CHUNK, gs), lambda bi, ci: (row(bi, ci), 0)),
                        pl.BlockSpec((M_CHUNK, gs), lambda bi, ci: (row(bi, ci), 0)),
                        pl.BlockSpec((M_CHUNK, LANES), lambda bi, ci: (row(bi, ci), dt_col))]
    full = lambda a: pl.BlockSpec(a.shape, lambda bi, ci: (0,) * a.ndim)
    out = jax.ShapeDtypeStruct((t, M_INNER), BF16)
    return pl.pallas_call(
        _ssd_kernel,
        grid=(bsz, nc),
        in_specs=data(fwd) + data(bwd) + [full(dtb), full(emat), full(alog_c)],
        out_specs=[pl.BlockSpec((M_CHUNK, M_INNER), lambda bi, ci: (fwd(bi, ci), 0)),
                   pl.BlockSpec((M_CHUNK, M_INNER), lambda bi, ci: (bwd(bi, ci), 0))],
        out_shape=[out, out],
        scratch_shapes=[pltpu.VMEM((2, M_INNER // LANES, M_STATE, LANES), F32)],
        compiler_params=_cparams("parallel", "arbitrary"),
        name="ssd_scan",
    )(xs, bm, cm, dt_raw, xs, bm, cm, dt_raw, dtb, emat, alog_c)


def _mamba_post(yf_ref, yb_ref, xs_ref, z_ref, d_skip, nw):
    ld = lambda ref: ref[...].astype(F32)
    y = (ld(yf_ref) + ld(yb_ref) + d_skip * ld(xs_ref)) * _silu(ld(z_ref))
    groups = [slice(g * M_GROUP_W, (g + 1) * M_GROUP_W) for g in range(M_GROUPS)]
    return jnp.concatenate([_rms(y[:, sl], nw[:, sl], 1e-5).astype(BF16) for sl in groups], axis=1)


def _diff_attn_kernel(zero_ref, q_ref, k_ref, v_ref, lam_ref, slope_ref, nw_ref, o_ref, bias_scr, s_scr, *, tq, tk,
                      lambda_init):
    seq = k_ref.shape[0]
    log2e = math.log2(math.e)

    @pl.when(pl.program_id(2) == 0)
    def _():
        rows = pl.program_id(1) * tq + lax.broadcasted_iota(jnp.int32, (tq, seq), 0)
        cols = lax.broadcasted_iota(jnp.int32, (tq, seq), 1)
        bias_scr[...] = (slope_ref[0][:, 0:1] * log2e) * jnp.abs(rows - cols).astype(F32)

    q = (q_ref[...].astype(F32) * (DF_HEAD ** -0.5 * log2e)).astype(BF16)
    lv = lam_ref[...]
    lam = (jnp.exp(jnp.sum(lv[0:1] * lv[1:2], keepdims=True))
           - jnp.exp(jnp.sum(lv[2:3] * lv[3:4], keepdims=True)) + lambda_init)
    map0 = lax.broadcasted_iota(jnp.int32, (tq, LANES), 1) < DF_HEAD
    qm = [jnp.where(map0, q, jnp.zeros_like(q)), jnp.where(map0, jnp.zeros_like(q), q)]
    tiles = [slice(t * tk, (t + 1) * tk) for t in range(seq // tk)]
    rt = min(LANES, tq)
    streams = [(mp, slice(rh * rt, (rh + 1) * rt)) for rh in range(tq // rt) for mp in range(2)]
    v1 = jnp.concatenate([v_ref[...], jnp.ones((seq, LANES), BF16)], axis=1)
    zero = zero_ref[0]

    def scores_tile(st, ts, mx):
        mp, rows = st
        s = _dot_nt(qm[mp][rows], k_ref[ts, :]) - bias_scr[rows, ts]
        s_scr[mp, rows, ts] = s
        for cs in range(tk // LANES):
            blk = s[:, cs * LANES:(cs + 1) * LANES]
            mx = blk if mx is None else jnp.maximum(mx, blk)
        return mx

    def weights(st, m):
        mp, rows = st
        return _dot(jnp.exp2(s_scr[mp + zero, rows, :] - m).astype(BF16), v1)

    m, acc = {}, {}
    for i in range(len(streams) + 1):
        if i < len(streams):
            mx = None
            for ts in tiles:
                mx = scores_tile(streams[i], ts, mx)
            m[i] = jnp.max(mx, axis=-1, keepdims=True)
        if i >= 1:
            acc[i - 1] = weights(streams[i - 1], m[i - 1])
        if i >= 2 and i % 2 == 0:
            rh = i // 2 - 1
            om = [acc[2 * rh + mp][:, 0:LANES] / acc[2 * rh + mp][:, LANES:LANES + 1] for mp in range(2)]
            o = om[0] - lam * om[1]
            o_ref[rh * rt:(rh + 1) * rt, :] = (_rms(o, nw_ref[...], 1e-5) * (1.0 - lambda_init)).astype(o_ref.dtype)


def _diff_attn_call(qkv, lam_vecs, slopes, subln_w, bsz, seq, lambda_init):
    t = qkv.shape[0]
    tq = min(1024, seq)
    nq = seq // tq
    return pl.pallas_call(
        functools.partial(_diff_attn_kernel, tq=tq, tk=min(512, seq), lambda_init=lambda_init),
        grid=(DF_HEADS, nq, bsz),
        in_specs=[pl.BlockSpec(memory_space=pltpu.SMEM),
                  pl.BlockSpec((tq, DF_V), lambda hi, qi, bi: (bi * nq + qi, hi)),
                  pl.BlockSpec((seq, DF_V), lambda hi, qi, bi: (bi, DF_HEADS + hi)),
                  pl.BlockSpec((seq, DF_V), lambda hi, qi, bi: (bi, 2 * DF_HEADS + hi)),
                  pl.BlockSpec((4, DF_HEAD), lambda hi, qi, bi: (0, 0)),
                  pl.BlockSpec((1, 1, LANES), lambda hi, qi, bi: (hi, 0, 0)),
                  pl.BlockSpec((1, DF_V), lambda hi, qi, bi: (0, 0))],
        out_specs=pl.BlockSpec((tq, DF_V), lambda hi, qi, bi: (bi * nq + qi, hi)),
        out_shape=jax.ShapeDtypeStruct((t, DF_HEADS * DF_V), BF16),
        scratch_shapes=[pltpu.VMEM((tq, seq), F32), pltpu.VMEM((2, tq, seq), F32)],
        compiler_params=_cparams("parallel", "parallel", "arbitrary"),
        name="diff_attn",
    )(jnp.zeros((1,), jnp.int32), qkv, qkv, qkv, lam_vecs, slopes, subln_w)


def _mixer_tail_kernel(yf_ref, yb_ref, r_ref, kf_ref, kb_ref, v_ref, g_ref, pvec_ref, mf_ref, mb_ref, xs_ref, z_ref,
                       dskip_ref, mnw_ref, yd_ref, pg_ref, x_ref, mod_ref, nw_ref, wr_ref, wm_ref, wd_ref, wo_ref,
                       o_ref):
    d = D_MODEL
    y_r = _rwkv_post(yf_ref, yb_ref, r_ref, kf_ref, kb_ref, v_ref, g_ref, pvec_ref[...])
    y_m = _mamba_post(mf_ref, mb_ref, xs_ref, z_ref, dskip_ref[...], mnw_ref[...])
    gate = lambda g: _sigmoid(pg_ref[:, g * d:(g + 1) * d].astype(F32))
    merged = (gate(0) * _dot(y_r, wr_ref[...]) + gate(1) * _dot(y_m, wm_ref[...])
              + gate(2) * _dot(yd_ref[...], wd_ref[...]))
    y = _dot(merged.astype(BF16), wo_ref[...])
    o_ref[...] = x_ref[...] + mod_ref[0][2:3] * _rms(y, nw_ref[...], EPS)


def _mixer_tail_call(rw, pvec, mm, d_e, mnw, yd, pg, x2, mod_sub, nw, wr, wm, wd, wo, seq):
    t, d = x2.shape
    tm = 256
    row = lambda w: pl.BlockSpec((tm, w), lambda i: (i, 0))
    vec = lambda a: pl.BlockSpec(a.shape, lambda i: (0, 0))
    res = lambda a: pl.BlockSpec(a.shape, lambda i: (0, 0), pipeline_mode=pl.Buffered(1))
    return pl.pallas_call(
        _mixer_tail_kernel,
        grid=(t // tm,),
        in_specs=([row(RW_DIM)] * 7 + [vec(pvec)] + [row(M_INNER)] * 4 + [vec(d_e), vec(mnw)]
                  + [row(DF_HEADS * DF_V), row(GATE_COLS), row(d),
                     pl.BlockSpec((1, 3, d), lambda i: (i * tm // seq, 0, 0)), vec(nw),
                     res(wr), res(wm), res(wd), res(wo)]),
        out_specs=row(d),
        out_shape=jax.ShapeDtypeStruct((t, d), F32),
        compiler_params=_cparams("parallel"),
        name="mixer_tail",
    )(*rw, pvec, *mm, d_e, mnw, yd, pg, x2, mod_sub, nw, wr, wm, wd, wo)


def _pad_cols(w, n):
    return jnp.pad(w, ((0, 0), (0, n - w.shape[1])))


def _pad_rows(w, n):
    return jnp.pad(w, ((0, n - w.shape[0]), (0, 0)))


def _dir_padded(w):
    z = jnp.zeros_like(w[0])
    return jnp.stack([jnp.concatenate([w[0], z], axis=0), jnp.concatenate([z, w[1]], axis=0)])


def _head_expand_matrix():
    e = np.zeros((2, LANES, M_INNER), np.float32)
    for d in range(2):
        for h in range(M_HEADS):
            e[d, d * M_HEADS + h, h * M_HEADDIM:(h + 1) * M_HEADDIM] = 1.0
    return jnp.asarray(e, dtype=BF16)


def kernel(x, c, ada_w, ada_b, norm_w, ffn_w13, ffn_w2, w_in, rwkv_mu, rwkv_w0, rwkv_w2, rwkv_a0, rwkv_a2, rwkv_g2, rwkv_k_k, rwkv_k_a, rwkv_r_k, rwkv_ln_w, rwkv_ln_b, rwkv_v0, rwkv_v1, rwkv_v2, mamba_conv_w, mamba_conv_b, mamba_dt_bias, mamba_a_log, mamba_d, mamba_norm_w, diff_lambda, diff_subln_w, w_branch_rwkv, w_branch_mamba, w_branch_diff, w_out):
    bsz, seq, d = x.shape
    depth = ada_w.shape[0]
    t = bsz * seq
    x2 = x.reshape(t, d)
    mod_all = _mod_call(c, ada_w, ada_b)
    emat = _head_expand_matrix()
    slopes = jnp.broadcast_to(
        jnp.asarray(2.0 ** (-8.0 * np.arange(1, DF_HEADS + 1) / DF_HEADS), F32)[:, None, None], (DF_HEADS, 1, LANES))
    v_first = None
    for l in range(depth):
        mod = mod_all[l].reshape(bsz, N_SUB, 3, d)
        x2 = _ffn_call(x2, mod[:, 0], norm_w[l, 0:2], ffn_w13[l, 0].astype(BF16), ffn_w2[l, 0].astype(BF16), seq)

        wl = w_in[l]
        o_m = RW_COLS
        o_d = o_m + M_COLS
        o_g = o_d + DF_COLS
        w_cat = jnp.concatenate([
            wl[:, 0:3 * RW_DIM],
            _pad_cols(wl[:, 3 * RW_DIM:RW_COLS], RW_CODES_PAD),
            _pad_cols(wl[:, o_m + M_INNER + M_XBC:o_d], INPROJ_TN - RW_CODES_PAD),
            wl[:, o_m:o_m + M_INNER],
            wl[:, o_m + M_INNER:o_m + M_INNER + M_XBC],
            wl[:, o_d:o_g],
            wl[:, o_g:],
        ], axis=1).astype(BF16)
        p_rkv, p_aux, p_z, p_xbc, p_qkv, p_gate = _inproj_call(
            x2, mod[:, 1], norm_w[l, 2:3], w_cat,
            (3 * RW_DIM, INPROJ_TN, M_INNER, M_XBC, DF_COLS, GATE_COLS),
            (BF16, F32, BF16, BF16, BF16, BF16), seq)
        p_codes = p_dt = p_aux

        mu = rwkv_mu[l]
        mu_p = mu[None, 0:3 * RW_DIM]
        mu_c = _pad_cols(mu[None, 3 * RW_DIM:], RW_CODES_PAD)
        v0 = rwkv_v0[l - 1] if l > 0 else jnp.zeros((RW_DIM,), F32)
        pvec = jnp.stack([rwkv_w0[l, 0], rwkv_w0[l, 1], rwkv_a0[l, 0], rwkv_a0[l, 1],
                          rwkv_k_k[l], rwkv_k_a[l], v0, jnp.zeros((RW_DIM,), F32)])
        g2p = _pad_rows(rwkv_g2[l], 256).astype(BF16)
        vres = None
        if l > 0:
            vres = (v_first, _pad_cols(rwkv_v1[l - 1], LANES).astype(BF16),
                    _pad_rows(rwkv_v2[l - 1], LANES).astype(BF16))
        r, v, kk, k_f, k_b, b_f, b_b, lw_f, lw_b, gate = _rwkv_prep_call(
            p_rkv, p_codes, mu_p, mu_c, pvec, _dir_padded(rwkv_w2[l]).astype(BF16), _dir_padded(rwkv_a2[l]).astype(BF16), g2p, vres, seq)
        if l == 0:
            v_first = v
        y_f, y_b = _rwkv_scan_call(r, v, kk, k_f, b_f, lw_f, k_b, b_b, lw_b, bsz, seq)
        pvec2 = jnp.concatenate([jnp.stack([rwkv_ln_w[l], rwkv_ln_b[l], rwkv_r_k[l].reshape(RW_DIM)]),
                                 jnp.zeros((5, RW_DIM), F32)])

        conv_w = _pad_rows(mamba_conv_w[l], 8)
        xs, bm, cm = _mamba_conv_call(p_xbc, conv_w, mamba_conv_b[l][None], seq)
        dtb = _pad_cols(mamba_dt_bias[l].reshape(1, 2 * M_HEADS), LANES)
        alog_c = _pad_cols(mamba_a_log[l].reshape(1, 2 * M_HEADS), LANES)
        ym_f, ym_b = _ssd_call(xs, bm, cm, p_dt, dtb, emat, alog_c, bsz, seq)
        d_e = jnp.repeat(mamba_d[l], M_HEADDIM)[None]

        lambda_init = 0.8 - 0.6 * math.exp(-0.3 * l)
        y_d = _diff_attn_call(p_qkv, diff_lambda[l], slopes, diff_subln_w[l][None], bsz, seq, lambda_init)

        x2 = _mixer_tail_call((y_f, y_b, r, k_f, k_b, v, gate), pvec2, (ym_f, ym_b, xs, p_z), d_e,
                              mamba_norm_w[l][None], y_d, p_gate, x2, mod[:, 1], norm_w[l, 3:4],
                              w_branch_rwkv[l].astype(BF16), w_branch_mamba[l].astype(BF16),
                              w_branch_diff[l].astype(BF16), w_out[l].astype(BF16), seq)

        x2 = _ffn_call(x2, mod[:, 2], norm_w[l, 4:6], ffn_w13[l, 1].astype(BF16), ffn_w2[l, 1].astype(BF16), seq)
    return x2.reshape(bsz, seq, d)
```

```python
import functools
import itertools
import math

import numpy as np
import jax
import jax.numpy as jnp
from jax import lax
from jax.experimental import pallas as pl
from jax.experimental.pallas import tpu as pltpu

F32 = jnp.float32
BF16 = jnp.bfloat16
HI = lax.Precision.HIGHEST

D_MODEL = 1024
N_SUB = 3
EPS = 1e-6
LANES = 128
SUBLANES = 8
HALO = 2 * SUBLANES
VMEM_LIMIT = 56 * 1024 * 1024

RW_HEAD = 64
RW_DIM = 1024
W_LORA = 64
A_LORA = 64
G_LORA = 160
RW_GN_EPS = 64e-5
RW_COLS = 3 * RW_DIM + 2 * W_LORA + 2 * A_LORA + G_LORA
RW_CODES_PAD = 512
RW_CHUNK = 64
M_INNER = 2048
M_HEADS = 32
M_HEADDIM = 64
M_GROUPS = 4
M_STATE = 128
M_CONV = 5
M_CHUNK = 128
M_XBC = M_INNER + 2 * M_GROUPS * M_STATE
M_COLS = M_INNER + M_XBC + 2 * M_HEADS
M_GROUP_W = M_INNER // M_GROUPS
CONV_SUB = 128
DF_HEADS = 8
DF_HEAD = 64
DF_V = 128
DF_COLS = 3 * DF_HEADS * 2 * DF_HEAD
GATE_COLS = 3 * D_MODEL
NEG_BIG = -1e30


def _cparams(*sem):
    return pltpu.CompilerParams(dimension_semantics=sem, vmem_limit_bytes=VMEM_LIMIT)


def _dot(a, b, **kw):
    return jnp.dot(a, b, preferred_element_type=F32, **kw)


def _dot_nt(a, b, **kw):
    return lax.dot_general(a, b, (((1,), (1,)), ((), ())), preferred_element_type=F32, **kw)


def _split2(x):
    hi = x.astype(BF16)
    return hi, (x - hi.astype(F32)).astype(BF16)


def _dot_sel(x, m_b):
    hi, lo = _split2(x)
    return _dot(hi, m_b) + _dot(lo, m_b)


def _sel_dot(m_b, x):
    hi, lo = _split2(x)
    return _dot(m_b, hi) + _dot(m_b, lo)


def _bdot(a, b):
    return _dot(a.astype(BF16), b.astype(BF16))


def _sigmoid(x):
    return 1.0 / (1.0 + jnp.exp(-x))


def _silu(x):
    return x * _sigmoid(x)


def _softplus(x):
    return jnp.maximum(x, 0.0) + jnp.log(1.0 + jnp.exp(-jnp.abs(x)))


def _rms(x, w, eps):
    return x * lax.rsqrt(jnp.mean(x * x, axis=-1, keepdims=True) + eps) * w


def _norm_mod(x, nw, shift, scale):
    return _rms(x, nw, EPS) * (1.0 + scale) + shift


def _pair_ones(scale=1.0):
    r = lax.broadcasted_iota(jnp.int32, (LANES, LANES), 0) // RW_HEAD
    c = lax.broadcasted_iota(jnp.int32, (LANES, LANES), 1) // RW_HEAD
    return jnp.where(r == c, scale, 0.0).astype(BF16)


def _shift_rows(cur, prevz, nextz, d):
    tm = cur.shape[0]
    rolled = pltpu.roll(cur, (-d) % tm, axis=0)
    rid = lax.broadcasted_iota(jnp.int32, (SUBLANES, cur.shape[1]), 0)
    if d < 0:
        fix = pltpu.roll(prevz, (-d) % SUBLANES, axis=0)
        top = jnp.where(rid < -d, fix, rolled[0:SUBLANES])
        return jnp.concatenate([top, rolled[SUBLANES:]], axis=0)
    fix = pltpu.roll(nextz, (SUBLANES - d) % SUBLANES, axis=0)
    bot = jnp.where(rid >= SUBLANES - d, fix, rolled[tm - SUBLANES:])
    return jnp.concatenate([rolled[:tm - SUBLANES], bot], axis=0)


def _shift_rows_mxu(cur_b, prevz, nextz, d):
    assert cur_b.dtype == BF16
    tm = cur_b.shape[0]
    ri = lax.broadcasted_iota(jnp.int32, (tm, tm), 0)
    ci = lax.broadcasted_iota(jnp.int32, (tm, tm), 1)
    shifted = _dot(jnp.where(ci == ri + d, 1.0, 0.0).astype(BF16), cur_b)
    rid = lax.broadcasted_iota(jnp.int32, (SUBLANES, cur_b.shape[1]), 0)
    if d < 0:
        fix = jnp.where(rid < -d, pltpu.roll(prevz, (-d) % SUBLANES, axis=0), 0.0)
        return jnp.concatenate([shifted[0:SUBLANES] + fix, shifted[SUBLANES:]], axis=0)
    fix = jnp.where(rid >= SUBLANES - d, pltpu.roll(nextz, (SUBLANES - d) % SUBLANES, axis=0), 0.0)
    return jnp.concatenate([shifted[:tm - SUBLANES], shifted[tm - SUBLANES:] + fix], axis=0)


def _nbr_mean_minus_self_mxu(cur_b, prevz, nextz):
    assert cur_b.dtype == BF16
    tm = cur_b.shape[0]
    ri = lax.broadcasted_iota(jnp.int32, (tm, tm), 0)
    ci = lax.broadcasted_iota(jnp.int32, (tm, tm), 1)
    band = jnp.where(ci == ri, -1.0, jnp.where(jnp.abs(ci - ri) == 1, 0.5, 0.0)).astype(BF16)
    out = _dot(band, cur_b)
    rid = lax.broadcasted_iota(jnp.int32, (SUBLANES, cur_b.shape[1]), 0)
    top = jnp.where(rid == 0, 0.5 * prevz[SUBLANES - 1:SUBLANES], 0.0)
    bot = jnp.where(rid == SUBLANES - 1, 0.5 * nextz[0:1], 0.0)
    return jnp.concatenate([out[0:SUBLANES] + top, out[SUBLANES:tm - SUBLANES], out[tm - SUBLANES:] + bot], axis=0)


def _halo_specs(tm, width, n_rows):
    rb = tm // HALO
    last = n_rows // HALO - 1
    return [
        pl.BlockSpec((tm, width), lambda i: (i, 0)),
        pl.BlockSpec((HALO, width), lambda i: (jnp.maximum(i * rb - 1, 0), 0)),
        pl.BlockSpec((HALO, width), lambda i: (jnp.minimum((i + 1) * rb, last), 0)),
    ]


def _halo_values(cur_ref, prev_ref, next_ref, tm, seq):
    i = pl.program_id(0)
    first = (i * tm) % seq == 0
    last = ((i + 1) * tm) % seq == 0
    prevz = jnp.where(first, 0.0, prev_ref[...].astype(F32)[HALO - SUBLANES:])
    nextz = jnp.where(last, 0.0, next_ref[...].astype(F32)[:SUBLANES])
    return cur_ref[...].astype(F32), prevz, nextz


def _mod_kernel(c_ref, w_ref, b_ref, o_ref):
    o_ref[0] = _dot(_silu(c_ref[...]), w_ref[0], precision=HI) + b_ref[0]


def _mod_call(c, ada_w, ada_b):
    depth, d, n = ada_w.shape
    bsz = c.shape[0]
    tn = 1152
    return pl.pallas_call(
        _mod_kernel,
        grid=(depth, n // tn),
        in_specs=[pl.BlockSpec((bsz, d), lambda l, j: (0, 0)),
                  pl.BlockSpec((1, d, tn), lambda l, j: (l, 0, j)),
                  pl.BlockSpec((1, 1, tn), lambda l, j: (l, 0, j))],
        out_specs=pl.BlockSpec((1, bsz, tn), lambda l, j: (l, 0, j)),
        out_shape=jax.ShapeDtypeStruct((depth, bsz, n), F32),
        compiler_params=_cparams("parallel", "parallel"),
        name="adaln_mod",
    )(c, ada_w, ada_b.reshape(depth, 1, n))


def _ffn_kernel(x_ref, mod_ref, nw_ref, w13_ref, w2_ref, o_ref, *, tf):
    dff = w2_ref.shape[0]
    m = mod_ref[0]
    x = x_ref[...]
    h = _norm_mod(x, nw_ref[0:1], m[0:1], m[1:2]).astype(BF16)
    acc = None
    for j in range(dff // tf):
        g = _dot(h, w13_ref[:, j * tf:(j + 1) * tf])
        u = _dot(h, w13_ref[:, dff + j * tf:dff + (j + 1) * tf])
        part = _dot((_silu(g) * u).astype(BF16), w2_ref[j * tf:(j + 1) * tf, :])
        acc = part if acc is None else acc + part
    o_ref[...] = x + 0.5 * m[2:3] * _rms(acc, nw_ref[1:2], EPS)


def _ffn_call(x2, mod_sub, nw2, w13, w2, seq):
    t, d = x2.shape
    tm = min(1024, seq)
    resident = lambda a: pl.BlockSpec(a.shape, lambda i: (0, 0), pipeline_mode=pl.Buffered(1))
    return pl.pallas_call(
        functools.partial(_ffn_kernel, tf=256),
        grid=(t // tm,),
        in_specs=[pl.BlockSpec((tm, d), lambda i: (i, 0)),
                  pl.BlockSpec((1, 3, d), lambda i: (i * tm // seq, 0, 0)),
                  pl.BlockSpec((2, d), lambda i: (0, 0)),
                  resident(w13), resident(w2)],
        out_specs=pl.BlockSpec((tm, d), lambda i: (i, 0)),
        out_shape=jax.ShapeDtypeStruct((t, d), F32),
        compiler_params=_cparams("parallel"),
        name="swiglu_halfstep",
    )(x2, mod_sub, nw2, w13, w2)


INPROJ_TN = 1024


def _inproj_kernel(x_ref, mod_ref, nw_ref, w_ref, *rest, starts):
    o_refs, h_scr = rest[:-1], rest[-1]
    j = pl.program_id(1)

    @pl.when(j == 0)
    def _():
        m = mod_ref[0]
        h_scr[...] = _norm_mod(x_ref[...], nw_ref[...], m[0:1], m[1:2]).astype(BF16)

    for k, o_ref in enumerate(o_refs):
        @pl.when((j >= starts[k]) & (j < starts[k + 1]))
        def _(o_ref=o_ref):
            o_ref[...] = _dot(h_scr[...], w_ref[...]).astype(o_ref.dtype)


def _inproj_call(x2, mod_sub, nw, w, widths, dtypes, seq):
    t, d = x2.shape
    tn = INPROJ_TN
    tm = min(1024, seq)
    starts = [0]
    for wd in widths:
        starts.append(starts[-1] + wd // tn)

    def out_spec(k):
        return pl.BlockSpec((tm, tn), lambda i, j: (i, jnp.clip(j - starts[k], 0, widths[k] // tn - 1)))

    return pl.pallas_call(
        functools.partial(_inproj_kernel, starts=tuple(starts)),
        grid=(t // tm, starts[-1]),
        in_specs=[pl.BlockSpec((tm, d), lambda i, j: (i, 0)),
                  pl.BlockSpec((1, 3, d), lambda i, j: (i * tm // seq, 0, 0)),
                  pl.BlockSpec((1, d), lambda i, j: (0, 0)),
                  pl.BlockSpec((d, tn), lambda i, j: (0, j))],
        out_specs=[out_spec(k) for k in range(len(widths))],
        out_shape=[jax.ShapeDtypeStruct((t, wd), dt) for wd, dt in zip(widths, dtypes)],
        scratch_shapes=[pltpu.VMEM((tm, d), BF16)],
        compiler_params=_cparams("parallel", "arbitrary"),
        name="norm_inproj",
    )(x2, mod_sub, nw, w)


def _rwkv_prep_kernel(*refs, tm, seq, has_vres):
    (p_ref, pp_ref, pn_ref, c_ref, cp_ref, cn_ref, mup_ref, muc_ref, pvec_ref,
     w2_ref, a2_ref, g2_ref) = refs[:12]
    refs = refs[12:]
    if has_vres:
        vf_ref, v1_ref, v2_ref = refs[:3]
        refs = refs[3:]
    r_o, v_o, kk_o, kf_o, kb_o, bf_o, bb_o, lwf_o, lwb_o, g_o = refs

    def shift_mix(cur_ref, prev_ref, next_ref, mu):
        cur, prevz, nextz = _halo_values(cur_ref, prev_ref, next_ref, tm, seq)
        if cur_ref.dtype == BF16:
            return cur + mu * _nbr_mean_minus_self_mxu(cur_ref[...], prevz, nextz)
        nb = 0.5 * (_shift_rows(cur, prevz, nextz, -1) + _shift_rows(cur, prevz, nextz, 1))
        return cur + mu * (nb - cur)

    p = shift_mix(p_ref, pp_ref, pn_ref, mup_ref[...])
    codes = shift_mix(c_ref, cp_ref, cn_ref, muc_ref[...])
    r = p[:, 0:RW_DIM]
    k = p[:, RW_DIM:2 * RW_DIM]
    v = p[:, 2 * RW_DIM:3 * RW_DIM]
    cw = jnp.tanh(codes[:, 0:2 * W_LORA])
    ca = codes[:, 2 * W_LORA:2 * W_LORA + 2 * A_LORA]
    cg = _sigmoid(codes[:, 2 * W_LORA + 2 * A_LORA:])
    pv = pvec_ref[...]
    if has_vres:
        lo = _bdot(_bdot(v, v1_ref[...]), v2_ref[...])
        v = v + (vf_ref[...].astype(F32) - v) * _sigmoid(pv[6:7] + lo)
    r_o[...] = r.astype(r_o.dtype)
    v_o[...] = v.astype(v_o.dtype)
    g_o[...] = _bdot(cg, g2_ref[...]).astype(g_o.dtype)
    iclr = []
    for d, lw_o in ((0, lwf_o), (1, lwb_o)):
        lw_o[...] = -math.exp(-0.5) * _sigmoid(pv[d:d + 1] + _bdot(cw, w2_ref[d]))
        iclr.append(_sigmoid(pv[2 + d:3 + d] + _bdot(ca, a2_ref[d])))
    kf_o[...] = (k * (1.0 + (iclr[0] - 1.0) * pv[5:6])).astype(kf_o.dtype)
    kb_o[...] = (k * (1.0 + (iclr[1] - 1.0) * pv[5:6])).astype(kb_o.dtype)
    ones2 = _pair_ones()
    kkr = k * pv[4:5]
    for cb in range(RW_DIM // LANES):
        sl = slice(cb * LANES, (cb + 1) * LANES)
        blk = kkr[:, sl]
        kkn = blk * lax.rsqrt(_dot_sel(blk * blk, ones2) + 1e-12)
        kk_o[:, sl] = kkn.astype(kk_o.dtype)
        bf_o[:, sl] = (kkn * iclr[0][:, sl]).astype(bf_o.dtype)
        bb_o[:, sl] = (kkn * iclr[1][:, sl]).astype(bb_o.dtype)


def _rwkv_prep_call(rkv, codes, mu_p, mu_c, pvec, w2p, a2p, g2p, vres, seq):
    t = rkv.shape[0]
    tm = 256
    has_vres = vres is not None
    full = lambda shape: pl.BlockSpec(shape, lambda i: (0,) * len(shape))
    in_specs = (_halo_specs(tm, 3 * RW_DIM, t) + _halo_specs(tm, RW_CODES_PAD, t)
                + [full((1, 3 * RW_DIM)), full((1, RW_CODES_PAD)), full((8, RW_DIM)),
                   full((2, 2 * W_LORA, RW_DIM)), full((2, 2 * A_LORA, RW_DIM)), full((256, RW_DIM))])
    args = [rkv, rkv, rkv, codes, codes, codes, mu_p, mu_c, pvec, w2p, a2p, g2p]
    if has_vres:
        in_specs += [pl.BlockSpec((tm, RW_DIM), lambda i: (i, 0)), full((RW_DIM, LANES)), full((LANES, RW_DIM))]
        args += list(vres)
    dtypes = [BF16] * 7 + [F32, F32, BF16]
    return pl.pallas_call(
        functools.partial(_rwkv_prep_kernel, tm=tm, seq=seq, has_vres=has_vres),
        grid=(t // tm,),
        in_specs=in_specs,
        out_specs=[pl.BlockSpec((tm, RW_DIM), lambda i: (i, 0))] * 10,
        out_shape=[jax.ShapeDtypeStruct((t, RW_DIM), dt) for dt in dtypes],
        compiler_params=_cparams("parallel"),
        name="rwkv_prep",
    )(*args)


def _rwkv_scan_kernel(rf_ref, vf_ref, kkf_ref, kf_ref, bf_ref, lwf_ref, rb_ref, vb_ref, kkb_ref, kb_ref, bb_ref,
                      lwb_ref, yf_ref, yb_ref, h_scr, *, n_chunks):
    c = RW_CHUNK
    n_pairs = RW_DIM // LANES

    @pl.when(pl.program_id(1) == 0)
    def _():
        h_scr[...] = jnp.zeros_like(h_scr)

    ri = lax.broadcasted_iota(jnp.int32, (LANES, LANES), 0)
    ci = lax.broadcasted_iota(jnp.int32, (LANES, LANES), 1)
    same = (ri // c) == (ci // c)
    tr = lax.broadcasted_iota(jnp.int32, (c, LANES), 0)
    tc = lax.broadcasted_iota(jnp.int32, (c, LANES), 1) % c
    strict = [tc < tr, tc > tr]
    incl = [tc <= tr, tc >= tr]
    blk = [(tr // w) == (tc // w) for w in (8, 16, 32, 64)]
    eye = jnp.where(tr == tc, 1.0, 0.0).astype(F32)
    r64 = lax.broadcasted_iota(jnp.int32, (c, c), 0)
    c64 = lax.broadcasted_iota(jnp.int32, (c, c), 1)
    tri = [jnp.where(c64 <= r64, 1.0, 0.0).astype(BF16), jnp.where(c64 >= r64, 1.0, 0.0).astype(BF16)]
    head0 = lax.broadcasted_iota(jnp.int32, (c, LANES), 1) < RW_HEAD
    end_row = [c - 1, 0]
    refs = [(rf_ref, vf_ref, kkf_ref, kf_ref, bf_ref, lwf_ref, yf_ref),
            (rb_ref, vb_ref, kkb_ref, kb_ref, bb_ref, lwb_ref, yb_ref)]
    lns = [slice(pr * LANES, (pr + 1) * LANES) for pr in range(n_pairs)]
    chains = [(d, p) for p in range(n_pairs) for d in range(2)]

    def stack(x):
        return jnp.concatenate([jnp.where(head0, x, 0.0), jnp.where(head0, 0.0, x)], axis=0).astype(BF16)

    def per_head(x, y):
        return _dot(x.astype(BF16), stack(y))

    def chunk(step, carry):
        rows = [pl.ds(pl.multiple_of(step * c, c), c), pl.ds(pl.multiple_of((n_chunks - 1 - step) * c, c), c)]
        wide = []
        for d in range(2):
            r_ref, v_ref, kk_ref, k_ref, b_ref, lw_ref, _ = refs[d]
            lw = lw_ref[rows[d], :]
            ld = lambda ref: ref[rows[d], :].astype(F32)
            cum = _sel_dot(tri[d], lw)
            tot = cum[end_row[d]:end_row[d] + 1]
            g_inv, g_end = jnp.exp(-cum), jnp.exp(tot - cum)
            k_all, b_all = ld(k_ref), ld(b_ref)
            wide.append(dict(at=-ld(kk_ref) * jnp.exp(cum - lw), rt=ld(r_ref) * jnp.exp(cum),
                             bt=b_all * g_inv, kt=k_all * g_inv, bg=b_all * g_end, kg=k_all * g_end,
                             v=ld(v_ref), g_tot=jnp.exp(tot)))
        idx = range(len(chains))
        sel = lambda name, i: wide[chains[i][0]][name][:, lns[chains[i][1]]]
        ar = [jnp.concatenate([sel("at", i), sel("rt", i)], axis=0).astype(BF16) for i in idx]
        bk = [jnp.concatenate([stack(sel("bt", i)), stack(sel("kt", i))], axis=0) for i in idx]
        v_s = [stack(sel("v", i)) for i in idx]
        ht = [h_scr[d, p] for d, p in chains]
        gram = [_dot_nt(ar[i], bk[i]) for i in idx]
        a_ab = [jnp.where(strict[d], gram[i][0:c, 0:LANES], 0.0) for i, (d, p) in enumerate(chains)]
        a_ak = [jnp.where(strict[d], gram[i][0:c, LANES:], 0.0).astype(BF16) for i, (d, p) in enumerate(chains)]
        a_rb = [jnp.where(incl[d], gram[i][c:, 0:LANES], 0.0) for i, (d, p) in enumerate(chains)]
        a_rk = [jnp.where(incl[d], gram[i][c:, LANES:], 0.0).astype(BF16) for i, (d, p) in enumerate(chains)]
        lhs = [jnp.concatenate([ar[i], jnp.concatenate([a_ak[i], a_rk[i]], axis=0)], axis=1) for i in idx]
        rhs = [jnp.concatenate([ht[i].T.astype(BF16), v_s[i]], axis=0) for i in idx]
        xy0 = [_dot(lhs[i], rhs[i]) for i in idx]
        p1 = [jnp.where(blk[0], a_ab[i], 0.0) for i in idx]
        p2 = [per_head(p1[i], p1[i]) for i in idx]
        p4 = [per_head(p2[i], p2[i]) for i in idx]
        t_inv = [eye + p1[i] + p2[i] + per_head(p1[i], p2[i]) for i in idx]
        t_inv = [t_inv[i] + per_head(t_inv[i], p4[i]) for i in idx]
        for lvl in range(1, 4):
            off = [jnp.where(blk[lvl] & ~blk[lvl - 1], a_ab[i], 0.0) for i in idx]
            mid = [per_head(t_inv[i], off[i]) for i in idx]
            t_inv = [t_inv[i] + per_head(mid[i], t_inv[i]) for i in idx]
        u = [per_head(t_inv[i], xy0[i][0:c]) for i in idx]
        y = [xy0[i][c:] + per_head(a_rb[i], u[i]) for i in idx]
        for d in range(2):
            ys = [y[i] for i, (dd, p) in enumerate(chains) if dd == d]
            refs[d][6][rows[d], :] = jnp.concatenate(ys, axis=1).astype(refs[d][6].dtype)
        for i, (d, p) in enumerate(chains):
            uv_t = jnp.concatenate([u[i], sel("v", i)], axis=0).T.astype(BF16)
            bkg = jnp.concatenate([sel("bg", i), sel("kg", i)], axis=0).astype(BF16)
            upd = jnp.where(same, _dot(uv_t, bkg), 0.0)
            h_scr[d, p] = ht[i] * wide[d]["g_tot"][:, lns[p]] + upd
        return carry

    lax.fori_loop(0, n_chunks, chunk, 0)


def _rwkv_scan_call(r, v, kk, k_f, b_f, lw_f, k_b, b_b, lw_b, bsz, seq):
    t = r.shape[0]
    tb = min(4 * RW_CHUNK, seq)
    nt = seq // tb
    fwd = pl.BlockSpec((tb, RW_DIM), lambda bi, ti: (bi * nt + ti, 0))
    bwd = pl.BlockSpec((tb, RW_DIM), lambda bi, ti: (bi * nt + nt - 1 - ti, 0))
    out = jax.ShapeDtypeStruct((t, RW_DIM), BF16)
    return pl.pallas_call(
        functools.partial(_rwkv_scan_kernel, n_chunks=tb // RW_CHUNK),
        grid=(bsz, nt),
        in_specs=[fwd] * 6 + [bwd] * 6,
        out_specs=[fwd, bwd],
        out_shape=[out, out],
        scratch_shapes=[pltpu.VMEM((2, RW_DIM // LANES, LANES, LANES), F32)],
        compiler_params=_cparams("parallel", "arbitrary"),
        name="rwkv_scan",
    )(r, v, kk, k_f, b_f, lw_f, r, v, kk, k_b, b_b, lw_b)


def _rwkv_post(yf_ref, yb_ref, r_ref, kf_ref, kb_ref, v_ref, g_ref, pv):
    mean2 = _pair_ones(1.0 / RW_HEAD)
    ones2 = _pair_ones()
    outs = []
    for cb in range(RW_DIM // LANES):
        sl = slice(cb * LANES, (cb + 1) * LANES)
        ld = lambda ref: ref[:, sl].astype(F32)
        y = ld(yf_ref) + ld(yb_ref)
        dlt = y - _dot_sel(y, mean2)
        var = _dot_sel(dlt * dlt, mean2)
        yn = dlt * lax.rsqrt(var + RW_GN_EPS) * pv[0:1, sl] + pv[1:2, sl]
        rk = ld(r_ref) * (ld(kf_ref) + ld(kb_ref)) * pv[2:3, sl]
        bonus = _dot_sel(rk, ones2) * ld(v_ref)
        outs.append(((yn + bonus) * ld(g_ref)).astype(BF16))
    return jnp.concatenate(outs, axis=1)


def _mamba_conv_kernel(x_ref, xp_ref, xn_ref, w_ref, b_ref, xs_o, bm_o, cm_o, *, tm, seq):
    _, prevz, nextz = _halo_values(x_ref, xp_ref, xn_ref, tm, seq)
    w = w_ref[...]
    half = (M_CONV - 1) // 2
    sb = CONV_SUB
    n_sub = tm // sb
    for s in range(n_sub):
        rows = slice(s * sb, (s + 1) * sb)
        cur_b = x_ref[rows, :]
        pz = prevz if s == 0 else x_ref[s * sb - HALO:s * sb, :].astype(F32)[HALO - SUBLANES:]
        nz = nextz if s == n_sub - 1 else x_ref[(s + 1) * sb:(s + 1) * sb + HALO, :].astype(F32)[:SUBLANES]
        acc = cur_b.astype(F32) * w[half:half + 1] + b_ref[...]
        for tap in range(M_CONV):
            if tap != half:
                acc = acc + _shift_rows_mxu(cur_b, pz, nz, tap - half) * w[tap:tap + 1]
        y = _silu(acc)
        xs_o[rows, :] = y[:, 0:M_INNER].astype(xs_o.dtype)
        bm_o[rows, :] = y[:, M_INNER:M_INNER + M_GROUPS * M_STATE].astype(bm_o.dtype)
        cm_o[rows, :] = y[:, M_INNER + M_GROUPS * M_STATE:].astype(cm_o.dtype)


def _mamba_conv_call(xbc, conv_w, conv_b, seq):
    t = xbc.shape[0]
    tm = min(512, seq)
    gs = M_GROUPS * M_STATE
    return pl.pallas_call(
        functools.partial(_mamba_conv_kernel, tm=tm, seq=seq),
        grid=(t // tm,),
        in_specs=_halo_specs(tm, M_XBC, t) + [pl.BlockSpec((8, M_XBC), lambda i: (0, 0)),
                                             pl.BlockSpec((1, M_XBC), lambda i: (0, 0))],
        out_specs=[pl.BlockSpec((tm, M_INNER), lambda i: (i, 0)),
                   pl.BlockSpec((tm, gs), lambda i: (i, 0)),
                   pl.BlockSpec((tm, gs), lambda i: (i, 0))],
        out_shape=[jax.ShapeDtypeStruct((t, M_INNER), BF16),
                   jax.ShapeDtypeStruct((t, gs), BF16),
                   jax.ShapeDtypeStruct((t, gs), BF16)],
        compiler_params=_cparams("parallel"),
        name="mamba_conv",
    )(xbc, xbc, xbc, conv_w, conv_b)


def _ssd_chunk(d, xs_ref, bm_ref, cm_ref, dt_ref, dtb_ref, e_ref, alogc_ref, y_ref, h_scr):
    c = M_CHUNK
    n_pairs = M_INNER // LANES
    ppg = n_pairs // M_GROUPS
    ri = lax.broadcasted_iota(jnp.int32, (c, c), 0)
    ci = lax.broadcasted_iota(jnp.int32, (c, c), 1)
    keep = (ci >= ri) if d else (ci <= ri)
    tri = jnp.where(keep, 1.0, 0.0).astype(BF16)
    end_row = 0 if d else c - 1
    head0 = lax.broadcasted_iota(jnp.int32, (c, LANES), 1) < M_HEADDIM

    dt = _softplus(dt_ref[...] + dtb_ref[...])
    acs_c = _sel_dot(tri, dt * (-jnp.exp(alogc_ref[...])))
    acs_ct = acs_c.T
    dte = _dot_sel(dt, e_ref[d])
    acs = _dot_sel(acs_c, e_ref[d])
    tot = acs[end_row:end_row + 1]
    yield
    xdt = xs_ref[...].astype(F32) * dte
    e_acs = jnp.exp(acs)
    x_end = (xdt * jnp.exp(tot - acs)).astype(BF16)
    xdt_b = xdt.astype(BF16)
    dec = jnp.exp(tot)
    gsl = [slice(g * M_STATE, (g + 1) * M_STATE) for g in range(M_GROUPS)]
    bm_b = [bm_ref[:, sl] for sl in gsl]
    cm_b = [cm_ref[:, sl] for sl in gsl]
    bm_t = [bm_ref[:, sl].astype(F32).T.astype(BF16) for sl in gsl]
    cb = [_dot_nt(cm_b[g], bm_b[g]) for g in range(M_GROUPS)]
    yield
    prs = range(n_pairs)
    lns = [slice(p * LANES, (p + 1) * LANES) for p in prs]
    ys = []
    for p in prs:
        pair = []
        for hh in range(2):
            idx = d * M_HEADS + 2 * p + hh
            seg = acs_c[:, idx:idx + 1] - acs_ct[idx:idx + 1, :]
            lmat = jnp.exp(jnp.where(keep, seg, NEG_BIG))
            pair.append(_dot((cb[p // ppg] * lmat).astype(BF16), xdt_b[:, lns[p]]))
        ys.append(jnp.where(head0, pair[0], pair[1]))
        yield
    h_prev = [h_scr[d, p] for p in prs]
    y_off = [_dot(cm_b[g], jnp.concatenate([h_prev[p].astype(BF16) for p in range(g * ppg, (g + 1) * ppg)], axis=1))
             for g in range(M_GROUPS)]
    y_ref[...] = (jnp.concatenate(ys, axis=1) + jnp.concatenate(y_off, axis=1) * e_acs).astype(y_ref.dtype)
    yield
    for g in range(M_GROUPS):
        upd = _dot(bm_t[g], x_end[:, g * M_GROUP_W:(g + 1) * M_GROUP_W])
        for q in range(ppg):
            p = g * ppg + q
            h_scr[d, p] = h_prev[p] * dec[:, lns[p]] + upd[:, q * LANES:(q + 1) * LANES]


def _ssd_kernel(xsf_ref, bmf_ref, cmf_ref, dtf_ref, xsb_ref, bmb_ref, cmb_ref, dtb_ref, bias_ref, e_ref, alogc_ref,
                yf_ref, yb_ref, h_scr):
    @pl.when(pl.program_id(1) == 0)
    def _():
        h_scr[...] = jnp.zeros_like(h_scr)

    shared = (bias_ref, e_ref, alogc_ref)
    both = (_ssd_chunk(0, xsf_ref, bmf_ref, cmf_ref, dtf_ref, *shared, yf_ref, h_scr),
            _ssd_chunk(1, xsb_ref, bmb_ref, cmb_ref, dtb_ref, *shared, yb_ref, h_scr))
    for _ in itertools.zip_longest(*both):
        pass


def _ssd_call(xs, bm, cm, dt_raw, dtb, emat, alog_c, bsz, seq):
    t = xs.shape[0]
    nc = seq // M_CHUNK
    gs = M_GROUPS * M_STATE
    dt_col = RW_CODES_PAD // LANES
    fwd = lambda bi, ci: bi * nc + ci
    bwd = lambda bi, ci: bi * nc + nc - 1 - ci
    data = lambda row: [pl.BlockSpec((M_CHUNK, M_INNER), lambda bi, ci: (row(bi, ci), 0)),
                        pl.BlockSpec((M_CHUNK, gs), lambda bi, ci: (row(bi, ci), 0)),
                        pl.BlockSpec((M_CHUNK, gs), lambda bi, ci: (row(bi, ci), 0)),
                        pl.BlockSpec((M_CHUNK, LANES), lambda bi, ci: (row(bi, ci), dt_col))]
    full = lambda a: pl.BlockSpec(a.shape, lambda bi, ci: (0,) * a.ndim)
    out = jax.ShapeDtypeStruct((t, M_INNER), BF16)
    return pl.pallas_call(
        _ssd_kernel,
        grid=(bsz, nc),
        in_specs=data(fwd) + data(bwd) + [full(dtb), full(emat), full(alog_c)],
        out_specs=[pl.BlockSpec((M_CHUNK, M_INNER), lambda bi, ci: (fwd(bi, ci), 0)),
                   pl.BlockSpec((M_CHUNK, M_INNER), lambda bi, ci: (bwd(bi, ci), 0))],
        out_shape=[out, out],
        scratch_shapes=[pltpu.VMEM((2, M_INNER // LANES, M_STATE, LANES), F32)],
        compiler_params=_cparams("parallel", "arbitrary"),
        name="ssd_scan",
    )(xs, bm, cm, dt_raw, xs, bm, cm, dt_raw, dtb, emat, alog_c)


def _mamba_post(yf_ref, yb_ref, xs_ref, z_ref, d_skip, nw):
    ld = lambda ref: ref[...].astype(F32)
    y = (ld(yf_ref) + ld(yb_ref) + d_skip * ld(xs_ref)) * _silu(ld(z_ref))
    groups = [slice(g * M_GROUP_W, (g + 1) * M_GROUP_W) for g in range(M_GROUPS)]
    return jnp.concatenate([_rms(y[:, sl], nw[:, sl], 1e-5).astype(BF16) for sl in groups], axis=1)


def _diff_attn_kernel(zero_ref, q_ref, k_ref, v_ref, lam_ref, slope_ref, nw_ref, o_ref, bias_scr, s_scr, *, tq, tk,
                      lambda_init):
    seq = k_ref.shape[0]
    log2e = math.log2(math.e)

    @pl.when(pl.program_id(2) == 0)
    def _():
        rows = pl.program_id(1) * tq + lax.broadcasted_iota(jnp.int32, (tq, seq), 0)
        cols = lax.broadcasted_iota(jnp.int32, (tq, seq), 1)
        bias_scr[...] = (slope_ref[0][:, 0:1] * log2e) * jnp.abs(rows - cols).astype(F32)

    q = (q_ref[...].astype(F32) * (DF_HEAD ** -0.5 * log2e)).astype(BF16)
    lv = lam_ref[...]
    lam = (jnp.exp(jnp.sum(lv[0:1] * lv[1:2], keepdims=True))
           - jnp.exp(jnp.sum(lv[2:3] * lv[3:4], keepdims=True)) + lambda_init)
    map0 = lax.broadcasted_iota(jnp.int32, (tq, LANES), 1) < DF_HEAD
    qm = [jnp.where(map0, q, jnp.zeros_like(q)), jnp.where(map0, jnp.zeros_like(q), q)]
    tiles = [slice(t * tk, (t + 1) * tk) for t in range(seq // tk)]
    rt = min(2 * LANES, tq)
    streams = [(mp, slice(rh * rt, (rh + 1) * rt)) for rh in range(tq // rt) for mp in range(2)]
    v1 = jnp.concatenate([v_ref[...], jnp.ones((seq, LANES), BF16)], axis=1)
    zero = zero_ref[0]

    def scores_tile(st, ts, mx):
        mp, rows = st
        s = _dot_nt(qm[mp][rows], k_ref[ts, :]) - bias_scr[rows, ts]
        s_scr[mp, rows, ts] = s
        for cs in range(tk // LANES):
            blk = s[:, cs * LANES:(cs + 1) * LANES]
            mx = blk if mx is None else jnp.maximum(mx, blk)
        return mx

    def weights(st, m):
        mp, rows = st
        return _dot(jnp.exp2(s_scr[mp + zero, rows, :] - m).astype(BF16), v1)

    m, acc = {}, {}
    for i in range(len(streams) + 1):
        if i < len(streams):
            mx = None
            for ts in tiles:
                mx = scores_tile(streams[i], ts, mx)
            m[i] = jnp.max(mx, axis=-1, keepdims=True)
        if i >= 1:
            acc[i - 1] = weights(streams[i - 1], m[i - 1])
        if i >= 2 and i % 2 == 0:
            rh = i // 2 - 1
            om = [acc[2 * rh + mp][:, 0:LANES] / acc[2 * rh + mp][:, LANES:LANES + 1] for mp in range(2)]
            o = om[0] - lam * om[1]
            o_ref[rh * rt:(rh + 1) * rt, :] = (_rms(o, nw_ref[...], 1e-5) * (1.0 - lambda_init)).astype(o_ref.dtype)


def _diff_attn_call(qkv, lam_vecs, slopes, subln_w, bsz, seq, lambda_init):
    t = qkv.shape[0]
    tq = min(1024, seq)
    nq = seq // tq
    return pl.pallas_call(
        functools.partial(_diff_attn_kernel, tq=tq, tk=min(512, seq), lambda_init=lambda_init),
        grid=(DF_HEADS, nq, bsz),
        in_specs=[pl.BlockSpec(memory_space=pltpu.SMEM),
                  pl.BlockSpec((tq, DF_V), lambda hi, qi, bi: (bi * nq + qi, hi)),
                  pl.BlockSpec((seq, DF_V), lambda hi, qi, bi: (bi, DF_HEADS + hi)),
                  pl.BlockSpec((seq, DF_V), lambda hi, qi, bi: (bi, 2 * DF_HEADS + hi)),
                  pl.BlockSpec((4, DF_HEAD), lambda hi, qi, bi: (0, 0)),
                  pl.BlockSpec((1, 1, LANES), lambda hi, qi, bi: (hi, 0, 0)),
                  pl.BlockSpec((1, DF_V), lambda hi, qi, bi: (0, 0))],
        out_specs=pl.BlockSpec((tq, DF_V), lambda hi, qi, bi: (bi * nq + qi, hi)),
        out_shape=jax.ShapeDtypeStruct((t, DF_HEADS * DF_V), BF16),
        scratch_shapes=[pltpu.VMEM((tq, seq), F32), pltpu.VMEM((2, tq, seq), F32)],
        compiler_params=_cparams("parallel", "parallel", "arbitrary"),
        name="diff_attn",
    )(jnp.zeros((1,), jnp.int32), qkv, qkv, qkv, lam_vecs, slopes, subln_w)


def _mixer_tail_kernel(yf_ref, yb_ref, r_ref, kf_ref, kb_ref, v_ref, g_ref, pvec_ref, mf_ref, mb_ref, xs_ref, z_ref,
                       dskip_ref, mnw_ref, yd_ref, pg_ref, x_ref, mod_ref, nw_ref, wr_ref, wm_ref, wd_ref, wo_ref,
                       o_ref):
    d = D_MODEL
    y_r = _rwkv_post(yf_ref, yb_ref, r_ref, kf_ref, kb_ref, v_ref, g_ref, pvec_ref[...])
    y_m = _mamba_post(mf_ref, mb_ref, xs_ref, z_ref, dskip_ref[...], mnw_ref[...])
    gate = lambda g: _sigmoid(pg_ref[:, g * d:(g + 1) * d].astype(F32))
    merged = (gate(0) * _dot(y_r, wr_ref[...]) + gate(1) * _dot(y_m, wm_ref[...])
              + gate(2) * _dot(yd_ref[...], wd_ref[...]))
    y = _dot(merged.astype(BF16), wo_ref[...])
    o_ref[...] = x_ref[...] + mod_ref[0][2:3] * _rms(y, nw_ref[...], EPS)


def _mixer_tail_call(rw, pvec, mm, d_e, mnw, yd, pg, x2, mod_sub, nw, wr, wm, wd, wo, seq):
    t, d = x2.shape
    tm = 256
    row = lambda w: pl.BlockSpec((tm, w), lambda i: (i, 0))
    vec = lambda a: pl.BlockSpec(a.shape, lambda i: (0, 0))
    res = lambda a: pl.BlockSpec(a.shape, lambda i: (0, 0), pipeline_mode=pl.Buffered(1))
    return pl.pallas_call(
        _mixer_tail_kernel,
        grid=(t // tm,),
        in_specs=([row(RW_DIM)] * 7 + [vec(pvec)] + [row(M_INNER)] * 4 + [vec(d_e), vec(mnw)]
                  + [row(DF_HEADS * DF_V), row(GATE_COLS), row(d),
                     pl.BlockSpec((1, 3, d), lambda i: (i * tm // seq, 0, 0)), vec(nw),
                     res(wr), res(wm), res(wd), res(wo)]),
        out_specs=row(d),
        out_shape=jax.ShapeDtypeStruct((t, d), F32),
        compiler_params=_cparams("parallel"),
        name="mixer_tail",
    )(*rw, pvec, *mm, d_e, mnw, yd, pg, x2, mod_sub, nw, wr, wm, wd, wo)


def _pad_cols(w, n):
    return jnp.pad(w, ((0, 0), (0, n - w.shape[1])))


def _pad_rows(w, n):
    return jnp.pad(w, ((0, n - w.shape[0]), (0, 0)))


def _dir_padded(w):
    z = jnp.zeros_like(w[0])
    return jnp.stack([jnp.concatenate([w[0], z], axis=0), jnp.concatenate([z, w[1]], axis=0)])


def _head_expand_matrix():
    e = np.zeros((2, LANES, M_INNER), np.float32)
    for d in range(2):
        for h in range(M_HEADS):
            e[d, d * M_HEADS + h, h * M_HEADDIM:(h + 1) * M_HEADDIM] = 1.0
    return jnp.asarray(e, dtype=BF16)


def kernel(x, c, ada_w, ada_b, norm_w, ffn_w13, ffn_w2, w_in, rwkv_mu, rwkv_w0, rwkv_w2, rwkv_a0, rwkv_a2, rwkv_g2, rwkv_k_k, rwkv_k_a, rwkv_r_k, rwkv_ln_w, rwkv_ln_b, rwkv_v0, rwkv_v1, rwkv_v2, mamba_conv_w, mamba_conv_b, mamba_dt_bias, mamba_a_log, mamba_d, mamba_norm_w, diff_lambda, diff_subln_w, w_branch_rwkv, w_branch_mamba, w_branch_diff, w_out):
    bsz, seq, d = x.shape
    depth = ada_w.shape[0]
    t = bsz * seq
    x2 = x.reshape(t, d)
    mod_all = _mod_call(c, ada_w, ada_b)
    emat = _head_expand_matrix()
    slopes = jnp.broadcast_to(
        jnp.asarray(2.0 ** (-8.0 * np.arange(1, DF_HEADS + 1) / DF_HEADS), F32)[:, None, None], (DF_HEADS, 1, LANES))
    v_first = None
    for l in range(depth):
        mod = mod_all[l].reshape(bsz, N_SUB, 3, d)
        x2 = _ffn_call(x2, mod[:, 0], norm_w[l, 0:2], ffn_w13[l, 0].astype(BF16), ffn_w2[l, 0].astype(BF16), seq)

        wl = w_in[l]
        o_m = RW_COLS
        o_d = o_m + M_COLS
        o_g = o_d + DF_COLS
        w_cat = jnp.concatenate([
            wl[:, 0:3 * RW_DIM],
            _pad_cols(wl[:, 3 * RW_DIM:RW_COLS], RW_CODES_PAD),
            _pad_cols(wl[:, o_m + M_INNER + M_XBC:o_d], INPROJ_TN - RW_CODES_PAD),
            wl[:, o_m:o_m + M_INNER],
            wl[:, o_m + M_INNER:o_m + M_INNER + M_XBC],
            wl[:, o_d:o_g],
            wl[:, o_g:],
        ], axis=1).astype(BF16)
        p_rkv, p_aux, p_z, p_xbc, p_qkv, p_gate = _inproj_call(
            x2, mod[:, 1], norm_w[l, 2:3], w_cat,
            (3 * RW_DIM, INPROJ_TN, M_INNER, M_XBC, DF_COLS, GATE_COLS),
            (BF16, F32, BF16, BF16, BF16, BF16), seq)
        p_codes = p_dt = p_aux

        mu = rwkv_mu[l]
        mu_p = mu[None, 0:3 * RW_DIM]
        mu_c = _pad_cols(mu[None, 3 * RW_DIM:], RW_CODES_PAD)
        v0 = rwkv_v0[l - 1] if l > 0 else jnp.zeros((RW_DIM,), F32)
        pvec = jnp.stack([rwkv_w0[l, 0], rwkv_w0[l, 1], rwkv_a0[l, 0], rwkv_a0[l, 1],
                          rwkv_k_k[l], rwkv_k_a[l], v0, jnp.zeros((RW_DIM,), F32)])
        g2p = _pad_rows(rwkv_g2[l], 256).astype(BF16)
        vres = None
        if l > 0:
            vres = (v_first, _pad_cols(rwkv_v1[l - 1], LANES).astype(BF16),
                    _pad_rows(rwkv_v2[l - 1], LANES).astype(BF16))
        r, v, kk, k_f, k_b, b_f, b_b, lw_f, lw_b, gate = _rwkv_prep_call(
            p_rkv, p_codes, mu_p, mu_c, pvec, _dir_padded(rwkv_w2[l]).astype(BF16), _dir_padded(rwkv_a2[l]).astype(BF16), g2p, vres, seq)
        if l == 0:
            v_first = v
        y_f, y_b = _rwkv_scan_call(r, v, kk, k_f, b_f, lw_f, k_b, b_b, lw_b, bsz, seq)
        pvec2 = jnp.concatenate([jnp.stack([rwkv_ln_w[l], rwkv_ln_b[l], rwkv_r_k[l].reshape(RW_DIM)]),
                                 jnp.zeros((5, RW_DIM), F32)])

        conv_w = _pad_rows(mamba_conv_w[l], 8)
        xs, bm, cm = _mamba_conv_call(p_xbc, conv_w, mamba_conv_b[l][None], seq)
        dtb = _pad_cols(mamba_dt_bias[l].reshape(1, 2 * M_HEADS), LANES)
        alog_c = _pad_cols(mamba_a_log[l].reshape(1, 2 * M_HEADS), LANES)
        ym_f, ym_b = _ssd_call(xs, bm, cm, p_dt, dtb, emat, alog_c, bsz, seq)
        d_e = jnp.repeat(mamba_d[l], M_HEADDIM)[None]

        lambda_init = 0.8 - 0.6 * math.exp(-0.3 * l)
        y_d = _diff_attn_call(p_qkv, diff_lambda[l], slopes, diff_subln_w[l][None], bsz, seq, lambda_init)

        x2 = _mixer_tail_call((y_f, y_b, r, k_f, k_b, v, gate), pvec2, (ym_f, ym_b, xs, p_z), d_e,
                              mamba_norm_w[l][None], y_d, p_gate, x2, mod[:, 1], norm_w[l, 3:4],
                              w_branch_rwkv[l].astype(BF16), w_branch_mamba[l].astype(BF16),
                              w_branch_diff[l].astype(BF16), w_out[l].astype(BF16), seq)

        x2 = _ffn_call(x2, mod[:, 2], norm_w[l, 4:6], ffn_w13[l, 1].astype(BF16), ffn_w2[l, 1].astype(BF16), seq)
    return x2.reshape(bsz, seq, d)
```

```python
import functools
import itertools
import math

import numpy as np
import jax
import jax.numpy as jnp
from jax import lax
from jax.experimental import pallas as pl
from jax.experimental.pallas import tpu as pltpu

F32 = jnp.float32
BF16 = jnp.bfloat16
HI = lax.Precision.HIGHEST

D_MODEL = 1024
N_SUB = 3
EPS = 1e-6
LANES = 128
SUBLANES = 8
HALO = 2 * SUBLANES
VMEM_LIMIT = 56 * 1024 * 1024

RW_HEAD = 64
RW_DIM = 1024
W_LORA = 64
A_LORA = 64
G_LORA = 160
RW_GN_EPS = 64e-5
RW_COLS = 3 * RW_DIM + 2 * W_LORA + 2 * A_LORA + G_LORA
RW_CODES_PAD = 512
RW_CHUNK = 64
M_INNER = 2048
M_HEADS = 32
M_HEADDIM = 64
M_GROUPS = 4
M_STATE = 128
M_CONV = 5
M_CHUNK = 128
M_XBC = M_INNER + 2 * M_GROUPS * M_STATE
M_COLS = M_INNER + M_XBC + 2 * M_HEADS
M_GROUP_W = M_INNER // M_GROUPS
CONV_SUB = 128
DF_HEADS = 8
DF_HEAD = 64
DF_V = 128
DF_COLS = 3 * DF_HEADS * 2 * DF_HEAD
GATE_COLS = 3 * D_MODEL
NEG_BIG = -1e30


def _cparams(*sem):
    return pltpu.CompilerParams(dimension_semantics=sem, vmem_limit_bytes=VMEM_LIMIT)


def _dot(a, b, **kw):
    return jnp.dot(a, b, preferred_element_type=F32, **kw)


def _dot_nt(a, b, **kw):
    return lax.dot_general(a, b, (((1,), (1,)), ((), ())), preferred_element_type=F32, **kw)


def _split2(x):
    hi = x.astype(BF16)
    return hi, (x - hi.astype(F32)).astype(BF16)


def _dot_sel(x, m_b):
    hi, lo = _split2(x)
    return _dot(hi, m_b) + _dot(lo, m_b)


def _sel_dot(m_b, x):
    hi, lo = _split2(x)
    return _dot(m_b, hi) + _dot(m_b, lo)


def _bdot(a, b):
    return _dot(a.astype(BF16), b.astype(BF16))


def _sigmoid(x):
    return 1.0 / (1.0 + jnp.exp(-x))


def _silu(x):
    return x * _sigmoid(x)


def _softplus(x):
    return jnp.maximum(x, 0.0) + jnp.log(1.0 + jnp.exp(-jnp.abs(x)))


def _rms(x, w, eps):
    return x * lax.rsqrt(jnp.mean(x * x, axis=-1, keepdims=True) + eps) * w


def _norm_mod(x, nw, shift, scale):
    return _rms(x, nw, EPS) * (1.0 + scale) + shift


def _pair_ones(scale=1.0):
    r = lax.broadcasted_iota(jnp.int32, (LANES, LANES), 0) // RW_HEAD
    c = lax.broadcasted_iota(jnp.int32, (LANES, LANES), 1) // RW_HEAD
    return jnp.where(r == c, scale, 0.0).astype(BF16)


def _shift_rows(cur, prevz, nextz, d):
    tm = cur.shape[0]
    rolled = pltpu.roll(cur, (-d) % tm, axis=0)
    rid = lax.broadcasted_iota(jnp.int32, (SUBLANES, cur.shape[1]), 0)
    if d < 0:
        fix = pltpu.roll(prevz, (-d) % SUBLANES, axis=0)
        top = jnp.where(rid < -d, fix, rolled[0:SUBLANES])
        return jnp.concatenate([top, rolled[SUBLANES:]], axis=0)
    fix = pltpu.roll(nextz, (SUBLANES - d) % SUBLANES, axis=0)
    bot = jnp.where(rid >= SUBLANES - d, fix, rolled[tm - SUBLANES:])
    return jnp.concatenate([rolled[:tm - SUBLANES], bot], axis=0)


def _shift_rows_mxu(cur_b, prevz, nextz, d):
    assert cur_b.dtype == BF16
    tm = cur_b.shape[0]
    ri = lax.broadcasted_iota(jnp.int32, (tm, tm), 0)
    ci = lax.broadcasted_iota(jnp.int32, (tm, tm), 1)
    shifted = _dot(jnp.where(ci == ri + d, 1.0, 0.0).astype(BF16), cur_b)
    rid = lax.broadcasted_iota(jnp.int32, (SUBLANES, cur_b.shape[1]), 0)
    if d < 0:
        fix = jnp.where(rid < -d, pltpu.roll(prevz, (-d) % SUBLANES, axis=0), 0.0)
        return jnp.concatenate([shifted[0:SUBLANES] + fix, shifted[SUBLANES:]], axis=0)
    fix = jnp.where(rid >= SUBLANES - d, pltpu.roll(nextz, (SUBLANES - d) % SUBLANES, axis=0), 0.0)
    return jnp.concatenate([shifted[:tm - SUBLANES], shifted[tm - SUBLANES:] + fix], axis=0)


def _nbr_mean_minus_self_mxu(cur_b, prevz, nextz):
    assert cur_b.dtype == BF16
    tm = cur_b.shape[0]
    ri = lax.broadcasted_iota(jnp.int32, (tm, tm), 0)
    ci = lax.broadcasted_iota(jnp.int32, (tm, tm), 1)
    band = jnp.where(ci == ri, -1.0, jnp.where(jnp.abs(ci - ri) == 1, 0.5, 0.0)).astype(BF16)
    out = _dot(band, cur_b)
    rid = lax.broadcasted_iota(jnp.int32, (SUBLANES, cur_b.shape[1]), 0)
    top = jnp.where(rid == 0, 0.5 * prevz[SUBLANES - 1:SUBLANES], 0.0)
    bot = jnp.where(rid == SUBLANES - 1, 0.5 * nextz[0:1], 0.0)
    return jnp.concatenate([out[0:SUBLANES] + top, out[SUBLANES:tm - SUBLANES], out[tm - SUBLANES:] + bot], axis=0)


def _halo_specs(tm, width, n_rows):
    rb = tm // HALO
    last = n_rows // HALO - 1
    return [
        pl.BlockSpec((tm, width), lambda i: (i, 0)),
        pl.BlockSpec((HALO, width), lambda i: (jnp.maximum(i * rb - 1, 0), 0)),
        pl.BlockSpec((HALO, width), lambda i: (jnp.minimum((i + 1) * rb, last), 0)),
    ]


def _halo_values(cur_ref, prev_ref, next_ref, tm, seq):
    i = pl.program_id(0)
    first = (i * tm) % seq == 0
    last = ((i + 1) * tm) % seq == 0
    prevz = jnp.where(first, 0.0, prev_ref[...].astype(F32)[HALO - SUBLANES:])
    nextz = jnp.where(last, 0.0, next_ref[...].astype(F32)[:SUBLANES])
    return cur_ref[...].astype(F32), prevz, nextz


def _mod_kernel(c_ref, w_ref, b_ref, o_ref):
    o_ref[0] = _dot(_silu(c_ref[...]), w_ref[0], precision=HI) + b_ref[0]


def _mod_call(c, ada_w, ada_b):
    depth, d, n = ada_w.shape
    bsz = c.shape[0]
    tn = 1152
    return pl.pallas_call(
        _mod_kernel,
        grid=(depth, n // tn),
        in_specs=[pl.BlockSpec((bsz, d), lambda l, j: (0, 0)),
                  pl.BlockSpec((1, d, tn), lambda l, j: (l, 0, j)),
                  pl.BlockSpec((1, 1, tn), lambda l, j: (l, 0, j))],
        out_specs=pl.BlockSpec((1, bsz, tn), lambda l, j: (l, 0, j)),
        out_shape=jax.ShapeDtypeStruct((depth, bsz, n), F32),
        compiler_params=_cparams("parallel", "parallel"),
        name="adaln_mod",
    )(c, ada_w, ada_b.reshape(depth, 1, n))


def _ffn_kernel(x_ref, mod_ref, nw_ref, w13_ref, w2_ref, o_ref, *, tf):
    dff = w2_ref.shape[0]
    m = mod_ref[0]
    x = x_ref[...]
    h = _norm_mod(x, nw_ref[0:1], m[0:1], m[1:2]).astype(BF16)
    acc = None
    for j in range(dff // tf):
        g = _dot(h, w13_ref[:, j * tf:(j + 1) * tf])
        u = _dot(h, w13_ref[:, dff + j * tf:dff + (j + 1) * tf])
        part = _dot((_silu(g) * u).astype(BF16), w2_ref[j * tf:(j + 1) * tf, :])
        acc = part if acc is None else acc + part
    o_ref[...] = x + 0.5 * m[2:3] * _rms(acc, nw_ref[1:2], EPS)


def _ffn_call(x2, mod_sub, nw2, w13, w2, seq):
    t, d = x2.shape
    tm = min(1024, seq)
    resident = lambda a: pl.BlockSpec(a.shape, lambda i: (0, 0), pipeline_mode=pl.Buffered(1))
    return pl.pallas_call(
        functools.partial(_ffn_kernel, tf=256),
        grid=(t // tm,),
        in_specs=[pl.BlockSpec((tm, d), lambda i: (i, 0)),
                  pl.BlockSpec((1, 3, d), lambda i: (i * tm // seq, 0, 0)),
                  pl.BlockSpec((2, d), lambda i: (0, 0)),
                  resident(w13), resident(w2)],
        out_specs=pl.BlockSpec((tm, d), lambda i: (i, 0)),
        out_shape=jax.ShapeDtypeStruct((t, d), F32),
        compiler_params=_cparams("parallel"),
        name="swiglu_halfstep",
    )(x2, mod_sub, nw2, w13, w2)


INPROJ_TN = 1024


def _inproj_kernel(x_ref, mod_ref, nw_ref, w_ref, *rest, starts):
    o_refs, h_scr = rest[:-1], rest[-1]
    j = pl.program_id(1)

    @pl.when(j == 0)
    def _():
        m = mod_ref[0]
        h_scr[...] = _norm_mod(x_ref[...], nw_ref[...], m[0:1], m[1:2]).astype(BF16)

    for k, o_ref in enumerate(o_refs):
        @pl.when((j >= starts[k]) & (j < starts[k + 1]))
        def _(o_ref=o_ref):
            o_ref[...] = _dot(h_scr[...], w_ref[...]).astype(o_ref.dtype)


def _inproj_call(x2, mod_sub, nw, w, widths, dtypes, seq):
    t, d = x2.shape
    tn = INPROJ_TN
    tm = min(1024, seq)
    starts = [0]
    for wd in widths:
        starts.append(starts[-1] + wd // tn)

    def out_spec(k):
        return pl.BlockSpec((tm, tn), lambda i, j: (i, jnp.clip(j - starts[k], 0, widths[k] // tn - 1)))

    return pl.pallas_call(
        functools.partial(_inproj_kernel, starts=tuple(starts)),
        grid=(t // tm, starts[-1]),
        in_specs=[pl.BlockSpec((tm, d), lambda i, j: (i, 0)),
                  pl.BlockSpec((1, 3, d), lambda i, j: (i * tm // seq, 0, 0)),
                  pl.BlockSpec((1, d), lambda i, j: (0, 0)),
                  pl.BlockSpec((d, tn), lambda i, j: (0, j))],
        out_specs=[out_spec(k) for k in range(len(widths))],
        out_shape=[jax.ShapeDtypeStruct((t, wd), dt) for wd, dt in zip(widths, dtypes)],
        scratch_shapes=[pltpu.VMEM((tm, d), BF16)],
        compiler_params=_cparams("parallel", "arbitrary"),
        name="norm_inproj",
    )(x2, mod_sub, nw, w)


def _rwkv_prep_kernel(*refs, tm, seq, has_vres):
    (p_ref, pp_ref, pn_ref, c_ref, cp_ref, cn_ref, mup_ref, muc_ref, pvec_ref,
     w2_ref, a2_ref, g2_ref) = refs[:12]
    refs = refs[12:]
    if has_vres:
        vf_ref, v1_ref, v2_ref = refs[:3]
        refs = refs[3:]
    r_o, v_o, kk_o, kf_o, kb_o, bf_o, bb_o, lwf_o, lwb_o, g_o = refs

    def shift_mix(cur_ref, prev_ref, next_ref, mu):
        cur, prevz, nextz = _halo_values(cur_ref, prev_ref, next_ref, tm, seq)
        if cur_ref.dtype == BF16:
            return cur + mu * _nbr_mean_minus_self_mxu(cur_ref[...], prevz, nextz)
        nb = 0.5 * (_shift_rows(cur, prevz, nextz, -1) + _shift_rows(cur, prevz, nextz, 1))
        return cur + mu * (nb - cur)

    p = shift_mix(p_ref, pp_ref, pn_ref, mup_ref[...])
    codes = shift_mix(c_ref, cp_ref, cn_ref, muc_ref[...])
    r = p[:, 0:RW_DIM]
    k = p[:, RW_DIM:2 * RW_DIM]
    v = p[:, 2 * RW_DIM:3 * RW_DIM]
    cw = jnp.tanh(codes[:, 0:2 * W_LORA])
    ca = codes[:, 2 * W_LORA:2 * W_LORA + 2 * A_LORA]
    cg = _sigmoid(codes[:, 2 * W_LORA + 2 * A_LORA:])
    pv = pvec_ref[...]
    if has_vres:
        lo = _bdot(_bdot(v, v1_ref[...]), v2_ref[...])
        v = v + (vf_ref[...].astype(F32) - v) * _sigmoid(pv[6:7] + lo)
    r_o[...] = r.astype(r_o.dtype)
    v_o[...] = v.astype(v_o.dtype)
    g_o[...] = _bdot(cg, g2_ref[...]).astype(g_o.dtype)
    iclr = []
    for d, lw_o in ((0, lwf_o), (1, lwb_o)):
        lw_o[...] = -math.exp(-0.5) * _sigmoid(pv[d:d + 1] + _bdot(cw, w2_ref[d]))
        iclr.append(_sigmoid(pv[2 + d:3 + d] + _bdot(ca, a2_ref[d])))
    kf_o[...] = (k * (1.0 + (iclr[0] - 1.0) * pv[5:6])).astype(kf_o.dtype)
    kb_o[...] = (k * (1.0 + (iclr[1] - 1.0) * pv[5:6])).astype(kb_o.dtype)
    ones2 = _pair_ones()
    kkr = k * pv[4:5]
    for cb in range(RW_DIM // LANES):
        sl = slice(cb * LANES, (cb + 1) * LANES)
        blk = kkr[:, sl]
        kkn = blk * lax.rsqrt(_dot_sel(blk * blk, ones2) + 1e-12)
        kk_o[:, sl] = kkn.astype(kk_o.dtype)
        bf_o[:, sl] = (kkn * iclr[0][:, sl]).astype(bf_o.dtype)
        bb_o[:, sl] = (kkn * iclr[1][:, sl]).astype(bb_o.dtype)


def _rwkv_prep_call(rkv, codes, mu_p, mu_c, pvec, w2p, a2p, g2p, vres, seq):
    t = rkv.shape[0]
    tm = 256
    has_vres = vres is not None
    full = lambda shape: pl.BlockSpec(shape, lambda i: (0,) * len(shape))
    in_specs = (_halo_specs(tm, 3 * RW_DIM, t) + _halo_specs(tm, RW_CODES_PAD, t)
                + [full((1, 3 * RW_DIM)), full((1, RW_CODES_PAD)), full((8, RW_DIM)),
                   full((2, 2 * W_LORA, RW_DIM)), full((2, 2 * A_LORA, RW_DIM)), full((256, RW_DIM))])
    args = [rkv, rkv, rkv, codes, codes, codes, mu_p, mu_c, pvec, w2p, a2p, g2p]
    if has_vres:
        in_specs += [pl.BlockSpec((tm, RW_DIM), lambda i: (i, 0)), full((RW_DIM, LANES)), full((LANES, RW_DIM))]
        args += list(vres)
    dtypes = [BF16] * 7 + [F32, F32, BF16]
    return pl.pallas_call(
        functools.partial(_rwkv_prep_kernel, tm=tm, seq=seq, has_vres=has_vres),
        grid=(t // tm,),
        in_specs=in_specs,
        out_specs=[pl.BlockSpec((tm, RW_DIM), lambda i: (i, 0))] * 10,
        out_shape=[jax.ShapeDtypeStruct((t, RW_DIM), dt) for dt in dtypes],
        compiler_params=_cparams("parallel"),
        name="rwkv_prep",
    )(*args)


def _rwkv_scan_kernel(rf_ref, vf_ref, kkf_ref, kf_ref, bf_ref, lwf_ref, rb_ref, vb_ref, kkb_ref, kb_ref, bb_ref,
                      lwb_ref, yf_ref, yb_ref, h_scr, *, n_chunks):
    c = RW_CHUNK
    n_pairs = RW_DIM // LANES

    @pl.when(pl.program_id(1) == 0)
    def _():
        h_scr[...] = jnp.zeros_like(h_scr)

    ri = lax.broadcasted_iota(jnp.int32, (LANES, LANES), 0)
    ci = lax.broadcasted_iota(jnp.int32, (LANES, LANES), 1)
    same = (ri // c) == (ci // c)
    tr = lax.broadcasted_iota(jnp.int32, (c, LANES), 0)
    tc = lax.broadcasted_iota(jnp.int32, (c, LANES), 1) % c
    strict = [tc < tr, tc > tr]
    incl = [tc <= tr, tc >= tr]
    blk = [(tr // w) == (tc // w) for w in (8, 16, 32, 64)]
    eye = jnp.where(tr == tc, 1.0, 0.0).astype(F32)
    r64 = lax.broadcasted_iota(jnp.int32, (c, c), 0)
    c64 = lax.broadcasted_iota(jnp.int32, (c, c), 1)
    tri = [jnp.where(c64 <= r64, 1.0, 0.0).astype(BF16), jnp.where(c64 >= r64, 1.0, 0.0).astype(BF16)]
    head0 = lax.broadcasted_iota(jnp.int32, (c, LANES), 1) < RW_HEAD
    end_row = [c - 1, 0]
    refs = [(rf_ref, vf_ref, kkf_ref, kf_ref, bf_ref, lwf_ref, yf_ref),
            (rb_ref, vb_ref, kkb_ref, kb_ref, bb_ref, lwb_ref, yb_ref)]
    lns = [slice(pr * LANES, (pr + 1) * LANES) for pr in range(n_pairs)]
    chains = [(d, p) for p in range(n_pairs) for d in range(2)]

    def stack(x):
        return jnp.concatenate([jnp.where(head0, x, 0.0), jnp.where(head0, 0.0, x)], axis=0).astype(BF16)

    def per_head(x, y):
        return _dot(x.astype(BF16), stack(y))

    def chunk(step, carry):
        rows = [pl.ds(pl.multiple_of(step * c, c), c), pl.ds(pl.multiple_of((n_chunks - 1 - step) * c, c), c)]
        wide = []
        for d in range(2):
            r_ref, v_ref, kk_ref, k_ref, b_ref, lw_ref, _ = refs[d]
            lw = lw_ref[rows[d], :]
            ld = lambda ref: ref[rows[d], :].astype(F32)
            cum = _sel_dot(tri[d], lw)
            tot = cum[end_row[d]:end_row[d] + 1]
            g_inv, g_end = jnp.exp(-cum), jnp.exp(tot - cum)
            k_all, b_all = ld(k_ref), ld(b_ref)
            wide.append(dict(at=-ld(kk_ref) * jnp.exp(cum - lw), rt=ld(r_ref) * jnp.exp(cum),
                             bt=b_all * g_inv, kt=k_all * g_inv, bg=b_all * g_end, kg=k_all * g_end,
                             v=ld(v_ref), g_tot=jnp.exp(tot)))
        idx = range(len(chains))
        sel = lambda name, i: wide[chains[i][0]][name][:, lns[chains[i][1]]]
        ar = [jnp.concatenate([sel("at", i), sel("rt", i)], axis=0).astype(BF16) for i in idx]
        bk = [jnp.concatenate([stack(sel("bt", i)), stack(sel("kt", i))], axis=0) for i in idx]
        v_s = [stack(sel("v", i)) for i in idx]
        ht = [h_scr[d, p] for d, p in chains]
        gram = [_dot_nt(ar[i], bk[i]) for i in idx]
        a_ab = [jnp.where(strict[d], gram[i][0:c, 0:LANES], 0.0) for i, (d, p) in enumerate(chains)]
        a_ak = [jnp.where(strict[d], gram[i][0:c, LANES:], 0.0).astype(BF16) for i, (d, p) in enumerate(chains)]
        a_rb = [jnp.where(incl[d], gram[i][c:, 0:LANES], 0.0) for i, (d, p) in enumerate(chains)]
        a_rk = [jnp.where(incl[d], gram[i][c:, LANES:], 0.0).astype(BF16) for i, (d, p) in enumerate(chains)]
        lhs = [jnp.concatenate([ar[i], jnp.concatenate([a_ak[i], a_rk[i]], axis=0)], axis=1) for i in idx]
        rhs = [jnp.concatenate([ht[i].T.astype(BF16), v_s[i]], axis=0) for i in idx]
        xy0 = [_dot(lhs[i], rhs[i]) for i in idx]
        p1 = [jnp.where(blk[0], a_ab[i], 0.0) for i in idx]
        p2 = [per_head(p1[i], p1[i]) for i in idx]
        p4 = [per_head(p2[i], p2[i]) for i in idx]
        t_inv = [eye + p1[i] + p2[i] + per_head(p1[i], p2[i]) for i in idx]
        t_inv = [t_inv[i] + per_head(t_inv[i], p4[i]) for i in idx]
        for lvl in range(1, 4):
            off = [jnp.where(blk[lvl] & ~blk[lvl - 1], a_ab[i], 0.0) for i in idx]
            mid = [per_head(t_inv[i], off[i]) for i in idx]
            t_inv = [t_inv[i] + per_head(mid[i], t_inv[i]) for i in idx]
        u = [per_head(t_inv[i], xy0[i][0:c]) for i in idx]
        y = [xy0[i][c:] + per_head(a_rb[i], u[i]) for i in idx]
        for d in range(2):
            ys = [y[i] for i, (dd, p) in enumerate(chains) if dd == d]
            refs[d][6][rows[d], :] = jnp.concatenate(ys, axis=1).astype(refs[d][6].dtype)
        for i, (d, p) in enumerate(chains):
            uv_t = jnp.concatenate([u[i], sel("v", i)], axis=0).T.astype(BF16)
            bkg = jnp.concatenate([sel("bg", i), sel("kg", i)], axis=0).astype(BF16)
            upd = jnp.where(same, _dot(uv_t, bkg), 0.0)
            h_scr[d, p] = ht[i] * wide[d]["g_tot"][:, lns[p]] + upd
        return carry

    lax.fori_loop(0, n_chunks, chunk, 0)


def _rwkv_scan_call(r, v, kk, k_f, b_f, lw_f, k_b, b_b, lw_b, bsz, seq):
    t = r.shape[0]
    tb = min(4 * RW_CHUNK, seq)
    nt = seq // tb
    fwd = pl.BlockSpec((tb, RW_DIM), lambda bi, ti: (bi * nt + ti, 0))
    bwd = pl.BlockSpec((tb, RW_DIM), lambda bi, ti: (bi * nt + nt - 1 - ti, 0))
    out = jax.ShapeDtypeStruct((t, RW_DIM), BF16)
    return pl.pallas_call(
        functools.partial(_rwkv_scan_kernel, n_chunks=tb // RW_CHUNK),
        grid=(bsz, nt),
        in_specs=[fwd] * 6 + [bwd] * 6,
        out_specs=[fwd, bwd],
        out_shape=[out, out],
        scratch_shapes=[pltpu.VMEM((2, RW_DIM // LANES, LANES, LANES), F32)],
        compiler_params=_cparams("parallel", "arbitrary"),
        name="rwkv_scan",
    )(r, v, kk, k_f, b_f, lw_f, r, v, kk, k_b, b_b, lw_b)


def _rwkv_post(yf_ref, yb_ref, r_ref, kf_ref, kb_ref, v_ref, g_ref, pv):
    mean2 = _pair_ones(1.0 / RW_HEAD)
    ones2 = _pair_ones()
    outs = []
    for cb in range(RW_DIM // LANES):
        sl = slice(cb * LANES, (cb + 1) * LANES)
        ld = lambda ref: ref[:, sl].astype(F32)
        y = ld(yf_ref) + ld(yb_ref)
        dlt = y - _dot_sel(y, mean2)
        var = _dot_sel(dlt * dlt, mean2)
        yn = dlt * lax.rsqrt(var + RW_GN_EPS) * pv[0:1, sl] + pv[1:2, sl]
        rk = ld(r_ref) * (ld(kf_ref) + ld(kb_ref)) * pv[2:3, sl]
        bonus = _dot_sel(rk, ones2) * ld(v_ref)
        outs.append(((yn + bonus) * ld(g_ref)).astype(BF16))
    return jnp.concatenate(outs, axis=1)


def _mamba_conv_kernel(x_ref, xp_ref, xn_ref, w_ref, b_ref, xs_o, bm_o, cm_o, *, tm, seq):
    _, prevz, nextz = _halo_values(x_ref, xp_ref, xn_ref, tm, seq)
    w = w_ref[...]
    half = (M_CONV - 1) // 2
    sb = CONV_SUB
    n_sub = tm // sb
    for s in range(n_sub):
        rows = slice(s * sb, (s + 1) * sb)
        cur_b = x_ref[rows, :]
        pz = prevz if s == 0 else x_ref[s * sb - HALO:s * sb, :].astype(F32)[HALO - SUBLANES:]
        nz = nextz if s == n_sub - 1 else x_ref[(s + 1) * sb:(s + 1) * sb + HALO, :].astype(F32)[:SUBLANES]
        acc = cur_b.astype(F32) * w[half:half + 1] + b_ref[...]
        for tap in range(M_CONV):
            if tap != half:
                acc = acc + _shift_rows_mxu(cur_b, pz, nz, tap - half) * w[tap:tap + 1]
        y = _silu(acc)
        xs_o[rows, :] = y[:, 0:M_INNER].astype(xs_o.dtype)
        bm_o[rows, :] = y[:, M_INNER:M_INNER + M_GROUPS * M_STATE].astype(bm_o.dtype)
        cm_o[rows, :] = y[:, M_INNER + M_GROUPS * M_STATE:].astype(cm_o.dtype)


def _mamba_conv_call(xbc, conv_w, conv_b, seq):
    t = xbc.shape[0]
    tm = min(512, seq)
    gs = M_GROUPS * M_STATE
    return pl.pallas_call(
        functools.partial(_mamba_conv_kernel, tm=tm, seq=seq),
        grid=(t // tm,),
        in_specs=_halo_specs(tm, M_XBC, t) + [pl.BlockSpec((8, M_XBC), lambda i: (0, 0)),
                                             pl.BlockSpec((1, M_XBC), lambda i: (0, 0))],
        out_specs=[pl.BlockSpec((tm, M_INNER), lambda i: (i, 0)),
                   pl.BlockSpec((tm, gs), lambda i: (i, 0)),
                   pl.BlockSpec((tm, gs), lambda i: (i, 0))],
        out_shape=[jax.ShapeDtypeStruct((t, M_INNER), BF16),
                   jax.ShapeDtypeStruct((t, gs), BF16),
                   jax.ShapeDtypeStruct((t, gs), BF16)],
        compiler_params=_cparams("parallel"),
        name="mamba_conv",
    )(xbc, xbc, xbc, conv_w, conv_b)


def _ssd_chunk(d, xs_ref, bm_ref, cm_ref, dt_ref, dtb_ref, e_ref, alogc_ref, y_ref, h_scr):
    c = M_CHUNK
    n_pairs = M_INNER // LANES
    ppg = n_pairs // M_GROUPS
    ri = lax.broadcasted_iota(jnp.int32, (c, c), 0)
    ci = lax.broadcasted_iota(jnp.int32, (c, c), 1)
    keep = (ci >= ri) if d else (ci <= ri)
    tri = jnp.where(keep, 1.0, 0.0).astype(BF16)
    end_row = 0 if d else c - 1
    head0 = lax.broadcasted_iota(jnp.int32, (c, LANES), 1) < M_HEADDIM

    dt = _softplus(dt_ref[...] + dtb_ref[...])
    acs_c = _sel_dot(tri, dt * (-jnp.exp(alogc_ref[...])))
    acs_ct = acs_c.T
    dte = _dot_sel(dt, e_ref[d])
    acs = _dot_sel(acs_c, e_ref[d])
    tot = acs[end_row:end_row + 1]
    yield
    xdt = xs_ref[...].astype(F32) * dte
    e_acs = jnp.exp(acs)
    x_end = (xdt * jnp.exp(tot - acs)).astype(BF16)
    xdt_b = xdt.astype(BF16)
    dec = jnp.exp(tot)
    gsl = [slice(g * M_STATE, (g + 1) * M_STATE) for g in range(M_GROUPS)]
    bm_b = [bm_ref[:, sl] for sl in gsl]
    cm_b = [cm_ref[:, sl] for sl in gsl]
    bm_t = [bm_ref[:, sl].astype(F32).T.astype(BF16) for sl in gsl]
    cb = [_dot_nt(cm_b[g], bm_b[g]) for g in range(M_GROUPS)]
    yield
    prs = range(n_pairs)
    lns = [slice(p * LANES, (p + 1) * LANES) for p in prs]
    ys = []
    for p in prs:
        pair = []
        for hh in range(2):
            idx = d * M_HEADS + 2 * p + hh
            seg = acs_c[:, idx:idx + 1] - acs_ct[idx:idx + 1, :]
            lmat = jnp.exp(jnp.where(keep, seg, NEG_BIG))
            pair.append((cb[p // ppg] * lmat).astype(BF16))
        both = _dot(jnp.concatenate(pair, axis=0), xdt_b[:, lns[p]])
        ys.append(jnp.where(head0, both[0:c], both[c:]))
        yield
    h_prev = [h_scr[d, p] for p in prs]
    y_off = [_dot(cm_b[g], jnp.concatenate([h_prev[p].astype(BF16) for p in range(g * ppg, (g + 1) * ppg)], axis=1))
             for g in range(M_GROUPS)]
    y_ref[...] = (jnp.concatenate(ys, axis=1) + jnp.concatenate(y_off, axis=1) * e_acs).astype(y_ref.dtype)
    yield
    for g in range(M_GROUPS):
        upd = _dot(bm_t[g], x_end[:, g * M_GROUP_W:(g + 1) * M_GROUP_W])
        for q in range(ppg):
            p = g * ppg + q
            h_scr[d, p] = h_prev[p] * dec[:, lns[p]] + upd[:, q * LANES:(q + 1) * LANES]


def _ssd_kernel(xsf_ref, bmf_ref, cmf_ref, dtf_ref, xsb_ref, bmb_ref, cmb_ref, dtb_ref, bias_ref, e_ref, alogc_ref,
                yf_ref, yb_ref, h_scr):
    @pl.when(pl.program_id(1) == 0)
    def _():
        h_scr[...] = jnp.zeros_like(h_scr)

    shared = (bias_ref, e_ref, alogc_ref)
    both = (_ssd_chunk(0, xsf_ref, bmf_ref, cmf_ref, dtf_ref, *shared, yf_ref, h_scr),
            _ssd_chunk(1, xsb_ref, bmb_ref, cmb_ref, dtb_ref, *shared, yb_ref, h_scr))
    for _ in itertools.zip_longest(*both):
        pass


def _ssd_call(xs, bm, cm, dt_raw, dtb, emat, alog_c, bsz, seq):
    t = xs.shape[0]
    nc = seq // M_CHUNK
    gs = M_GROUPS * M_STATE
    dt_col = RW_CODES_PAD // LANES
    fwd = lambda bi, ci: bi * nc + ci
    bwd = lambda bi, ci: bi * nc + nc - 1 - ci
    data = lambda row: [pl.BlockSpec((M_CHUNK, M_INNER), lambda bi, ci: (row(bi, ci), 0)),
                        pl.BlockSpec((M_CHUNK, gs), lambda bi, ci: (row(bi, ci), 0)),
                        pl.BlockSpec((M_CHUNK, gs), lambda bi, ci: (row(bi, ci), 0)),
                        pl.BlockSpec((M_CHUNK, LANES), lambda bi, ci: (row(bi, ci), dt_col))]
    full = lambda a: pl.BlockSpec(a.shape, lambda bi, ci: (0,) * a.ndim)
    out = jax.ShapeDtypeStruct((t, M_INNER), BF16)
    return pl.pallas_call(
        _ssd_kernel,
        grid=(bsz, nc),
        in_specs=data(fwd) + data(bwd) + [full(dtb), full(emat), full(alog_c)],
        out_specs=[pl.BlockSpec((M_CHUNK, M_INNER), lambda bi, ci: (fwd(bi, ci), 0)),
                   pl.BlockSpec((M_CHUNK, M_INNER), lambda bi, ci: (bwd(bi, ci), 0))],
        out_shape=[out, out],
        scratch_shapes=[pltpu.VMEM((2, M_INNER // LANES, M_STATE, LANES), F32)],
        compiler_params=_cparams("parallel", "arbitrary"),
        name="ssd_scan",
    )(xs, bm, cm, dt_raw, xs, bm, cm, dt_raw, dtb, emat, alog_c)


def _mamba_post(yf_ref, yb_ref, xs_ref, z_ref, d_skip, nw):
    ld = lambda ref: ref[...].astype(F32)
    y = (ld(yf_ref) + ld(yb_ref) + d_skip * ld(xs_ref)) * _silu(ld(z_ref))
    groups = [slice(g * M_GROUP_W, (g + 1) * M_GROUP_W) for g in range(M_GROUPS)]
    return jnp.concatenate([_rms(y[:, sl], nw[:, sl], 1e-5).astype(BF16) for sl in groups], axis=1)


def _diff_attn_kernel(zero_ref, q_ref, k_ref, v_ref, lam_ref, slope_ref, nw_ref, o_ref, bias_scr, s_scr, *, tq, tk,
                      lambda_init):
    seq = k_ref.shape[0]
    log2e = math.log2(math.e)

    @pl.when(pl.program_id(2) == 0)
    def _():
        rows = pl.program_id(1) * tq + lax.broadcasted_iota(jnp.int32, (tq, seq), 0)
        cols = lax.broadcasted_iota(jnp.int32, (tq, seq), 1)
        bias_scr[...] = (slope_ref[0][:, 0:1] * log2e) * jnp.abs(rows - cols).astype(F32)

    q = (q_ref[...].astype(F32) * (DF_HEAD ** -0.5 * log2e)).astype(BF16)
    lv = lam_ref[...]
    lam = (jnp.exp(jnp.sum(lv[0:1] * lv[1:2], keepdims=True))
           - jnp.exp(jnp.sum(lv[2:3] * lv[3:4], keepdims=True)) + lambda_init)
    map0 = lax.broadcasted_iota(jnp.int32, (tq, LANES), 1) < DF_HEAD
    qm = [jnp.where(map0, q, jnp.zeros_like(q)), jnp.where(map0, jnp.zeros_like(q), q)]
    tiles = [slice(t * tk, (t + 1) * tk) for t in range(seq // tk)]
    rt = min(4 * LANES, tq)
    streams = [(mp, slice(rh * rt, (rh + 1) * rt)) for rh in range(tq // rt) for mp in range(2)]
    v1 = jnp.concatenate([v_ref[...], jnp.ones((seq, LANES), BF16)], axis=1)
    zero = zero_ref[0]

    def scores_tile(st, ts, mx):
        mp, rows = st
        s = _dot_nt(qm[mp][rows], k_ref[ts, :]) - bias_scr[rows, ts]
        s_scr[mp, rows, ts] = s
        for cs in range(tk // LANES):
            blk = s[:, cs * LANES:(cs + 1) * LANES]
            mx = blk if mx is None else jnp.maximum(mx, blk)
        return mx

    def weights(st, m):
        mp, rows = st
        return _dot(jnp.exp2(s_scr[mp + zero, rows, :] - m).astype(BF16), v1)

    m, acc = {}, {}
    for i in range(len(streams) + 1):
        if i < len(streams):
            mx = None
            for ts in tiles:
                mx = scores_tile(streams[i], ts, mx)
            m[i] = jnp.max(mx, axis=-1, keepdims=True)
        if i >= 1:
            acc[i - 1] = weights(streams[i - 1], m[i - 1])
        if i >= 2 and i % 2 == 0:
            rh = i // 2 - 1
            om = [acc[2 * rh + mp][:, 0:LANES] / acc[2 * rh + mp][:, LANES:LANES + 1] for mp in range(2)]
            o = om[0] - lam * om[1]
            o_ref[rh * rt:(rh + 1) * rt, :] = (_rms(o, nw_ref[...], 1e-5) * (1.0 - lambda_init)).astype(o_ref.dtype)


def _diff_attn_call(qkv, lam_vecs, slopes, subln_w, bsz, seq, lambda_init):
    t = qkv.shape[0]
    tq = min(1024, seq)
    nq = seq // tq
    return pl.pallas_call(
        functools.partial(_diff_attn_kernel, tq=tq, tk=min(512, seq), lambda_init=lambda_init),
        grid=(DF_HEADS, nq, bsz),
        in_specs=[pl.BlockSpec(memory_space=pltpu.SMEM),
                  pl.BlockSpec((tq, DF_V), lambda hi, qi, bi: (bi * nq + qi, hi)),
                  pl.BlockSpec((seq, DF_V), lambda hi, qi, bi: (bi, DF_HEADS + hi)),
                  pl.BlockSpec((seq, DF_V), lambda hi, qi, bi: (bi, 2 * DF_HEADS + hi)),
                  pl.BlockSpec((4, DF_HEAD), lambda hi, qi, bi: (0, 0)),
                  pl.BlockSpec((1, 1, LANES), lambda hi, qi, bi: (hi, 0, 0)),
                  pl.BlockSpec((1, DF_V), lambda hi, qi, bi: (0, 0))],
        out_specs=pl.BlockSpec((tq, DF_V), lambda hi, qi, bi: (bi * nq + qi, hi)),
        out_shape=jax.ShapeDtypeStruct((t, DF_HEADS * DF_V), BF16),
        scratch_shapes=[pltpu.VMEM((tq, seq), F32), pltpu.VMEM((2, tq, seq), F32)],
        compiler_params=_cparams("parallel", "parallel", "arbitrary"),
        name="diff_attn",
    )(jnp.zeros((1,), jnp.int32), qkv, qkv, qkv, lam_vecs, slopes, subln_w)


def _mixer_tail_kernel(yf_ref, yb_ref, r_ref, kf_ref, kb_ref, v_ref, g_ref, pvec_ref, mf_ref, mb_ref, xs_ref, z_ref,
                       dskip_ref, mnw_ref, yd_ref, pg_ref, x_ref, mod_ref, nw_ref, wr_ref, wm_ref, wd_ref, wo_ref,
                       o_ref):
    d = D_MODEL
    y_r = _rwkv_post(yf_ref, yb_ref, r_ref, kf_ref, kb_ref, v_ref, g_ref, pvec_ref[...])
    y_m = _mamba_post(mf_ref, mb_ref, xs_ref, z_ref, dskip_ref[...], mnw_ref[...])
    gate = lambda g: _sigmoid(pg_ref[:, g * d:(g + 1) * d].astype(F32))
    merged = (gate(0) * _dot(y_r, wr_ref[...]) + gate(1) * _dot(y_m, wm_ref[...])
              + gate(2) * _dot(yd_ref[...], wd_ref[...]))
    y = _dot(merged.astype(BF16), wo_ref[...])
    o_ref[...] = x_ref[...] + mod_ref[0][2:3] * _rms(y, nw_ref[...], EPS)


def _mixer_tail_call(rw, pvec, mm, d_e, mnw, yd, pg, x2, mod_sub, nw, wr, wm, wd, wo, seq):
    t, d = x2.shape
    tm = 256
    row = lambda w: pl.BlockSpec((tm, w), lambda i: (i, 0))
    vec = lambda a: pl.BlockSpec(a.shape, lambda i: (0, 0))
    res = lambda a: pl.BlockSpec(a.shape, lambda i: (0, 0), pipeline_mode=pl.Buffered(1))
    return pl.pallas_call(
        _mixer_tail_kernel,
        grid=(t // tm,),
        in_specs=([row(RW_DIM)] * 7 + [vec(pvec)] + [row(M_INNER)] * 4 + [vec(d_e), vec(mnw)]
                  + [row(DF_HEADS * DF_V), row(GATE_COLS), row(d),
                     pl.BlockSpec((1, 3, d), lambda i: (i * tm // seq, 0, 0)), vec(nw),
                     res(wr), res(wm), res(wd), res(wo)]),
        out_specs=row(d),
        out_shape=jax.ShapeDtypeStruct((t, d), F32),
        compiler_params=_cparams("parallel"),
        name="mixer_tail",
    )(*rw, pvec, *mm, d_e, mnw, yd, pg, x2, mod_sub, nw, wr, wm, wd, wo)


def _pad_cols(w, n):
    return jnp.pad(w, ((0, 0), (0, n - w.shape[1])))


def _pad_rows(w, n):
    return jnp.pad(w, ((0, n - w.shape[0]), (0, 0)))


def _dir_padded(w):
    z = jnp.zeros_like(w[0])
    return jnp.stack([jnp.concatenate([w[0], z], axis=0), jnp.concatenate([z, w[1]], axis=0)])


def _head_expand_matrix():
    e = np.zeros((2, LANES, M_INNER), np.float32)
    for d in range(2):
        for h in range(M_HEADS):
            e[d, d * M_HEADS + h, h * M_HEADDIM:(h + 1) * M_HEADDIM] = 1.0
    return jnp.asarray(e, dtype=BF16)


def kernel(x, c, ada_w, ada_b, norm_w, ffn_w13, ffn_w2, w_in, rwkv_mu, rwkv_w0, rwkv_w2, rwkv_a0, rwkv_a2, rwkv_g2, rwkv_k_k, rwkv_k_a, rwkv_r_k, rwkv_ln_w, rwkv_ln_b, rwkv_v0, rwkv_v1, rwkv_v2, mamba_conv_w, mamba_conv_b, mamba_dt_bias, mamba_a_log, mamba_d, mamba_norm_w, diff_lambda, diff_subln_w, w_branch_rwkv, w_branch_mamba, w_branch_diff, w_out):
    bsz, seq, d = x.shape
    depth = ada_w.shape[0]
    t = bsz * seq
    x2 = x.reshape(t, d)
    mod_all = _mod_call(c, ada_w, ada_b)
    emat = _head_expand_matrix()
    slopes = jnp.broadcast_to(
        jnp.asarray(2.0 ** (-8.0 * np.arange(1, DF_HEADS + 1) / DF_HEADS), F32)[:, None, None], (DF_HEADS, 1, LANES))
    v_first = None
    for l in range(depth):
        mod = mod_all[l].reshape(bsz, N_SUB, 3, d)
        x2 = _ffn_call(x2, mod[:, 0], norm_w[l, 0:2], ffn_w13[l, 0].astype(BF16), ffn_w2[l, 0].astype(BF16), seq)

        wl = w_in[l]
        o_m = RW_COLS
        o_d = o_m + M_COLS
        o_g = o_d + DF_COLS
        w_cat = jnp.concatenate([
            wl[:, 0:3 * RW_DIM],
            _pad_cols(wl[:, 3 * RW_DIM:RW_COLS], RW_CODES_PAD),
            _pad_cols(wl[:, o_m + M_INNER + M_XBC:o_d], INPROJ_TN - RW_CODES_PAD),
            wl[:, o_m:o_m + M_INNER],
            wl[:, o_m + M_INNER:o_m + M_INNER + M_XBC],
            wl[:, o_d:o_g],
            wl[:, o_g:],
        ], axis=1).astype(BF16)
        p_rkv, p_aux, p_z, p_xbc, p_qkv, p_gate = _inproj_call(
            x2, mod[:, 1], norm_w[l, 2:3], w_cat,
            (3 * RW_DIM, INPROJ_TN, M_INNER, M_XBC, DF_COLS, GATE_COLS),
            (BF16, F32, BF16, BF16, BF16, BF16), seq)
        p_codes = p_dt = p_aux

        mu = rwkv_mu[l]
        mu_p = mu[None, 0:3 * RW_DIM]
        mu_c = _pad_cols(mu[None, 3 * RW_DIM:], RW_CODES_PAD)
        v0 = rwkv_v0[l - 1] if l > 0 else jnp.zeros((RW_DIM,), F32)
        pvec = jnp.stack([rwkv_w0[l, 0], rwkv_w0[l, 1], rwkv_a0[l, 0], rwkv_a0[l, 1],
                          rwkv_k_k[l], rwkv_k_a[l], v0, jnp.zeros((RW_DIM,), F32)])
        g2p = _pad_rows(rwkv_g2[l], 256).astype(BF16)
        vres = None
        if l > 0:
            vres = (v_first, _pad_cols(rwkv_v1[l - 1], LANES).astype(BF16),
                    _pad_rows(rwkv_v2[l - 1], LANES).astype(BF16))
        r, v, kk, k_f, k_b, b_f, b_b, lw_f, lw_b, gate = _rwkv_prep_call(
            p_rkv, p_codes, mu_p, mu_c, pvec, _dir_padded(rwkv_w2[l]).astype(BF16), _dir_padded(rwkv_a2[l]).astype(BF16), g2p, vres, seq)
        if l == 0:
            v_first = v
        y_f, y_b = _rwkv_scan_call(r, v, kk, k_f, b_f, lw_f, k_b, b_b, lw_b, bsz, seq)
        pvec2 = jnp.concatenate([jnp.stack([rwkv_ln_w[l], rwkv_ln_b[l], rwkv_r_k[l].reshape(RW_DIM)]),
                                 jnp.zeros((5, RW_DIM), F32)])

        conv_w = _pad_rows(mamba_conv_w[l], 8)
        xs, bm, cm = _mamba_conv_call(p_xbc, conv_w, mamba_conv_b[l][None], seq)
        dtb = _pad_cols(mamba_dt_bias[l].reshape(1, 2 * M_HEADS), LANES)
        alog_c = _pad_cols(mamba_a_log[l].reshape(1, 2 * M_HEADS), LANES)
        ym_f, ym_b = _ssd_call(xs, bm, cm, p_dt, dtb, emat, alog_c, bsz, seq)
        d_e = jnp.repeat(mamba_d[l], M_HEADDIM)[None]

        lambda_init = 0.8 - 0.6 * math.exp(-0.3 * l)
        y_d = _diff_attn_call(p_qkv, diff_lambda[l], slopes, diff_subln_w[l][None], bsz, seq, lambda_init)

        x2 = _mixer_tail_call((y_f, y_b, r, k_f, k_b, v, gate), pvec2, (ym_f, ym_b, xs, p_z), d_e,
                              mamba_norm_w[l][None], y_d, p_gate, x2, mod[:, 1], norm_w[l, 3:4],
                              w_branch_rwkv[l].astype(BF16), w_branch_mamba[l].astype(BF16),
                              w_branch_diff[l].astype(BF16), w_out[l].astype(BF16), seq)

        x2 = _ffn_call(x2, mod[:, 2], norm_w[l, 4:6], ffn_w13[l, 1].astype(BF16), ffn_w2[l, 1].astype(BF16), seq)
    return x2.reshape(bsz, seq, d)
```

```python
import functools
import itertools
import math

import numpy as np
import jax
import jax.numpy as jnp
from jax import lax
from jax.experimental import pallas as pl
from jax.experimental.pallas import tpu as pltpu

F32 = jnp.float32
BF16 = jnp.bfloat16
HI = lax.Precision.HIGHEST

D_MODEL = 1024
N_SUB = 3
EPS = 1e-6
LANES = 128
SUBLANES = 8
HALO = 2 * SUBLANES
VMEM_LIMIT = 56 * 1024 * 1024

RW_HEAD = 64
RW_DIM = 1024
W_LORA = 64
A_LORA = 64
G_LORA = 160
RW_GN_EPS = 64e-5
RW_COLS = 3 * RW_DIM + 2 * W_LORA + 2 * A_LORA + G_LORA
RW_CODES_PAD = 512
RW_CHUNK = 64
M_INNER = 2048
M_HEADS = 32
M_HEADDIM = 64
M_GROUPS = 4
M_STATE = 128
M_CONV = 5
M_CHUNK = 128
M_XBC = M_INNER + 2 * M_GROUPS * M_STATE
M_COLS = M_INNER + M_XBC + 2 * M_HEADS
M_GROUP_W = M_INNER // M_GROUPS
CONV_SUB = 128
DF_HEADS = 8
DF_HEAD = 64
DF_V = 128
DF_COLS = 3 * DF_HEADS * 2 * DF_HEAD
GATE_COLS = 3 * D_MODEL
NEG_BIG = -1e30


def _cparams(*sem):
    return pltpu.CompilerParams(dimension_semantics=sem, vmem_limit_bytes=VMEM_LIMIT)


def _dot(a, b, **kw):
    return jnp.dot(a, b, preferred_element_type=F32, **kw)


def _dot_nt(a, b, **kw):
    return lax.dot_general(a, b, (((1,), (1,)), ((), ())), preferred_element_type=F32, **kw)


def _split2(x):
    hi = x.astype(BF16)
    return hi, (x - hi.astype(F32)).astype(BF16)


def _dot_sel(x, m_b):
    hi, lo = _split2(x)
    return _dot(hi, m_b) + _dot(lo, m_b)


def _sel_dot(m_b, x):
    hi, lo = _split2(x)
    return _dot(m_b, hi) + _dot(m_b, lo)


def _bdot(a, b):
    return _dot(a.astype(BF16), b.astype(BF16))


def _sigmoid(x):
    return 1.0 / (1.0 + jnp.exp(-x))


def _silu(x):
    return x * _sigmoid(x)


def _softplus(x):
    return jnp.maximum(x, 0.0) + jnp.log(1.0 + jnp.exp(-jnp.abs(x)))


def _rms(x, w, eps):
    return x * lax.rsqrt(jnp.mean(x * x, axis=-1, keepdims=True) + eps) * w


def _norm_mod(x, nw, shift, scale):
    return _rms(x, nw, EPS) * (1.0 + scale) + shift


def _pair_ones(scale=1.0):
    r = lax.broadcasted_iota(jnp.int32, (LANES, LANES), 0) // RW_HEAD
    c = lax.broadcasted_iota(jnp.int32, (LANES, LANES), 1) // RW_HEAD
    return jnp.where(r == c, scale, 0.0).astype(BF16)


def _shift_rows(cur, prevz, nextz, d):
    tm = cur.shape[0]
    rolled = pltpu.roll(cur, (-d) % tm, axis=0)
    rid = lax.broadcasted_iota(jnp.int32, (SUBLANES, cur.shape[1]), 0)
    if d < 0:
        fix = pltpu.roll(prevz, (-d) % SUBLANES, axis=0)
        top = jnp.where(rid < -d, fix, rolled[0:SUBLANES])
        return jnp.concatenate([top, rolled[SUBLANES:]], axis=0)
    fix = pltpu.roll(nextz, (SUBLANES - d) % SUBLANES, axis=0)
    bot = jnp.where(rid >= SUBLANES - d, fix, rolled[tm - SUBLANES:])
    return jnp.concatenate([rolled[:tm - SUBLANES], bot], axis=0)


def _shift_rows_mxu(cur_b, prevz, nextz, d):
    assert cur_b.dtype == BF16
    tm = cur_b.shape[0]
    ri = lax.broadcasted_iota(jnp.int32, (tm, tm), 0)
    ci = lax.broadcasted_iota(jnp.int32, (tm, tm), 1)
    shifted = _dot(jnp.where(ci == ri + d, 1.0, 0.0).astype(BF16), cur_b)
    rid = lax.broadcasted_iota(jnp.int32, (SUBLANES, cur_b.shape[1]), 0)
    if d < 0:
        fix = jnp.where(rid < -d, pltpu.roll(prevz, (-d) % SUBLANES, axis=0), 0.0)
        return jnp.concatenate([shifted[0:SUBLANES] + fix, shifted[SUBLANES:]], axis=0)
    fix = jnp.where(rid >= SUBLANES - d, pltpu.roll(nextz, (SUBLANES - d) % SUBLANES, axis=0), 0.0)
    return jnp.concatenate([shifted[:tm - SUBLANES], shifted[tm - SUBLANES:] + fix], axis=0)


def _nbr_mean_minus_self_mxu(cur_b, prevz, nextz):
    assert cur_b.dtype == BF16
    tm = cur_b.shape[0]
    ri = lax.broadcasted_iota(jnp.int32, (tm, tm), 0)
    ci = lax.broadcasted_iota(jnp.int32, (tm, tm), 1)
    band = jnp.where(ci == ri, -1.0, jnp.where(jnp.abs(ci - ri) == 1, 0.5, 0.0)).astype(BF16)
    out = _dot(band, cur_b)
    rid = lax.broadcasted_iota(jnp.int32, (SUBLANES, cur_b.shape[1]), 0)
    top = jnp.where(rid == 0, 0.5 * prevz[SUBLANES - 1:SUBLANES], 0.0)
    bot = jnp.where(rid == SUBLANES - 1, 0.5 * nextz[0:1], 0.0)
    return jnp.concatenate([out[0:SUBLANES] + top, out[SUBLANES:tm - SUBLANES], out[tm - SUBLANES:] + bot], axis=0)


def _halo_specs(tm, width, n_rows):
    rb = tm // HALO
    last = n_rows // HALO - 1
    return [
        pl.BlockSpec((tm, width), lambda i: (i, 0)),
        pl.BlockSpec((HALO, width), lambda i: (jnp.maximum(i * rb - 1, 0), 0)),
        pl.BlockSpec((HALO, width), lambda i: (jnp.minimum((i + 1) * rb, last), 0)),
    ]


def _halo_values(cur_ref, prev_ref, next_ref, tm, seq):
    i = pl.program_id(0)
    first = (i * tm) % seq == 0
    last = ((i + 1) * tm) % seq == 0
    prevz = jnp.where(first, 0.0, prev_ref[...].astype(F32)[HALO - SUBLANES:])
    nextz = jnp.where(last, 0.0, next_ref[...].astype(F32)[:SUBLANES])
    return cur_ref[...].astype(F32), prevz, nextz


def _mod_kernel(c_ref, w_ref, b_ref, o_ref):
    o_ref[0] = _dot(_silu(c_ref[...]), w_ref[0], precision=HI) + b_ref[0]


def _mod_call(c, ada_w, ada_b):
    depth, d, n = ada_w.shape
    bsz = c.shape[0]
    tn = 1152
    return pl.pallas_call(
        _mod_kernel,
        grid=(depth, n // tn),
        in_specs=[pl.BlockSpec((bsz, d), lambda l, j: (0, 0)),
                  pl.BlockSpec((1, d, tn), lambda l, j: (l, 0, j)),
                  pl.BlockSpec((1, 1, tn), lambda l, j: (l, 0, j))],
        out_specs=pl.BlockSpec((1, bsz, tn), lambda l, j: (l, 0, j)),
        out_shape=jax.ShapeDtypeStruct((depth, bsz, n), F32),
        compiler_params=_cparams("parallel", "parallel"),
        name="adaln_mod",
    )(c, ada_w, ada_b.reshape(depth, 1, n))


def _ffn_kernel(x_ref, mod_ref, nw_ref, w13_ref, w2_ref, o_ref, *, tf):
    dff = w2_ref.shape[0]
    m = mod_ref[0]
    x = x_ref[...]
    h = _norm_mod(x, nw_ref[0:1], m[0:1], m[1:2]).astype(BF16)
    acc = None
    for j in range(dff // tf):
        g = _dot(h, w13_ref[:, j * tf:(j + 1) * tf])
        u = _dot(h, w13_ref[:, dff + j * tf:dff + (j + 1) * tf])
        part = _dot((_silu(g) * u).astype(BF16), w2_ref[j * tf:(j + 1) * tf, :])
        acc = part if acc is None else acc + part
    o_ref[...] = x + 0.5 * m[2:3] * _rms(acc, nw_ref[1:2], EPS)


def _ffn_call(x2, mod_sub, nw2, w13, w2, seq):
    t, d = x2.shape
    tm = min(1024, seq)
    resident = lambda a: pl.BlockSpec(a.shape, lambda i: (0, 0), pipeline_mode=pl.Buffered(1))
    return pl.pallas_call(
        functools.partial(_ffn_kernel, tf=256),
        grid=(t // tm,),
        in_specs=[pl.BlockSpec((tm, d), lambda i: (i, 0)),
                  pl.BlockSpec((1, 3, d), lambda i: (i * tm // seq, 0, 0)),
                  pl.BlockSpec((2, d), lambda i: (0, 0)),
                  resident(w13), resident(w2)],
        out_specs=pl.BlockSpec((tm, d), lambda i: (i, 0)),
        out_shape=jax.ShapeDtypeStruct((t, d), F32),
        compiler_params=_cparams("parallel"),
        name="swiglu_halfstep",
    )(x2, mod_sub, nw2, w13, w2)


INPROJ_TN = 1024


def _inproj_kernel(x_ref, mod_ref, nw_ref, w_ref, *rest, starts):
    o_refs, h_scr = rest[:-1], rest[-1]
    j = pl.program_id(1)

    @pl.when(j == 0)
    def _():
        m = mod_ref[0]
        h_scr[...] = _norm_mod(x_ref[...], nw_ref[...], m[0:1], m[1:2]).astype(BF16)

    for k, o_ref in enumerate(o_refs):
        @pl.when((j >= starts[k]) & (j < starts[k + 1]))
        def _(o_ref=o_ref):
            o_ref[...] = _dot(h_scr[...], w_ref[...]).astype(o_ref.dtype)


def _inproj_call(x2, mod_sub, nw, w, widths, dtypes, seq):
    t, d = x2.shape
    tn = INPROJ_TN
    tm = min(1024, seq)
    starts = [0]
    for wd in widths:
        starts.append(starts[-1] + wd // tn)

    def out_spec(k):
        return pl.BlockSpec((tm, tn), lambda i, j: (i, jnp.clip(j - starts[k], 0, widths[k] // tn - 1)))

    return pl.pallas_call(
        functools.partial(_inproj_kernel, starts=tuple(starts)),
        grid=(t // tm, starts[-1]),
        in_specs=[pl.BlockSpec((tm, d), lambda i, j: (i, 0)),
                  pl.BlockSpec((1, 3, d), lambda i, j: (i * tm // seq, 0, 0)),
                  pl.BlockSpec((1, d), lambda i, j: (0, 0)),
                  pl.BlockSpec((d, tn), lambda i, j: (0, j))],
        out_specs=[out_spec(k) for k in range(len(widths))],
        out_shape=[jax.ShapeDtypeStruct((t, wd), dt) for wd, dt in zip(widths, dtypes)],
        scratch_shapes=[pltpu.VMEM((tm, d), BF16)],
        compiler_params=_cparams("parallel", "arbitrary"),
        name="norm_inproj",
    )(x2, mod_sub, nw, w)


def _rwkv_prep_kernel(*refs, tm, seq, has_vres):
    (p_ref, pp_ref, pn_ref, c_ref, cp_ref, cn_ref, mup_ref, muc_ref, pvec_ref,
     w2_ref, a2_ref, g2_ref) = refs[:12]
    refs = refs[12:]
    if has_vres:
        vf_ref, v1_ref, v2_ref = refs[:3]
        refs = refs[3:]
    r_o, v_o, kk_o, kf_o, kb_o, bf_o, bb_o, lwf_o, lwb_o, g_o = refs

    def shift_mix(cur_ref, prev_ref, next_ref, mu):
        cur, prevz, nextz = _halo_values(cur_ref, prev_ref, next_ref, tm, seq)
        if cur_ref.dtype == BF16:
            return cur + mu * _nbr_mean_minus_self_mxu(cur_ref[...], prevz, nextz)
        nb = 0.5 * (_shift_rows(cur, prevz, nextz, -1) + _shift_rows(cur, prevz, nextz, 1))
        return cur + mu * (nb - cur)

    p = shift_mix(p_ref, pp_ref, pn_ref, mup_ref[...])
    codes = shift_mix(c_ref, cp_ref, cn_ref, muc_ref[...])
    r = p[:, 0:RW_DIM]
    k = p[:, RW_DIM:2 * RW_DIM]
    v = p[:, 2 * RW_DIM:3 * RW_DIM]
    cw = jnp.tanh(codes[:, 0:2 * W_LORA])
    ca = codes[:, 2 * W_LORA:2 * W_LORA + 2 * A_LORA]
    cg = _sigmoid(codes[:, 2 * W_LORA + 2 * A_LORA:])
    pv = pvec_ref[...]
    if has_vres:
        lo = _bdot(_bdot(v, v1_ref[...]), v2_ref[...])
        v = v + (vf_ref[...].astype(F32) - v) * _sigmoid(pv[6:7] + lo)
    r_o[...] = r.astype(r_o.dtype)
    v_o[...] = v.astype(v_o.dtype)
    g_o[...] = _bdot(cg, g2_ref[...]).astype(g_o.dtype)
    iclr = []
    for d, lw_o in ((0, lwf_o), (1, lwb_o)):
        lw_o[...] = -math.exp(-0.5) * _sigmoid(pv[d:d + 1] + _bdot(cw, w2_ref[d]))
        iclr.append(_sigmoid(pv[2 + d:3 + d] + _bdot(ca, a2_ref[d])))
    kf_o[...] = (k * (1.0 + (iclr[0] - 1.0) * pv[5:6])).astype(kf_o.dtype)
    kb_o[...] = (k * (1.0 + (iclr[1] - 1.0) * pv[5:6])).astype(kb_o.dtype)
    ones2 = _pair_ones()
    kkr = k * pv[4:5]
    for cb in range(RW_DIM // LANES):
        sl = slice(cb * LANES, (cb + 1) * LANES)
        blk = kkr[:, sl]
        kkn = blk * lax.rsqrt(_dot_sel(blk * blk, ones2) + 1e-12)
        kk_o[:, sl] = kkn.astype(kk_o.dtype)
        bf_o[:, sl] = (kkn * iclr[0][:, sl]).astype(bf_o.dtype)
        bb_o[:, sl] = (kkn * iclr[1][:, sl]).astype(bb_o.dtype)


def _rwkv_prep_call(rkv, codes, mu_p, mu_c, pvec, w2p, a2p, g2p, vres, seq):
    t = rkv.shape[0]
    tm = 256
    has_vres = vres is not None
    full = lambda shape: pl.BlockSpec(shape, lambda i: (0,) * len(shape))
    in_specs = (_halo_specs(tm, 3 * RW_DIM, t) + _halo_specs(tm, RW_CODES_PAD, t)
                + [full((1, 3 * RW_DIM)), full((1, RW_CODES_PAD)), full((8, RW_DIM)),
                   full((2, 2 * W_LORA, RW_DIM)), full((2, 2 * A_LORA, RW_DIM)), full((256, RW_DIM))])
    args = [rkv, rkv, rkv, codes, codes, codes, mu_p, mu_c, pvec, w2p, a2p, g2p]
    if has_vres:
        in_specs += [pl.BlockSpec((tm, RW_DIM), lambda i: (i, 0)), full((RW_DIM, LANES)), full((LANES, RW_DIM))]
        args += list(vres)
    dtypes = [BF16] * 7 + [F32, F32, BF16]
    return pl.pallas_call(
        functools.partial(_rwkv_prep_kernel, tm=tm, seq=seq, has_vres=has_vres),
        grid=(t // tm,),
        in_specs=in_specs,
        out_specs=[pl.BlockSpec((tm, RW_DIM), lambda i: (i, 0))] * 10,
        out_shape=[jax.ShapeDtypeStruct((t, RW_DIM), dt) for dt in dtypes],
        compiler_params=_cparams("parallel"),
        name="rwkv_prep",
    )(*args)


def _rwkv_scan_kernel(rf_ref, vf_ref, kkf_ref, kf_ref, bf_ref, lwf_ref, rb_ref, vb_ref, kkb_ref, kb_ref, bb_ref,
                      lwb_ref, yf_ref, yb_ref, h_scr, *, n_chunks):
    c = RW_CHUNK
    n_pairs = RW_DIM // LANES

    @pl.when(pl.program_id(1) == 0)
    def _():
        h_scr[...] = jnp.zeros_like(h_scr)

    ri = lax.broadcasted_iota(jnp.int32, (LANES, LANES), 0)
    ci = lax.broadcasted_iota(jnp.int32, (LANES, LANES), 1)
    same = (ri // c) == (ci // c)
    tr = lax.broadcasted_iota(jnp.int32, (c, LANES), 0)
    tc = lax.broadcasted_iota(jnp.int32, (c, LANES), 1) % c
    strict = [tc < tr, tc > tr]
    incl = [tc <= tr, tc >= tr]
    blk = [(tr // w) == (tc // w) for w in (8, 16, 32, 64)]
    eye = jnp.where(tr == tc, 1.0, 0.0).astype(F32)
    r64 = lax.broadcasted_iota(jnp.int32, (c, c), 0)
    c64 = lax.broadcasted_iota(jnp.int32, (c, c), 1)
    tri = [jnp.where(c64 <= r64, 1.0, 0.0).astype(BF16), jnp.where(c64 >= r64, 1.0, 0.0).astype(BF16)]
    head0 = lax.broadcasted_iota(jnp.int32, (c, LANES), 1) < RW_HEAD
    end_row = [c - 1, 0]
    refs = [(rf_ref, vf_ref, kkf_ref, kf_ref, bf_ref, lwf_ref, yf_ref),
            (rb_ref, vb_ref, kkb_ref, kb_ref, bb_ref, lwb_ref, yb_ref)]
    lns = [slice(pr * LANES, (pr + 1) * LANES) for pr in range(n_pairs)]
    chains = [(d, p) for p in range(n_pairs) for d in range(2)]

    def stack(x):
        return jnp.concatenate([jnp.where(head0, x, 0.0), jnp.where(head0, 0.0, x)], axis=0).astype(BF16)

    def per_head(x, y):
        return _dot(x.astype(BF16), stack(y))

    def chunk(step, carry):
        rows = [pl.ds(pl.multiple_of(step * c, c), c), pl.ds(pl.multiple_of((n_chunks - 1 - step) * c, c), c)]
        wide = []
        for d in range(2):
            r_ref, v_ref, kk_ref, k_ref, b_ref, lw_ref, _ = refs[d]
            lw = lw_ref[rows[d], :]
            ld = lambda ref: ref[rows[d], :].astype(F32)
            cum = _sel_dot(tri[d], lw)
            tot = cum[end_row[d]:end_row[d] + 1]
            g_inv, g_end = jnp.exp(-cum), jnp.exp(tot - cum)
            k_all, b_all = ld(k_ref), ld(b_ref)
            wide.append(dict(at=-ld(kk_ref) * jnp.exp(cum - lw), rt=ld(r_ref) * jnp.exp(cum),
                             bt=b_all * g_inv, kt=k_all * g_inv, bg=b_all * g_end, kg=k_all * g_end,
                             v=ld(v_ref), g_tot=jnp.exp(tot)))
        idx = range(len(chains))
        sel = lambda name, i: wide[chains[i][0]][name][:, lns[chains[i][1]]]
        ar = [jnp.concatenate([sel("at", i), sel("rt", i)], axis=0).astype(BF16) for i in idx]
        bk = [jnp.concatenate([stack(sel("bt", i)), stack(sel("kt", i))], axis=0) for i in idx]
        v_s = [stack(sel("v", i)) for i in idx]
        ht = [h_scr[d, p] for d, p in chains]
        gram = [_dot_nt(ar[i], bk[i]) for i in idx]
        a_ab = [jnp.where(strict[d], gram[i][0:c, 0:LANES], 0.0) for i, (d, p) in enumerate(chains)]
        a_ak = [jnp.where(strict[d], gram[i][0:c, LANES:], 0.0).astype(BF16) for i, (d, p) in enumerate(chains)]
        a_rb = [jnp.where(incl[d], gram[i][c:, 0:LANES], 0.0) for i, (d, p) in enumerate(chains)]
        a_rk = [jnp.where(incl[d], gram[i][c:, LANES:], 0.0).astype(BF16) for i, (d, p) in enumerate(chains)]
        lhs = [jnp.concatenate([ar[i], jnp.concatenate([a_ak[i], a_rk[i]], axis=0)], axis=1) for i in idx]
        rhs = [jnp.concatenate([ht[i].T.astype(BF16), v_s[i]], axis=0) for i in idx]
        xy0 = [_dot(lhs[i], rhs[i]) for i in idx]
        p1 = [jnp.where(blk[0], a_ab[i], 0.0) for i in idx]
        p2 = [per_head(p1[i], p1[i]) for i in idx]
        p4 = [per_head(p2[i], p2[i]) for i in idx]
        t_inv = [eye + p1[i] + p2[i] + per_head(p1[i], p2[i]) for i in idx]
        t_inv = [t_inv[i] + per_head(t_inv[i], p4[i]) for i in idx]
        for lvl in range(1, 4):
            off = [jnp.where(blk[lvl] & ~blk[lvl - 1], a_ab[i], 0.0) for i in idx]
            mid = [per_head(t_inv[i], off[i]) for i in idx]
            t_inv = [t_inv[i] + per_head(mid[i], t_inv[i]) for i in idx]
        u = [per_head(t_inv[i], xy0[i][0:c]) for i in idx]
        y = [xy0[i][c:] + per_head(a_rb[i], u[i]) for i in idx]
        for d in range(2):
            ys = [y[i] for i, (dd, p) in enumerate(chains) if dd == d]
            refs[d][6][rows[d], :] = jnp.concatenate(ys, axis=1).astype(refs[d][6].dtype)
        for i, (d, p) in enumerate(chains):
            uv_t = jnp.concatenate([u[i], sel("v", i)], axis=0).T.astype(BF16)
            bkg = jnp.concatenate([sel("bg", i), sel("kg", i)], axis=0).astype(BF16)
            upd = jnp.where(same, _dot(uv_t, bkg), 0.0)
            h_scr[d, p] = ht[i] * wide[d]["g_tot"][:, lns[p]] + upd
        return carry

    lax.fori_loop(0, n_chunks, chunk, 0)


def _rwkv_scan_call(r, v, kk, k_f, b_f, lw_f, k_b, b_b, lw_b, bsz, seq):
    t = r.shape[0]
    tb = min(8 * RW_CHUNK, seq)
    nt = seq // tb
    fwd = pl.BlockSpec((tb, RW_DIM), lambda bi, ti: (bi * nt + ti, 0))
    bwd = pl.BlockSpec((tb, RW_DIM), lambda bi, ti: (bi * nt + nt - 1 - ti, 0))
    out = jax.ShapeDtypeStruct((t, RW_DIM), BF16)
    return pl.pallas_call(
        functools.partial(_rwkv_scan_kernel, n_chunks=tb // RW_CHUNK),
        grid=(bsz, nt),
        in_specs=[fwd] * 6 + [bwd] * 6,
        out_specs=[fwd, bwd],
        out_shape=[out, out],
        scratch_shapes=[pltpu.VMEM((2, RW_DIM // LANES, LANES, LANES), F32)],
        compiler_params=_cparams("parallel", "arbitrary"),
        name="rwkv_scan",
    )(r, v, kk, k_f, b_f, lw_f, r, v, kk, k_b, b_b, lw_b)


def _rwkv_post(yf_ref, yb_ref, r_ref, kf_ref, kb_ref, v_ref, g_ref, pv):
    mean2 = _pair_ones(1.0 / RW_HEAD)
    ones2 = _pair_ones()
    tm = yf_ref.shape[0]
    sls = [slice(cb * LANES, (cb + 1) * LANES) for cb in range(RW_DIM // LANES)]
    ld = lambda ref, sl: ref[:, sl].astype(F32)
    y = jnp.concatenate([ld(yf_ref, sl) + ld(yb_ref, sl) for sl in sls], axis=0)
    dlt = y - _dot_sel(y, mean2)
    core = dlt * lax.rsqrt(_dot_sel(dlt * dlt, mean2) + RW_GN_EPS)
    rk = jnp.concatenate([ld(r_ref, sl) * (ld(kf_ref, sl) + ld(kb_ref, sl)) * pv[2:3, sl] for sl in sls], axis=0)
    rk_sum = _dot_sel(rk, ones2)
    outs = []
    for cb, sl in enumerate(sls):
        rows = slice(cb * tm, (cb + 1) * tm)
        yn = core[rows] * pv[0:1, sl] + pv[1:2, sl]
        outs.append(((yn + rk_sum[rows] * ld(v_ref, sl)) * ld(g_ref, sl)).astype(BF16))
    return jnp.concatenate(outs, axis=1)


def _mamba_conv_kernel(x_ref, xp_ref, xn_ref, w_ref, b_ref, xs_o, bm_o, cm_o, *, tm, seq):
    _, prevz, nextz = _halo_values(x_ref, xp_ref, xn_ref, tm, seq)
    w = w_ref[...]
    half = (M_CONV - 1) // 2
    sb = CONV_SUB
    n_sub = tm // sb
    for s in range(n_sub):
        rows = slice(s * sb, (s + 1) * sb)
        cur_b = x_ref[rows, :]
        pz = prevz if s == 0 else x_ref[s * sb - HALO:s * sb, :].astype(F32)[HALO - SUBLANES:]
        nz = nextz if s == n_sub - 1 else x_ref[(s + 1) * sb:(s + 1) * sb + HALO, :].astype(F32)[:SUBLANES]
        acc = cur_b.astype(F32) * w[half:half + 1] + b_ref[...]
        for tap in range(M_CONV):
            if tap != half:
                acc = acc + _shift_rows_mxu(cur_b, pz, nz, tap - half) * w[tap:tap + 1]
        y = _silu(acc)
        xs_o[rows, :] = y[:, 0:M_INNER].astype(xs_o.dtype)
        bm_o[rows, :] = y[:, M_INNER:M_INNER + M_GROUPS * M_STATE].astype(bm_o.dtype)
        cm_o[rows, :] = y[:, M_INNER + M_GROUPS * M_STATE:].astype(cm_o.dtype)


def _mamba_conv_call(xbc, conv_w, conv_b, seq):
    t = xbc.shape[0]
    tm = min(512, seq)
    gs = M_GROUPS * M_STATE
    return pl.pallas_call(
        functools.partial(_mamba_conv_kernel, tm=tm, seq=seq),
        grid=(t // tm,),
        in_specs=_halo_specs(tm, M_XBC, t) + [pl.BlockSpec((8, M_XBC), lambda i: (0, 0)),
                                             pl.BlockSpec((1, M_XBC), lambda i: (0, 0))],
        out_specs=[pl.BlockSpec((tm, M_INNER), lambda i: (i, 0)),
                   pl.BlockSpec((tm, gs), lambda i: (i, 0)),
                   pl.BlockSpec((tm, gs), lambda i: (i, 0))],
        out_shape=[jax.ShapeDtypeStruct((t, M_INNER), BF16),
                   jax.ShapeDtypeStruct((t, gs), BF16),
                   jax.ShapeDtypeStruct((t, gs), BF16)],
        compiler_params=_cparams("parallel"),
        name="mamba_conv",
    )(xbc, xbc, xbc, conv_w, conv_b)


def _ssd_chunk(d, xs_ref, bm_ref, cm_ref, dt_ref, dtb_ref, e_ref, alogc_ref, y_ref, h_scr):
    c = M_CHUNK
    n_pairs = M_INNER // LANES
    ppg = n_pairs // M_GROUPS
    ri = lax.broadcasted_iota(jnp.int32, (c, c), 0)
    ci = lax.broadcasted_iota(jnp.int32, (c, c), 1)
    keep = (ci >= ri) if d else (ci <= ri)
    tri = jnp.where(keep, 1.0, 0.0).astype(BF16)
    end_row = 0 if d else c - 1
    head0 = lax.broadcasted_iota(jnp.int32, (c, LANES), 1) < M_HEADDIM

    dt = _softplus(dt_ref[...] + dtb_ref[...])
    acs_c = _sel_dot(tri, dt * (-jnp.exp(alogc_ref[...])))
    acs_ct = acs_c.T
    pieces = jnp.concatenate(_split2(dt) + _split2(acs_c), axis=0)
    ex = _dot(pieces, e_ref[d])
    dte = ex[0:c] + ex[c:2 * c]
    acs = ex[2 * c:3 * c] + ex[3 * c:]
    tot = acs[end_row:end_row + 1]
    yield
    xdt = xs_ref[...].astype(F32) * dte
    e_acs = jnp.exp(acs)
    x_end = (xdt * jnp.exp(tot - acs)).astype(BF16)
    xdt_b = xdt.astype(BF16)
    dec = jnp.exp(tot)
    gsl = [slice(g * M_STATE, (g + 1) * M_STATE) for g in range(M_GROUPS)]
    bm_b = [bm_ref[:, sl] for sl in gsl]
    cm_b = [cm_ref[:, sl] for sl in gsl]
    bm_t = [bm_ref[:, sl].astype(F32).T.astype(BF16) for sl in gsl]
    cb = [_dot_nt(cm_b[g], bm_b[g]) for g in range(M_GROUPS)]
    yield
    prs = range(n_pairs)
    lns = [slice(p * LANES, (p + 1) * LANES) for p in prs]
    ys = []
    for p in prs:
        pair = []
        for hh in range(2):
            idx = d * M_HEADS + 2 * p + hh
            seg = acs_c[:, idx:idx + 1] - acs_ct[idx:idx + 1, :]
            lmat = jnp.exp(jnp.where(keep, seg, NEG_BIG))
            pair.append(_dot((cb[p // ppg] * lmat).astype(BF16), xdt_b[:, lns[p]]))
        ys.append(jnp.where(head0, pair[0], pair[1]))
        yield
    h_prev = [h_scr[d, p] for p in prs]
    y_off = [_dot(cm_b[g], jnp.concatenate([h_prev[p].astype(BF16) for p in range(g * ppg, (g + 1) * ppg)], axis=1))
             for g in range(M_GROUPS)]
    y_ref[...] = (jnp.concatenate(ys, axis=1) + jnp.concatenate(y_off, axis=1) * e_acs).astype(y_ref.dtype)
    yield
    for g in range(M_GROUPS):
        upd = _dot(bm_t[g], x_end[:, g * M_GROUP_W:(g + 1) * M_GROUP_W])
        for q in range(ppg):
            p = g * ppg + q
            h_scr[d, p] = h_prev[p] * dec[:, lns[p]] + upd[:, q * LANES:(q + 1) * LANES]


def _ssd_kernel(xsf_ref, bmf_ref, cmf_ref, dtf_ref, xsb_ref, bmb_ref, cmb_ref, dtb_ref, bias_ref, e_ref, alogc_ref,
                yf_ref, yb_ref, h_scr):
    @pl.when(pl.program_id(1) == 0)
    def _():
        h_scr[...] = jnp.zeros_like(h_scr)

    shared = (bias_ref, e_ref, alogc_ref)
    both = (_ssd_chunk(0, xsf_ref, bmf_ref, cmf_ref, dtf_ref, *shared, yf_ref, h_scr),
            _ssd_chunk(1, xsb_ref, bmb_ref, cmb_ref, dtb_ref, *shared, yb_ref, h_scr))
    for _ in itertools.zip_longest(*both):
        pass


def _ssd_call(xs, bm, cm, dt_raw, dtb, emat, alog_c, bsz, seq):
    t = xs.shape[0]
    nc = seq // M_CHUNK
    gs = M_GROUPS * M_STATE
    dt_col = RW_CODES_PAD // LANES
    fwd = lambda bi, ci: bi * nc + ci
    bwd = lambda bi, ci: bi * nc + nc - 1 - ci
    data = lambda row: [pl.BlockSpec((M_CHUNK, M_INNER), lambda bi, ci: (row(bi, ci), 0)),
                        pl.BlockSpec((M_CHUNK, gs), lambda bi, ci: (row(bi, ci), 0)),
                        pl.BlockSpec((M_CHUNK, gs), lambda bi, ci: (row(bi, ci), 0)),
                        pl.BlockSpec((M_CHUNK, LANES), lambda bi, ci: (row(bi, ci), dt_col))]
    full = lambda a: pl.BlockSpec(a.shape, lambda bi, ci: (0,) * a.ndim)
    out = jax.ShapeDtypeStruct((t, M_INNER), BF16)
    return pl.pallas_call(
        _ssd_kernel,
        grid=(bsz, nc),
        in_specs=data(fwd) + data(bwd) + [full(dtb), full(emat), full(alog_c)],
        out_specs=[pl.BlockSpec((M_CHUNK, M_INNER), lambda bi, ci: (fwd(bi, ci), 0)),
                   pl.BlockSpec((M_CHUNK, M_INNER), lambda bi, ci: (bwd(bi, ci), 0))],
        out_shape=[out, out],
        scratch_shapes=[pltpu.VMEM((2, M_INNER // LANES, M_STATE, LANES), F32)],
        compiler_params=_cparams("parallel", "arbitrary"),
        name="ssd_scan",
    )(xs, bm, cm, dt_raw, xs, bm, cm, dt_raw, dtb, emat, alog_c)


def _mamba_post(yf_ref, yb_ref, xs_ref, z_ref, d_skip, nw):
    ld = lambda ref: ref[...].astype(F32)
    y = (ld(yf_ref) + ld(yb_ref) + d_skip * ld(xs_ref)) * _silu(ld(z_ref))
    groups = [slice(g * M_GROUP_W, (g + 1) * M_GROUP_W) for g in range(M_GROUPS)]
    return jnp.concatenate([_rms(y[:, sl], nw[:, sl], 1e-5).astype(BF16) for sl in groups], axis=1)


def _diff_attn_kernel(zero_ref, q_ref, k_ref, v_ref, lam_ref, slope_ref, nw_ref, o_ref, bias_scr, s_scr, *, tq, tk,
                      lambda_init):
    seq = k_ref.shape[0]
    log2e = math.log2(math.e)

    @pl.when(pl.program_id(2) == 0)
    def _():
        rows = pl.program_id(1) * tq + lax.broadcasted_iota(jnp.int32, (tq, seq), 0)
        cols = lax.broadcasted_iota(jnp.int32, (tq, seq), 1)
        bias_scr[...] = (slope_ref[0][:, 0:1] * log2e) * jnp.abs(rows - cols).astype(F32)

    q = (q_ref[...].astype(F32) * (DF_HEAD ** -0.5 * log2e)).astype(BF16)
    lv = lam_ref[...]
    lam = (jnp.exp(jnp.sum(lv[0:1] * lv[1:2], keepdims=True))
           - jnp.exp(jnp.sum(lv[2:3] * lv[3:4], keepdims=True)) + lambda_init)
    map0 = lax.broadcasted_iota(jnp.int32, (tq, LANES), 1) < DF_HEAD
    qm = [jnp.where(map0, q, jnp.zeros_like(q)), jnp.where(map0, jnp.zeros_like(q), q)]
    tiles = [slice(t * tk, (t + 1) * tk) for t in range(seq // tk)]
    rt = min(8 * LANES, tq)
    streams = [(mp, slice(rh * rt, (rh + 1) * rt)) for rh in range(tq // rt) for mp in range(2)]
    v1 = jnp.concatenate([v_ref[...], jnp.ones((seq, LANES), BF16)], axis=1)
    zero = zero_ref[0]

    def scores_tile(st, ts, mx):
        mp, rows = st
        s = _dot_nt(qm[mp][rows], k_ref[ts, :]) - bias_scr[rows, ts]
        s_scr[mp, rows, ts] = s
        for cs in range(tk // LANES):
            blk = s[:, cs * LANES:(cs + 1) * LANES]
            mx = blk if mx is None else jnp.maximum(mx, blk)
        return mx

    def weights(st, m):
        mp, rows = st
        return _dot(jnp.exp2(s_scr[mp + zero, rows, :] - m).astype(BF16), v1)

    m, acc = {}, {}
    for i in range(len(streams) + 1):
        if i < len(streams):
            mx = None
            for ts in tiles:
                mx = scores_tile(streams[i], ts, mx)
            m[i] = jnp.max(mx, axis=-1, keepdims=True)
        if i >= 1:
            acc[i - 1] = weights(streams[i - 1], m[i - 1])
        if i >= 2 and i % 2 == 0:
            rh = i // 2 - 1
            om = [acc[2 * rh + mp][:, 0:LANES] / acc[2 * rh + mp][:, LANES:LANES + 1] for mp in range(2)]
            o = om[0] - lam * om[1]
            o_ref[rh * rt:(rh + 1) * rt, :] = (_rms(o, nw_ref[...], 1e-5) * (1.0 - lambda_init)).astype(o_ref.dtype)


def _diff_attn_call(qkv, lam_vecs, slopes, subln_w, bsz, seq, lambda_init):
    t = qkv.shape[0]
    tq = min(1024, seq)
    nq = seq // tq
    return pl.pallas_call(
        functools.partial(_diff_attn_kernel, tq=tq, tk=min(512, seq), lambda_init=lambda_init),
        grid=(DF_HEADS, nq, bsz),
        in_specs=[pl.BlockSpec(memory_space=pltpu.SMEM),
                  pl.BlockSpec((tq, DF_V), lambda hi, qi, bi: (bi * nq + qi, hi)),
                  pl.BlockSpec((seq, DF_V), lambda hi, qi, bi: (bi, DF_HEADS + hi)),
                  pl.BlockSpec((seq, DF_V), lambda hi, qi, bi: (bi, 2 * DF_HEADS + hi)),
                  pl.BlockSpec((4, DF_HEAD), lambda hi, qi, bi: (0, 0)),
                  pl.BlockSpec((1, 1, LANES), lambda hi, qi, bi: (hi, 0, 0)),
                  pl.BlockSpec((1, DF_V), lambda hi, qi, bi: (0, 0))],
        out_specs=pl.BlockSpec((tq, DF_V), lambda hi, qi, bi: (bi * nq + qi, hi)),
        out_shape=jax.ShapeDtypeStruct((t, DF_HEADS * DF_V), BF16),
        scratch_shapes=[pltpu.VMEM((tq, seq), F32), pltpu.VMEM((2, tq, seq), F32)],
        compiler_params=_cparams("parallel", "parallel", "arbitrary"),
        name="diff_attn",
    )(jnp.zeros((1,), jnp.int32), qkv, qkv, qkv, lam_vecs, slopes, subln_w)


def _mixer_tail_kernel(yf_ref, yb_ref, r_ref, kf_ref, kb_ref, v_ref, g_ref, pvec_ref, mf_ref, mb_ref, xs_ref, z_ref,
                       dskip_ref, mnw_ref, yd_ref, pg_ref, x_ref, mod_ref, nw_ref, wr_ref, wm_ref, wd_ref, wo_ref,
                       o_ref):
    d = D_MODEL
    y_r = _rwkv_post(yf_ref, yb_ref, r_ref, kf_ref, kb_ref, v_ref, g_ref, pvec_ref[...])
    y_m = _mamba_post(mf_ref, mb_ref, xs_ref, z_ref, dskip_ref[...], mnw_ref[...])
    gate = lambda g: _sigmoid(pg_ref[:, g * d:(g + 1) * d].astype(F32))
    merged = (gate(0) * _dot(y_r, wr_ref[...]) + gate(1) * _dot(y_m, wm_ref[...])
              + gate(2) * _dot(yd_ref[...], wd_ref[...]))
    y = _dot(merged.astype(BF16), wo_ref[...])
    o_ref[...] = x_ref[...] + mod_ref[0][2:3] * _rms(y, nw_ref[...], EPS)


def _mixer_tail_call(rw, pvec, mm, d_e, mnw, yd, pg, x2, mod_sub, nw, wr, wm, wd, wo, seq):
    t, d = x2.shape
    tm = 256
    row = lambda w: pl.BlockSpec((tm, w), lambda i: (i, 0))
    vec = lambda a: pl.BlockSpec(a.shape, lambda i: (0, 0))
    res = lambda a: pl.BlockSpec(a.shape, lambda i: (0, 0), pipeline_mode=pl.Buffered(1))
    return pl.pallas_call(
        _mixer_tail_kernel,
        grid=(t // tm,),
        in_specs=([row(RW_DIM)] * 7 + [vec(pvec)] + [row(M_INNER)] * 4 + [vec(d_e), vec(mnw)]
                  + [row(DF_HEADS * DF_V), row(GATE_COLS), row(d),
                     pl.BlockSpec((1, 3, d), lambda i: (i * tm // seq, 0, 0)), vec(nw),
                     res(wr), res(wm), res(wd), res(wo)]),
        out_specs=row(d),
        out_shape=jax.ShapeDtypeStruct((t, d), F32),
        compiler_params=_cparams("parallel"),
        name="mixer_tail",
    )(*rw, pvec, *mm, d_e, mnw, yd, pg, x2, mod_sub, nw, wr, wm, wd, wo)


def _pad_cols(w, n):
    return jnp.pad(w, ((0, 0), (0, n - w.shape[1])))


def _pad_rows(w, n):
    return jnp.pad(w, ((0, n - w.shape[0]), (0, 0)))


def _dir_padded(w):
    z = jnp.zeros_like(w[0])
    return jnp.stack([jnp.concatenate([w[0], z], axis=0), jnp.concatenate([z, w[1]], axis=0)])


def _head_expand_matrix():
    e = np.zeros((2, LANES, M_INNER), np.float32)
    for d in range(2):
        for h in range(M_HEADS):
            e[d, d * M_HEADS + h, h * M_HEADDIM:(h + 1) * M_HEADDIM] = 1.0
    return jnp.asarray(e, dtype=BF16)


def kernel(x, c, ada_w, ada_b, norm_w, ffn_w13, ffn_w2, w_in, rwkv_mu, rwkv_w0, rwkv_w2, rwkv_a0, rwkv_a2, rwkv_g2, rwkv_k_k, rwkv_k_a, rwkv_r_k, rwkv_ln_w, rwkv_ln_b, rwkv_v0, rwkv_v1, rwkv_v2, mamba_conv_w, mamba_conv_b, mamba_dt_bias, mamba_a_log, mamba_d, mamba_norm_w, diff_lambda, diff_subln_w, w_branch_rwkv, w_branch_mamba, w_branch_diff, w_out):
    bsz, seq, d = x.shape
    depth = ada_w.shape[0]
    t = bsz * seq
    x2 = x.reshape(t, d)
    mod_all = _mod_call(c, ada_w, ada_b)
    emat = _head_expand_matrix()
    slopes = jnp.broadcast_to(
        jnp.asarray(2.0 ** (-8.0 * np.arange(1, DF_HEADS + 1) / DF_HEADS), F32)[:, None, None], (DF_HEADS, 1, LANES))
    v_first = None
    for l in range(depth):
        mod = mod_all[l].reshape(bsz, N_SUB, 3, d)
        x2 = _ffn_call(x2, mod[:, 0], norm_w[l, 0:2], ffn_w13[l, 0].astype(BF16), ffn_w2[l, 0].astype(BF16), seq)

        wl = w_in[l]
        o_m = RW_COLS
        o_d = o_m + M_COLS
        o_g = o_d + DF_COLS
        w_cat = jnp.concatenate([
            wl[:, 0:3 * RW_DIM],
            _pad_cols(wl[:, 3 * RW_DIM:RW_COLS], RW_CODES_PAD),
            _pad_cols(wl[:, o_m + M_INNER + M_XBC:o_d], INPROJ_TN - RW_CODES_PAD),
            wl[:, o_m:o_m + M_INNER],
            wl[:, o_m + M_INNER:o_m + M_INNER + M_XBC],
            wl[:, o_d:o_g],
            wl[:, o_g:],
        ], axis=1).astype(BF16)
        p_rkv, p_aux, p_z, p_xbc, p_qkv, p_gate = _inproj_call(
            x2, mod[:, 1], norm_w[l, 2:3], w_cat,
            (3 * RW_DIM, INPROJ_TN, M_INNER, M_XBC, DF_COLS, GATE_COLS),
            (BF16, F32, BF16, BF16, BF16, BF16), seq)
        p_codes = p_dt = p_aux

        mu = rwkv_mu[l]
        mu_p = mu[None, 0:3 * RW_DIM]
        mu_c = _pad_cols(mu[None, 3 * RW_DIM:], RW_CODES_PAD)
        v0 = rwkv_v0[l - 1] if l > 0 else jnp.zeros((RW_DIM,), F32)
        pvec = jnp.stack([rwkv_w0[l, 0], rwkv_w0[l, 1], rwkv_a0[l, 0], rwkv_a0[l, 1],
                          rwkv_k_k[l], rwkv_k_a[l], v0, jnp.zeros((RW_DIM,), F32)])
        g2p = _pad_rows(rwkv_g2[l], 256).astype(BF16)
        vres = None
        if l > 0:
            vres = (v_first, _pad_cols(rwkv_v1[l - 1], LANES).astype(BF16),
                    _pad_rows(rwkv_v2[l - 1], LANES).astype(BF16))
        r, v, kk, k_f, k_b, b_f, b_b, lw_f, lw_b, gate = _rwkv_prep_call(
            p_rkv, p_codes, mu_p, mu_c, pvec, _dir_padded(rwkv_w2[l]).astype(BF16), _dir_padded(rwkv_a2[l]).astype(BF16), g2p, vres, seq)
        if l == 0:
            v_first = v
        y_f, y_b = _rwkv_scan_call(r, v, kk, k_f, b_f, lw_f, k_b, b_b, lw_b, bsz, seq)
        pvec2 = jnp.concatenate([jnp.stack([rwkv_ln_w[l], rwkv_ln_b[l], rwkv_r_k[l].reshape(RW_DIM)]),
                                 jnp.zeros((5, RW_DIM), F32)])

        conv_w = _pad_rows(mamba_conv_w[l], 8)
        xs, bm, cm = _mamba_conv_call(p_xbc, conv_w, mamba_conv_b[l][None], seq)
        dtb = _pad_cols(mamba_dt_bias[l].reshape(1, 2 * M_HEADS), LANES)
        alog_c = _pad_cols(mamba_a_log[l].reshape(1, 2 * M_HEADS), LANES)
        ym_f, ym_b = _ssd_call(xs, bm, cm, p_dt, dtb, emat, alog_c, bsz, seq)
        d_e = jnp.repeat(mamba_d[l], M_HEADDIM)[None]

        lambda_init = 0.8 - 0.6 * math.exp(-0.3 * l)
        y_d = _diff_attn_call(p_qkv, diff_lambda[l], slopes, diff_subln_w[l][None], bsz, seq, lambda_init)

        x2 = _mixer_tail_call((y_f, y_b, r, k_f, k_b, v, gate), pvec2, (ym_f, ym_b, xs, p_z), d_e,
                              mamba_norm_w[l][None], y_d, p_gate, x2, mod[:, 1], norm_w[l, 3:4],
                              w_branch_rwkv[l].astype(BF16), w_branch_mamba[l].astype(BF16),
                              w_branch_diff[l].astype(BF16), w_out[l].astype(BF16), seq)

        x2 = _ffn_call(x2, mod[:, 2], norm_w[l, 4:6], ffn_w13[l, 1].astype(BF16), ffn_w2[l, 1].astype(BF16), seq)
    return x2.reshape(bsz, seq, d)
```
